```python
import math
import jax
import jax.numpy as jnp
from jax import lax
import numpy as np

D_MODEL = 1024
BATCH = 8
SEQ = 2048
DEPTH = 4

GRID_W = 64
CTX_LEN = 256
HEAD_DIM = 64
ROPE_THETA = 10000.0
Q_BLOCK = 128
RMS_EPS = 1e-6
SUBLN_EPS = 1e-5

A_HEADS = D_MODEL // (2 * HEAD_DIM)
A_KV_HEADS = A_HEADS // 4
A_WIDTH = A_HEADS * HEAD_DIM
A_KV_WIDTH = A_KV_HEADS * HEAD_DIM
A_COLS = A_WIDTH + 2 * A_KV_WIDTH

B_HEADS = D_MODEL // (2 * HEAD_DIM)
B_WIDTH = B_HEADS * HEAD_DIM
RWKV_W_LORA = 64
RWKV_A_LORA = 64
RWKV_V_LORA = 32
RWKV_G_LORA = 128
RWKV_GN_EPS = 64e-5
DECAY_SCALE = math.exp(-0.5)
B_SPLITS = (B_WIDTH, 2 * B_WIDTH, 3 * B_WIDTH, 3 * B_WIDTH + RWKV_W_LORA,
            3 * B_WIDTH + RWKV_W_LORA + RWKV_A_LORA)
B_COLS = 3 * B_WIDTH + RWKV_W_LORA + RWKV_A_LORA + RWKV_G_LORA
EVEN_COLS = A_COLS + B_COLS

C_HEADS = D_MODEL // (2 * HEAD_DIM)
C_QK = C_HEADS * 2 * HEAD_DIM
ODD_COLS = 2 * C_QK + C_HEADS * 2 * HEAD_DIM

N_GROUPS = 4
EXPERTS_PER_GROUP = 8
N_EXPERTS = N_GROUPS * EXPERTS_PER_GROUP
TOP_K = 2
EXPERT_HIDDEN = D_MODEL // 2
EXPERT_BLOCK = 128

N_EVEN = (DEPTH + 1) // 2
N_ODD = DEPTH // 2

kernel_name = 'hybrid_gqa_rwkv7_diffattn_hmoe_dit'


def rms_norm(x, gain=None, eps=RMS_EPS):
    xf = x.astype(jnp.float32)
    y = xf * lax.rsqrt(jnp.mean(xf * xf, axis=-1, keepdims=True) + eps)
    if gain is not None:
        y = y * gain.astype(jnp.float32)
    return y.astype(x.dtype)


def modulate(h, shift, scale):
    return h * (1.0 + scale) + shift


def axial_rope_tables(rows):
    nf = HEAD_DIM // 4
    inv = ROPE_THETA ** (-jnp.arange(nf, dtype=jnp.float32) / nf)
    r_idx = jnp.repeat(jnp.arange(rows, dtype=jnp.float32), GRID_W)
    c_idx = jnp.tile(jnp.arange(GRID_W, dtype=jnp.float32), rows)
    ang = jnp.concatenate([r_idx[:, None] * inv, c_idx[:, None] * inv], axis=-1)
    return jnp.cos(ang), jnp.sin(ang)


def apply_axial_rope(x, cos, sin):
    shp = x.shape
    nf = shp[-1] // 4
    n_head_axes = x.ndim - 3
    xr = x.astype(jnp.float32).reshape(*shp[:-1], 2, 2, nf)
    cs = cos.reshape(cos.shape[0], *([1] * n_head_axes), 2, nf)
    sn = sin.reshape(sin.shape[0], *([1] * n_head_axes), 2, nf)
    x1, x2 = xr[..., 0, :], xr[..., 1, :]
    out = jnp.stack([x1 * cs - x2 * sn, x1 * sn + x2 * cs], axis=-2)
    return out.reshape(shp).astype(x.dtype)


def sweep_query_blocks(attend, q):
    b, t = q.shape[:2]
    nblk = t // Q_BLOCK
    qb = jnp.moveaxis(q.reshape(b, nblk, Q_BLOCK, *q.shape[2:]), 1, 0)
    ob = jnp.moveaxis(lax.map(attend, qb), 0, 1)
    return ob.reshape(b, t, *ob.shape[3:])


def gqa_project(pa, q_gain, k_gain):
    b, t, _ = pa.shape
    q = pa[..., :A_WIDTH].reshape(b, t, A_KV_HEADS, A_HEADS // A_KV_HEADS, HEAD_DIM)
    k = pa[..., A_WIDTH:A_WIDTH + A_KV_WIDTH].reshape(b, t, A_KV_HEADS, HEAD_DIM)
    v = pa[..., A_WIDTH + A_KV_WIDTH:].reshape(b, t, A_KV_HEADS, HEAD_DIM)
    return rms_norm(q, q_gain), rms_norm(k, k_gain), v


def gqa_attend(q, k, v):
    s = jnp.einsum('bqhgd,bkhd->bhgqk', q, k, preferred_element_type=jnp.float32) * HEAD_DIM ** -0.5
    p = jax.nn.softmax(s, axis=-1).astype(v.dtype)
    return jnp.einsum('bhgqk,bkhd->bqhgd', p, v)


def token_shift(u, mu_prev, mu_next):
    zero = jnp.zeros_like(u[:, :1])
    prev = jnp.concatenate([zero, u[:, :-1]], axis=1)
    nxt = jnp.concatenate([u[:, 1:], zero], axis=1)
    return u + mu_prev * (prev - u) + mu_next * (nxt - u)


def rwkv_prepare(u, v_first, p):
    b, t, _ = u.shape
    u = token_shift(u, p['mu_prev'], p['mu_next'])
    r, k, v, wlr, alr, glr = jnp.split(u, B_SPLITS, axis=-1)
    if v_first is None:
        v_first = v
    else:
        v = v + (v_first - v) * jax.nn.sigmoid(p['v0'] + (v @ p['v1']) @ p['v2'])
    heads = lambda z: z.astype(jnp.float32).reshape(b, t, B_HEADS, HEAD_DIM)
    kk = heads(k * p['k_k'])
    kk = kk / jnp.maximum(jnp.linalg.norm(kk, axis=-1, keepdims=True), 1e-12)
    k_a = p['k_a'].astype(jnp.float32).reshape(B_HEADS, HEAD_DIM)
    tw = jnp.tanh(wlr)
    dirs = []
    for dr in range(2):
        w = jnp.exp(-DECAY_SCALE * jax.nn.sigmoid(heads(p['w0'][dr] + tw @ p['w2'][dr])))
        a = jax.nn.sigmoid(heads(p['a0'][dr] + alr @ p['a2'][dr]))
        kd = heads(k) * (1.0 + (a - 1.0) * k_a)
        dirs.append({'w': w, 'a': a, 'k': kd})
    g = jax.nn.sigmoid(glr) @ p['g2']
    return {'r': heads(r), 'v': heads(v), 'kk': kk, 'g': g, 'dirs': dirs}, v_first


def delta_scan(s0, seq, dr, reverse):
    dd = seq['dirs'][dr]
    xs = tuple(jnp.moveaxis(z, 1, 0) for z in (seq['r'], dd['w'], dd['k'], seq['v'], seq['kk'], dd['a']))

    def step(s, inp):
        r_t, w_t, k_t, v_t, kk_t, a_t = inp
        s_kk = jnp.einsum('bhvk,bhk->bhv', s, kk_t)
        s = (s * w_t[:, :, None, :] - s_kk[..., None] * (kk_t * a_t)[:, :, None, :]
             + v_t[..., None] * k_t[:, :, None, :])
        return s, jnp.einsum('bhvk,bhk->bhv', s, r_t)

    s_last, ys = lax.scan(step, s0, xs, reverse=reverse)
    return s_last, jnp.moveaxis(ys, 0, 1)


def rwkv_readout(y, seq, p):
    b, t = y.shape[:2]
    mu = jnp.mean(y, axis=-1, keepdims=True)
    var = jnp.mean(jnp.square(y - mu), axis=-1, keepdims=True)
    gn_w = p['gn_w'].astype(jnp.float32).reshape(B_HEADS, HEAD_DIM)
    gn_b = p['gn_b'].astype(jnp.float32).reshape(B_HEADS, HEAD_DIM)
    y = (y - mu) * lax.rsqrt(var + RWKV_GN_EPS) * gn_w + gn_b
    r_k = p['r_k'].astype(jnp.float32)
    for dd in seq['dirs']:
        y = y + jnp.sum(seq['r'] * dd['k'] * r_k, axis=-1, keepdims=True) * seq['v']
    g = seq['g']
    return (y.reshape(b, t, B_WIDTH) * g.astype(jnp.float32)).astype(g.dtype)


def rwkv_mix(u_lat, u_ctx, v_first, p):
    vf_lat, vf_ctx = (None, None) if v_first is None else v_first
    lat, vf_lat = rwkv_prepare(u_lat, vf_lat, p)
    ctx, vf_ctx = rwkv_prepare(u_ctx, vf_ctx, p)
    s0 = jnp.zeros((u_lat.shape[0], B_HEADS, HEAD_DIM, HEAD_DIM), jnp.float32)
    y_lat = 0.0
    y_ctx = 0.0
    for dr, reverse in enumerate((False, True)):
        s_ctx, yc = delta_scan(s0, ctx, dr, reverse)
        _, yl = delta_scan(s_ctx, lat, dr, reverse)
        y_lat = y_lat + yl
        y_ctx = y_ctx + yc
    return rwkv_readout(y_lat, lat, p), rwkv_readout(y_ctx, ctx, p), (vf_lat, vf_ctx)


def even_mixer(hl, hc, v_first, p, cos, sin):
    b, t, _ = hl.shape
    pl = hl @ p['w_in']
    pc = hc @ p['w_in']
    ql, kl, vl = gqa_project(pl[..., :A_COLS], p['q_gain'], p['k_gain'])
    qc, kc, vc = gqa_project(pc[..., :A_COLS], p['q_gain'], p['k_gain'])
    ql = apply_axial_rope(ql, cos, sin)
    kl = apply_axial_rope(kl, cos, sin)
    k_all = jnp.concatenate([kc, kl], axis=1)
    v_all = jnp.concatenate([vc, vl], axis=1)
    al = sweep_query_blocks(lambda qb: gqa_attend(qb, k_all, v_all), ql).reshape(b, t, A_WIDTH)
    ac = gqa_attend(qc, kc, vc).reshape(b, hc.shape[1], A_WIDTH)
    bl, bc, v_first = rwkv_mix(pl[..., A_COLS:], pc[..., A_COLS:], v_first, p)
    out_l = jnp.concatenate([al, bl], axis=-1) @ p['w_out']
    out_c = jnp.concatenate([ac, bc], axis=-1) @ p['w_out']
    return out_l, out_c, v_first


def diff_attend(q, k, v, lam):
    s = jnp.einsum('bqhcd,bkhcd->bhcqk', q, k, preferred_element_type=jnp.float32) * HEAD_DIM ** -0.5
    pr = jax.nn.softmax(s, axis=-1)
    pr = pr[:, :, 0] - lam * pr[:, :, 1]
    return jnp.einsum('bhqk,bkhe->bqhe', pr.astype(v.dtype), v)


def diff_mixer(hl, hc, p, lambda_init, cos, sin, need_ctx):
    def split_q(z):
        return z.reshape(*z.shape[:2], C_HEADS, 2, HEAD_DIM)

    def split_kv(z):
        k = z[..., :C_QK].reshape(*z.shape[:2], C_HEADS, 2, HEAD_DIM)
        v = z[..., C_QK:].reshape(*z.shape[:2], C_HEADS, 2 * HEAD_DIM)
        return k, v

    def finish(o):
        o = rms_norm(o, p['subln'], SUBLN_EPS) * (1.0 - lambda_init)
        return o.reshape(*o.shape[:2], C_HEADS * 2 * HEAD_DIM) @ p['w_out']

    pl = hl @ p['w_in']
    ql = apply_axial_rope(split_q(pl[..., :C_QK]), cos, sin)
    kl, vl = split_kv(pl[..., C_QK:])
    kl = apply_axial_rope(kl, cos, sin)
    kc, vc = split_kv(hc @ p['w_in'][:, C_QK:])
    f32 = jnp.float32
    lam = (jnp.exp(jnp.sum(p['lq1'].astype(f32) * p['lk1'].astype(f32)))
           - jnp.exp(jnp.sum(p['lq2'].astype(f32) * p['lk2'].astype(f32))) + lambda_init)
    k_all = jnp.concatenate([kc, kl], axis=1)
    v_all = jnp.concatenate([vc, vl], axis=1)
    out_l = finish(sweep_query_blocks(lambda qb: diff_attend(qb, k_all, v_all, lam), ql))
    if not need_ctx:
        return out_l, None
    qc = split_q(hc @ p['w_in'][:, :C_QK])
    return out_l, finish(diff_attend(qc, kc, vc, lam))


def grouped_expert_mlp(h, expert_id, w_gate, w_up, w_down):
    n, d = h.shape
    k = expert_id.shape[1]
    m = n * k
    e_flat = expert_id.reshape(m)
    tok = jnp.arange(m, dtype=jnp.int32) // k
    counts = jnp.bincount(e_flat, length=N_EXPERTS).astype(jnp.int32)
    padded = (counts + EXPERT_BLOCK - 1) // EXPERT_BLOCK * EXPERT_BLOCK
    pad_end = jnp.cumsum(padded)
    pad_start = pad_end - padded
    raw_start = jnp.cumsum(counts) - counts
    order = jnp.argsort(e_flat)
    e_sorted = e_flat[order]
    slot_sorted = pad_start[e_sorted] + jnp.arange(m, dtype=jnp.int32) - raw_start[e_sorted]
    n_blocks = -(-(m + N_EXPERTS * (EXPERT_BLOCK - 1)) // EXPERT_BLOCK)
    slot_tok = jnp.full((n_blocks * EXPERT_BLOCK,), n, jnp.int32).at[slot_sorted].set(tok[order])
    block_start = jnp.arange(n_blocks, dtype=jnp.int32) * EXPERT_BLOCK
    block_expert = jnp.minimum(jnp.searchsorted(pad_end, block_start, side='right'), N_EXPERTS - 1)
    h_pad = jnp.concatenate([h, jnp.zeros((1, d), h.dtype)], axis=0)
    xb = h_pad[slot_tok].reshape(n_blocks, EXPERT_BLOCK, d)

    def run_block(args):
        xblk, e = args
        return (jax.nn.silu(xblk @ w_gate[e]) * (xblk @ w_up[e])) @ w_down[e]

    yb = lax.map(run_block, (xb, block_expert)).reshape(n_blocks * EXPERT_BLOCK, d)
    slot = jnp.zeros((m,), jnp.int32).at[order].set(slot_sorted)
    return yb[slot].reshape(n, k, d)


def moe_ffn(h, w_grp, b_grp, w_rt, b_rt, w_gate, w_up, w_down):
    n = h.shape[0]
    f32 = jnp.float32
    hf = h.astype(f32)
    grp_logits = hf @ w_grp.astype(f32) + b_grp.astype(f32)
    grp_sel = jnp.argmax(grp_logits, axis=-1)
    grp_w = jnp.take_along_axis(jax.nn.softmax(grp_logits, axis=-1), grp_sel[:, None], axis=1)
    exp_logits = (hf @ w_rt.astype(f32) + b_rt.astype(f32)).reshape(n, N_GROUPS, EXPERTS_PER_GROUP)
    exp_logits = jnp.take_along_axis(exp_logits, grp_sel[:, None, None], axis=1)[:, 0]
    top_p, top_i = lax.top_k(jax.nn.softmax(exp_logits, axis=-1), TOP_K)
    gates = grp_w * top_p / jnp.sum(top_p, axis=-1, keepdims=True)
    expert_id = (grp_sel[:, None] * EXPERTS_PER_GROUP + top_i).astype(jnp.int32)
    y = grouped_expert_mlp(h, expert_id, w_gate, w_up, w_down)
    return jnp.einsum('nk,nkd->nd', gates.astype(h.dtype), y)


def setup_inputs(seed: int = 0) -> dict:
    key = jax.random.key(seed)
    ks = iter(jax.random.split(key, 48))
    d = D_MODEL

    def nrm(shape, scale=1.0):
        return scale * jax.random.normal(next(ks), shape, jnp.float32)

    def unif(shape, lo, hi):
        return jax.random.uniform(next(ks), shape, jnp.float32, lo, hi)

    return {
        'x': nrm((BATCH, SEQ, d)),
        'c': nrm((BATCH, d)),
        'ctx': nrm((BATCH, CTX_LEN, d)),
        'c_ctx': nrm((d,)),
        'mod_w': nrm((DEPTH, d, 6 * d), 0.3 * d ** -0.5),
        'mod_b': nrm((DEPTH, 6 * d), 0.02),
        'ev_w_in': nrm((N_EVEN, d, EVEN_COLS), d ** -0.5),
        'ev_w_out': nrm((N_EVEN, A_WIDTH + B_WIDTH, d), (A_WIDTH + B_WIDTH) ** -0.5),
        'ev_q_gain': 1.0 + nrm((N_EVEN, HEAD_DIM), 0.05),
        'ev_k_gain': 1.0 + nrm((N_EVEN, HEAD_DIM), 0.05),
        'ev_mu_prev': unif((N_EVEN, B_COLS), 0.0, 0.5),
        'ev_mu_next': unif((N_EVEN, B_COLS), 0.0, 0.5),
        'ev_w0': -1.0 + nrm((N_EVEN, 2, B_WIDTH), 0.5),
        'ev_w2': nrm((N_EVEN, 2, RWKV_W_LORA, B_WIDTH), 0.5 * RWKV_W_LORA ** -0.5),
        'ev_a0': nrm((N_EVEN, 2, B_WIDTH), 0.5),
        'ev_a2': nrm((N_EVEN, 2, RWKV_A_LORA, B_WIDTH), 0.5 * RWKV_A_LORA ** -0.5),
        'ev_g2': nrm((N_EVEN, RWKV_G_LORA, B_WIDTH), RWKV_G_LORA ** -0.5),
        'ev_k_k': 0.85 + nrm((N_EVEN, B_WIDTH), 0.05),
        'ev_k_a': 1.0 + nrm((N_EVEN, B_WIDTH), 0.05),
        'ev_r_k': nrm((N_EVEN, B_HEADS, HEAD_DIM), 0.1),
        'ev_gn_w': 1.0 + nrm((N_EVEN, B_WIDTH), 0.05),
        'ev_gn_b': nrm((N_EVEN, B_WIDTH), 0.02),
        'ev_v0': nrm((N_EVEN - 1, B_WIDTH), 0.5),
        'ev_v1': nrm((N_EVEN - 1, B_WIDTH, RWKV_V_LORA), B_WIDTH ** -0.5),
        'ev_v2': nrm((N_EVEN - 1, RWKV_V_LORA, B_WIDTH), RWKV_V_LORA ** -0.5),
        'od_w_in': nrm((N_ODD, d, ODD_COLS), d ** -0.5),
        'od_w_out': nrm((N_ODD, C_HEADS * 2 * HEAD_DIM, d), d ** -0.5),
        'od_lq1': nrm((N_ODD, HEAD_DIM), 0.1),
        'od_lk1': nrm((N_ODD, HEAD_DIM), 0.1),
        'od_lq2': nrm((N_ODD, HEAD_DIM), 0.1),
        'od_lk2': nrm((N_ODD, HEAD_DIM), 0.1),
        'od_subln': 1.0 + nrm((N_ODD, 2 * HEAD_DIM), 0.05),
        'moe_w_grp': nrm((DEPTH, d, N_GROUPS), d ** -0.5),
        'moe_b_grp': nrm((DEPTH, N_GROUPS), 0.01),
        'moe_w_rt': nrm((DEPTH, d, N_EXPERTS), d ** -0.5),
        'moe_b_rt': nrm((DEPTH, N_EXPERTS), 0.01),
        'moe_w_gate': nrm((DEPTH, N_EXPERTS, d, EXPERT_HIDDEN), d ** -0.5),
        'moe_w_up': nrm((DEPTH, N_EXPERTS, d, EXPERT_HIDDEN), d ** -0.5),
        'moe_w_down': nrm((DEPTH, N_EXPERTS, EXPERT_HIDDEN, d), EXPERT_HIDDEN ** -0.5),
        'final_gain': 1.0 + nrm((d,), 0.05),
    }


def reference(x, c, ctx, c_ctx, mod_w, mod_b, ev_w_in, ev_w_out, ev_q_gain, ev_k_gain,
              ev_mu_prev, ev_mu_next, ev_w0, ev_w2, ev_a0, ev_a2, ev_g2, ev_k_k, ev_k_a, ev_r_k,
              ev_gn_w, ev_gn_b, ev_v0, ev_v1, ev_v2, od_w_in, od_w_out, od_lq1, od_lk1, od_lq2,
              od_lk2, od_subln, moe_w_grp, moe_b_grp, moe_w_rt, moe_b_rt, moe_w_gate, moe_w_up,
              moe_w_down, final_gain):
    bsz, n_lat, d = x.shape
    rows = n_lat // GRID_W
    cos, sin = axial_rope_tables(rows)
    cond_lat = jax.nn.silu(c)
    cond_ctx = jax.nn.silu(c_ctx)[None, :]
    xc = ctx
    v_first = None
    for layer in range(DEPTH):
        last = layer == DEPTH - 1
        mods_l = jnp.split((cond_lat @ mod_w[layer] + mod_b[layer])[:, None, :], 6, axis=-1)
        mods_c = jnp.split((cond_ctx @ mod_w[layer] + mod_b[layer])[:, None, :], 6, axis=-1)
        hl = modulate(rms_norm(x), mods_l[0], mods_l[1])
        hc = modulate(rms_norm(xc), mods_c[0], mods_c[1])
        if layer % 2 == 0:
            e = layer // 2
            p = {'w_in': ev_w_in[e], 'w_out': ev_w_out[e], 'q_gain': ev_q_gain[e],
                 'k_gain': ev_k_gain[e], 'mu_prev': ev_mu_prev[e], 'mu_next': ev_mu_next[e],
                 'w0': ev_w0[e], 'w2': ev_w2[e], 'a0': ev_a0[e], 'a2': ev_a2[e], 'g2': ev_g2[e],
                 'k_k': ev_k_k[e], 'k_a': ev_k_a[e], 'r_k': ev_r_k[e], 'gn_w': ev_gn_w[e],
                 'gn_b': ev_gn_b[e]}
            if e > 0:
                p['v0'] = ev_v0[e - 1]
                p['v1'] = ev_v1[e - 1]
                p['v2'] = ev_v2[e - 1]
            out_l, out_c, v_first = even_mixer(hl, hc, v_first, p, cos, sin)
        else:
            o = layer // 2
            p = {'w_in': od_w_in[o], 'w_out': od_w_out[o], 'lq1': od_lq1[o], 'lk1': od_lk1[o],
                 'lq2': od_lq2[o], 'lk2': od_lk2[o], 'subln': od_subln[o]}
            lambda_init = 0.8 - 0.6 * math.exp(-0.3 * layer)
            out_l, out_c = diff_mixer(hl, hc, p, lambda_init, cos, sin, not last)
        x = x + mods_l[2] * out_l
        hl = modulate(rms_norm(x), mods_l[3], mods_l[4])
        moe_p = (moe_w_grp[layer], moe_b_grp[layer], moe_w_rt[layer], moe_b_rt[layer],
                 moe_w_gate[layer], moe_w_up[layer], moe_w_down[layer])
        if last:
            x = x + mods_l[5] * moe_ffn(hl.reshape(-1, d), *moe_p).reshape(x.shape)
        else:
            xc = xc + mods_c[2] * out_c
            hc = modulate(rms_norm(xc), mods_c[3], mods_c[4])
            n_ctx = hc.shape[0] * hc.shape[1]
            y = moe_ffn(jnp.concatenate([hc.reshape(-1, d), hl.reshape(-1, d)], axis=0), *moe_p)
            xc = xc + mods_c[5] * y[:n_ctx].reshape(xc.shape)
            x = x + mods_l[5] * y[n_ctx:].reshape(x.shape)
    return rms_norm(x, final_gain)
```

```python
import functools
import math

import jax
import jax.numpy as jnp
from jax import lax
from jax.experimental import pallas as pl
from jax.experimental.pallas import tpu as pltpu

D_MODEL = 1024
DEPTH = 4
GRID_W = 64
CTX_LEN = 256
HEAD_DIM = 64
ROPE_THETA = 10000.0
RMS_EPS = 1e-6
SUBLN_EPS = 1e-5

A_HEADS = 8
A_KV_HEADS = 2
A_WIDTH = 512
A_KV_WIDTH = 128
A_COLS = 768

B_HEADS = 8
B_WIDTH = 512
RWKV_W_LORA = 64
RWKV_A_LORA = 64
RWKV_G_LORA = 128
RWKV_GN_EPS = 64e-5
DECAY_SCALE = math.exp(-0.5)
B_SPLITS = (512, 1024, 1536, 1600, 1664)

C_HEADS = 8
C_QK = 1024

N_GROUPS = 4
EXPERTS_PER_GROUP = 8
N_EXPERTS = 32
TOP_K = 2
EXPERT_HIDDEN = 512
EXPERT_BLOCK = 128

LANES = 128
SCAN_CHUNK = 64
ROW_TILE = 256
Q_TILE = 128
VMEM_LIMIT = 56 * 1024 * 1024

_HI = lax.Precision.HIGHEST
_F32 = jnp.float32
_BF16 = jnp.bfloat16


def _mm_kernel(a_ref, w_ref, o_ref, *, precision):
    o_ref[...] = jnp.dot(a_ref[...], w_ref[...], precision=precision,
                         preferred_element_type=_F32).astype(o_ref.dtype)


def _mm(a, w, *, out_dtype=_F32, tm=ROW_TILE, tn=None, precision=None):
    m, k = a.shape
    n = w.shape[1]
    tn = n if tn is None else tn
    tm = min(tm, m)
    return pl.pallas_call(
        functools.partial(_mm_kernel, precision=precision),
        grid=(m // tm, n // tn),
        in_specs=[pl.BlockSpec((tm, k), lambda i, j: (i, 0)),
                  pl.BlockSpec((k, tn), lambda i, j: (0, j))],
        out_specs=pl.BlockSpec((tm, tn), lambda i, j: (i, j)),
        out_shape=jax.ShapeDtypeStruct((m, n), out_dtype),
        compiler_params=pltpu.CompilerParams(
            dimension_semantics=("arbitrary", "arbitrary"), vmem_limit_bytes=VMEM_LIMIT),
    )(a, w)


def _mm_bf16(a, w, **kw):
    return _mm(a.astype(_BF16), w.astype(_BF16), **kw)


def _lane_lo(shape):
    return lax.broadcasted_iota(jnp.int32, shape, len(shape) - 1) < HEAD_DIM


def _split_heads(q2):
    lo = _lane_lo(q2.shape)
    zero = jnp.zeros_like(q2)
    return jnp.concatenate([jnp.where(lo, q2, zero), jnp.where(lo, zero, q2)], axis=0)


def _scores(qs, k):
    return lax.dot_general(qs, k, (((1,), (1,)), ((), ())), preferred_element_type=_F32)


def _gqa_kernel(q_ref, k_ref, v_ref, o_ref):
    q = q_ref[0]
    tq = q.shape[0]
    qs = jnp.concatenate([_split_heads(q[:, :LANES]), _split_heads(q[:, LANES:])], axis=0)
    s = _scores(qs, k_ref[0])
    p = jnp.exp(s - jnp.max(s, axis=-1, keepdims=True))
    l = jnp.sum(p, axis=-1, keepdims=True)
    o = jnp.dot(p.astype(_BF16), v_ref[0], preferred_element_type=_F32) / l
    lo = _lane_lo((tq, LANES))
    o_ref[0] = jnp.concatenate([jnp.where(lo, o[:tq], o[tq:2 * tq]),
                                jnp.where(lo, o[2 * tq:3 * tq], o[3 * tq:])], axis=1).astype(o_ref.dtype)


def _gqa_attend(q, kk, vv, *, q_start, q_len, k_len):
    b = q.shape[0]
    tq = Q_TILE
    off = q_start // tq
    return pl.pallas_call(
        _gqa_kernel,
        grid=(b, A_KV_HEADS, q_len // tq),
        in_specs=[pl.BlockSpec((1, tq, 2 * LANES), lambda i, g, j: (i, j + off, g)),
                  pl.BlockSpec((1, k_len, LANES), lambda i, g, j: (i, 0, g)),
                  pl.BlockSpec((1, k_len, LANES), lambda i, g, j: (i, 0, g))],
        out_specs=pl.BlockSpec((1, tq, 2 * LANES), lambda i, g, j: (i, j, g)),
        out_shape=jax.ShapeDtypeStruct((b, q_len, A_WIDTH), _BF16),
        compiler_params=pltpu.CompilerParams(
            dimension_semantics=("arbitrary",) * 3, vmem_limit_bytes=VMEM_LIMIT),
    )(q, kk, vv)


def _diff_kernel(lam_ref, q_ref, k_ref, v_ref, g_ref, o_ref):
    q = q_ref[0]
    tq = q.shape[0]
    s = _scores(_split_heads(q), k_ref[0])
    p = jnp.exp(s - jnp.max(s, axis=-1, keepdims=True))
    l = jnp.sum(p, axis=-1, keepdims=True)
    l1, l2 = l[:tq], l[tq:]
    pr = p[:tq] - (lam_ref[0, 0] * l1 / l2) * p[tq:]
    o = jnp.dot(pr.astype(_BF16), v_ref[0], preferred_element_type=_F32) / l1
    y = o * lax.rsqrt(jnp.mean(o * o, axis=-1, keepdims=True) + SUBLN_EPS)
    o_ref[0] = (y * g_ref[...]).astype(o_ref.dtype)


def _diff_attend(lam, q, k, v, gain, *, q_start, q_len, k_len):
    b = q.shape[0]
    tq = Q_TILE
    off = q_start // tq
    return pl.pallas_call(
        _diff_kernel,
        grid=(b, C_HEADS, q_len // tq),
        in_specs=[pl.BlockSpec(memory_space=pltpu.SMEM),
                  pl.BlockSpec((1, tq, LANES), lambda i, h, j: (i, j + off, h)),
                  pl.BlockSpec((1, k_len, LANES), lambda i, h, j: (i, 0, h)),
                  pl.BlockSpec((1, k_len, LANES), lambda i, h, j: (i, 0, h)),
                  pl.BlockSpec((1, LANES), lambda i, h, j: (0, 0))],
        out_specs=pl.BlockSpec((1, tq, LANES), lambda i, h, j: (i, j, h)),
        out_shape=jax.ShapeDtypeStruct((b, q_len, C_HEADS * LANES), _BF16),
        compiler_params=pltpu.CompilerParams(
            dimension_semantics=("arbitrary",) * 3, vmem_limit_bytes=VMEM_LIMIT),
    )(lam, q, k, v, gain)


def _dot_hi(a, b):
    return jnp.dot(a, b, precision=_HI, preferred_element_type=_F32)


def _dot_nt(a, b):
    return lax.dot_general(a, b, (((1,), (1,)), ((), ())), precision=_HI, preferred_element_type=_F32)


def _dot_tn(a, b):
    return lax.dot_general(a, b, (((0,), (0,)), ((), ())), precision=_HI, preferred_element_type=_F32)


def _scan_chunk(h, r, v, kk, lw, a, kd):
    c = SCAN_CHUNK
    lo = _lane_lo((c, LANES))
    row = lax.broadcasted_iota(jnp.int32, (LANES, LANES), 0)
    col = lax.broadcasted_iota(jnp.int32, (LANES, LANES), 1)
    before = (col & (c - 1)) < (row & (c - 1))
    upto = (col & (c - 1)) <= (row & (c - 1))
    eye = row == col

    def blockdiag(x):
        zero = jnp.zeros_like(x)
        return jnp.concatenate([jnp.where(lo, x, zero), jnp.where(lo, zero, x)], axis=0)

    tr = lax.broadcasted_iota(jnp.int32, (c, c), 0)
    tc = lax.broadcasted_iota(jnp.int32, (c, c), 1)
    cum = _dot_hi((tc <= tr).astype(_F32), lw)
    cum_end = cum[c - 1:c, :]
    p_inv = jnp.exp(-cum)
    p_end = jnp.exp(cum_end - cum)
    kka = kk * a
    a_rs = blockdiag(-kk * jnp.exp(cum - lw))
    r_rs = blockdiag(r * jnp.exp(cum))
    b_rs = blockdiag(kka * p_inv)
    k_rs = blockdiag(kd * p_inv)
    v_rs = blockdiag(v)
    bp_rs = blockdiag(kka * p_end)
    kp_rs = blockdiag(kd * p_end)

    zero = jnp.zeros((LANES, LANES), _F32)
    low = jnp.where(before, _dot_nt(a_rs, b_rs), zero)
    g = jnp.where(before, _dot_nt(a_rs, k_rs), zero)
    rb = jnp.where(upto, _dot_nt(r_rs, b_rs), zero)
    rk = jnp.where(upto, _dot_nt(r_rs, k_rs), zero)
    x = low
    m = jnp.where(eye, 1.0, 0.0).astype(_F32) + x
    for _ in range(int(math.log2(c)) - 1):
        x = _dot_hi(x, x)
        m = m + _dot_hi(m, x)
    a2 = _dot_hi(m, a_rs)
    u0 = _dot_hi(m, _dot_hi(g, v_rs))
    t = jnp.where(eye, jnp.broadcast_to(jnp.exp(cum_end), (LANES, LANES)), zero) + _dot_tn(bp_rs, a2)
    f = _dot_tn(bp_rs, u0) + _dot_tn(kp_rs, v_rs)
    q = r_rs + _dot_hi(rb, a2)
    y0 = _dot_hi(rb, u0) + _dot_hi(rk, v_rs)
    y = _dot_hi(q, h) + y0
    h_new = _dot_hi(t, h) + f
    return y[:c] + y[c:], h_new


def _scan_kernel(r_ref, v_ref, kk_ref, lw_ref, a_ref, kd_ref, y_ref, h_ref):
    @pl.when(pl.program_id(1) == 0)
    def _():
        h_ref[...] = jnp.zeros_like(h_ref)

    for p in range(r_ref.shape[0]):
        y, h = _scan_chunk(h_ref[p], r_ref[p], v_ref[p], kk_ref[p], lw_ref[p], a_ref[p], kd_ref[p])
        y_ref[p] = y
        h_ref[p] = h


def _delta_scan(r, v, kk, lw, a, kd, *, pairs_per_step=2):
    p, t, _ = r.shape
    g = pairs_per_step
    spec = pl.BlockSpec((g, SCAN_CHUNK, LANES), lambda i, c: (i, c, 0))
    return pl.pallas_call(
        _scan_kernel,
        grid=(p // g, t // SCAN_CHUNK),
        in_specs=[spec] * 6,
        out_specs=spec,
        out_shape=jax.ShapeDtypeStruct((p, t, LANES), _F32),
        scratch_shapes=[pltpu.VMEM((g, LANES, LANES), _F32)],
        compiler_params=pltpu.CompilerParams(
            dimension_semantics=("arbitrary", "arbitrary"), vmem_limit_bytes=VMEM_LIMIT),
    )(r, v, kk, lw, a, kd)


def _moe_kernel(be_ref, nv_ref, x_ref, wg_ref, wu_ref, wd_ref, o_ref):
    del be_ref

    @pl.when(pl.program_id(0) < nv_ref[0])
    def _():
        x = x_ref[...]
        gate = jnp.dot(x, wg_ref[0], preferred_element_type=_F32)
        up = jnp.dot(x, wu_ref[0], preferred_element_type=_F32)
        hid = (gate * jax.nn.sigmoid(gate) * up).astype(_BF16)
        o_ref[...] = jnp.dot(hid, wd_ref[0], preferred_element_type=_F32)

    @pl.when(pl.program_id(0) >= nv_ref[0])
    def _():
        o_ref[...] = jnp.zeros_like(o_ref)


def _moe_blocks(block_expert, n_valid, xs, wg, wu, wd):
    rows, d = xs.shape
    nb = rows // EXPERT_BLOCK
    grid_spec = pltpu.PrefetchScalarGridSpec(
        num_scalar_prefetch=2,
        grid=(nb,),
        in_specs=[pl.BlockSpec((EXPERT_BLOCK, d), lambda i, be, nv: (i, 0)),
                  pl.BlockSpec((1, d, EXPERT_HIDDEN), lambda i, be, nv: (be[i], 0, 0)),
                  pl.BlockSpec((1, d, EXPERT_HIDDEN), lambda i, be, nv: (be[i], 0, 0)),
                  pl.BlockSpec((1, EXPERT_HIDDEN, d), lambda i, be, nv: (be[i], 0, 0))],
        out_specs=pl.BlockSpec((EXPERT_BLOCK, d), lambda i, be, nv: (i, 0)),
    )
    return pl.pallas_call(
        _moe_kernel,
        grid_spec=grid_spec,
        out_shape=jax.ShapeDtypeStruct((rows, d), _F32),
        compiler_params=pltpu.CompilerParams(
            dimension_semantics=("arbitrary",), vmem_limit_bytes=VMEM_LIMIT),
    )(block_expert, n_valid, xs, wg, wu, wd)


def _rms(x, gain=None, eps=RMS_EPS):
    y = x * lax.rsqrt(jnp.mean(x * x, axis=-1, keepdims=True) + eps)
    return y if gain is None else y * gain


def _rope_tables(n_lat):
    nf = HEAD_DIM // 4
    inv = ROPE_THETA ** (-jnp.arange(nf, dtype=_F32) / nf)
    rows = n_lat // GRID_W
    r_idx = jnp.repeat(jnp.arange(rows, dtype=_F32), GRID_W)
    c_idx = jnp.tile(jnp.arange(GRID_W, dtype=_F32), rows)
    ang = jnp.concatenate([r_idx[:, None] * inv, c_idx[:, None] * inv], axis=-1)
    cos = jnp.concatenate([jnp.ones((CTX_LEN, 2 * nf), _F32), jnp.cos(ang)], axis=0)
    sin = jnp.concatenate([jnp.zeros((CTX_LEN, 2 * nf), _F32), jnp.sin(ang)], axis=0)
    return cos, sin


def _rope(x, cos, sin):
    shp = x.shape
    nf = HEAD_DIM // 4
    n_head_axes = x.ndim - 3
    xr = x.reshape(*shp[:-1], 2, 2, nf)
    cs = cos.reshape(cos.shape[0], *([1] * n_head_axes), 2, nf)
    sn = sin.reshape(sin.shape[0], *([1] * n_head_axes), 2, nf)
    x1, x2 = xr[..., 0, :], xr[..., 1, :]
    return jnp.stack([x1 * cs - x2 * sn, x1 * sn + x2 * cs], axis=-2).reshape(shp)


def _flip_segments(z):
    return jnp.concatenate([z[:, :CTX_LEN][:, ::-1], z[:, CTX_LEN:][:, ::-1]], axis=1)


def _token_shift(u, mu_prev, mu_next):
    t = u.shape[1]
    pos = jnp.arange(t)[None, :, None]
    prev = jnp.where((pos == 0) | (pos == CTX_LEN), 0.0, jnp.roll(u, 1, axis=1))
    nxt = jnp.where((pos == CTX_LEN - 1) | (pos == t - 1), 0.0, jnp.roll(u, -1, axis=1))
    return u + mu_prev * (prev - u) + mu_next * (nxt - u)


def _rwkv(u, v_first, p):
    b, t, _ = u.shape
    n = b * t
    u = _token_shift(u, p['mu_prev'], p['mu_next'])
    r, k, v, wlr, alr, glr = jnp.split(u, B_SPLITS, axis=-1)
    if v_first is None:
        v_first = v
    else:
        lora = _mm_bf16(_mm_bf16(v.reshape(n, -1), p['v1']), p['v2']).reshape(b, t, -1)
        v = v + (v_first - v) * jax.nn.sigmoid(p['v0'] + lora)
    heads = lambda z: z.reshape(b, t, B_HEADS, HEAD_DIM)
    kk = heads(k * p['k_k'])
    kk = kk / jnp.maximum(jnp.linalg.norm(kk, axis=-1, keepdims=True), 1e-12)
    k_a = p['k_a'].reshape(B_HEADS, HEAD_DIM)
    tw = jnp.tanh(wlr).reshape(n, -1)
    alr2 = alr.reshape(n, -1)
    lws, aas, kds = [], [], []
    for dr in range(2):
        lws.append(-DECAY_SCALE * jax.nn.sigmoid(heads(p['w0'][dr] + _mm_bf16(tw, p['w2'][dr]).reshape(b, t, -1))))
        a = jax.nn.sigmoid(heads(p['a0'][dr] + _mm_bf16(alr2, p['a2'][dr]).reshape(b, t, -1)))
        aas.append(a)
        kds.append(heads(k) * (1.0 + (a - 1.0) * k_a))
    g = _mm_bf16(jax.nn.sigmoid(glr).reshape(n, -1), p['g2']).reshape(b, t, -1)
    rh, vh = heads(r), heads(v)

    def pack(z0, z1):
        z = jnp.concatenate([z0, _flip_segments(z1)], axis=-1)
        return jnp.moveaxis(z, 2, 1).reshape(b * B_HEADS, t, LANES)

    y = _delta_scan(pack(rh, rh), pack(vh, vh), pack(kk, kk), pack(*lws), pack(*aas), pack(*kds))
    y = jnp.moveaxis(y.reshape(b, B_HEADS, t, LANES), 1, 2)
    y = y[..., :HEAD_DIM] + _flip_segments(y[..., HEAD_DIM:])
    mu = jnp.mean(y, axis=-1, keepdims=True)
    var = jnp.mean(jnp.square(y - mu), axis=-1, keepdims=True)
    y = (y - mu) * lax.rsqrt(var + RWKV_GN_EPS) * p['gn_w'].reshape(B_HEADS, HEAD_DIM) \
        + p['gn_b'].reshape(B_HEADS, HEAD_DIM)
    for kd in kds:
        y = y + jnp.sum(rh * kd * p['r_k'], axis=-1, keepdims=True) * vh
    return y.reshape(b, t, B_WIDTH) * g, v_first


def _even_mixer(h, v_first, p, cos, sin):
    b, t, d = h.shape
    n = b * t
    proj = _mm_bf16(h.reshape(n, d), p['w_in']).reshape(b, t, -1)
    q = _rms(proj[..., :A_WIDTH].reshape(b, t, A_HEADS, HEAD_DIM), p['q_gain'])
    k = _rms(proj[..., A_WIDTH:A_WIDTH + A_KV_WIDTH].reshape(b, t, A_KV_HEADS, HEAD_DIM), p['k_gain'])
    v = proj[..., A_WIDTH + A_KV_WIDTH:A_COLS].reshape(b, t, A_KV_HEADS, HEAD_DIM)
    q = (_rope(q, cos, sin) * HEAD_DIM ** -0.5).astype(_BF16).reshape(b, t, A_WIDTH)
    k = _rope(k, cos, sin).astype(_BF16)
    v = v.astype(_BF16)
    kk = jnp.concatenate([k[:, :, 0], k[:, :, 0], k[:, :, 1], k[:, :, 1]], axis=-1)
    vv = jnp.concatenate([v[:, :, 0], v[:, :, 0], v[:, :, 1], v[:, :, 1]], axis=-1)
    att = jnp.concatenate([
        _gqa_attend(q, kk, vv, q_start=0, q_len=CTX_LEN, k_len=CTX_LEN),
        _gqa_attend(q, kk, vv, q_start=CTX_LEN, q_len=t - CTX_LEN, k_len=t)], axis=1)
    mix, v_first = _rwkv(proj[..., A_COLS:], v_first, p)
    cat = jnp.concatenate([att, mix.astype(_BF16)], axis=-1)
    return _mm_bf16(cat.reshape(n, -1), p['w_out']).reshape(b, t, d), v_first


def _diff_mixer(h, p, lambda_init, cos, sin, need_ctx):
    b, t, d = h.shape
    n = b * t
    proj = _mm_bf16(h.reshape(n, d), p['w_in']).reshape(b, t, -1)
    q = _rope(proj[..., :C_QK].reshape(b, t, 2 * C_HEADS, HEAD_DIM), cos, sin) * HEAD_DIM ** -0.5
    k = _rope(proj[..., C_QK:2 * C_QK].reshape(b, t, 2 * C_HEADS, HEAD_DIM), cos, sin)
    q = q.astype(_BF16).reshape(b, t, C_QK)
    k = k.astype(_BF16).reshape(b, t, C_QK)
    v = proj[..., 2 * C_QK:].astype(_BF16)
    lam = (jnp.exp(jnp.sum(p['lq1'] * p['lk1'])) - jnp.exp(jnp.sum(p['lq2'] * p['lk2']))
           + lambda_init).reshape(1, 1)
    gain = (p['subln'] * (1.0 - lambda_init)).reshape(1, LANES)
    lat = _diff_attend(lam, q, k, v, gain, q_start=CTX_LEN, q_len=t - CTX_LEN, k_len=t)
    if need_ctx:
        ctx = _diff_attend(lam, q, k, v, gain, q_start=0, q_len=CTX_LEN, k_len=CTX_LEN)
    else:
        ctx = jnp.zeros((b, CTX_LEN, lat.shape[-1]), lat.dtype)
    o = jnp.concatenate([ctx, lat], axis=1)
    return _mm_bf16(o.reshape(n, -1), p['w_out']).reshape(b, t, d)


def _moe(h, w_grp, b_grp, w_rt, b_rt, w_gate, w_up, w_down):
    n, d = h.shape
    w_router = jnp.concatenate(
        [w_grp, w_rt, jnp.zeros((d, LANES - N_GROUPS - N_EXPERTS), _F32)], axis=1)
    logits = _mm(h, w_router, precision=_HI)
    grp_logits = logits[:, :N_GROUPS] + b_grp
    grp_sel = jnp.argmax(grp_logits, axis=-1)
    grp_w = jnp.take_along_axis(jax.nn.softmax(grp_logits, axis=-1), grp_sel[:, None], axis=1)
    exp_logits = (logits[:, N_GROUPS:N_GROUPS + N_EXPERTS] + b_rt).reshape(n, N_GROUPS, EXPERTS_PER_GROUP)
    exp_logits = jnp.take_along_axis(exp_logits, grp_sel[:, None, None], axis=1)[:, 0]
    top_p, top_i = lax.top_k(jax.nn.softmax(exp_logits, axis=-1), TOP_K)
    gates = grp_w * top_p / jnp.sum(top_p, axis=-1, keepdims=True)
    expert_id = (grp_sel[:, None] * EXPERTS_PER_GROUP + top_i).astype(jnp.int32)

    m = n * TOP_K
    e_flat = expert_id.reshape(m)
    tok = jnp.arange(m, dtype=jnp.int32) // TOP_K
    counts = jnp.bincount(e_flat, length=N_EXPERTS).astype(jnp.int32)
    padded = (counts + EXPERT_BLOCK - 1) // EXPERT_BLOCK * EXPERT_BLOCK
    pad_end = jnp.cumsum(padded)
    pad_start = pad_end - padded
    raw_start = jnp.cumsum(counts) - counts
    order = jnp.argsort(e_flat)
    e_sorted = e_flat[order]
    slot_sorted = pad_start[e_sorted] + jnp.arange(m, dtype=jnp.int32) - raw_start[e_sorted]
    n_blocks = -(-(m + N_EXPERTS * (EXPERT_BLOCK - 1)) // EXPERT_BLOCK)
    slot_tok = jnp.full((n_blocks * EXPERT_BLOCK,), n, jnp.int32).at[slot_sorted].set(tok[order])
    block_start = jnp.arange(n_blocks, dtype=jnp.int32) * EXPERT_BLOCK
    block_expert = jnp.minimum(jnp.searchsorted(pad_end, block_start, side='right'),
                               N_EXPERTS - 1).astype(jnp.int32)
    n_valid = (pad_end[-1:] // EXPERT_BLOCK).astype(jnp.int32)
    h_pad = jnp.concatenate([h.astype(_BF16), jnp.zeros((1, d), _BF16)], axis=0)
    xs = h_pad[slot_tok]
    yb = _moe_blocks(block_expert, n_valid, xs, w_gate.astype(_BF16), w_up.astype(_BF16),
                     w_down.astype(_BF16))
    slot = jnp.zeros((m,), jnp.int32).at[order].set(slot_sorted)
    y = yb[slot].reshape(n, TOP_K, d)
    return jnp.sum(gates[:, :, None] * y, axis=1)


def kernel(x, c, ctx, c_ctx, mod_w, mod_b, ev_w_in, ev_w_out, ev_q_gain, ev_k_gain, ev_mu_prev, ev_mu_next, ev_w0, ev_w2, ev_a0, ev_a2, ev_g2, ev_k_k, ev_k_a, ev_r_k, ev_gn_w, ev_gn_b, ev_v0, ev_v1, ev_v2, od_w_in, od_w_out, od_lq1, od_lk1, od_lq2, od_lk2, od_subln, moe_w_grp, moe_b_grp, moe_w_rt, moe_b_rt, moe_w_gate, moe_w_up, moe_w_down, final_gain):
    bsz, n_lat, d = x.shape
    t = CTX_LEN + n_lat
    cos, sin = _rope_tables(n_lat)
    is_ctx = (jnp.arange(t) < CTX_LEN)[None, :, None]
    cond = jax.nn.silu(jnp.concatenate([c, c_ctx[None, :], jnp.zeros((16 - bsz - 1, d), _F32)], axis=0))
    xs = jnp.concatenate([ctx, x], axis=1)
    v_first = None
    for layer in range(DEPTH):
        last = layer == DEPTH - 1
        mods = _mm(cond, mod_w[layer], tn=1536, precision=_HI) + mod_b[layer]
        mods = [jnp.where(is_ctx, mc[None, None, :], ml[:, None, :])
                for ml, mc in zip(jnp.split(mods[:bsz], 6, axis=-1), jnp.split(mods[bsz], 6, axis=-1))]
        h = _rms(xs) * (1.0 + mods[1]) + mods[0]
        if layer % 2 == 0:
            e = layer // 2
            p = {'w_in': ev_w_in[e], 'w_out': ev_w_out[e], 'q_gain': ev_q_gain[e],
                 'k_gain': ev_k_gain[e], 'mu_prev': ev_mu_prev[e], 'mu_next': ev_mu_next[e],
                 'w0': ev_w0[e], 'w2': ev_w2[e], 'a0': ev_a0[e], 'a2': ev_a2[e], 'g2': ev_g2[e],
                 'k_k': ev_k_k[e], 'k_a': ev_k_a[e], 'r_k': ev_r_k[e], 'gn_w': ev_gn_w[e],
                 'gn_b': ev_gn_b[e]}
            if e > 0:
                p['v0'], p['v1'], p['v2'] = ev_v0[e - 1], ev_v1[e - 1], ev_v2[e - 1]
            out, v_first = _even_mixer(h, v_first, p, cos, sin)
        else:
            o = layer // 2
            p = {'w_in': od_w_in[o], 'w_out': od_w_out[o], 'lq1': od_lq1[o], 'lk1': od_lk1[o],
                 'lq2': od_lq2[o], 'lk2': od_lk2[o], 'subln': od_subln[o]}
            lambda_init = 0.8 - 0.6 * math.exp(-0.3 * layer)
            out = _diff_mixer(h, p, lambda_init, cos, sin, not last)
        xs = xs + mods[2] * out
        h = _rms(xs) * (1.0 + mods[4]) + mods[3]
        y = _moe(h.reshape(-1, d), moe_w_grp[layer], moe_b_grp[layer], moe_w_rt[layer],
                 moe_b_rt[layer], moe_w_gate[layer], moe_w_up[layer], moe_w_down[layer])
        xs = xs + mods[5] * y.reshape(xs.shape)
    return _rms(xs[:, CTX_LEN:], final_gain)
```

```python
import functools
import math

import jax
import jax.numpy as jnp
from jax import lax
from jax.experimental import pallas as pl
from jax.experimental.pallas import tpu as pltpu

D_MODEL = 1024
DEPTH = 4
GRID_W = 64
CTX_LEN = 256
HEAD_DIM = 64
ROPE_THETA = 10000.0
RMS_EPS = 1e-6
SUBLN_EPS = 1e-5

A_HEADS = 8
A_KV_HEADS = 2
A_WIDTH = 512
A_KV_WIDTH = 128
A_COLS = 768

B_HEADS = 8
B_WIDTH = 512
RWKV_W_LORA = 64
RWKV_A_LORA = 64
RWKV_G_LORA = 128
RWKV_GN_EPS = 64e-5
DECAY_SCALE = math.exp(-0.5)
B_SPLITS = (512, 1024, 1536, 1600, 1664)

C_HEADS = 8
C_QK = 1024

N_GROUPS = 4
EXPERTS_PER_GROUP = 8
N_EXPERTS = 32
TOP_K = 2
EXPERT_HIDDEN = 512
EXPERT_BLOCK = 128

LANES = 128
SCAN_CHUNK = 64
ROW_TILE = 256
Q_TILE = 128
VMEM_LIMIT = 56 * 1024 * 1024

_HI = lax.Precision.HIGHEST
_F32 = jnp.float32
_BF16 = jnp.bfloat16


def _mm_kernel(a_ref, w_ref, o_ref, *, precision):
    o_ref[...] = jnp.dot(a_ref[...], w_ref[...], precision=precision,
                         preferred_element_type=_F32).astype(o_ref.dtype)


def _mm(a, w, *, out_dtype=_F32, tm=ROW_TILE, tn=None, precision=None):
    m, k = a.shape
    n = w.shape[1]
    tn = n if tn is None else tn
    tm = min(tm, m)
    return pl.pallas_call(
        functools.partial(_mm_kernel, precision=precision),
        grid=(m // tm, n // tn),
        in_specs=[pl.BlockSpec((tm, k), lambda i, j: (i, 0)),
                  pl.BlockSpec((k, tn), lambda i, j: (0, j))],
        out_specs=pl.BlockSpec((tm, tn), lambda i, j: (i, j)),
        out_shape=jax.ShapeDtypeStruct((m, n), out_dtype),
        name=f"mm_{k}x{n}",
        compiler_params=pltpu.CompilerParams(
            dimension_semantics=("arbitrary", "arbitrary"), vmem_limit_bytes=VMEM_LIMIT),
    )(a, w)


def _mm_bf16(a, w, **kw):
    return _mm(a.astype(_BF16), w.astype(_BF16), **kw)


def _lane_lo(shape):
    return lax.broadcasted_iota(jnp.int32, shape, len(shape) - 1) < HEAD_DIM


def _split_heads(q2):
    lo = _lane_lo(q2.shape)
    zero = jnp.zeros_like(q2)
    return jnp.concatenate([jnp.where(lo, q2, zero), jnp.where(lo, zero, q2)], axis=0)


def _scores(qs, k):
    return lax.dot_general(qs, k, (((1,), (1,)), ((), ())), preferred_element_type=_F32)


def _gqa_kernel(q_ref, k_ref, v_ref, o_ref):
    q = q_ref[0]
    tq = q.shape[0]
    qs = jnp.concatenate([_split_heads(q[:, :LANES]), _split_heads(q[:, LANES:])], axis=0)
    s = _scores(qs, k_ref[0])
    p = jnp.exp(s - jnp.max(s, axis=-1, keepdims=True))
    l = jnp.sum(p, axis=-1, keepdims=True)
    o = jnp.dot(p.astype(_BF16), v_ref[0], preferred_element_type=_F32) / l
    lo = _lane_lo((tq, LANES))
    o_ref[0] = jnp.concatenate([jnp.where(lo, o[:tq], o[tq:2 * tq]),
                                jnp.where(lo, o[2 * tq:3 * tq], o[3 * tq:])], axis=1).astype(o_ref.dtype)


def _gqa_attend(q, kk, vv, *, q_start, q_len, k_len):
    b = q.shape[0]
    tq = Q_TILE
    off = q_start // tq
    return pl.pallas_call(
        _gqa_kernel,
        grid=(b, A_KV_HEADS, q_len // tq),
        in_specs=[pl.BlockSpec((1, tq, 2 * LANES), lambda i, g, j: (i, j + off, g)),
                  pl.BlockSpec((1, k_len, LANES), lambda i, g, j: (i, 0, g)),
                  pl.BlockSpec((1, k_len, LANES), lambda i, g, j: (i, 0, g))],
        out_specs=pl.BlockSpec((1, tq, 2 * LANES), lambda i, g, j: (i, j, g)),
        out_shape=jax.ShapeDtypeStruct((b, q_len, A_WIDTH), _BF16),
        name=f"gqa_attn_k{k_len}",
        compiler_params=pltpu.CompilerParams(
            dimension_semantics=("arbitrary",) * 3, vmem_limit_bytes=VMEM_LIMIT),
    )(q, kk, vv)


def _diff_kernel(lam_ref, q_ref, k_ref, v_ref, g_ref, o_ref):
    q = q_ref[0]
    tq = q.shape[0]
    s = _scores(_split_heads(q), k_ref[0])
    p = jnp.exp(s - jnp.max(s, axis=-1, keepdims=True))
    l = jnp.sum(p, axis=-1, keepdims=True)
    l1, l2 = l[:tq], l[tq:]
    pr = p[:tq] - (lam_ref[0, 0] * l1 / l2) * p[tq:]
    o = jnp.dot(pr.astype(_BF16), v_ref[0], preferred_element_type=_F32) / l1
    y = o * lax.rsqrt(jnp.mean(o * o, axis=-1, keepdims=True) + SUBLN_EPS)
    o_ref[0] = (y * g_ref[...]).astype(o_ref.dtype)


def _diff_attend(lam, q, k, v, gain, *, q_start, q_len, k_len):
    b = q.shape[0]
    tq = Q_TILE
    off = q_start // tq
    return pl.pallas_call(
        _diff_kernel,
        grid=(b, C_HEADS, q_len // tq),
        in_specs=[pl.BlockSpec(memory_space=pltpu.SMEM),
                  pl.BlockSpec((1, tq, LANES), lambda i, h, j: (i, j + off, h)),
                  pl.BlockSpec((1, k_len, LANES), lambda i, h, j: (i, 0, h)),
                  pl.BlockSpec((1, k_len, LANES), lambda i, h, j: (i, 0, h)),
                  pl.BlockSpec((1, LANES), lambda i, h, j: (0, 0))],
        out_specs=pl.BlockSpec((1, tq, LANES), lambda i, h, j: (i, j, h)),
        out_shape=jax.ShapeDtypeStruct((b, q_len, C_HEADS * LANES), _BF16),
        name=f"diff_attn_k{k_len}",
        compiler_params=pltpu.CompilerParams(
            dimension_semantics=("arbitrary",) * 3, vmem_limit_bytes=VMEM_LIMIT),
    )(lam, q, k, v, gain)


_NN = (((1,), (0,)), ((), ()))
_NT = (((1,), (1,)), ((), ()))
_TN = (((0,), (0,)), ((), ()))


def _split2(x):
    hi = x.astype(_BF16)
    return hi, (x - hi.astype(_F32)).astype(_BF16)


def _dot3(a, b, dims=_NN):
    d = lambda u, w: lax.dot_general(u, w, dims, preferred_element_type=_F32)
    return (d(a[1], b[0]) + d(a[0], b[1])) + d(a[0], b[0])


def _scan_chunk(h, r, v, kk, lw, cum, cum_end, a, kd, sgn):
    c = SCAN_CHUNK
    lo = _lane_lo((c, LANES))
    row = lax.broadcasted_iota(jnp.int32, (LANES, LANES), 0)
    col = lax.broadcasted_iota(jnp.int32, (LANES, LANES), 1)
    delta = ((col & (c - 1)) - (row & (c - 1))) * sgn
    before = delta < 0
    upto = delta <= 0
    eye = row == col
    zero = jnp.zeros((LANES, LANES), _F32)

    def blockdiag(x):
        z = jnp.zeros_like(x)
        return jnp.concatenate([jnp.where(lo, x, z), jnp.where(lo, z, x)], axis=0)

    p_inv = jnp.exp(-cum)
    p_end = jnp.exp(cum_end - cum)
    kka = kk * a
    a_rs = blockdiag(-kk * jnp.exp(cum - lw))
    r_rs = blockdiag(r * jnp.exp(cum))
    b_rs = blockdiag(kka * p_inv)
    k_rs = blockdiag(kd * p_inv)
    v_rs = blockdiag(v)
    bp_rs = blockdiag(kka * p_end)
    kp_rs = blockdiag(kd * p_end)

    z = _dot3(_split2(jnp.concatenate([a_rs, r_rs], axis=0)),
              _split2(jnp.concatenate([b_rs, k_rs], axis=0)), _NT)
    yield
    low = jnp.where(before, z[:LANES, :LANES], zero)
    g = jnp.where(before, z[:LANES, LANES:], zero)
    rb = jnp.where(upto, z[LANES:, :LANES], zero)
    rk = jnp.where(upto, z[LANES:, LANES:], zero)

    xs = _split2(low)
    x = _dot3(xs, xs)
    yield
    m = jnp.where(eye, 1.0, 0.0).astype(_F32) + low
    n = 2
    while 2 * n < c:
        res = _dot3(_split2(x), _split2(jnp.concatenate([m, x], axis=1)))
        m = m + res[:, :LANES]
        x = res[:, LANES:]
        n *= 2
        yield
    m = m + _dot3(_split2(x), _split2(m))
    yield
    gv = _dot3(_split2(g), _split2(v_rs))
    yield
    au = _dot3(_split2(m), _split2(jnp.concatenate([a_rs, gv], axis=1)))
    yield
    rhs = _split2(jnp.concatenate(
        [au, jnp.concatenate([zero, v_rs], axis=1)], axis=0))
    tf = _dot3(_split2(jnp.concatenate([bp_rs, kp_rs], axis=0)), rhs, _TN)
    qy = _dot3(_split2(jnp.concatenate([rb, rk], axis=1)), rhs)
    yield
    t = jnp.where(eye, jnp.broadcast_to(jnp.exp(cum_end), (LANES, LANES)), zero) + tf[:, :LANES]
    q = r_rs + qy[:, :LANES]
    out = _dot3(_split2(jnp.concatenate([q, t], axis=0)), _split2(h))
    y = out[:LANES] + qy[:, LANES:]
    h_new = out[LANES:] + tf[:, LANES:]
    return y[:c] + y[c:], h_new


def _scan_kernel(r_ref, v_ref, kk_ref, lw_ref, a_ref, kd_ref, y_ref, h_ref):
    c = SCAN_CHUNK

    @pl.when(pl.program_id(2) == 0)
    def _():
        h_ref[...] = jnp.zeros_like(h_ref)

    sgn = 1 - 2 * pl.program_id(0)
    tr = lax.broadcasted_iota(jnp.int32, (c, c), 0)
    tc = lax.broadcasted_iota(jnp.int32, (c, c), 1)
    tri = jnp.where((tc - tr) * sgn <= 0, 1.0, 0.0).astype(_BF16)
    lw = lw_ref[0, 0]
    l1 = lw.astype(_BF16)
    rem = lw - l1.astype(_F32)
    l2 = rem.astype(_BF16)
    l3 = (rem - l2.astype(_F32)).astype(_BF16)
    d = lambda w: jnp.dot(tri, w, preferred_element_type=_F32)
    cum = (d(l3) + d(l2)) + d(l1)
    cum_end = jnp.sum(lw, axis=0, keepdims=True)
    slices = [slice(p * LANES, (p + 1) * LANES) for p in range(lw.shape[1] // LANES)]
    chains = [_scan_chunk(h_ref[p], r_ref[0, :, sl], v_ref[0, :, sl], kk_ref[0, :, sl], lw[:, sl],
                          cum[:, sl], cum_end[:, sl], a_ref[0, 0, :, sl], kd_ref[0, 0, :, sl], sgn)
              for p, sl in enumerate(slices)]
    results = {}
    while len(results) < len(chains):
        for p, chain in enumerate(chains):
            try:
                next(chain)
            except StopIteration as stop:
                results[p] = stop.value
    for p, sl in enumerate(slices):
        y_ref[0, 0, :, sl], h_ref[p] = results[p]


def _delta_scan(r, v, kk, lw, a, kd):
    b, t, w = r.shape
    c = SCAN_CHUNK
    n_chunks = t // c
    n_ctx = CTX_LEN // c

    def chunk(d, j):
        rev = jnp.where(j < n_ctx, n_ctx - 1 - j, n_chunks + n_ctx - 1 - j)
        return jnp.where(d == 0, j, rev)

    shared = pl.BlockSpec((1, c, w), lambda d, i, j: (i, chunk(d, j), 0))
    per_dir = pl.BlockSpec((1, 1, c, w), lambda d, i, j: (d, i, chunk(d, j), 0))
    return pl.pallas_call(
        _scan_kernel,
        grid=(2, b, n_chunks),
        in_specs=[shared, shared, shared, per_dir, per_dir, per_dir],
        out_specs=per_dir,
        out_shape=jax.ShapeDtypeStruct((2, b, t, w), _F32),
        scratch_shapes=[pltpu.VMEM((w // LANES, LANES, LANES), _F32)],
        name="delta_scan",
        compiler_params=pltpu.CompilerParams(
            dimension_semantics=("arbitrary",) * 3, vmem_limit_bytes=VMEM_LIMIT),
    )(r, v, kk, lw, a, kd)


def _moe_kernel(be_ref, nv_ref, x_ref, wg_ref, wu_ref, wd_ref, o_ref):
    del be_ref

    @pl.when(pl.program_id(0) < nv_ref[0])
    def _():
        x = x_ref[...]
        gate = jnp.dot(x, wg_ref[0], preferred_element_type=_F32)
        up = jnp.dot(x, wu_ref[0], preferred_element_type=_F32)
        hid = (gate * jax.nn.sigmoid(gate) * up).astype(_BF16)
        o_ref[...] = jnp.dot(hid, wd_ref[0], preferred_element_type=_F32)

    @pl.when(pl.program_id(0) >= nv_ref[0])
    def _():
        o_ref[...] = jnp.zeros_like(o_ref)


def _moe_blocks(block_expert, n_valid, xs, wg, wu, wd):
    rows, d = xs.shape
    nb = rows // EXPERT_BLOCK
    grid_spec = pltpu.PrefetchScalarGridSpec(
        num_scalar_prefetch=2,
        grid=(nb,),
        in_specs=[pl.BlockSpec((EXPERT_BLOCK, d), lambda i, be, nv: (i, 0)),
                  pl.BlockSpec((1, d, EXPERT_HIDDEN), lambda i, be, nv: (be[i], 0, 0)),
                  pl.BlockSpec((1, d, EXPERT_HIDDEN), lambda i, be, nv: (be[i], 0, 0)),
                  pl.BlockSpec((1, EXPERT_HIDDEN, d), lambda i, be, nv: (be[i], 0, 0))],
        out_specs=pl.BlockSpec((EXPERT_BLOCK, d), lambda i, be, nv: (i, 0)),
    )
    return pl.pallas_call(
        _moe_kernel,
        grid_spec=grid_spec,
        out_shape=jax.ShapeDtypeStruct((rows, d), _F32),
        name="moe_ffn",
        compiler_params=pltpu.CompilerParams(
            dimension_semantics=("arbitrary",), vmem_limit_bytes=VMEM_LIMIT),
    )(block_expert, n_valid, xs, wg, wu, wd)


def _rms(x, gain=None, eps=RMS_EPS):
    y = x * lax.rsqrt(jnp.mean(x * x, axis=-1, keepdims=True) + eps)
    return y if gain is None else y * gain


def _rope_tables(n_lat):
    nf = HEAD_DIM // 4
    inv = ROPE_THETA ** (-jnp.arange(nf, dtype=_F32) / nf)
    rows = n_lat // GRID_W
    r_idx = jnp.repeat(jnp.arange(rows, dtype=_F32), GRID_W)
    c_idx = jnp.tile(jnp.arange(GRID_W, dtype=_F32), rows)
    ang = jnp.concatenate([r_idx[:, None] * inv, c_idx[:, None] * inv], axis=-1)
    cos = jnp.concatenate([jnp.ones((CTX_LEN, 2 * nf), _F32), jnp.cos(ang)], axis=0)
    sin = jnp.concatenate([jnp.zeros((CTX_LEN, 2 * nf), _F32), jnp.sin(ang)], axis=0)
    return cos, sin


def _rope(x, cos, sin):
    shp = x.shape
    nf = HEAD_DIM // 4
    n_head_axes = x.ndim - 3
    xr = x.reshape(*shp[:-1], 2, 2, nf)
    cs = cos.reshape(cos.shape[0], *([1] * n_head_axes), 2, nf)
    sn = sin.reshape(sin.shape[0], *([1] * n_head_axes), 2, nf)
    x1, x2 = xr[..., 0, :], xr[..., 1, :]
    return jnp.stack([x1 * cs - x2 * sn, x1 * sn + x2 * cs], axis=-2).reshape(shp)


def _token_shift(u, mu_prev, mu_next):
    t = u.shape[1]
    pos = jnp.arange(t)[None, :, None]
    prev = jnp.where((pos == 0) | (pos == CTX_LEN), 0.0, jnp.roll(u, 1, axis=1))
    nxt = jnp.where((pos == CTX_LEN - 1) | (pos == t - 1), 0.0, jnp.roll(u, -1, axis=1))
    return u + mu_prev * (prev - u) + mu_next * (nxt - u)


def _rwkv(u, v_first, p):
    b, t, _ = u.shape
    n = b * t
    u = _token_shift(u, p['mu_prev'], p['mu_next'])
    r, k, v, wlr, alr, glr = jnp.split(u, B_SPLITS, axis=-1)
    if v_first is None:
        v_first = v
    else:
        lora = _mm_bf16(_mm_bf16(v.reshape(n, -1), p['v1']), p['v2']).reshape(b, t, -1)
        v = v + (v_first - v) * jax.nn.sigmoid(p['v0'] + lora)
    heads = lambda z: z.reshape(b, t, B_HEADS, HEAD_DIM)
    kk = heads(k * p['k_k'])
    kk = kk / jnp.maximum(jnp.linalg.norm(kk, axis=-1, keepdims=True), 1e-12)
    k_a = p['k_a'].reshape(B_HEADS, HEAD_DIM)
    tw = jnp.tanh(wlr).reshape(n, -1)
    alr2 = alr.reshape(n, -1)
    lws, aas, kds = [], [], []
    for dr in range(2):
        lws.append(-DECAY_SCALE * jax.nn.sigmoid(heads(p['w0'][dr] + _mm_bf16(tw, p['w2'][dr]).reshape(b, t, -1))))
        a = jax.nn.sigmoid(heads(p['a0'][dr] + _mm_bf16(alr2, p['a2'][dr]).reshape(b, t, -1)))
        aas.append(a)
        kds.append(heads(k) * (1.0 + (a - 1.0) * k_a))
    g = _mm_bf16(jax.nn.sigmoid(glr).reshape(n, -1), p['g2']).reshape(b, t, -1)
    rh, vh = heads(r), heads(v)
    flat = lambda zs: jnp.stack([z.reshape(b, t, B_WIDTH) for z in zs])
    y = _delta_scan(r, v, kk.reshape(b, t, B_WIDTH), flat(lws), flat(aas), flat(kds))
    y = heads(y[0] + y[1])
    mu = jnp.mean(y, axis=-1, keepdims=True)
    var = jnp.mean(jnp.square(y - mu), axis=-1, keepdims=True)
    y = (y - mu) * lax.rsqrt(var + RWKV_GN_EPS) * p['gn_w'].reshape(B_HEADS, HEAD_DIM) \
        + p['gn_b'].reshape(B_HEADS, HEAD_DIM)
    for kd in kds:
        y = y + jnp.sum(rh * kd * p['r_k'], axis=-1, keepdims=True) * vh
    return y.reshape(b, t, B_WIDTH) * g, v_first


def _even_mixer(h, v_first, p, cos, sin):
    b, t, d = h.shape
    n = b * t
    proj = _mm_bf16(h.reshape(n, d), p['w_in']).reshape(b, t, -1)
    q = _rms(proj[..., :A_WIDTH].reshape(b, t, A_HEADS, HEAD_DIM), p['q_gain'])
    k = _rms(proj[..., A_WIDTH:A_WIDTH + A_KV_WIDTH].reshape(b, t, A_KV_HEADS, HEAD_DIM), p['k_gain'])
    v = proj[..., A_WIDTH + A_KV_WIDTH:A_COLS].reshape(b, t, A_KV_HEADS, HEAD_DIM)
    q = (_rope(q, cos, sin) * HEAD_DIM ** -0.5).astype(_BF16).reshape(b, t, A_WIDTH)
    k = _rope(k, cos, sin).astype(_BF16)
    v = v.astype(_BF16)
    kk = jnp.concatenate([k[:, :, 0], k[:, :, 0], k[:, :, 1], k[:, :, 1]], axis=-1)
    vv = jnp.concatenate([v[:, :, 0], v[:, :, 0], v[:, :, 1], v[:, :, 1]], axis=-1)
    att = jnp.concatenate([
        _gqa_attend(q, kk, vv, q_start=0, q_len=CTX_LEN, k_len=CTX_LEN),
        _gqa_attend(q, kk, vv, q_start=CTX_LEN, q_len=t - CTX_LEN, k_len=t)], axis=1)
    mix, v_first = _rwkv(proj[..., A_COLS:], v_first, p)
    cat = jnp.concatenate([att, mix.astype(_BF16)], axis=-1)
    return _mm_bf16(cat.reshape(n, -1), p['w_out']).reshape(b, t, d), v_first


def _diff_mixer(h, p, lambda_init, cos, sin, need_ctx):
    b, t, d = h.shape
    n = b * t
    proj = _mm_bf16(h.reshape(n, d), p['w_in']).reshape(b, t, -1)
    q = _rope(proj[..., :C_QK].reshape(b, t, 2 * C_HEADS, HEAD_DIM), cos, sin) * HEAD_DIM ** -0.5
    k = _rope(proj[..., C_QK:2 * C_QK].reshape(b, t, 2 * C_HEADS, HEAD_DIM), cos, sin)
    q = q.astype(_BF16).reshape(b, t, C_QK)
    k = k.astype(_BF16).reshape(b, t, C_QK)
    v = proj[..., 2 * C_QK:].astype(_BF16)
    lam = (jnp.exp(jnp.sum(p['lq1'] * p['lk1'])) - jnp.exp(jnp.sum(p['lq2'] * p['lk2']))
           + lambda_init).reshape(1, 1)
    gain = (p['subln'] * (1.0 - lambda_init)).reshape(1, LANES)
    lat = _diff_attend(lam, q, k, v, gain, q_start=CTX_LEN, q_len=t - CTX_LEN, k_len=t)
    if need_ctx:
        ctx = _diff_attend(lam, q, k, v, gain, q_start=0, q_len=CTX_LEN, k_len=CTX_LEN)
    else:
        ctx = jnp.zeros((b, CTX_LEN, lat.shape[-1]), lat.dtype)
    o = jnp.concatenate([ctx, lat], axis=1)
    return _mm_bf16(o.reshape(n, -1), p['w_out']).reshape(b, t, d)


def _moe(h, w_grp, b_grp, w_rt, b_rt, w_gate, w_up, w_down):
    n, d = h.shape
    w_router = jnp.concatenate(
        [w_grp, w_rt, jnp.zeros((d, LANES - N_GROUPS - N_EXPERTS), _F32)], axis=1)
    logits = _mm(h, w_router, precision=_HI)
    grp_logits = logits[:, :N_GROUPS] + b_grp
    grp_sel = jnp.argmax(grp_logits, axis=-1)
    grp_w = jnp.take_along_axis(jax.nn.softmax(grp_logits, axis=-1), grp_sel[:, None], axis=1)
    exp_logits = (logits[:, N_GROUPS:N_GROUPS + N_EXPERTS] + b_rt).reshape(n, N_GROUPS, EXPERTS_PER_GROUP)
    exp_logits = jnp.take_along_axis(exp_logits, grp_sel[:, None, None], axis=1)[:, 0]
    top_p, top_i = lax.top_k(jax.nn.softmax(exp_logits, axis=-1), TOP_K)
    gates = grp_w * top_p / jnp.sum(top_p, axis=-1, keepdims=True)
    expert_id = (grp_sel[:, None] * EXPERTS_PER_GROUP + top_i).astype(jnp.int32)

    m = n * TOP_K
    e_flat = expert_id.reshape(m)
    tok = jnp.arange(m, dtype=jnp.int32) // TOP_K
    counts = jnp.bincount(e_flat, length=N_EXPERTS).astype(jnp.int32)
    padded = (counts + EXPERT_BLOCK - 1) // EXPERT_BLOCK * EXPERT_BLOCK
    pad_end = jnp.cumsum(padded)
    pad_start = pad_end - padded
    raw_start = jnp.cumsum(counts) - counts
    order = jnp.argsort(e_flat)
    e_sorted = e_flat[order]
    slot_sorted = pad_start[e_sorted] + jnp.arange(m, dtype=jnp.int32) - raw_start[e_sorted]
    n_blocks = -(-(m + N_EXPERTS * (EXPERT_BLOCK - 1)) // EXPERT_BLOCK)
    slot_tok = jnp.full((n_blocks * EXPERT_BLOCK,), n, jnp.int32).at[slot_sorted].set(tok[order])
    block_start = jnp.arange(n_blocks, dtype=jnp.int32) * EXPERT_BLOCK
    block_expert = jnp.minimum(jnp.searchsorted(pad_end, block_start, side='right'),
                               N_EXPERTS - 1).astype(jnp.int32)
    n_valid = (pad_end[-1:] // EXPERT_BLOCK).astype(jnp.int32)
    h_pad = jnp.concatenate([h.astype(_BF16), jnp.zeros((1, d), _BF16)], axis=0)
    xs = h_pad[slot_tok]
    yb = _moe_blocks(block_expert, n_valid, xs, w_gate.astype(_BF16), w_up.astype(_BF16),
                     w_down.astype(_BF16))
    slot = jnp.zeros((m,), jnp.int32).at[order].set(slot_sorted)
    y = yb[slot].reshape(n, TOP_K, d)
    return jnp.sum(gates[:, :, None] * y, axis=1)


def kernel(x, c, ctx, c_ctx, mod_w, mod_b, ev_w_in, ev_w_out, ev_q_gain, ev_k_gain, ev_mu_prev, ev_mu_next, ev_w0, ev_w2, ev_a0, ev_a2, ev_g2, ev_k_k, ev_k_a, ev_r_k, ev_gn_w, ev_gn_b, ev_v0, ev_v1, ev_v2, od_w_in, od_w_out, od_lq1, od_lk1, od_lq2, od_lk2, od_subln, moe_w_grp, moe_b_grp, moe_w_rt, moe_b_rt, moe_w_gate, moe_w_up, moe_w_down, final_gain):
    bsz, n_lat, d = x.shape
    t = CTX_LEN + n_lat
    cos, sin = _rope_tables(n_lat)
    is_ctx = (jnp.arange(t) < CTX_LEN)[None, :, None]
    cond = jax.nn.silu(jnp.concatenate([c, c_ctx[None, :], jnp.zeros((16 - bsz - 1, d), _F32)], axis=0))
    xs = jnp.concatenate([ctx, x], axis=1)
    v_first = None
    for layer in range(DEPTH):
        last = layer == DEPTH - 1
        mods = _mm(cond, mod_w[layer], tn=1536, precision=_HI) + mod_b[layer]
        mods = [jnp.where(is_ctx, mc[None, None, :], ml[:, None, :])
                for ml, mc in zip(jnp.split(mods[:bsz], 6, axis=-1), jnp.split(mods[bsz], 6, axis=-1))]
        h = _rms(xs) * (1.0 + mods[1]) + mods[0]
        if layer % 2 == 0:
            e = layer // 2
            p = {'w_in': ev_w_in[e], 'w_out': ev_w_out[e], 'q_gain': ev_q_gain[e],
                 'k_gain': ev_k_gain[e], 'mu_prev': ev_mu_prev[e], 'mu_next': ev_mu_next[e],
                 'w0': ev_w0[e], 'w2': ev_w2[e], 'a0': ev_a0[e], 'a2': ev_a2[e], 'g2': ev_g2[e],
                 'k_k': ev_k_k[e], 'k_a': ev_k_a[e], 'r_k': ev_r_k[e], 'gn_w': ev_gn_w[e],
                 'gn_b': ev_gn_b[e]}
            if e > 0:
                p['v0'], p['v1'], p['v2'] = ev_v0[e - 1], ev_v1[e - 1], ev_v2[e - 1]
            out, v_first = _even_mixer(h, v_first, p, cos, sin)
        else:
            o = layer // 2
            p = {'w_in': od_w_in[o], 'w_out': od_w_out[o], 'lq1': od_lq1[o], 'lk1': od_lk1[o],
                 'lq2': od_lq2[o], 'lk2': od_lk2[o], 'subln': od_subln[o]}
            lambda_init = 0.8 - 0.6 * math.exp(-0.3 * layer)
            out = _diff_mixer(h, p, lambda_init, cos, sin, not last)
        xs = xs + mods[2] * out
        h = _rms(xs) * (1.0 + mods[4]) + mods[3]
        y = _moe(h.reshape(-1, d), moe_w_grp[layer], moe_b_grp[layer], moe_w_rt[layer],
                 moe_b_rt[layer], moe_w_gate[layer], moe_w_up[layer], moe_w_down[layer])
        xs = xs + mods[5] * y.reshape(xs.shape)
    return _rms(xs[:, CTX_LEN:], final_gain)
```

```python
import functools
import math

import jax
import jax.numpy as jnp
from jax import lax
from jax.experimental import pallas as pl
from jax.experimental.pallas import tpu as pltpu

D_MODEL = 1024
DEPTH = 4
SEQ = 2048
GRID_W = 64
CTX_LEN = 256
HEAD_DIM = 64
ROPE_THETA = 10000.0
RMS_EPS = 1e-6
SUBLN_EPS = 1e-5

A_HEADS = 8
A_KV_HEADS = 2
A_WIDTH = 512
A_KV_WIDTH = 128
A_COLS = 768

B_HEADS = 8
B_WIDTH = 512
B_COLS = 1792
RWKV_W_LORA = 64
RWKV_A_LORA = 64
RWKV_V_LORA = 32
RWKV_G_LORA = 128
RWKV_GN_EPS = 64e-5
DECAY_SCALE = math.exp(-0.5)

C_HEADS = 8
C_QK = 1024

N_GROUPS = 4
EXPERTS_PER_GROUP = 8
N_EXPERTS = 32
TOP_K = 2
EXPERT_HIDDEN = 512
EXPERT_BLOCK = 128

LANES = 128
SCAN_CHUNK = 64
ROW_TILE = 256
Q_TILE = 128
VMEM_LIMIT = 56 * 1024 * 1024

_HI = lax.Precision.HIGHEST
_F32 = jnp.float32
_BF16 = jnp.bfloat16

_NN = (((1,), (0,)), ((), ()))
_NT = (((1,), (1,)), ((), ()))
_TN = (((0,), (0,)), ((), ()))


def _params(n_axes):
    return pltpu.CompilerParams(dimension_semantics=("arbitrary",) * n_axes,
                                vmem_limit_bytes=VMEM_LIMIT)


def _full(shape):
    return pl.BlockSpec(shape, lambda *_: (0,) * len(shape))


def _split2(x):
    hi = x.astype(_BF16)
    return hi, (x - hi.astype(_F32)).astype(_BF16)


def _dot3(a, b, dims=_NN):
    d = lambda u, w: lax.dot_general(u, w, dims, preferred_element_type=_F32)
    return (d(a[1], b[0]) + d(a[0], b[1])) + d(a[0], b[0])


def _lane_lo(shape):
    return lax.broadcasted_iota(jnp.int32, shape, len(shape) - 1) < HEAD_DIM


def _head_sum_matrix(value):
    row = lax.broadcasted_iota(jnp.int32, (LANES, LANES), 0)
    col = lax.broadcasted_iota(jnp.int32, (LANES, LANES), 1)
    return jnp.where((row // HEAD_DIM) == (col // HEAD_DIM), value, 0.0).astype(_BF16)


def _head_sum(x, mat):
    hi, lo = _split2(x)
    return jnp.dot(lo, mat, preferred_element_type=_F32) + jnp.dot(hi, mat, preferred_element_type=_F32)


def _rms_rows(x, eps=RMS_EPS):
    return x * lax.rsqrt(jnp.mean(x * x, axis=-1, keepdims=True) + eps)


def _rope_slab(x, cos, sin):
    lane = lax.broadcasted_iota(jnp.int32, x.shape, 1)
    partner = jnp.where((lane & 16) == 0, pltpu.roll(x, LANES - 16, axis=1), pltpu.roll(x, 16, axis=1))
    return x * cos + partner * sin


def _mods_index(i):
    tiles = (CTX_LEN + SEQ) // ROW_TILE
    return (i // tiles) * 2 + jnp.minimum(i % tiles, 1)


def _mm_kernel(a_ref, w_ref, o_ref, *, precision):
    o_ref[...] = jnp.dot(a_ref[...], w_ref[...], precision=precision,
                         preferred_element_type=_F32).astype(o_ref.dtype)


def _mm(a, w, *, tn, precision=None):
    m, k = a.shape
    n = w.shape[1]
    return pl.pallas_call(
        functools.partial(_mm_kernel, precision=precision),
        grid=(n // tn,),
        in_specs=[pl.BlockSpec((m, k), lambda j: (0, 0)),
                  pl.BlockSpec((k, tn), lambda j: (0, j))],
        out_specs=pl.BlockSpec((m, tn), lambda j: (0, j)),
        out_shape=jax.ShapeDtypeStruct((m, n), _F32),
        name=f"mm_{k}x{n}",
        compiler_params=_params(1),
    )(a, w)


def _proj_kernel(*refs, combine, rope):
    it = iter(refs)
    x_ref, m_ref = next(it), next(it)
    if combine:
        y_ref, g_ref, mp_ref = next(it), next(it), next(it)
    w_ref = next(it)
    if rope:
        cos_ref, sin_ref = next(it), next(it)
    o_ref = next(it)
    x = x_ref[...]
    if combine:
        xo_ref = next(it)
        y, g = y_ref[...], g_ref[...]
        x = x + mp_ref[0, 5:6, :] * (g[:, 0:1] * y[:, :D_MODEL] + g[:, 1:2] * y[:, D_MODEL:])
        xo_ref[...] = x
    h = _rms_rows(x) * (1.0 + m_ref[0, 1:2, :]) + m_ref[0, 0:1, :]
    p = jnp.dot(h.astype(_BF16), w_ref[...], preferred_element_type=_F32)
    if rope:
        cos, sin = cos_ref[...], sin_ref[...]
        for s in range(2 * C_QK // LANES):
            sl = slice(s * LANES, (s + 1) * LANES)
            slab = _rope_slab(p[:, sl], cos, sin)
            if s < C_QK // LANES:
                slab = slab * HEAD_DIM ** -0.5
            o_ref[:, sl] = slab.astype(o_ref.dtype)
        o_ref[:, 2 * C_QK:] = p[:, 2 * C_QK:].astype(o_ref.dtype)
    else:
        o_ref[...] = p


def _project(x, mods, w, *, moe=None, rope=None):
    n, d = x.shape
    cols = w.shape[1]
    tm = ROW_TILE
    tiles = (CTX_LEN + SEQ) // tm
    row = lambda width: pl.BlockSpec((tm, width), lambda i: (i, 0))
    mod_spec = pl.BlockSpec((1, 6, d), lambda i: (_mods_index(i), 0, 0))
    args, specs = [x, mods], [row(d), mod_spec]
    if moe is not None:
        args += list(moe)
        specs += [row(2 * d), row(TOP_K), mod_spec]
    args.append(w)
    specs.append(_full(w.shape))
    if rope is not None:
        args += list(rope)
        specs += [pl.BlockSpec((tm, LANES), lambda i: (i % tiles, 0))] * 2
    out_shape = [jax.ShapeDtypeStruct((n, cols), _BF16 if rope is not None else _F32)]
    out_specs = [row(cols)]
    if moe is not None:
        out_shape.append(jax.ShapeDtypeStruct((n, d), _F32))
        out_specs.append(row(d))
    out = pl.pallas_call(
        functools.partial(_proj_kernel, combine=moe is not None, rope=rope is not None),
        grid=(n // tm,), in_specs=specs, out_specs=out_specs, out_shape=out_shape,
        name=f"project_{cols}", compiler_params=_params(1),
    )(*args)
    return out if moe is not None else (out[0], x)


def _prep_kernel(*refs, has_v_first):
    it = iter(refs)
    p_ref, pp_ref, pn_ref, cos_ref, sin_ref = (next(it) for _ in range(5))
    gain_ref, mu_ref, wa0_ref, kv_ref, wl_ref, g2_ref = (next(it) for _ in range(6))
    if has_v_first:
        v1_ref, v2_ref, vf_ref = next(it), next(it), next(it)
    q_ref, kk2_ref, vv2_ref, r_ref, v_ref, kkn_ref, lw_ref, a_ref, kd_ref, g_ref = it

    tiles = (CTX_LEN + SEQ) // ROW_TILE
    j = pl.program_id(0) % tiles
    p = p_ref[...]
    cos, sin = cos_ref[...], sin_ref[...]
    mean_mat = _head_sum_matrix(1.0 / HEAD_DIM)
    ones_mat = _head_sum_matrix(1.0)
    lo = _lane_lo((ROW_TILE, LANES))

    def normed(x, gain):
        return x * lax.rsqrt(_head_sum(x * x, mean_mat) + RMS_EPS) * gain

    for s in range(A_WIDTH // LANES):
        sl = slice(s * LANES, (s + 1) * LANES)
        q = _rope_slab(normed(p[:, sl], gain_ref[0:1, :]), cos, sin)
        q_ref[:, sl] = (q * HEAD_DIM ** -0.5).astype(_BF16)
    k = _rope_slab(normed(p[:, A_WIDTH:A_WIDTH + LANES], gain_ref[1:2, :]), cos, sin)
    v = p[:, A_WIDTH + LANES:A_COLS]
    for src, dst in ((k, kk2_ref), (v, vv2_ref)):
        swapped = pltpu.roll(src, HEAD_DIM, axis=1)
        dst[:, :LANES] = jnp.where(lo, src, swapped).astype(_BF16)
        dst[:, LANES:] = jnp.where(lo, swapped, src).astype(_BF16)

    u = p[:, A_COLS:]
    rid = lax.broadcasted_iota(jnp.int32, (ROW_TILE, 1), 0)
    has_prev = jnp.where(j >= 2, 1.0, 0.0)
    has_next = jnp.where((j >= 1) & (j < tiles - 1), 1.0, 0.0)
    first = pp_ref[7:8, A_COLS:] * has_prev
    last = pn_ref[0:1, A_COLS:] * has_next
    prev = jnp.where(rid == 0, first, pltpu.roll(u, 1, axis=0))
    nxt = jnp.where(rid == ROW_TILE - 1, last, pltpu.roll(u, ROW_TILE - 1, axis=0))
    u = u + mu_ref[0:1, :] * (prev - u) + mu_ref[1:2, :] * (nxt - u)
    r, k, v = u[:, :B_WIDTH], u[:, B_WIDTH:2 * B_WIDTH], u[:, 2 * B_WIDTH:3 * B_WIDTH]
    wa, glr = u[:, 3 * B_WIDTH:3 * B_WIDTH + LANES], u[:, 3 * B_WIDTH + LANES:]
    if has_v_first:
        mid = jnp.dot(v.astype(_BF16), v1_ref[...], preferred_element_type=_F32)
        lora = jnp.dot(mid.astype(_BF16), v2_ref[...], preferred_element_type=_F32)
        v = v + (vf_ref[...] - v) * jax.nn.sigmoid(kv_ref[2:3, :] + lora)
    r_ref[...] = r
    v_ref[...] = v
    kks = k * kv_ref[0:1, :]
    for s in range(B_WIDTH // LANES):
        sl = slice(s * LANES, (s + 1) * LANES)
        x = kks[:, sl]
        kkn_ref[:, sl] = x * lax.rsqrt(jnp.maximum(_head_sum(x * x, ones_mat), 1e-24))
    xw = jnp.where(lo, jnp.tanh(wa), wa).astype(_BF16)
    for dr in range(2):
        z = jnp.dot(xw, wl_ref[dr], preferred_element_type=_F32) + wa0_ref[dr:dr + 1, :]
        a = jax.nn.sigmoid(z[:, B_WIDTH:])
        lw_ref[dr] = -DECAY_SCALE * jax.nn.sigmoid(z[:, :B_WIDTH])
        a_ref[dr] = a
        kd_ref[dr] = k * (1.0 + (a - 1.0) * kv_ref[1:2, :])
    g_ref[...] = jnp.dot(jax.nn.sigmoid(glr).astype(_BF16), g2_ref[...], preferred_element_type=_F32)


def _prepare_even(proj, cos, sin, p, v_first):
    n, cols = proj.shape
    tm = ROW_TILE
    tiles = (CTX_LEN + SEQ) // tm
    halo = 8
    per_tile = tm // halo
    row = lambda width: pl.BlockSpec((tm, width), lambda i: (i, 0))
    both = pl.BlockSpec((2, tm, B_WIDTH), lambda i: (0, i, 0))
    tab = pl.BlockSpec((tm, LANES), lambda i: (i % tiles, 0))
    pad = lambda w, rows, width: jnp.zeros((rows, width), _BF16).at[:w.shape[0], :w.shape[1]].set(w.astype(_BF16))
    gains = jnp.stack([jnp.tile(p['q_gain'], 2), jnp.tile(p['k_gain'], 2)])
    mu = jnp.stack([p['mu_prev'], p['mu_next']])
    wa0 = jnp.concatenate([p['w0'], p['a0']], axis=1)
    kvec = jnp.stack([p['k_k'], p['k_a'], p.get('v0', jnp.zeros((B_WIDTH,), _F32))])
    zero = jnp.zeros((RWKV_W_LORA, B_WIDTH), _F32)
    wl = jnp.stack([jnp.concatenate([jnp.concatenate([p['w2'][dr], zero], axis=1),
                                     jnp.concatenate([zero, p['a2'][dr]], axis=1)], axis=0)
                    for dr in range(2)]).astype(_BF16)
    args = [proj, proj, proj, cos, sin, gains, mu, wa0, kvec, wl, p['g2'].astype(_BF16)]
    specs = [row(cols),
             pl.BlockSpec((halo, cols), lambda i: (jnp.maximum(i * per_tile - 1, 0), 0)),
             pl.BlockSpec((halo, cols), lambda i: (jnp.minimum((i + 1) * per_tile, n // halo - 1), 0)),
             tab, tab, _full(gains.shape), _full(mu.shape), _full(wa0.shape), _full(kvec.shape),
             _full(wl.shape), _full(p['g2'].shape)]
    if v_first is not None:
        v1 = pad(p['v1'], B_WIDTH, LANES)
        v2 = pad(p['v2'], LANES, B_WIDTH)
        args += [v1, v2, v_first]
        specs += [_full(v1.shape), _full(v2.shape), row(B_WIDTH)]
    f32 = lambda *shape: jax.ShapeDtypeStruct(shape, _F32)
    bf16 = lambda *shape: jax.ShapeDtypeStruct(shape, _BF16)
    out_shape = [bf16(n, A_WIDTH), bf16(n, 2 * LANES), bf16(n, 2 * LANES),
                 f32(n, B_WIDTH), f32(n, B_WIDTH), f32(n, B_WIDTH),
                 f32(2, n, B_WIDTH), f32(2, n, B_WIDTH), f32(2, n, B_WIDTH), f32(n, B_WIDTH)]
    out_specs = [row(A_WIDTH), row(2 * LANES), row(2 * LANES), row(B_WIDTH), row(B_WIDTH), row(B_WIDTH),
                 both, both, both, row(B_WIDTH)]
    return pl.pallas_call(
        functools.partial(_prep_kernel, has_v_first=v_first is not None),
        grid=(n // tm,), in_specs=specs, out_specs=out_specs, out_shape=out_shape,
        name="prepare_even", compiler_params=_params(1),
    )(*args)


def _split_heads(q2):
    lo = _lane_lo(q2.shape)
    zero = jnp.zeros_like(q2)
    return jnp.concatenate([jnp.where(lo, q2, zero), jnp.where(lo, zero, q2)], axis=0)


def _scores(qs, k):
    return lax.dot_general(qs, k, _NT, preferred_element_type=_F32)


def _gqa_tile(q, k, v):
    tq = q.shape[0]
    qs = jnp.concatenate([_split_heads(q[:, :LANES]), _split_heads(q[:, LANES:])], axis=0)
    s = _scores(qs, k)
    p = jnp.exp(s - jnp.max(s, axis=-1, keepdims=True))
    l = jnp.sum(p, axis=-1, keepdims=True)
    o = jnp.dot(p.astype(_BF16), v, preferred_element_type=_F32) / l
    lo = _lane_lo((tq, LANES))
    return jnp.concatenate([jnp.where(lo, o[:tq], o[tq:2 * tq]),
                            jnp.where(lo, o[2 * tq:3 * tq], o[3 * tq:])], axis=1)


def _gqa_kernel(q_ref, k_ref, v_ref, o_ref):
    is_ctx = pl.program_id(2) < CTX_LEN // Q_TILE

    @pl.when(is_ctx)
    def _():
        o_ref[0] = _gqa_tile(q_ref[0], k_ref[0, :CTX_LEN], v_ref[0, :CTX_LEN]).astype(o_ref.dtype)

    @pl.when(jnp.logical_not(is_ctx))
    def _():
        o_ref[0] = _gqa_tile(q_ref[0], k_ref[0], v_ref[0]).astype(o_ref.dtype)


def _gqa_attend(q, kk, vv):
    b, t, _ = q.shape
    tq = Q_TILE
    kv_spec = pl.BlockSpec((1, t, LANES), lambda i, g, j: (i, 0, g))
    return pl.pallas_call(
        _gqa_kernel,
        grid=(b, A_KV_HEADS, t // tq),
        in_specs=[pl.BlockSpec((1, tq, 2 * LANES), lambda i, g, j: (i, j, g)), kv_spec, kv_spec],
        out_specs=pl.BlockSpec((1, tq, 2 * LANES), lambda i, g, j: (i, j, g)),
        out_shape=jax.ShapeDtypeStruct((b, t, A_WIDTH), _BF16),
        name="gqa_attn", compiler_params=_params(3),
    )(q, kk, vv)


def _diff_tile(lam, q, k, v, gain):
    tq = q.shape[0]
    s = _scores(_split_heads(q), k)
    p = jnp.exp(s - jnp.max(s, axis=-1, keepdims=True))
    l = jnp.sum(p, axis=-1, keepdims=True)
    l1, l2 = l[:tq], l[tq:]
    pr = p[:tq] - (lam * l1 / l2) * p[tq:]
    o = jnp.dot(pr.astype(_BF16), v, preferred_element_type=_F32) / l1
    y = o * lax.rsqrt(jnp.mean(o * o, axis=-1, keepdims=True) + SUBLN_EPS)
    return y * gain


def _diff_kernel(lam_ref, q_ref, k_ref, v_ref, g_ref, o_ref):
    is_ctx = pl.program_id(2) < CTX_LEN // Q_TILE
    lam, gain = lam_ref[0, 0], g_ref[...]

    @pl.when(is_ctx)
    def _():
        o_ref[0] = _diff_tile(lam, q_ref[0], k_ref[0, :CTX_LEN], v_ref[0, :CTX_LEN], gain).astype(o_ref.dtype)

    @pl.when(jnp.logical_not(is_ctx))
    def _():
        o_ref[0] = _diff_tile(lam, q_ref[0], k_ref[0], v_ref[0], gain).astype(o_ref.dtype)


def _diff_attend(lam, qkv, gain):
    b, t, _ = qkv.shape
    tq = Q_TILE
    return pl.pallas_call(
        _diff_kernel,
        grid=(b, C_HEADS, t // tq),
        in_specs=[pl.BlockSpec(memory_space=pltpu.SMEM),
                  pl.BlockSpec((1, tq, LANES), lambda i, h, j: (i, j, h)),
                  pl.BlockSpec((1, t, LANES), lambda i, h, j: (i, 0, C_HEADS + h)),
                  pl.BlockSpec((1, t, LANES), lambda i, h, j: (i, 0, 2 * C_HEADS + h)),
                  _full((1, LANES))],
        out_specs=pl.BlockSpec((1, tq, LANES), lambda i, h, j: (i, j, h)),
        out_shape=jax.ShapeDtypeStruct((b, t, C_HEADS * LANES), _BF16),
        name="diff_attn", compiler_params=_params(3),
    )(lam, qkv, qkv, qkv, gain)


def _scan_chunk(h, r, v, kk, lw, cum, cum_end, a, kd, sgn):
    c = SCAN_CHUNK
    lo = _lane_lo((c, LANES))
    row = lax.broadcasted_iota(jnp.int32, (LANES, LANES), 0)
    col = lax.broadcasted_iota(jnp.int32, (LANES, LANES), 1)
    delta = ((col & (c - 1)) - (row & (c - 1))) * sgn
    before = delta < 0
    upto = delta <= 0
    eye = row == col
    zero = jnp.zeros((LANES, LANES), _F32)

    def blockdiag(x):
        z = jnp.zeros_like(x)
        return jnp.concatenate([jnp.where(lo, x, z), jnp.where(lo, z, x)], axis=0)

    p_inv = jnp.exp(-cum)
    p_end = jnp.exp(cum_end - cum)
    kka = kk * a
    a_rs = blockdiag(-kk * jnp.exp(cum - lw))
    r_rs = blockdiag(r * jnp.exp(cum))
    b_rs = blockdiag(kka * p_inv)
    k_rs = blockdiag(kd * p_inv)
    v_rs = blockdiag(v)
    bp_rs = blockdiag(kka * p_end)
    kp_rs = blockdiag(kd * p_end)

    z = _dot3(_split2(jnp.concatenate([a_rs, r_rs], axis=0)),
              _split2(jnp.concatenate([b_rs, k_rs], axis=0)), _NT)
    yield
    low = jnp.where(before, z[:LANES, :LANES], zero)
    g = jnp.where(before, z[:LANES, LANES:], zero)
    rb = jnp.where(upto, z[LANES:, :LANES], zero)
    rk = jnp.where(upto, z[LANES:, LANES:], zero)

    xs = _split2(low)
    x = _dot3(xs, xs)
    yield
    m = jnp.where(eye, 1.0, 0.0).astype(_F32) + low
    n = 2
    while 2 * n < c:
        res = _dot3(_split2(x), _split2(jnp.concatenate([m, x], axis=1)))
        m = m + res[:, :LANES]
        x = res[:, LANES:]
        n *= 2
        yield
    m = m + _dot3(_split2(x), _split2(m))
    yield
    gv = _dot3(_split2(g), _split2(v_rs))
    yield
    au = _dot3(_split2(m), _split2(jnp.concatenate([a_rs, gv], axis=1)))
    yield
    rhs = _split2(jnp.concatenate(
        [au, jnp.concatenate([zero, v_rs], axis=1)], axis=0))
    tf = _dot3(_split2(jnp.concatenate([bp_rs, kp_rs], axis=0)), rhs, _TN)
    qy = _dot3(_split2(jnp.concatenate([rb, rk], axis=1)), rhs)
    yield
    t = jnp.where(eye, jnp.broadcast_to(jnp.exp(cum_end), (LANES, LANES)), zero) + tf[:, :LANES]
    q = r_rs + qy[:, :LANES]
    out = _dot3(_split2(jnp.concatenate([q, t], axis=0)), _split2(h))
    y = out[:LANES] + qy[:, LANES:]
    h_new = out[LANES:] + tf[:, LANES:]
    return y[:c] + y[c:], h_new


def _scan_kernel(r_ref, v_ref, kk_ref, lw_ref, a_ref, kd_ref, y_ref, h_ref):
    c = SCAN_CHUNK

    @pl.when(pl.program_id(2) == 0)
    def _():
        h_ref[...] = jnp.zeros_like(h_ref)

    sgn = 1 - 2 * pl.program_id(0)
    tr = lax.broadcasted_iota(jnp.int32, (c, c), 0)
    tc = lax.broadcasted_iota(jnp.int32, (c, c), 1)
    tri = jnp.where((tc - tr) * sgn <= 0, 1.0, 0.0).astype(_BF16)
    lw = lw_ref[0, 0]
    l1 = lw.astype(_BF16)
    rem = lw - l1.astype(_F32)
    l2 = rem.astype(_BF16)
    l3 = (rem - l2.astype(_F32)).astype(_BF16)
    d = lambda w: jnp.dot(tri, w, preferred_element_type=_F32)
    cum = (d(l3) + d(l2)) + d(l1)
    cum_end = jnp.sum(lw, axis=0, keepdims=True)
    slices = [slice(p * LANES, (p + 1) * LANES) for p in range(lw.shape[1] // LANES)]
    chains = [_scan_chunk(h_ref[p], r_ref[0, :, sl], v_ref[0, :, sl], kk_ref[0, :, sl], lw[:, sl],
                          cum[:, sl], cum_end[:, sl], a_ref[0, 0, :, sl], kd_ref[0, 0, :, sl], sgn)
              for p, sl in enumerate(slices)]
    results = {}
    while len(results) < len(chains):
        for p, chain in enumerate(chains):
            try:
                next(chain)
            except StopIteration as stop:
                results[p] = stop.value
    for p, sl in enumerate(slices):
        y_ref[0, 0, :, sl], h_ref[p] = results[p]


def _delta_scan(r, v, kk, lw, a, kd):
    b, t, w = r.shape
    c = SCAN_CHUNK
    n_chunks = t // c
    n_ctx = CTX_LEN // c

    def chunk(d, j):
        rev = jnp.where(j < n_ctx, n_ctx - 1 - j, n_chunks + n_ctx - 1 - j)
        return jnp.where(d == 0, j, rev)

    shared = pl.BlockSpec((1, c, w), lambda d, i, j: (i, chunk(d, j), 0))
    per_dir = pl.BlockSpec((1, 1, c, w), lambda d, i, j: (d, i, chunk(d, j), 0))
    return pl.pallas_call(
        _scan_kernel,
        grid=(2, b, n_chunks),
        in_specs=[shared, shared, shared, per_dir, per_dir, per_dir],
        out_specs=per_dir,
        out_shape=jax.ShapeDtypeStruct((2, b, t, w), _F32),
        scratch_shapes=[pltpu.VMEM((w // LANES, LANES, LANES), _F32)],
        name="delta_scan", compiler_params=_params(3),
    )(r, v, kk, lw, a, kd)


def _post_kernel(*refs, even):
    it = iter(refs)
    if even:
        y_ref, r_ref, v_ref, kd_ref, g_ref, gn_ref = (next(it) for _ in range(6))
    att_ref, x_ref, m_ref, wo_ref, wrh_ref, wrl_ref, br_ref, xo_ref, h_ref, lg_ref = it
    if even:
        mean_mat = _head_sum_matrix(1.0 / HEAD_DIM)
        ones_mat = _head_sum_matrix(1.0)
        parts = [att_ref[...]]
        for s in range(B_WIDTH // LANES):
            sl = slice(s * LANES, (s + 1) * LANES)
            y = y_ref[0, :, sl] + y_ref[1, :, sl]
            dev = y - _head_sum(y, mean_mat)
            var = _head_sum(dev * dev, mean_mat)
            y = dev * lax.rsqrt(var + RWKV_GN_EPS) * gn_ref[0:1, sl] + gn_ref[1:2, sl]
            r, v = r_ref[:, sl], v_ref[:, sl]
            for dr in range(2):
                y = y + _head_sum(r * kd_ref[dr, :, sl] * gn_ref[2:3, sl], ones_mat) * v
            parts.append((y * g_ref[:, sl]).astype(_BF16))
        mixed = jnp.concatenate(parts, axis=1)
    else:
        mixed = att_ref[...]
    out = jnp.dot(mixed, wo_ref[...], preferred_element_type=_F32)
    x = x_ref[...] + m_ref[0, 2:3, :] * out
    xo_ref[...] = x
    h = _rms_rows(x) * (1.0 + m_ref[0, 4:5, :]) + m_ref[0, 3:4, :]
    h_ref[...] = h.astype(_BF16)
    lg_ref[...] = _dot3(_split2(h), (wrh_ref[...], wrl_ref[...])) + br_ref[...]


def _mixer_out(att, x, mods, w_out, w_router, b_router, rwkv=None):
    n, d = x.shape
    tm = ROW_TILE
    row = lambda width: pl.BlockSpec((tm, width), lambda i: (i, 0))
    both = pl.BlockSpec((2, tm, B_WIDTH), lambda i: (0, i, 0))
    args, specs = [], []
    if rwkv is not None:
        y, r, v, kd, g, gn = rwkv
        args += [y, r, v, kd, g, gn]
        specs += [both, row(B_WIDTH), row(B_WIDTH), both, row(B_WIDTH), _full(gn.shape)]
    wr_hi, wr_lo = _split2(w_router)
    args += [att, x, mods, w_out, wr_hi, wr_lo, b_router]
    specs += [row(att.shape[1]), row(d), pl.BlockSpec((1, 6, d), lambda i: (_mods_index(i), 0, 0)),
              _full(w_out.shape), _full(wr_hi.shape), _full(wr_lo.shape), _full(b_router.shape)]
    return pl.pallas_call(
        functools.partial(_post_kernel, even=rwkv is not None),
        grid=(n // tm,), in_specs=specs,
        out_specs=[row(d), row(d), row(LANES)],
        out_shape=[jax.ShapeDtypeStruct((n, d), _F32), jax.ShapeDtypeStruct((n, d), _BF16),
                   jax.ShapeDtypeStruct((n, LANES), _F32)],
        name="mixer_out_even" if rwkv is not None else "mixer_out_odd", compiler_params=_params(1),
    )(*args)


def _moe_kernel(be_ref, nv_ref, x_ref, wg_ref, wu_ref, wd_ref, o_ref):
    del be_ref

    @pl.when(pl.program_id(0) < nv_ref[0])
    def _():
        x = x_ref[...]
        gate = jnp.dot(x, wg_ref[0], preferred_element_type=_F32)
        up = jnp.dot(x, wu_ref[0], preferred_element_type=_F32)
        hid = (gate * jax.nn.sigmoid(gate) * up).astype(_BF16)
        o_ref[...] = jnp.dot(hid, wd_ref[0], preferred_element_type=_F32)

    @pl.when(pl.program_id(0) >= nv_ref[0])
    def _():
        o_ref[...] = jnp.zeros_like(o_ref)


def _moe_blocks(block_expert, n_valid, xs, wg, wu, wd):
    rows, d = xs.shape
    nb = rows // EXPERT_BLOCK
    grid_spec = pltpu.PrefetchScalarGridSpec(
        num_scalar_prefetch=2,
        grid=(nb,),
        in_specs=[pl.BlockSpec((EXPERT_BLOCK, d), lambda i, be, nv: (i, 0)),
                  pl.BlockSpec((1, d, EXPERT_HIDDEN), lambda i, be, nv: (be[i], 0, 0)),
                  pl.BlockSpec((1, d, EXPERT_HIDDEN), lambda i, be, nv: (be[i], 0, 0)),
                  pl.BlockSpec((1, EXPERT_HIDDEN, d), lambda i, be, nv: (be[i], 0, 0))],
        out_specs=pl.BlockSpec((EXPERT_BLOCK, d), lambda i, be, nv: (i, 0)),
    )
    return pl.pallas_call(
        _moe_kernel,
        grid_spec=grid_spec,
        out_shape=jax.ShapeDtypeStruct((rows, d), _F32),
        name="moe_ffn", compiler_params=_params(1),
    )(block_expert, n_valid, xs, wg, wu, wd)


def _moe(h, logits, w_gate, w_up, w_down):
    n, d = h.shape
    grp_logits = logits[:, :N_GROUPS]
    grp_sel = jnp.argmax(grp_logits, axis=-1)
    grp_w = jnp.take_along_axis(jax.nn.softmax(grp_logits, axis=-1), grp_sel[:, None], axis=1)
    exp_logits = logits[:, N_GROUPS:N_GROUPS + N_EXPERTS].reshape(n, N_GROUPS, EXPERTS_PER_GROUP)
    exp_logits = jnp.take_along_axis(exp_logits, grp_sel[:, None, None], axis=1)[:, 0]
    top_p, top_i = lax.top_k(jax.nn.softmax(exp_logits, axis=-1), TOP_K)
    gates = grp_w * top_p / jnp.sum(top_p, axis=-1, keepdims=True)
    expert_id = (grp_sel[:, None] * EXPERTS_PER_GROUP + top_i).astype(jnp.int32)

    m = n * TOP_K
    e_flat = expert_id.reshape(m)
    tok = jnp.arange(m, dtype=jnp.int32) // TOP_K
    counts = jnp.bincount(e_flat, length=N_EXPERTS).astype(jnp.int32)
    padded = (counts + EXPERT_BLOCK - 1) // EXPERT_BLOCK * EXPERT_BLOCK
    pad_end = jnp.cumsum(padded)
    pad_start = pad_end - padded
    raw_start = jnp.cumsum(counts) - counts
    order = jnp.argsort(e_flat)
    e_sorted = e_flat[order]
    slot_sorted = pad_start[e_sorted] + jnp.arange(m, dtype=jnp.int32) - raw_start[e_sorted]
    n_blocks = -(-(m + N_EXPERTS * (EXPERT_BLOCK - 1)) // EXPERT_BLOCK)
    slot_tok = jnp.full((n_blocks * EXPERT_BLOCK,), n, jnp.int32).at[slot_sorted].set(tok[order])
    block_start = jnp.arange(n_blocks, dtype=jnp.int32) * EXPERT_BLOCK
    block_expert = jnp.minimum(jnp.searchsorted(pad_end, block_start, side='right'),
                               N_EXPERTS - 1).astype(jnp.int32)
    n_valid = (pad_end[-1:] // EXPERT_BLOCK).astype(jnp.int32)
    h_pad = jnp.concatenate([h, jnp.zeros((1, d), h.dtype)], axis=0)
    xs = h_pad[slot_tok]
    yb = _moe_blocks(block_expert, n_valid, xs, w_gate.astype(_BF16), w_up.astype(_BF16),
                     w_down.astype(_BF16))
    slot = jnp.zeros((m,), jnp.int32).at[order].set(slot_sorted)
    return yb[slot].reshape(n, TOP_K * d), gates


def _final_kernel(x_ref, y_ref, g_ref, m_ref, gain_ref, o_ref):
    y, g = y_ref[...], g_ref[...]
    x = x_ref[...] + m_ref[0, 5:6, :] * (g[:, 0:1] * y[:, :D_MODEL] + g[:, 1:2] * y[:, D_MODEL:])
    o_ref[...] = _rms_rows(x) * gain_ref[...]


def _final(x, y, gates, mods, gain, bsz, n_lat):
    d = x.shape[1]
    tm = ROW_TILE
    lat_tiles = n_lat // tm
    tiles = (CTX_LEN + n_lat) // tm
    src = lambda i: (i // lat_tiles) * tiles + CTX_LEN // tm + i % lat_tiles
    row = lambda width: pl.BlockSpec((tm, width), lambda i: (src(i), 0))
    return pl.pallas_call(
        _final_kernel,
        grid=(bsz * lat_tiles,),
        in_specs=[row(d), row(TOP_K * d), row(TOP_K),
                  pl.BlockSpec((1, 6, d), lambda i: ((i // lat_tiles) * 2 + 1, 0, 0)), _full((1, d))],
        out_specs=pl.BlockSpec((tm, d), lambda i: (i, 0)),
        out_shape=jax.ShapeDtypeStruct((bsz * n_lat, d), _F32),
        name="final_norm", compiler_params=_params(1),
    )(x, y, gates, mods, gain.reshape(1, d))


def _rope_tables(n_lat):
    nf = HEAD_DIM // 4
    inv = ROPE_THETA ** (-jnp.arange(nf, dtype=_F32) / nf)
    rows = n_lat // GRID_W
    r_ang = jnp.repeat(jnp.arange(rows, dtype=_F32), GRID_W)[:, None] * inv
    c_ang = jnp.tile(jnp.arange(GRID_W, dtype=_F32), rows)[:, None] * inv
    cos = jnp.concatenate([jnp.cos(r_ang)] * 2 + [jnp.cos(c_ang)] * 2, axis=-1)
    sin = jnp.concatenate([-jnp.sin(r_ang), jnp.sin(r_ang), -jnp.sin(c_ang), jnp.sin(c_ang)], axis=-1)
    cos = jnp.concatenate([jnp.ones((CTX_LEN, HEAD_DIM), _F32), cos], axis=0)
    sin = jnp.concatenate([jnp.zeros((CTX_LEN, HEAD_DIM), _F32), sin], axis=0)
    return jnp.tile(cos, (1, 2)), jnp.tile(sin, (1, 2))


def kernel(x, c, ctx, c_ctx, mod_w, mod_b, ev_w_in, ev_w_out, ev_q_gain, ev_k_gain, ev_mu_prev, ev_mu_next, ev_w0, ev_w2, ev_a0, ev_a2, ev_g2, ev_k_k, ev_k_a, ev_r_k, ev_gn_w, ev_gn_b, ev_v0, ev_v1, ev_v2, od_w_in, od_w_out, od_lq1, od_lk1, od_lq2, od_lk2, od_subln, moe_w_grp, moe_b_grp, moe_w_rt, moe_b_rt, moe_w_gate, moe_w_up, moe_w_down, final_gain):
    bsz, n_lat, d = x.shape
    t = CTX_LEN + n_lat
    n = bsz * t
    cos, sin = _rope_tables(n_lat)
    cond = jax.nn.silu(jnp.concatenate([c, c_ctx[None, :], jnp.zeros((16 - bsz - 1, d), _F32)], axis=0))
    xs = jnp.concatenate([ctx, x], axis=1).reshape(n, d)
    v_first = None
    moe = None
    for layer in range(DEPTH):
        m = _mm(cond, mod_w[layer], tn=1536, precision=_HI) + mod_b[layer]
        m = m.reshape(16, 6, d)
        mods = jnp.stack([jnp.broadcast_to(m[bsz], (bsz, 6, d)), m[:bsz]], axis=1).reshape(2 * bsz, 6, d)
        w_router = jnp.concatenate(
            [moe_w_grp[layer], moe_w_rt[layer], jnp.zeros((d, LANES - N_GROUPS - N_EXPERTS), _F32)], axis=1)
        b_router = jnp.concatenate(
            [moe_b_grp[layer], moe_b_rt[layer], jnp.zeros((LANES - N_GROUPS - N_EXPERTS,), _F32)])[None, :]
        if layer % 2 == 0:
            e = layer // 2
            p = {'q_gain': ev_q_gain[e], 'k_gain': ev_k_gain[e], 'mu_prev': ev_mu_prev[e],
                 'mu_next': ev_mu_next[e], 'w0': ev_w0[e], 'w2': ev_w2[e], 'a0': ev_a0[e],
                 'a2': ev_a2[e], 'g2': ev_g2[e], 'k_k': ev_k_k[e], 'k_a': ev_k_a[e]}
            if e > 0:
                p['v0'], p['v1'], p['v2'] = ev_v0[e - 1], ev_v1[e - 1], ev_v2[e - 1]
            proj, xs = _project(xs, mods, ev_w_in[e].astype(_BF16), moe=moe)
            q, kk2, vv2, r, v, kkn, lw, a, kd, g = _prepare_even(proj, cos, sin, p, v_first)
            if v_first is None:
                v_first = v
            att = _gqa_attend(q.reshape(bsz, t, -1), kk2.reshape(bsz, t, -1), vv2.reshape(bsz, t, -1))
            seq = lambda z: z.reshape(*z.shape[:-2], bsz, t, B_WIDTH)
            y = _delta_scan(seq(r), seq(v), seq(kkn), seq(lw), seq(a), seq(kd)).reshape(2, n, B_WIDTH)
            gn = jnp.stack([ev_gn_w[e], ev_gn_b[e], ev_r_k[e].reshape(B_WIDTH)])
            xs, h, logits = _mixer_out(att.reshape(n, -1), xs, mods, ev_w_out[e].astype(_BF16),
                                       w_router, b_router, rwkv=(y, r, v, kd, g, gn))
        else:
            o = layer // 2
            lambda_init = 0.8 - 0.6 * math.exp(-0.3 * layer)
            qkv, xs = _project(xs, mods, od_w_in[o].astype(_BF16), moe=moe, rope=(cos, sin))
            lam = (jnp.exp(jnp.sum(od_lq1[o] * od_lk1[o])) - jnp.exp(jnp.sum(od_lq2[o] * od_lk2[o]))
                   + lambda_init).reshape(1, 1)
            gain = (od_subln[o] * (1.0 - lambda_init)).reshape(1, LANES)
            att = _diff_attend(lam, qkv.reshape(bsz, t, -1), gain)
            xs, h, logits = _mixer_out(att.reshape(n, -1), xs, mods, od_w_out[o].astype(_BF16),
                                       w_router, b_router)
        y, gates = _moe(h, logits, moe_w_gate[layer], moe_w_up[layer], moe_w_down[layer])
        moe = (y, gates, mods)
    return _final(xs, moe[0], moe[1], mods, final_gain, bsz, n_lat).reshape(bsz, n_lat, d)
```

```python
import functools
import math

import jax
import jax.numpy as jnp
from jax import lax
from jax.experimental import pallas as pl
from jax.experimental.pallas import tpu as pltpu

D_MODEL = 1024
DEPTH = 4
SEQ = 2048
GRID_W = 64
CTX_LEN = 256
HEAD_DIM = 64
ROPE_THETA = 10000.0
RMS_EPS = 1e-6
SUBLN_EPS = 1e-5

A_HEADS = 8
A_KV_HEADS = 2
A_WIDTH = 512
A_KV_WIDTH = 128
A_COLS = 768

B_HEADS = 8
B_WIDTH = 512
B_COLS = 1792
RWKV_W_LORA = 64
RWKV_A_LORA = 64
RWKV_V_LORA = 32
RWKV_G_LORA = 128
RWKV_GN_EPS = 64e-5
DECAY_SCALE = math.exp(-0.5)

C_HEADS = 8
C_QK = 1024

N_GROUPS = 4
EXPERTS_PER_GROUP = 8
N_EXPERTS = 32
TOP_K = 2
EXPERT_HIDDEN = 512
EXPERT_BLOCK = 128

LANES = 128
SCAN_CHUNK = 64
ROW_TILE = 256
Q_TILE = 128
VMEM_LIMIT = 56 * 1024 * 1024

_HI = lax.Precision.HIGHEST
_F32 = jnp.float32
_BF16 = jnp.bfloat16

_NN = (((1,), (0,)), ((), ()))
_NT = (((1,), (1,)), ((), ()))
_TN = (((0,), (0,)), ((), ()))


def _params(n_axes):
    return pltpu.CompilerParams(dimension_semantics=("arbitrary",) * n_axes,
                                vmem_limit_bytes=VMEM_LIMIT)


def _full(shape):
    return pl.BlockSpec(shape, lambda *_: (0,) * len(shape))


def _split2(x):
    hi = x.astype(_BF16)
    return hi, (x - hi.astype(_F32)).astype(_BF16)


def _dot3(a, b, dims=_NN):
    d = lambda u, w: lax.dot_general(u, w, dims, preferred_element_type=_F32)
    return (d(a[1], b[0]) + d(a[0], b[1])) + d(a[0], b[0])


def _lane_lo(shape):
    return lax.broadcasted_iota(jnp.int32, shape, len(shape) - 1) < HEAD_DIM


def _head_sum_matrix(value):
    row = lax.broadcasted_iota(jnp.int32, (LANES, LANES), 0)
    col = lax.broadcasted_iota(jnp.int32, (LANES, LANES), 1)
    return jnp.where((row // HEAD_DIM) == (col // HEAD_DIM), value, 0.0).astype(_BF16)


def _head_sum(x, mat):
    hi, lo = _split2(x)
    return jnp.dot(lo, mat, preferred_element_type=_F32) + jnp.dot(hi, mat, preferred_element_type=_F32)


def _rms_rows(x, eps=RMS_EPS):
    return x * lax.rsqrt(jnp.mean(x * x, axis=-1, keepdims=True) + eps)


def _rope_slab(x, cos, sin):
    lane = lax.broadcasted_iota(jnp.int32, x.shape, 1)
    partner = jnp.where((lane & 16) == 0, pltpu.roll(x, LANES - 16, axis=1), pltpu.roll(x, 16, axis=1))
    return x * cos + partner * sin


def _mods_index(i):
    tiles = (CTX_LEN + SEQ) // ROW_TILE
    return (i // tiles) * 2 + jnp.minimum(i % tiles, 1)


def _mm_kernel(a_ref, w_ref, o_ref, *, precision):
    o_ref[...] = jnp.dot(a_ref[...], w_ref[...], precision=precision,
                         preferred_element_type=_F32).astype(o_ref.dtype)


def _mm(a, w, *, tn, precision=None):
    m, k = a.shape
    n = w.shape[1]
    return pl.pallas_call(
        functools.partial(_mm_kernel, precision=precision),
        grid=(n // tn,),
        in_specs=[pl.BlockSpec((m, k), lambda j: (0, 0)),
                  pl.BlockSpec((k, tn), lambda j: (0, j))],
        out_specs=pl.BlockSpec((m, tn), lambda j: (0, j)),
        out_shape=jax.ShapeDtypeStruct((m, n), _F32),
        name=f"mm_{k}x{n}",
        compiler_params=_params(1),
    )(a, w)


def _proj_kernel(*refs, combine, rope):
    it = iter(refs)
    x_ref, m_ref = next(it), next(it)
    if combine:
        y1_ref, y2_ref, g_ref, mp_ref = (next(it) for _ in range(4))
    w_ref = next(it)
    if rope:
        cos_ref, sin_ref = next(it), next(it)
    o_ref = next(it)
    x = x_ref[...]
    if combine:
        xo_ref = next(it)
        g = g_ref[...]
        x = x + mp_ref[0, 5:6, :] * (g[:, 0:1] * y1_ref[...] + g[:, 1:2] * y2_ref[...])
        xo_ref[...] = x
    h = _rms_rows(x) * (1.0 + m_ref[0, 1:2, :]) + m_ref[0, 0:1, :]
    p = jnp.dot(h.astype(_BF16), w_ref[...], preferred_element_type=_F32)
    if rope:
        cos, sin = cos_ref[...], sin_ref[...]
        for s in range(2 * C_QK // LANES):
            sl = slice(s * LANES, (s + 1) * LANES)
            slab = _rope_slab(p[:, sl], cos, sin)
            if s < C_QK // LANES:
                slab = slab * HEAD_DIM ** -0.5
            o_ref[:, sl] = slab.astype(o_ref.dtype)
        o_ref[:, 2 * C_QK:] = p[:, 2 * C_QK:].astype(o_ref.dtype)
    else:
        o_ref[...] = p


def _project(x, mods, w, *, moe=None, rope=None):
    n, d = x.shape
    cols = w.shape[1]
    tm = ROW_TILE
    tiles = (CTX_LEN + SEQ) // tm
    row = lambda width: pl.BlockSpec((tm, width), lambda i: (i, 0))
    mod_spec = pl.BlockSpec((1, 6, d), lambda i: (_mods_index(i), 0, 0))
    args, specs = [x, mods], [row(d), mod_spec]
    if moe is not None:
        args += list(moe)
        specs += [row(d), row(d), row(LANES), mod_spec]
    args.append(w)
    specs.append(_full(w.shape))
    if rope is not None:
        args += list(rope)
        specs += [pl.BlockSpec((tm, LANES), lambda i: (i % tiles, 0))] * 2
    out_shape = [jax.ShapeDtypeStruct((n, cols), _BF16 if rope is not None else _F32)]
    out_specs = [row(cols)]
    if moe is not None:
        out_shape.append(jax.ShapeDtypeStruct((n, d), _F32))
        out_specs.append(row(d))
    out = pl.pallas_call(
        functools.partial(_proj_kernel, combine=moe is not None, rope=rope is not None),
        grid=(n // tm,), in_specs=specs, out_specs=out_specs, out_shape=out_shape,
        name=f"project_{cols}", compiler_params=_params(1),
    )(*args)
    return out if moe is not None else (out[0], x)


def _prep_kernel(*refs, has_v_first):
    it = iter(refs)
    p_ref, pp_ref, pn_ref, cos_ref, sin_ref = (next(it) for _ in range(5))
    gain_ref, mu_ref, wa0_ref, kv_ref, wl_ref, g2_ref = (next(it) for _ in range(6))
    if has_v_first:
        v1_ref, v2_ref, vf_ref = next(it), next(it), next(it)
    q_ref, kk2_ref, vv2_ref, r_ref, v_ref, kkn_ref, lw_ref, a_ref, kd_ref, g_ref = it

    tiles = (CTX_LEN + SEQ) // ROW_TILE
    j = pl.program_id(0) % tiles
    p = p_ref[...]
    cos, sin = cos_ref[...], sin_ref[...]
    mean_mat = _head_sum_matrix(1.0 / HEAD_DIM)
    ones_mat = _head_sum_matrix(1.0)
    lo = _lane_lo((ROW_TILE, LANES))

    def normed(x, gain):
        return x * lax.rsqrt(_head_sum(x * x, mean_mat) + RMS_EPS) * gain

    for s in range(A_WIDTH // LANES):
        sl = slice(s * LANES, (s + 1) * LANES)
        q = _rope_slab(normed(p[:, sl], gain_ref[0:1, :]), cos, sin)
        q_ref[:, sl] = (q * HEAD_DIM ** -0.5).astype(_BF16)
    k = _rope_slab(normed(p[:, A_WIDTH:A_WIDTH + LANES], gain_ref[1:2, :]), cos, sin)
    v = p[:, A_WIDTH + LANES:A_COLS]
    for src, dst in ((k, kk2_ref), (v, vv2_ref)):
        swapped = pltpu.roll(src, HEAD_DIM, axis=1)
        dst[:, :LANES] = jnp.where(lo, src, swapped).astype(_BF16)
        dst[:, LANES:] = jnp.where(lo, swapped, src).astype(_BF16)

    u = p[:, A_COLS:]
    rid = lax.broadcasted_iota(jnp.int32, (ROW_TILE, 1), 0)
    has_prev = jnp.where(j >= 2, 1.0, 0.0)
    has_next = jnp.where((j >= 1) & (j < tiles - 1), 1.0, 0.0)
    first = pp_ref[7:8, A_COLS:] * has_prev
    last = pn_ref[0:1, A_COLS:] * has_next
    prev = jnp.where(rid == 0, first, pltpu.roll(u, 1, axis=0))
    nxt = jnp.where(rid == ROW_TILE - 1, last, pltpu.roll(u, ROW_TILE - 1, axis=0))
    u = u + mu_ref[0:1, :] * (prev - u) + mu_ref[1:2, :] * (nxt - u)
    r, k, v = u[:, :B_WIDTH], u[:, B_WIDTH:2 * B_WIDTH], u[:, 2 * B_WIDTH:3 * B_WIDTH]
    wa, glr = u[:, 3 * B_WIDTH:3 * B_WIDTH + LANES], u[:, 3 * B_WIDTH + LANES:]
    if has_v_first:
        mid = jnp.dot(v.astype(_BF16), v1_ref[...], preferred_element_type=_F32)
        lora = jnp.dot(mid.astype(_BF16), v2_ref[...], preferred_element_type=_F32)
        v = v + (vf_ref[...] - v) * jax.nn.sigmoid(kv_ref[2:3, :] + lora)
    r_ref[...] = r
    v_ref[...] = v
    kks = k * kv_ref[0:1, :]
    for s in range(B_WIDTH // LANES):
        sl = slice(s * LANES, (s + 1) * LANES)
        x = kks[:, sl]
        kkn_ref[:, sl] = x * lax.rsqrt(jnp.maximum(_head_sum(x * x, ones_mat), 1e-24))
    xw = jnp.where(lo, jnp.tanh(wa), wa).astype(_BF16)
    for dr in range(2):
        z = jnp.dot(xw, wl_ref[dr], preferred_element_type=_F32) + wa0_ref[dr:dr + 1, :]
        a = jax.nn.sigmoid(z[:, B_WIDTH:])
        lw_ref[dr] = -DECAY_SCALE * jax.nn.sigmoid(z[:, :B_WIDTH])
        a_ref[dr] = a
        kd_ref[dr] = k * (1.0 + (a - 1.0) * kv_ref[1:2, :])
    g_ref[...] = jnp.dot(jax.nn.sigmoid(glr).astype(_BF16), g2_ref[...], preferred_element_type=_F32)


def _prepare_even(proj, cos, sin, p, v_first):
    n, cols = proj.shape
    tm = ROW_TILE
    tiles = (CTX_LEN + SEQ) // tm
    halo = 8
    per_tile = tm // halo
    row = lambda width: pl.BlockSpec((tm, width), lambda i: (i, 0))
    both = pl.BlockSpec((2, tm, B_WIDTH), lambda i: (0, i, 0))
    tab = pl.BlockSpec((tm, LANES), lambda i: (i % tiles, 0))
    pad = lambda w, rows, width: jnp.zeros((rows, width), _BF16).at[:w.shape[0], :w.shape[1]].set(w.astype(_BF16))
    gains = jnp.stack([jnp.tile(p['q_gain'], 2), jnp.tile(p['k_gain'], 2)])
    mu = jnp.stack([p['mu_prev'], p['mu_next']])
    wa0 = jnp.concatenate([p['w0'], p['a0']], axis=1)
    kvec = jnp.stack([p['k_k'], p['k_a'], p.get('v0', jnp.zeros((B_WIDTH,), _F32))])
    zero = jnp.zeros((RWKV_W_LORA, B_WIDTH), _F32)
    wl = jnp.stack([jnp.concatenate([jnp.concatenate([p['w2'][dr], zero], axis=1),
                                     jnp.concatenate([zero, p['a2'][dr]], axis=1)], axis=0)
                    for dr in range(2)]).astype(_BF16)
    args = [proj, proj, proj, cos, sin, gains, mu, wa0, kvec, wl, p['g2'].astype(_BF16)]
    specs = [row(cols),
             pl.BlockSpec((halo, cols), lambda i: (jnp.maximum(i * per_tile - 1, 0), 0)),
             pl.BlockSpec((halo, cols), lambda i: (jnp.minimum((i + 1) * per_tile, n // halo - 1), 0)),
             tab, tab, _full(gains.shape), _full(mu.shape), _full(wa0.shape), _full(kvec.shape),
             _full(wl.shape), _full(p['g2'].shape)]
    if v_first is not None:
        v1 = pad(p['v1'], B_WIDTH, LANES)
        v2 = pad(p['v2'], LANES, B_WIDTH)
        args += [v1, v2, v_first]
        specs += [_full(v1.shape), _full(v2.shape), row(B_WIDTH)]
    f32 = lambda *shape: jax.ShapeDtypeStruct(shape, _F32)
    bf16 = lambda *shape: jax.ShapeDtypeStruct(shape, _BF16)
    out_shape = [bf16(n, A_WIDTH), bf16(n, 2 * LANES), bf16(n, 2 * LANES),
                 f32(n, B_WIDTH), f32(n, B_WIDTH), f32(n, B_WIDTH),
                 f32(2, n, B_WIDTH), f32(2, n, B_WIDTH), f32(2, n, B_WIDTH), f32(n, B_WIDTH)]
    out_specs = [row(A_WIDTH), row(2 * LANES), row(2 * LANES), row(B_WIDTH), row(B_WIDTH), row(B_WIDTH),
                 both, both, both, row(B_WIDTH)]
    return pl.pallas_call(
        functools.partial(_prep_kernel, has_v_first=v_first is not None),
        grid=(n // tm,), in_specs=specs, out_specs=out_specs, out_shape=out_shape,
        name="prepare_even", compiler_params=_params(1),
    )(*args)


def _split_heads(q2):
    lo = _lane_lo(q2.shape)
    zero = jnp.zeros_like(q2)
    return jnp.concatenate([jnp.where(lo, q2, zero), jnp.where(lo, zero, q2)], axis=0)


def _scores(qs, k):
    return lax.dot_general(qs, k, _NT, preferred_element_type=_F32)


def _gqa_tile(q, k, v):
    tq = q.shape[0]
    qs = jnp.concatenate([_split_heads(q[:, :LANES]), _split_heads(q[:, LANES:])], axis=0)
    s = _scores(qs, k)
    p = jnp.exp(s - jnp.max(s, axis=-1, keepdims=True))
    l = jnp.sum(p, axis=-1, keepdims=True)
    o = jnp.dot(p.astype(_BF16), v, preferred_element_type=_F32) / l
    lo = _lane_lo((tq, LANES))
    return jnp.concatenate([jnp.where(lo, o[:tq], o[tq:2 * tq]),
                            jnp.where(lo, o[2 * tq:3 * tq], o[3 * tq:])], axis=1)


def _gqa_kernel(q_ref, k_ref, v_ref, o_ref):
    is_ctx = pl.program_id(2) < CTX_LEN // Q_TILE

    @pl.when(is_ctx)
    def _():
        o_ref[0] = _gqa_tile(q_ref[0], k_ref[0, :CTX_LEN], v_ref[0, :CTX_LEN]).astype(o_ref.dtype)

    @pl.when(jnp.logical_not(is_ctx))
    def _():
        o_ref[0] = _gqa_tile(q_ref[0], k_ref[0], v_ref[0]).astype(o_ref.dtype)


def _gqa_attend(q, kk, vv):
    b, t, _ = q.shape
    tq = Q_TILE
    kv_spec = pl.BlockSpec((1, t, LANES), lambda i, g, j: (i, 0, g))
    return pl.pallas_call(
        _gqa_kernel,
        grid=(b, A_KV_HEADS, t // tq),
        in_specs=[pl.BlockSpec((1, tq, 2 * LANES), lambda i, g, j: (i, j, g)), kv_spec, kv_spec],
        out_specs=pl.BlockSpec((1, tq, 2 * LANES), lambda i, g, j: (i, j, g)),
        out_shape=jax.ShapeDtypeStruct((b, t, A_WIDTH), _BF16),
        name="gqa_attn", compiler_params=_params(3),
    )(q, kk, vv)


def _diff_tile(lam, q, k, v, gain):
    tq = q.shape[0]
    s = _scores(_split_heads(q), k)
    p = jnp.exp(s - jnp.max(s, axis=-1, keepdims=True))
    l = jnp.sum(p, axis=-1, keepdims=True)
    l1, l2 = l[:tq], l[tq:]
    pr = p[:tq] - (lam * l1 / l2) * p[tq:]
    o = jnp.dot(pr.astype(_BF16), v, preferred_element_type=_F32) / l1
    y = o * lax.rsqrt(jnp.mean(o * o, axis=-1, keepdims=True) + SUBLN_EPS)
    return y * gain


def _diff_kernel(lam_ref, q_ref, k_ref, v_ref, g_ref, o_ref):
    is_ctx = pl.program_id(2) < CTX_LEN // Q_TILE
    lam, gain = lam_ref[0, 0], g_ref[...]

    @pl.when(is_ctx)
    def _():
        o_ref[0] = _diff_tile(lam, q_ref[0], k_ref[0, :CTX_LEN], v_ref[0, :CTX_LEN], gain).astype(o_ref.dtype)

    @pl.when(jnp.logical_not(is_ctx))
    def _():
        o_ref[0] = _diff_tile(lam, q_ref[0], k_ref[0], v_ref[0], gain).astype(o_ref.dtype)


def _diff_attend(lam, qkv, gain):
    b, t, _ = qkv.shape
    tq = Q_TILE
    return pl.pallas_call(
        _diff_kernel,
        grid=(b, C_HEADS, t // tq),
        in_specs=[pl.BlockSpec(memory_space=pltpu.SMEM),
                  pl.BlockSpec((1, tq, LANES), lambda i, h, j: (i, j, h)),
                  pl.BlockSpec((1, t, LANES), lambda i, h, j: (i, 0, C_HEADS + h)),
                  pl.BlockSpec((1, t, LANES), lambda i, h, j: (i, 0, 2 * C_HEADS + h)),
                  _full((1, LANES))],
        out_specs=pl.BlockSpec((1, tq, LANES), lambda i, h, j: (i, j, h)),
        out_shape=jax.ShapeDtypeStruct((b, t, C_HEADS * LANES), _BF16),
        name="diff_attn", compiler_params=_params(3),
    )(lam, qkv, qkv, qkv, gain)


def _scan_chunk(h, r, v, kk, lw, cum, cum_end, a, kd, sgn):
    c = SCAN_CHUNK
    lo = _lane_lo((c, LANES))
    row = lax.broadcasted_iota(jnp.int32, (LANES, LANES), 0)
    col = lax.broadcasted_iota(jnp.int32, (LANES, LANES), 1)
    delta = ((col & (c - 1)) - (row & (c - 1))) * sgn
    before = delta < 0
    upto = delta <= 0
    eye = row == col
    zero = jnp.zeros((LANES, LANES), _F32)

    def blockdiag(x):
        z = jnp.zeros_like(x)
        return jnp.concatenate([jnp.where(lo, x, z), jnp.where(lo, z, x)], axis=0)

    p_inv = jnp.exp(-cum)
    p_end = jnp.exp(cum_end - cum)
    kka = kk * a
    a_rs = blockdiag(-kk * jnp.exp(cum - lw))
    r_rs = blockdiag(r * jnp.exp(cum))
    b_rs = blockdiag(kka * p_inv)
    k_rs = blockdiag(kd * p_inv)
    v_rs = blockdiag(v)
    bp_rs = blockdiag(kka * p_end)
    kp_rs = blockdiag(kd * p_end)

    z = _dot3(_split2(jnp.concatenate([a_rs, r_rs], axis=0)),
              _split2(jnp.concatenate([b_rs, k_rs], axis=0)), _NT)
    yield
    low = jnp.where(before, z[:LANES, :LANES], zero)
    g = jnp.where(before, z[:LANES, LANES:], zero)
    rb = jnp.where(upto, z[LANES:, :LANES], zero)
    rk = jnp.where(upto, z[LANES:, LANES:], zero)

    xs = _split2(low)
    x = _dot3(xs, xs)
    yield
    m = jnp.where(eye, 1.0, 0.0).astype(_F32) + low
    n = 2
    while 2 * n < c:
        res = _dot3(_split2(x), _split2(jnp.concatenate([m, x], axis=1)))
        m = m + res[:, :LANES]
        x = res[:, LANES:]
        n *= 2
        yield
    m = m + _dot3(_split2(x), _split2(m))
    yield
    gv = _dot3(_split2(g), _split2(v_rs))
    yield
    au = _dot3(_split2(m), _split2(jnp.concatenate([a_rs, gv], axis=1)))
    yield
    rhs = _split2(jnp.concatenate(
        [au, jnp.concatenate([zero, v_rs], axis=1)], axis=0))
    tf = _dot3(_split2(jnp.concatenate([bp_rs, kp_rs], axis=0)), rhs, _TN)
    qy = _dot3(_split2(jnp.concatenate([rb, rk], axis=1)), rhs)
    yield
    t = jnp.where(eye, jnp.broadcast_to(jnp.exp(cum_end), (LANES, LANES)), zero) + tf[:, :LANES]
    q = r_rs + qy[:, :LANES]
    out = _dot3(_split2(jnp.concatenate([q, t], axis=0)), _split2(h))
    y = out[:LANES] + qy[:, LANES:]
    h_new = out[LANES:] + tf[:, LANES:]
    return y[:c] + y[c:], h_new


def _scan_kernel(r_ref, v_ref, kk_ref, lw_ref, a_ref, kd_ref, y_ref, h_ref):
    c = SCAN_CHUNK

    @pl.when(pl.program_id(2) == 0)
    def _():
        h_ref[...] = jnp.zeros_like(h_ref)

    sgn = 1 - 2 * pl.program_id(0)
    tr = lax.broadcasted_iota(jnp.int32, (c, c), 0)
    tc = lax.broadcasted_iota(jnp.int32, (c, c), 1)
    tri = jnp.where((tc - tr) * sgn <= 0, 1.0, 0.0).astype(_BF16)
    lw = lw_ref[0, 0]
    l1 = lw.astype(_BF16)
    rem = lw - l1.astype(_F32)
    l2 = rem.astype(_BF16)
    l3 = (rem - l2.astype(_F32)).astype(_BF16)
    d = lambda w: jnp.dot(tri, w, preferred_element_type=_F32)
    cum = (d(l3) + d(l2)) + d(l1)
    cum_end = jnp.sum(lw, axis=0, keepdims=True)
    slices = [slice(p * LANES, (p + 1) * LANES) for p in range(lw.shape[1] // LANES)]
    chains = [_scan_chunk(h_ref[p], r_ref[0, :, sl], v_ref[0, :, sl], kk_ref[0, :, sl], lw[:, sl],
                          cum[:, sl], cum_end[:, sl], a_ref[0, 0, :, sl], kd_ref[0, 0, :, sl], sgn)
              for p, sl in enumerate(slices)]
    results = {}
    while len(results) < len(chains):
        for p, chain in enumerate(chains):
            try:
                next(chain)
            except StopIteration as stop:
                results[p] = stop.value
    for p, sl in enumerate(slices):
        y_ref[0, 0, :, sl], h_ref[p] = results[p]


def _delta_scan(r, v, kk, lw, a, kd):
    b, t, w = r.shape
    c = SCAN_CHUNK
    n_chunks = t // c
    n_ctx = CTX_LEN // c

    def chunk(d, j):
        rev = jnp.where(j < n_ctx, n_ctx - 1 - j, n_chunks + n_ctx - 1 - j)
        return jnp.where(d == 0, j, rev)

    shared = pl.BlockSpec((1, c, w), lambda d, i, j: (i, chunk(d, j), 0))
    per_dir = pl.BlockSpec((1, 1, c, w), lambda d, i, j: (d, i, chunk(d, j), 0))
    return pl.pallas_call(
        _scan_kernel,
        grid=(2, b, n_chunks),
        in_specs=[shared, shared, shared, per_dir, per_dir, per_dir],
        out_specs=per_dir,
        out_shape=jax.ShapeDtypeStruct((2, b, t, w), _F32),
        scratch_shapes=[pltpu.VMEM((w // LANES, LANES, LANES), _F32)],
        name="delta_scan", compiler_params=_params(3),
    )(r, v, kk, lw, a, kd)


def _post_kernel(*refs, even):
    it = iter(refs)
    if even:
        y_ref, r_ref, v_ref, kd_ref, g_ref, gn_ref = (next(it) for _ in range(6))
    att_ref, x_ref, m_ref, wo_ref, wrh_ref, wrl_ref, br_ref = (next(it) for _ in range(7))
    xo_ref, h_ref, route_ref, count_ref, carry_ref = it
    if even:
        mean_mat = _head_sum_matrix(1.0 / HEAD_DIM)
        ones_mat = _head_sum_matrix(1.0)
        parts = [att_ref[...]]
        for s in range(B_WIDTH // LANES):
            sl = slice(s * LANES, (s + 1) * LANES)
            y = y_ref[0, :, sl] + y_ref[1, :, sl]
            dev = y - _head_sum(y, mean_mat)
            var = _head_sum(dev * dev, mean_mat)
            y = dev * lax.rsqrt(var + RWKV_GN_EPS) * gn_ref[0:1, sl] + gn_ref[1:2, sl]
            r, v = r_ref[:, sl], v_ref[:, sl]
            for dr in range(2):
                y = y + _head_sum(r * kd_ref[dr, :, sl] * gn_ref[2:3, sl], ones_mat) * v
            parts.append((y * g_ref[:, sl]).astype(_BF16))
        mixed = jnp.concatenate(parts, axis=1)
    else:
        mixed = att_ref[...]
    out = jnp.dot(mixed, wo_ref[...], preferred_element_type=_F32)
    x = x_ref[...] + m_ref[0, 2:3, :] * out
    xo_ref[...] = x
    h = _rms_rows(x) * (1.0 + m_ref[0, 4:5, :]) + m_ref[0, 3:4, :]
    h_ref[...] = h
    logits = _dot3(_split2(h), (wrh_ref[...], wrl_ref[...])) + br_ref[...]
    _route(logits, route_ref, count_ref, carry_ref)


def _route(logits, route_ref, count_ref, carry_ref):
    tm = logits.shape[0]
    lane = lax.broadcasted_iota(jnp.int32, (tm, LANES), 1)
    neg = -1e30
    first_lane = lambda hit: jnp.min(jnp.where(hit, lane, LANES), axis=-1, keepdims=True)

    is_grp = lane < N_GROUPS
    grp = jnp.where(is_grp, logits, neg)
    g_max = jnp.max(grp, axis=-1, keepdims=True)
    g_sel = first_lane(grp == g_max)
    grp_w = 1.0 / jnp.sum(jnp.where(is_grp, jnp.exp(grp - g_max), 0.0), axis=-1, keepdims=True)

    base = N_GROUPS + EXPERTS_PER_GROUP * g_sel
    in_grp = (lane >= base) & (lane < base + EXPERTS_PER_GROUP)
    ex = jnp.where(in_grp, logits, neg)
    e_max = jnp.max(ex, axis=-1, keepdims=True)
    pe = jnp.where(in_grp, jnp.exp(ex - e_max), 0.0)
    prob = pe / jnp.sum(pe, axis=-1, keepdims=True)
    p1 = jnp.max(prob, axis=-1, keepdims=True)
    l1 = first_lane(in_grp & (prob == p1))
    rest = jnp.where(in_grp & (lane != l1), prob, -1.0)
    p2 = jnp.max(rest, axis=-1, keepdims=True)
    l2 = first_lane(rest == p2)
    scale = grp_w / (p1 + p2)
    e1, e2 = l1 - N_GROUPS, l2 - N_GROUPS

    @pl.when(pl.program_id(0) == 0)
    def _():
        carry_ref[...] = jnp.zeros_like(carry_ref)

    oh1 = jnp.where(lane == e1, 1.0, 0.0)
    oh2 = jnp.where(lane == e2, 1.0, 0.0)
    rr = lax.broadcasted_iota(jnp.int32, (tm, tm), 0)
    cc = lax.broadcasted_iota(jnp.int32, (tm, tm), 1)
    tri = jnp.where(cc < rr, 1.0, 0.0).astype(_BF16)
    before = lambda oh: jnp.dot(tri, oh.astype(_BF16), preferred_element_type=_F32)
    carry = carry_ref[...]
    tot1 = jnp.sum(oh1, axis=0, keepdims=True)
    rank1 = jnp.sum(oh1 * (before(oh1) + carry), axis=-1, keepdims=True)
    rank2 = jnp.sum(oh2 * (before(oh2) + (carry + tot1)), axis=-1, keepdims=True)
    carry = carry + tot1 + jnp.sum(oh2, axis=0, keepdims=True)
    carry_ref[...] = carry
    count_ref[...] = carry

    out = jnp.zeros((tm, LANES), _F32)
    for i, val in enumerate((p1 * scale, p2 * scale, e1.astype(_F32), e2.astype(_F32), rank1, rank2)):
        out = jnp.where(lane == i, val, out)
    route_ref[...] = out


def _mixer_out(att, x, mods, w_out, w_router, b_router, rwkv=None):
    n, d = x.shape
    tm = ROW_TILE
    row = lambda width: pl.BlockSpec((tm, width), lambda i: (i, 0))
    both = pl.BlockSpec((2, tm, B_WIDTH), lambda i: (0, i, 0))
    args, specs = [], []
    if rwkv is not None:
        y, r, v, kd, g, gn = rwkv
        args += [y, r, v, kd, g, gn]
        specs += [both, row(B_WIDTH), row(B_WIDTH), both, row(B_WIDTH), _full(gn.shape)]
    wr_hi, wr_lo = _split2(w_router)
    args += [att, x, mods, w_out, wr_hi, wr_lo, b_router]
    specs += [row(att.shape[1]), row(d), pl.BlockSpec((1, 6, d), lambda i: (_mods_index(i), 0, 0)),
              _full(w_out.shape), _full(wr_hi.shape), _full(wr_lo.shape), _full(b_router.shape)]
    return pl.pallas_call(
        functools.partial(_post_kernel, even=rwkv is not None),
        grid=(n // tm,), in_specs=specs,
        out_specs=[row(d), row(d), row(LANES), _full((1, LANES))],
        out_shape=[jax.ShapeDtypeStruct((n, d), _F32), jax.ShapeDtypeStruct((n, d), _F32),
                   jax.ShapeDtypeStruct((n, LANES), _F32), jax.ShapeDtypeStruct((1, LANES), _F32)],
        scratch_shapes=[pltpu.VMEM((1, LANES), _F32)],
        name="mixer_out_even" if rwkv is not None else "mixer_out_odd", compiler_params=_params(1),
    )(*args)


def _moe_kernel(be_ref, nv_ref, x_ref, wg_ref, wu_ref, wd_ref, o_ref):
    del be_ref

    @pl.when(pl.program_id(0) < nv_ref[0])
    def _():
        x = x_ref[...].astype(_BF16)
        gate = jnp.dot(x, wg_ref[0], preferred_element_type=_F32)
        up = jnp.dot(x, wu_ref[0], preferred_element_type=_F32)
        hid = (gate * jax.nn.sigmoid(gate) * up).astype(_BF16)
        o_ref[...] = jnp.dot(hid, wd_ref[0], preferred_element_type=_F32)

    @pl.when(pl.program_id(0) >= nv_ref[0])
    def _():
        o_ref[...] = jnp.zeros_like(o_ref)


def _moe_blocks(block_expert, n_valid, xs, wg, wu, wd):
    rows, d = xs.shape
    nb = rows // EXPERT_BLOCK
    grid_spec = pltpu.PrefetchScalarGridSpec(
        num_scalar_prefetch=2,
        grid=(nb,),
        in_specs=[pl.BlockSpec((EXPERT_BLOCK, d), lambda i, be, nv: (i, 0)),
                  pl.BlockSpec((1, d, EXPERT_HIDDEN), lambda i, be, nv: (be[i], 0, 0)),
                  pl.BlockSpec((1, d, EXPERT_HIDDEN), lambda i, be, nv: (be[i], 0, 0)),
                  pl.BlockSpec((1, EXPERT_HIDDEN, d), lambda i, be, nv: (be[i], 0, 0))],
        out_specs=pl.BlockSpec((EXPERT_BLOCK, d), lambda i, be, nv: (i, 0)),
    )
    return pl.pallas_call(
        _moe_kernel,
        grid_spec=grid_spec,
        out_shape=jax.ShapeDtypeStruct((rows, d), _F32),
        name="moe_ffn", compiler_params=_params(1),
    )(block_expert, n_valid, xs, wg, wu, wd)


def _moe(h, route, counts, w_gate, w_up, w_down):
    n, d = h.shape
    eid = route[:, 2:4].astype(jnp.int32)
    rank = route[:, 4:6].astype(jnp.int32)
    cnt = counts[0, :N_EXPERTS].astype(jnp.int32)
    padded = (cnt + EXPERT_BLOCK - 1) // EXPERT_BLOCK * EXPERT_BLOCK
    pad_end = jnp.cumsum(padded)
    slot = (pad_end - padded)[eid] + rank
    n_blocks = -(-(n * TOP_K + N_EXPERTS * (EXPERT_BLOCK - 1)) // EXPERT_BLOCK)
    tok = jnp.broadcast_to(jnp.arange(n, dtype=jnp.int32)[:, None], (n, TOP_K))
    slot_tok = jnp.zeros((n_blocks * EXPERT_BLOCK,), jnp.int32).at[slot.reshape(-1)].set(tok.reshape(-1))
    block_start = jnp.arange(n_blocks, dtype=jnp.int32) * EXPERT_BLOCK
    block_expert = jnp.minimum(jnp.searchsorted(pad_end, block_start, side='right'),
                               N_EXPERTS - 1).astype(jnp.int32)
    n_valid = (pad_end[-1:] // EXPERT_BLOCK).astype(jnp.int32)
    yb = _moe_blocks(block_expert, n_valid, h[slot_tok], w_gate.astype(_BF16), w_up.astype(_BF16),
                     w_down.astype(_BF16))
    return yb[slot[:, 0]], yb[slot[:, 1]]


def _final_kernel(x_ref, y1_ref, y2_ref, g_ref, m_ref, gain_ref, o_ref):
    g = g_ref[...]
    x = x_ref[...] + m_ref[0, 5:6, :] * (g[:, 0:1] * y1_ref[...] + g[:, 1:2] * y2_ref[...])
    o_ref[...] = _rms_rows(x) * gain_ref[...]


def _final(x, y1, y2, route, mods, gain, bsz, n_lat):
    d = x.shape[1]
    tm = ROW_TILE
    lat_tiles = n_lat // tm
    tiles = (CTX_LEN + n_lat) // tm
    src = lambda i: (i // lat_tiles) * tiles + CTX_LEN // tm + i % lat_tiles
    row = lambda width: pl.BlockSpec((tm, width), lambda i: (src(i), 0))
    return pl.pallas_call(
        _final_kernel,
        grid=(bsz * lat_tiles,),
        in_specs=[row(d), row(d), row(d), row(LANES),
                  pl.BlockSpec((1, 6, d), lambda i: ((i // lat_tiles) * 2 + 1, 0, 0)), _full((1, d))],
        out_specs=pl.BlockSpec((tm, d), lambda i: (i, 0)),
        out_shape=jax.ShapeDtypeStruct((bsz * n_lat, d), _F32),
        name="final_norm", compiler_params=_params(1),
    )(x, y1, y2, route, mods, gain.reshape(1, d))


def _rope_tables(n_lat):
    nf = HEAD_DIM // 4
    inv = ROPE_THETA ** (-jnp.arange(nf, dtype=_F32) / nf)
    rows = n_lat // GRID_W
    r_ang = jnp.repeat(jnp.arange(rows, dtype=_F32), GRID_W)[:, None] * inv
    c_ang = jnp.tile(jnp.arange(GRID_W, dtype=_F32), rows)[:, None] * inv
    cos = jnp.concatenate([jnp.cos(r_ang)] * 2 + [jnp.cos(c_ang)] * 2, axis=-1)
    sin = jnp.concatenate([-jnp.sin(r_ang), jnp.sin(r_ang), -jnp.sin(c_ang), jnp.sin(c_ang)], axis=-1)
    cos = jnp.concatenate([jnp.ones((CTX_LEN, HEAD_DIM), _F32), cos], axis=0)
    sin = jnp.concatenate([jnp.zeros((CTX_LEN, HEAD_DIM), _F32), sin], axis=0)
    return jnp.tile(cos, (1, 2)), jnp.tile(sin, (1, 2))


def kernel(x, c, ctx, c_ctx, mod_w, mod_b, ev_w_in, ev_w_out, ev_q_gain, ev_k_gain, ev_mu_prev, ev_mu_next, ev_w0, ev_w2, ev_a0, ev_a2, ev_g2, ev_k_k, ev_k_a, ev_r_k, ev_gn_w, ev_gn_b, ev_v0, ev_v1, ev_v2, od_w_in, od_w_out, od_lq1, od_lk1, od_lq2, od_lk2, od_subln, moe_w_grp, moe_b_grp, moe_w_rt, moe_b_rt, moe_w_gate, moe_w_up, moe_w_down, final_gain):
    bsz, n_lat, d = x.shape
    t = CTX_LEN + n_lat
    n = bsz * t
    cos, sin = _rope_tables(n_lat)
    cond = jax.nn.silu(jnp.concatenate([c, c_ctx[None, :], jnp.zeros((16 - bsz - 1, d), _F32)], axis=0))
    xs = jnp.concatenate([ctx, x], axis=1).reshape(n, d)
    v_first = None
    moe = None
    for layer in range(DEPTH):
        m = _mm(cond, mod_w[layer], tn=1536, precision=_HI) + mod_b[layer]
        m = m.reshape(16, 6, d)
        mods = jnp.stack([jnp.broadcast_to(m[bsz], (bsz, 6, d)), m[:bsz]], axis=1).reshape(2 * bsz, 6, d)
        w_router = jnp.concatenate(
            [moe_w_grp[layer], moe_w_rt[layer], jnp.zeros((d, LANES - N_GROUPS - N_EXPERTS), _F32)], axis=1)
        b_router = jnp.concatenate(
            [moe_b_grp[layer], moe_b_rt[layer], jnp.zeros((LANES - N_GROUPS - N_EXPERTS,), _F32)])[None, :]
        if layer % 2 == 0:
            e = layer // 2
            p = {'q_gain': ev_q_gain[e], 'k_gain': ev_k_gain[e], 'mu_prev': ev_mu_prev[e],
                 'mu_next': ev_mu_next[e], 'w0': ev_w0[e], 'w2': ev_w2[e], 'a0': ev_a0[e],
                 'a2': ev_a2[e], 'g2': ev_g2[e], 'k_k': ev_k_k[e], 'k_a': ev_k_a[e]}
            if e > 0:
                p['v0'], p['v1'], p['v2'] = ev_v0[e - 1], ev_v1[e - 1], ev_v2[e - 1]
            proj, xs = _project(xs, mods, ev_w_in[e].astype(_BF16), moe=moe)
            q, kk2, vv2, r, v, kkn, lw, a, kd, g = _prepare_even(proj, cos, sin, p, v_first)
            if v_first is None:
                v_first = v
            att = _gqa_attend(q.reshape(bsz, t, -1), kk2.reshape(bsz, t, -1), vv2.reshape(bsz, t, -1))
            seq = lambda z: z.reshape(*z.shape[:-2], bsz, t, B_WIDTH)
            y = _delta_scan(seq(r), seq(v), seq(kkn), seq(lw), seq(a), seq(kd)).reshape(2, n, B_WIDTH)
            gn = jnp.stack([ev_gn_w[e], ev_gn_b[e], ev_r_k[e].reshape(B_WIDTH)])
            xs, h, route, counts = _mixer_out(att.reshape(n, -1), xs, mods, ev_w_out[e].astype(_BF16),
                                       w_router, b_router, rwkv=(y, r, v, kd, g, gn))
        else:
            o = layer // 2
            lambda_init = 0.8 - 0.6 * math.exp(-0.3 * layer)
            qkv, xs = _project(xs, mods, od_w_in[o].astype(_BF16), moe=moe, rope=(cos, sin))
            lam = (jnp.exp(jnp.sum(od_lq1[o] * od_lk1[o])) - jnp.exp(jnp.sum(od_lq2[o] * od_lk2[o]))
                   + lambda_init).reshape(1, 1)
            gain = (od_subln[o] * (1.0 - lambda_init)).reshape(1, LANES)
            att = _diff_attend(lam, qkv.reshape(bsz, t, -1), gain)
            xs, h, route, counts = _mixer_out(att.reshape(n, -1), xs, mods, od_w_out[o].astype(_BF16),
                                       w_router, b_router)
        y1, y2 = _moe(h, route, counts, moe_w_gate[layer], moe_w_up[layer], moe_w_down[layer])
        moe = (y1, y2, route, mods)
    return _final(xs, y1, y2, route, mods, final_gain, bsz, n_lat).reshape(bsz, n_lat, d)
```

```python
import functools
import math

import jax
import jax.numpy as jnp
from jax import lax
from jax.experimental import pallas as pl
from jax.experimental.pallas import tpu as pltpu

D_MODEL = 1024
DEPTH = 4
SEQ = 2048
GRID_W = 64
CTX_LEN = 256
HEAD_DIM = 64
ROPE_THETA = 10000.0
RMS_EPS = 1e-6
SUBLN_EPS = 1e-5

A_HEADS = 8
A_KV_HEADS = 2
A_WIDTH = 512
A_KV_WIDTH = 128
A_COLS = 768

B_HEADS = 8
B_WIDTH = 512
B_COLS = 1792
RWKV_W_LORA = 64
RWKV_A_LORA = 64
RWKV_V_LORA = 32
RWKV_G_LORA = 128
RWKV_GN_EPS = 64e-5
DECAY_SCALE = math.exp(-0.5)

C_HEADS = 8
C_QK = 1024

N_GROUPS = 4
EXPERTS_PER_GROUP = 8
N_EXPERTS = 32
TOP_K = 2
EXPERT_HIDDEN = 512
EXPERT_BLOCK = 128

LANES = 128
SCAN_CHUNK = 64
ROW_TILE = 256
Q_TILE = 128
VMEM_LIMIT = 56 * 1024 * 1024

_HI = lax.Precision.HIGHEST
_F32 = jnp.float32
_BF16 = jnp.bfloat16

_NN = (((1,), (0,)), ((), ()))
_NT = (((1,), (1,)), ((), ()))
_TN = (((0,), (0,)), ((), ()))


def _params(n_axes):
    return pltpu.CompilerParams(dimension_semantics=("arbitrary",) * n_axes,
                                vmem_limit_bytes=VMEM_LIMIT)


def _full(shape):
    return pl.BlockSpec(shape, lambda *_: (0,) * len(shape))


def _split2(x):
    hi = x.astype(_BF16)
    return hi, (x - hi.astype(_F32)).astype(_BF16)


def _dot3(a, b, dims=_NN):
    d = lambda u, w: lax.dot_general(u, w, dims, preferred_element_type=_F32)
    return (d(a[1], b[0]) + d(a[0], b[1])) + d(a[0], b[0])


def _lane_lo(shape):
    return lax.broadcasted_iota(jnp.int32, shape, len(shape) - 1) < HEAD_DIM


def _head_sum_matrix(value):
    row = lax.broadcasted_iota(jnp.int32, (LANES, LANES), 0)
    col = lax.broadcasted_iota(jnp.int32, (LANES, LANES), 1)
    return jnp.where((row // HEAD_DIM) == (col // HEAD_DIM), value, 0.0).astype(_BF16)


def _head_sum(x, mat):
    hi, lo = _split2(x)
    return jnp.dot(lo, mat, preferred_element_type=_F32) + jnp.dot(hi, mat, preferred_element_type=_F32)


def _rms_rows(x, eps=RMS_EPS):
    return x * lax.rsqrt(jnp.mean(x * x, axis=-1, keepdims=True) + eps)


def _rope_slab(x, cos, sin):
    lane = lax.broadcasted_iota(jnp.int32, x.shape, 1)
    partner = jnp.where((lane & 16) == 0, pltpu.roll(x, LANES - 16, axis=1), pltpu.roll(x, 16, axis=1))
    return x * cos + partner * sin


def _mods_index(i):
    tiles = (CTX_LEN + SEQ) // ROW_TILE
    return (i // tiles) * 2 + jnp.minimum(i % tiles, 1)


def _mm_kernel(a_ref, w_ref, o_ref, *, precision):
    o_ref[...] = jnp.dot(a_ref[...], w_ref[...], precision=precision,
                         preferred_element_type=_F32).astype(o_ref.dtype)


def _mm(a, w, *, tn, precision=None):
    m, k = a.shape
    n = w.shape[1]
    return pl.pallas_call(
        functools.partial(_mm_kernel, precision=precision),
        grid=(n // tn,),
        in_specs=[pl.BlockSpec((m, k), lambda j: (0, 0)),
                  pl.BlockSpec((k, tn), lambda j: (0, j))],
        out_specs=pl.BlockSpec((m, tn), lambda j: (0, j)),
        out_shape=jax.ShapeDtypeStruct((m, n), _F32),
        name=f"mm_{k}x{n}",
        compiler_params=_params(1),
    )(a, w)


def _proj_kernel(*refs, combine, rope):
    it = iter(refs)
    x_ref, m_ref = next(it), next(it)
    if combine:
        y1_ref, y2_ref, g_ref, mp_ref = (next(it) for _ in range(4))
    w_ref = next(it)
    if rope:
        cos_ref, sin_ref = next(it), next(it)
    o_ref = next(it)
    x = x_ref[...]
    if combine:
        xo_ref = next(it)
        g = g_ref[...]
        x = x + mp_ref[0, 5:6, :] * (g[:, 0:1] * y1_ref[...] + g[:, 1:2] * y2_ref[...])
        xo_ref[...] = x
    h = _rms_rows(x) * (1.0 + m_ref[0, 1:2, :]) + m_ref[0, 0:1, :]
    p = jnp.dot(h.astype(_BF16), w_ref[...], preferred_element_type=_F32)
    if rope:
        cos, sin = cos_ref[...], sin_ref[...]
        for s in range(2 * C_QK // LANES):
            sl = slice(s * LANES, (s + 1) * LANES)
            slab = _rope_slab(p[:, sl], cos, sin)
            if s < C_QK // LANES:
                slab = slab * HEAD_DIM ** -0.5
            o_ref[:, sl] = slab.astype(o_ref.dtype)
        o_ref[:, 2 * C_QK:] = p[:, 2 * C_QK:].astype(o_ref.dtype)
    else:
        o_ref[...] = p


def _project(x, mods, w, *, moe=None, rope=None):
    n, d = x.shape
    cols = w.shape[1]
    tm = ROW_TILE
    tiles = (CTX_LEN + SEQ) // tm
    row = lambda width: pl.BlockSpec((tm, width), lambda i: (i, 0))
    mod_spec = pl.BlockSpec((1, 6, d), lambda i: (_mods_index(i), 0, 0))
    args, specs = [x, mods], [row(d), mod_spec]
    if moe is not None:
        args += list(moe)
        specs += [row(d), row(d), row(LANES), mod_spec]
    args.append(w)
    specs.append(_full(w.shape))
    if rope is not None:
        args += list(rope)
        specs += [pl.BlockSpec((tm, LANES), lambda i: (i % tiles, 0))] * 2
    out_shape = [jax.ShapeDtypeStruct((n, cols), _BF16 if rope is not None else _F32)]
    out_specs = [row(cols)]
    if moe is not None:
        out_shape.append(jax.ShapeDtypeStruct((n, d), _F32))
        out_specs.append(row(d))
    out = pl.pallas_call(
        functools.partial(_proj_kernel, combine=moe is not None, rope=rope is not None),
        grid=(n // tm,), in_specs=specs, out_specs=out_specs, out_shape=out_shape,
        name=f"project_{cols}", compiler_params=_params(1),
    )(*args)
    return out if moe is not None else (out[0], x)


def _prep_kernel(*refs, has_v_first):
    it = iter(refs)
    p_ref, pp_ref, pn_ref, cos_ref, sin_ref = (next(it) for _ in range(5))
    gain_ref, mu_ref, wa0_ref, kv_ref, wl_ref, g2_ref = (next(it) for _ in range(6))
    if has_v_first:
        v1_ref, v2_ref, vf_ref = next(it), next(it), next(it)
    q_ref, kk2_ref, vv2_ref, r_ref, v_ref, kkn_ref, lw_ref, a_ref, kd_ref, g_ref = it

    tiles = (CTX_LEN + SEQ) // ROW_TILE
    j = pl.program_id(0) % tiles
    p = p_ref[...]
    cos, sin = cos_ref[...], sin_ref[...]
    mean_mat = _head_sum_matrix(1.0 / HEAD_DIM)
    ones_mat = _head_sum_matrix(1.0)
    lo = _lane_lo((ROW_TILE, LANES))

    def normed(x, gain):
        return x * lax.rsqrt(_head_sum(x * x, mean_mat) + RMS_EPS) * gain

    for s in range(A_WIDTH // LANES):
        sl = slice(s * LANES, (s + 1) * LANES)
        q = _rope_slab(normed(p[:, sl], gain_ref[0:1, :]), cos, sin)
        q_ref[:, sl] = (q * HEAD_DIM ** -0.5).astype(_BF16)
    k = _rope_slab(normed(p[:, A_WIDTH:A_WIDTH + LANES], gain_ref[1:2, :]), cos, sin)
    v = p[:, A_WIDTH + LANES:A_COLS]
    for src, dst in ((k, kk2_ref), (v, vv2_ref)):
        swapped = pltpu.roll(src, HEAD_DIM, axis=1)
        dst[:, :LANES] = jnp.where(lo, src, swapped).astype(_BF16)
        dst[:, LANES:] = jnp.where(lo, swapped, src).astype(_BF16)

    u = p[:, A_COLS:]
    rid = lax.broadcasted_iota(jnp.int32, (ROW_TILE, 1), 0)
    has_prev = jnp.where(j >= 2, 1.0, 0.0)
    has_next = jnp.where((j >= 1) & (j < tiles - 1), 1.0, 0.0)
    first = pp_ref[7:8, A_COLS:] * has_prev
    last = pn_ref[0:1, A_COLS:] * has_next
    prev = jnp.where(rid == 0, first, pltpu.roll(u, 1, axis=0))
    nxt = jnp.where(rid == ROW_TILE - 1, last, pltpu.roll(u, ROW_TILE - 1, axis=0))
    u = u + mu_ref[0:1, :] * (prev - u) + mu_ref[1:2, :] * (nxt - u)
    r, k, v = u[:, :B_WIDTH], u[:, B_WIDTH:2 * B_WIDTH], u[:, 2 * B_WIDTH:3 * B_WIDTH]
    wa, glr = u[:, 3 * B_WIDTH:3 * B_WIDTH + LANES], u[:, 3 * B_WIDTH + LANES:]
    if has_v_first:
        mid = jnp.dot(v.astype(_BF16), v1_ref[...], preferred_element_type=_F32)
        lora = jnp.dot(mid.astype(_BF16), v2_ref[...], preferred_element_type=_F32)
        v = v + (vf_ref[...] - v) * jax.nn.sigmoid(kv_ref[2:3, :] + lora)
    r_ref[...] = r
    v_ref[...] = v
    kks = k * kv_ref[0:1, :]
    for s in range(B_WIDTH // LANES):
        sl = slice(s * LANES, (s + 1) * LANES)
        x = kks[:, sl]
        kkn_ref[:, sl] = x * lax.rsqrt(jnp.maximum(_head_sum(x * x, ones_mat), 1e-24))
    xw = jnp.where(lo, jnp.tanh(wa), wa).astype(_BF16)
    for dr in range(2):
        z = jnp.dot(xw, wl_ref[dr], preferred_element_type=_F32) + wa0_ref[dr:dr + 1, :]
        a = jax.nn.sigmoid(z[:, B_WIDTH:])
        lw_ref[dr] = -DECAY_SCALE * jax.nn.sigmoid(z[:, :B_WIDTH])
        a_ref[dr] = a
        kd_ref[dr] = k * (1.0 + (a - 1.0) * kv_ref[1:2, :])
    g_ref[...] = jnp.dot(jax.nn.sigmoid(glr).astype(_BF16), g2_ref[...], preferred_element_type=_F32)


def _prepare_even(proj, cos, sin, p, v_first):
    n, cols = proj.shape
    tm = ROW_TILE
    tiles = (CTX_LEN + SEQ) // tm
    halo = 8
    per_tile = tm // halo
    row = lambda width: pl.BlockSpec((tm, width), lambda i: (i, 0))
    both = pl.BlockSpec((2, tm, B_WIDTH), lambda i: (0, i, 0))
    tab = pl.BlockSpec((tm, LANES), lambda i: (i % tiles, 0))
    pad = lambda w, rows, width: jnp.zeros((rows, width), _BF16).at[:w.shape[0], :w.shape[1]].set(w.astype(_BF16))
    gains = jnp.stack([jnp.tile(p['q_gain'], 2), jnp.tile(p['k_gain'], 2)])
    mu = jnp.stack([p['mu_prev'], p['mu_next']])
    wa0 = jnp.concatenate([p['w0'], p['a0']], axis=1)
    kvec = jnp.stack([p['k_k'], p['k_a'], p.get('v0', jnp.zeros((B_WIDTH,), _F32))])
    zero = jnp.zeros((RWKV_W_LORA, B_WIDTH), _F32)
    wl = jnp.stack([jnp.concatenate([jnp.concatenate([p['w2'][dr], zero], axis=1),
                                     jnp.concatenate([zero, p['a2'][dr]], axis=1)], axis=0)
                    for dr in range(2)]).astype(_BF16)
    args = [proj, proj, proj, cos, sin, gains, mu, wa0, kvec, wl, p['g2'].astype(_BF16)]
    specs = [row(cols),
             pl.BlockSpec((halo, cols), lambda i: (jnp.maximum(i * per_tile - 1, 0), 0)),
             pl.BlockSpec((halo, cols), lambda i: (jnp.minimum((i + 1) * per_tile, n // halo - 1), 0)),
             tab, tab, _full(gains.shape), _full(mu.shape), _full(wa0.shape), _full(kvec.shape),
             _full(wl.shape), _full(p['g2'].shape)]
    if v_first is not None:
        v1 = pad(p['v1'], B_WIDTH, LANES)
        v2 = pad(p['v2'], LANES, B_WIDTH)
        args += [v1, v2, v_first]
        specs += [_full(v1.shape), _full(v2.shape), row(B_WIDTH)]
    f32 = lambda *shape: jax.ShapeDtypeStruct(shape, _F32)
    bf16 = lambda *shape: jax.ShapeDtypeStruct(shape, _BF16)
    out_shape = [bf16(n, A_WIDTH), bf16(n, 2 * LANES), bf16(n, 2 * LANES),
                 f32(n, B_WIDTH), f32(n, B_WIDTH), f32(n, B_WIDTH),
                 f32(2, n, B_WIDTH), f32(2, n, B_WIDTH), f32(2, n, B_WIDTH), f32(n, B_WIDTH)]
    out_specs = [row(A_WIDTH), row(2 * LANES), row(2 * LANES), row(B_WIDTH), row(B_WIDTH), row(B_WIDTH),
                 both, both, both, row(B_WIDTH)]
    return pl.pallas_call(
        functools.partial(_prep_kernel, has_v_first=v_first is not None),
        grid=(n // tm,), in_specs=specs, out_specs=out_specs, out_shape=out_shape,
        name="prepare_even", compiler_params=_params(1),
    )(*args)


def _split_heads(q2):
    lo = _lane_lo(q2.shape)
    zero = jnp.zeros_like(q2)
    return jnp.concatenate([jnp.where(lo, q2, zero), jnp.where(lo, zero, q2)], axis=0)


def _scores(qs, k):
    return lax.dot_general(qs, k, _NT, preferred_element_type=_F32)


def _interleave(chains):
    results = {}
    while len(results) < len(chains):
        for p, chain in enumerate(chains):
            try:
                next(chain)
            except StopIteration as stop:
                results[p] = stop.value
    return [results[p] for p in range(len(chains))]


def _gqa_tile(q, k, v):
    tq = q.shape[0]
    qs = jnp.concatenate([_split_heads(q[:, :LANES]), _split_heads(q[:, LANES:])], axis=0)
    s = _scores(qs, k)
    yield
    p = jnp.exp(s - jnp.max(s, axis=-1, keepdims=True))
    l = jnp.sum(p, axis=-1, keepdims=True)
    p = p.astype(_BF16)
    yield
    o = jnp.dot(p, v, preferred_element_type=_F32) / l
    lo = _lane_lo((tq, LANES))
    return jnp.concatenate([jnp.where(lo, o[:tq], o[tq:2 * tq]),
                            jnp.where(lo, o[2 * tq:3 * tq], o[3 * tq:])], axis=1)


def _attend_halves(tile_fn, q_ref, o_ref):
    halves = [slice(i * Q_TILE, (i + 1) * Q_TILE) for i in range(q_ref.shape[1] // Q_TILE)]
    outs = _interleave([tile_fn(q_ref[0, sl]) for sl in halves])
    for sl, o in zip(halves, outs):
        o_ref[0, sl] = o.astype(o_ref.dtype)


def _gqa_kernel(q_ref, k_ref, v_ref, o_ref):
    is_ctx = pl.program_id(2) < CTX_LEN // q_ref.shape[1]

    @pl.when(is_ctx)
    def _():
        _attend_halves(lambda q: _gqa_tile(q, k_ref[0, :CTX_LEN], v_ref[0, :CTX_LEN]), q_ref, o_ref)

    @pl.when(jnp.logical_not(is_ctx))
    def _():
        _attend_halves(lambda q: _gqa_tile(q, k_ref[0], v_ref[0]), q_ref, o_ref)


def _gqa_attend(q, kk, vv):
    b, t, _ = q.shape
    tq = 2 * Q_TILE
    kv_spec = pl.BlockSpec((1, t, LANES), lambda i, g, j: (i, 0, g))
    return pl.pallas_call(
        _gqa_kernel,
        grid=(b, A_KV_HEADS, t // tq),
        in_specs=[pl.BlockSpec((1, tq, 2 * LANES), lambda i, g, j: (i, j, g)), kv_spec, kv_spec],
        out_specs=pl.BlockSpec((1, tq, 2 * LANES), lambda i, g, j: (i, j, g)),
        out_shape=jax.ShapeDtypeStruct((b, t, A_WIDTH), _BF16),
        name="gqa_attn", compiler_params=_params(3),
    )(q, kk, vv)


def _diff_tile(lam, q, k, v, gain):
    tq = q.shape[0]
    s = _scores(_split_heads(q), k)
    yield
    p = jnp.exp(s - jnp.max(s, axis=-1, keepdims=True))
    l = jnp.sum(p, axis=-1, keepdims=True)
    l1, l2 = l[:tq], l[tq:]
    pr = (p[:tq] - (lam * l1 / l2) * p[tq:]).astype(_BF16)
    yield
    o = jnp.dot(pr, v, preferred_element_type=_F32) / l1
    y = o * lax.rsqrt(jnp.mean(o * o, axis=-1, keepdims=True) + SUBLN_EPS)
    return y * gain


def _diff_kernel(lam_ref, q_ref, k_ref, v_ref, g_ref, o_ref):
    is_ctx = pl.program_id(2) < CTX_LEN // q_ref.shape[1]
    lam, gain = lam_ref[0, 0], g_ref[...]

    @pl.when(is_ctx)
    def _():
        _attend_halves(lambda q: _diff_tile(lam, q, k_ref[0, :CTX_LEN], v_ref[0, :CTX_LEN], gain),
                       q_ref, o_ref)

    @pl.when(jnp.logical_not(is_ctx))
    def _():
        _attend_halves(lambda q: _diff_tile(lam, q, k_ref[0], v_ref[0], gain), q_ref, o_ref)


def _diff_attend(lam, qkv, gain):
    b, t, _ = qkv.shape
    tq = 2 * Q_TILE
    return pl.pallas_call(
        _diff_kernel,
        grid=(b, C_HEADS, t // tq),
        in_specs=[pl.BlockSpec(memory_space=pltpu.SMEM),
                  pl.BlockSpec((1, tq, LANES), lambda i, h, j: (i, j, h)),
                  pl.BlockSpec((1, t, LANES), lambda i, h, j: (i, 0, C_HEADS + h)),
                  pl.BlockSpec((1, t, LANES), lambda i, h, j: (i, 0, 2 * C_HEADS + h)),
                  _full((1, LANES))],
        out_specs=pl.BlockSpec((1, tq, LANES), lambda i, h, j: (i, j, h)),
        out_shape=jax.ShapeDtypeStruct((b, t, C_HEADS * LANES), _BF16),
        name="diff_attn", compiler_params=_params(3),
    )(lam, qkv, qkv, qkv, gain)


def _scan_chunk(h, r, v, kk, lw, cum, cum_end, a, kd, sgn):
    c = SCAN_CHUNK
    lo = _lane_lo((c, LANES))
    row = lax.broadcasted_iota(jnp.int32, (LANES, LANES), 0)
    col = lax.broadcasted_iota(jnp.int32, (LANES, LANES), 1)
    delta = ((col & (c - 1)) - (row & (c - 1))) * sgn
    before = delta < 0
    upto = delta <= 0
    eye = row == col
    zero = jnp.zeros((LANES, LANES), _F32)

    def blockdiag(x):
        z = jnp.zeros_like(x)
        return jnp.concatenate([jnp.where(lo, x, z), jnp.where(lo, z, x)], axis=0)

    p_inv = jnp.exp(-cum)
    p_end = jnp.exp(cum_end - cum)
    kka = kk * a
    a_rs = blockdiag(-kk * jnp.exp(cum - lw))
    r_rs = blockdiag(r * jnp.exp(cum))
    b_rs = blockdiag(kka * p_inv)
    k_rs = blockdiag(kd * p_inv)
    v_rs = blockdiag(v)
    bp_rs = blockdiag(kka * p_end)
    kp_rs = blockdiag(kd * p_end)

    z = _dot3(_split2(jnp.concatenate([a_rs, r_rs], axis=0)),
              _split2(jnp.concatenate([b_rs, k_rs], axis=0)), _NT)
    yield
    low = jnp.where(before, z[:LANES, :LANES], zero)
    g = jnp.where(before, z[:LANES, LANES:], zero)
    rb = jnp.where(upto, z[LANES:, :LANES], zero)
    rk = jnp.where(upto, z[LANES:, LANES:], zero)

    xs = _split2(low)
    x = _dot3(xs, xs)
    yield
    m = jnp.where(eye, 1.0, 0.0).astype(_F32) + low
    n = 2
    while 2 * n < c:
        res = _dot3(_split2(x), _split2(jnp.concatenate([m, x], axis=1)))
        m = m + res[:, :LANES]
        x = res[:, LANES:]
        n *= 2
        yield
    m = m + _dot3(_split2(x), _split2(m))
    yield
    gv = _dot3(_split2(g), _split2(v_rs))
    yield
    au = _dot3(_split2(m), _split2(jnp.concatenate([a_rs, gv], axis=1)))
    yield
    rhs = _split2(jnp.concatenate(
        [au, jnp.concatenate([zero, v_rs], axis=1)], axis=0))
    tf = _dot3(_split2(jnp.concatenate([bp_rs, kp_rs], axis=0)), rhs, _TN)
    qy = _dot3(_split2(jnp.concatenate([rb, rk], axis=1)), rhs)
    yield
    t = jnp.where(eye, jnp.broadcast_to(jnp.exp(cum_end), (LANES, LANES)), zero) + tf[:, :LANES]
    q = r_rs + qy[:, :LANES]
    out = _dot3(_split2(jnp.concatenate([q, t], axis=0)), _split2(h))
    y = out[:LANES] + qy[:, LANES:]
    h_new = out[LANES:] + tf[:, LANES:]
    return y[:c] + y[c:], h_new


def _scan_kernel(r_ref, v_ref, kk_ref, lw_ref, a_ref, kd_ref, y_ref, h_ref):
    c = SCAN_CHUNK

    @pl.when(pl.program_id(2) == 0)
    def _():
        h_ref[...] = jnp.zeros_like(h_ref)

    sgn = 1 - 2 * pl.program_id(0)
    tr = lax.broadcasted_iota(jnp.int32, (c, c), 0)
    tc = lax.broadcasted_iota(jnp.int32, (c, c), 1)
    tri = jnp.where((tc - tr) * sgn <= 0, 1.0, 0.0).astype(_BF16)
    lw = lw_ref[0, 0]
    l1 = lw.astype(_BF16)
    rem = lw - l1.astype(_F32)
    l2 = rem.astype(_BF16)
    l3 = (rem - l2.astype(_F32)).astype(_BF16)
    d = lambda w: jnp.dot(tri, w, preferred_element_type=_F32)
    cum = (d(l3) + d(l2)) + d(l1)
    cum_end = jnp.sum(lw, axis=0, keepdims=True)
    slices = [slice(p * LANES, (p + 1) * LANES) for p in range(lw.shape[1] // LANES)]
    chains = [_scan_chunk(h_ref[p], r_ref[0, :, sl], v_ref[0, :, sl], kk_ref[0, :, sl], lw[:, sl],
                          cum[:, sl], cum_end[:, sl], a_ref[0, 0, :, sl], kd_ref[0, 0, :, sl], sgn)
              for p, sl in enumerate(slices)]
    for p, (sl, result) in enumerate(zip(slices, _interleave(chains))):
        y_ref[0, 0, :, sl], h_ref[p] = result


def _delta_scan(r, v, kk, lw, a, kd):
    b, t, w = r.shape
    c = SCAN_CHUNK
    n_chunks = t // c
    n_ctx = CTX_LEN // c

    def chunk(d, j):
        rev = jnp.where(j < n_ctx, n_ctx - 1 - j, n_chunks + n_ctx - 1 - j)
        return jnp.where(d == 0, j, rev)

    shared = pl.BlockSpec((1, c, w), lambda d, i, j: (i, chunk(d, j), 0))
    per_dir = pl.BlockSpec((1, 1, c, w), lambda d, i, j: (d, i, chunk(d, j), 0))
    return pl.pallas_call(
        _scan_kernel,
        grid=(2, b, n_chunks),
        in_specs=[shared, shared, shared, per_dir, per_dir, per_dir],
        out_specs=per_dir,
        out_shape=jax.ShapeDtypeStruct((2, b, t, w), _F32),
        scratch_shapes=[pltpu.VMEM((w // LANES, LANES, LANES), _F32)],
        name="delta_scan", compiler_params=_params(3),
    )(r, v, kk, lw, a, kd)


def _post_kernel(*refs, even):
    it = iter(refs)
    if even:
        y_ref, r_ref, v_ref, kd_ref, g_ref, gn_ref = (next(it) for _ in range(6))
    att_ref, x_ref, m_ref, wo_ref, wrh_ref, wrl_ref, br_ref = (next(it) for _ in range(7))
    xo_ref, h_ref, route_ref, count_ref, carry_ref = it
    if even:
        mean_mat = _head_sum_matrix(1.0 / HEAD_DIM)
        ones_mat = _head_sum_matrix(1.0)
        parts = [att_ref[...]]
        for s in range(B_WIDTH // LANES):
            sl = slice(s * LANES, (s + 1) * LANES)
            y = y_ref[0, :, sl] + y_ref[1, :, sl]
            dev = y - _head_sum(y, mean_mat)
            var = _head_sum(dev * dev, mean_mat)
            y = dev * lax.rsqrt(var + RWKV_GN_EPS) * gn_ref[0:1, sl] + gn_ref[1:2, sl]
            r, v = r_ref[:, sl], v_ref[:, sl]
            for dr in range(2):
                y = y + _head_sum(r * kd_ref[dr, :, sl] * gn_ref[2:3, sl], ones_mat) * v
            parts.append((y * g_ref[:, sl]).astype(_BF16))
        mixed = jnp.concatenate(parts, axis=1)
    else:
        mixed = att_ref[...]
    out = jnp.dot(mixed, wo_ref[...], preferred_element_type=_F32)
    x = x_ref[...] + m_ref[0, 2:3, :] * out
    xo_ref[...] = x
    h = _rms_rows(x) * (1.0 + m_ref[0, 4:5, :]) + m_ref[0, 3:4, :]
    h_ref[...] = h
    logits = _dot3(_split2(h), (wrh_ref[...], wrl_ref[...])) + br_ref[...]
    _route(logits, route_ref, count_ref, carry_ref)


def _route(logits, route_ref, count_ref, carry_ref):
    tm = logits.shape[0]
    lane = lax.broadcasted_iota(jnp.int32, (tm, LANES), 1)
    neg = -1e30
    first_lane = lambda hit: jnp.min(jnp.where(hit, lane, LANES), axis=-1, keepdims=True)

    is_grp = lane < N_GROUPS
    grp = jnp.where(is_grp, logits, neg)
    g_max = jnp.max(grp, axis=-1, keepdims=True)
    g_sel = first_lane(grp == g_max)
    grp_w = 1.0 / jnp.sum(jnp.where(is_grp, jnp.exp(grp - g_max), 0.0), axis=-1, keepdims=True)

    base = N_GROUPS + EXPERTS_PER_GROUP * g_sel
    in_grp = (lane >= base) & (lane < base + EXPERTS_PER_GROUP)
    ex = jnp.where(in_grp, logits, neg)
    e_max = jnp.max(ex, axis=-1, keepdims=True)
    pe = jnp.where(in_grp, jnp.exp(ex - e_max), 0.0)
    prob = pe / jnp.sum(pe, axis=-1, keepdims=True)
    p1 = jnp.max(prob, axis=-1, keepdims=True)
    l1 = first_lane(in_grp & (prob == p1))
    rest = jnp.where(in_grp & (lane != l1), prob, -1.0)
    p2 = jnp.max(rest, axis=-1, keepdims=True)
    l2 = first_lane(rest == p2)
    scale = grp_w / (p1 + p2)
    e1, e2 = l1 - N_GROUPS, l2 - N_GROUPS

    @pl.when(pl.program_id(0) == 0)
    def _():
        carry_ref[...] = jnp.zeros_like(carry_ref)

    oh1 = jnp.where(lane == e1, 1.0, 0.0)
    oh2 = jnp.where(lane == e2, 1.0, 0.0)
    rr = lax.broadcasted_iota(jnp.int32, (tm, tm), 0)
    cc = lax.broadcasted_iota(jnp.int32, (tm, tm), 1)
    tri = jnp.where(cc < rr, 1.0, 0.0).astype(_BF16)
    before = lambda oh: jnp.dot(tri, oh.astype(_BF16), preferred_element_type=_F32)
    carry = carry_ref[...]
    tot1 = jnp.sum(oh1, axis=0, keepdims=True)
    rank1 = jnp.sum(oh1 * (before(oh1) + carry), axis=-1, keepdims=True)
    rank2 = jnp.sum(oh2 * (before(oh2) + (carry + tot1)), axis=-1, keepdims=True)
    carry = carry + tot1 + jnp.sum(oh2, axis=0, keepdims=True)
    carry_ref[...] = carry
    count_ref[...] = carry

    out = jnp.zeros((tm, LANES), _F32)
    for i, val in enumerate((p1 * scale, p2 * scale, e1.astype(_F32), e2.astype(_F32), rank1, rank2)):
        out = jnp.where(lane == i, val, out)
    route_ref[...] = out


def _mixer_out(att, x, mods, w_out, w_router, b_router, rwkv=None):
    n, d = x.shape
    tm = ROW_TILE
    row = lambda width: pl.BlockSpec((tm, width), lambda i: (i, 0))
    both = pl.BlockSpec((2, tm, B_WIDTH), lambda i: (0, i, 0))
    args, specs = [], []
    if rwkv is not None:
        y, r, v, kd, g, gn = rwkv
        args += [y, r, v, kd, g, gn]
        specs += [both, row(B_WIDTH), row(B_WIDTH), both, row(B_WIDTH), _full(gn.shape)]
    wr_hi, wr_lo = _split2(w_router)
    args += [att, x, mods, w_out, wr_hi, wr_lo, b_router]
    specs += [row(att.shape[1]), row(d), pl.BlockSpec((1, 6, d), lambda i: (_mods_index(i), 0, 0)),
              _full(w_out.shape), _full(wr_hi.shape), _full(wr_lo.shape), _full(b_router.shape)]
    return pl.pallas_call(
        functools.partial(_post_kernel, even=rwkv is not None),
        grid=(n // tm,), in_specs=specs,
        out_specs=[row(d), row(d), row(LANES), _full((1, LANES))],
        out_shape=[jax.ShapeDtypeStruct((n, d), _F32), jax.ShapeDtypeStruct((n, d), _F32),
                   jax.ShapeDtypeStruct((n, LANES), _F32), jax.ShapeDtypeStruct((1, LANES), _F32)],
        scratch_shapes=[pltpu.VMEM((1, LANES), _F32)],
        name="mixer_out_even" if rwkv is not None else "mixer_out_odd", compiler_params=_params(1),
    )(*args)


def _moe_kernel(be_ref, nv_ref, x_ref, wg_ref, wu_ref, wd_ref, o_ref):
    del be_ref

    @pl.when(pl.program_id(0) < nv_ref[0])
    def _():
        x = x_ref[...].astype(_BF16)
        gate = jnp.dot(x, wg_ref[0], preferred_element_type=_F32)
        up = jnp.dot(x, wu_ref[0], preferred_element_type=_F32)
        hid = (gate * jax.nn.sigmoid(gate) * up).astype(_BF16)
        o_ref[...] = jnp.dot(hid, wd_ref[0], preferred_element_type=_F32)

    @pl.when(pl.program_id(0) >= nv_ref[0])
    def _():
        o_ref[...] = jnp.zeros_like(o_ref)


def _moe_blocks(block_expert, n_valid, xs, wg, wu, wd):
    rows, d = xs.shape
    nb = rows // EXPERT_BLOCK
    grid_spec = pltpu.PrefetchScalarGridSpec(
        num_scalar_prefetch=2,
        grid=(nb,),
        in_specs=[pl.BlockSpec((EXPERT_BLOCK, d), lambda i, be, nv: (i, 0)),
                  pl.BlockSpec((1, d, EXPERT_HIDDEN), lambda i, be, nv: (be[i], 0, 0)),
                  pl.BlockSpec((1, d, EXPERT_HIDDEN), lambda i, be, nv: (be[i], 0, 0)),
                  pl.BlockSpec((1, EXPERT_HIDDEN, d), lambda i, be, nv: (be[i], 0, 0))],
        out_specs=pl.BlockSpec((EXPERT_BLOCK, d), lambda i, be, nv: (i, 0)),
    )
    return pl.pallas_call(
        _moe_kernel,
        grid_spec=grid_spec,
        out_shape=jax.ShapeDtypeStruct((rows, d), _F32),
        name="moe_ffn", compiler_params=_params(1),
    )(block_expert, n_valid, xs, wg, wu, wd)


def _moe(h, route, counts, w_gate, w_up, w_down):
    n, d = h.shape
    eid = route[:, 2:4].astype(jnp.int32)
    rank = route[:, 4:6].astype(jnp.int32)
    cnt = counts[0, :N_EXPERTS].astype(jnp.int32)
    padded = (cnt + EXPERT_BLOCK - 1) // EXPERT_BLOCK * EXPERT_BLOCK
    pad_end = jnp.cumsum(padded)
    slot = (pad_end - padded)[eid] + rank
    n_blocks = -(-(n * TOP_K + N_EXPERTS * (EXPERT_BLOCK - 1)) // EXPERT_BLOCK)
    tok = jnp.broadcast_to(jnp.arange(n, dtype=jnp.int32)[:, None], (n, TOP_K))
    slot_tok = jnp.zeros((n_blocks * EXPERT_BLOCK,), jnp.int32).at[slot.reshape(-1)].set(tok.reshape(-1))
    block_start = jnp.arange(n_blocks, dtype=jnp.int32) * EXPERT_BLOCK
    block_expert = jnp.minimum(jnp.searchsorted(pad_end, block_start, side='right'),
                               N_EXPERTS - 1).astype(jnp.int32)
    n_valid = (pad_end[-1:] // EXPERT_BLOCK).astype(jnp.int32)
    yb = _moe_blocks(block_expert, n_valid, h[slot_tok], w_gate.astype(_BF16), w_up.astype(_BF16),
                     w_down.astype(_BF16))
    return yb[slot[:, 0]], yb[slot[:, 1]]


def _final_kernel(x_ref, y1_ref, y2_ref, g_ref, m_ref, gain_ref, o_ref):
    g = g_ref[...]
    x = x_ref[...] + m_ref[0, 5:6, :] * (g[:, 0:1] * y1_ref[...] + g[:, 1:2] * y2_ref[...])
    o_ref[...] = _rms_rows(x) * gain_ref[...]


def _final(x, y1, y2, route, mods, gain, bsz, n_lat):
    d = x.shape[1]
    tm = ROW_TILE
    lat_tiles = n_lat // tm
    tiles = (CTX_LEN + n_lat) // tm
    src = lambda i: (i // lat_tiles) * tiles + CTX_LEN // tm + i % lat_tiles
    row = lambda width: pl.BlockSpec((tm, width), lambda i: (src(i), 0))
    return pl.pallas_call(
        _final_kernel,
        grid=(bsz * lat_tiles,),
        in_specs=[row(d), row(d), row(d), row(LANES),
                  pl.BlockSpec((1, 6, d), lambda i: ((i // lat_tiles) * 2 + 1, 0, 0)), _full((1, d))],
        out_specs=pl.BlockSpec((tm, d), lambda i: (i, 0)),
        out_shape=jax.ShapeDtypeStruct((bsz * n_lat, d), _F32),
        name="final_norm", compiler_params=_params(1),
    )(x, y1, y2, route, mods, gain.reshape(1, d))


def _rope_tables(n_lat):
    nf = HEAD_DIM // 4
    inv = ROPE_THETA ** (-jnp.arange(nf, dtype=_F32) / nf)
    rows = n_lat // GRID_W
    r_ang = jnp.repeat(jnp.arange(rows, dtype=_F32), GRID_W)[:, None] * inv
    c_ang = jnp.tile(jnp.arange(GRID_W, dtype=_F32), rows)[:, None] * inv
    cos = jnp.concatenate([jnp.cos(r_ang)] * 2 + [jnp.cos(c_ang)] * 2, axis=-1)
    sin = jnp.concatenate([-jnp.sin(r_ang), jnp.sin(r_ang), -jnp.sin(c_ang), jnp.sin(c_ang)], axis=-1)
    cos = jnp.concatenate([jnp.ones((CTX_LEN, HEAD_DIM), _F32), cos], axis=0)
    sin = jnp.concatenate([jnp.zeros((CTX_LEN, HEAD_DIM), _F32), sin], axis=0)
    return jnp.tile(cos, (1, 2)), jnp.tile(sin, (1, 2))


def kernel(x, c, ctx, c_ctx, mod_w, mod_b, ev_w_in, ev_w_out, ev_q_gain, ev_k_gain, ev_mu_prev, ev_mu_next, ev_w0, ev_w2, ev_a0, ev_a2, ev_g2, ev_k_k, ev_k_a, ev_r_k, ev_gn_w, ev_gn_b, ev_v0, ev_v1, ev_v2, od_w_in, od_w_out, od_lq1, od_lk1, od_lq2, od_lk2, od_subln, moe_w_grp, moe_b_grp, moe_w_rt, moe_b_rt, moe_w_gate, moe_w_up, moe_w_down, final_gain):
    bsz, n_lat, d = x.shape
    t = CTX_LEN + n_lat
    n = bsz * t
    cos, sin = _rope_tables(n_lat)
    cond = jax.nn.silu(jnp.concatenate([c, c_ctx[None, :], jnp.zeros((16 - bsz - 1, d), _F32)], axis=0))
    xs = jnp.concatenate([ctx, x], axis=1).reshape(n, d)
    v_first = None
    moe = None
    for layer in range(DEPTH):
        m = _mm(cond, mod_w[layer], tn=1536, precision=_HI) + mod_b[layer]
        m = m.reshape(16, 6, d)
        mods = jnp.stack([jnp.broadcast_to(m[bsz], (bsz, 6, d)), m[:bsz]], axis=1).reshape(2 * bsz, 6, d)
        w_router = jnp.concatenate(
            [moe_w_grp[layer], moe_w_rt[layer], jnp.zeros((d, LANES - N_GROUPS - N_EXPERTS), _F32)], axis=1)
        b_router = jnp.concatenate(
            [moe_b_grp[layer], moe_b_rt[layer], jnp.zeros((LANES - N_GROUPS - N_EXPERTS,), _F32)])[None, :]
        if layer % 2 == 0:
            e = layer // 2
            p = {'q_gain': ev_q_gain[e], 'k_gain': ev_k_gain[e], 'mu_prev': ev_mu_prev[e],
                 'mu_next': ev_mu_next[e], 'w0': ev_w0[e], 'w2': ev_w2[e], 'a0': ev_a0[e],
                 'a2': ev_a2[e], 'g2': ev_g2[e], 'k_k': ev_k_k[e], 'k_a': ev_k_a[e]}
            if e > 0:
                p['v0'], p['v1'], p['v2'] = ev_v0[e - 1], ev_v1[e - 1], ev_v2[e - 1]
            proj, xs = _project(xs, mods, ev_w_in[e].astype(_BF16), moe=moe)
            q, kk2, vv2, r, v, kkn, lw, a, kd, g = _prepare_even(proj, cos, sin, p, v_first)
            if v_first is None:
                v_first = v
            att = _gqa_attend(q.reshape(bsz, t, -1), kk2.reshape(bsz, t, -1), vv2.reshape(bsz, t, -1))
            seq = lambda z: z.reshape(*z.shape[:-2], bsz, t, B_WIDTH)
            y = _delta_scan(seq(r), seq(v), seq(kkn), seq(lw), seq(a), seq(kd)).reshape(2, n, B_WIDTH)
            gn = jnp.stack([ev_gn_w[e], ev_gn_b[e], ev_r_k[e].reshape(B_WIDTH)])
            xs, h, route, counts = _mixer_out(att.reshape(n, -1), xs, mods, ev_w_out[e].astype(_BF16),
                                       w_router, b_router, rwkv=(y, r, v, kd, g, gn))
        else:
            o = layer // 2
            lambda_init = 0.8 - 0.6 * math.exp(-0.3 * layer)
            qkv, xs = _project(xs, mods, od_w_in[o].astype(_BF16), moe=moe, rope=(cos, sin))
            lam = (jnp.exp(jnp.sum(od_lq1[o] * od_lk1[o])) - jnp.exp(jnp.sum(od_lq2[o] * od_lk2[o]))
                   + lambda_init).reshape(1, 1)
            gain = (od_subln[o] * (1.0 - lambda_init)).reshape(1, LANES)
            att = _diff_attend(lam, qkv.reshape(bsz, t, -1), gain)
            xs, h, route, counts = _mixer_out(att.reshape(n, -1), xs, mods, od_w_out[o].astype(_BF16),
                                       w_router, b_router)
        y1, y2 = _moe(h, route, counts, moe_w_gate[layer], moe_w_up[layer], moe_w_down[layer])
        moe = (y1, y2, route, mods)
    return _final(xs, y1, y2, route, mods, final_gain, bsz, n_lat).reshape(bsz, n_lat, d)
```

```python
import functools
import math

import jax
import jax.numpy as jnp
from jax import lax
from jax.experimental import pallas as pl
from jax.experimental.pallas import tpu as pltpu

D_MODEL = 1024
DEPTH = 4
SEQ = 2048
GRID_W = 64
CTX_LEN = 256
HEAD_DIM = 64
ROPE_THETA = 10000.0
RMS_EPS = 1e-6
SUBLN_EPS = 1e-5

A_HEADS = 8
A_KV_HEADS = 2
A_WIDTH = 512
A_KV_WIDTH = 128
A_COLS = 768

B_HEADS = 8
B_WIDTH = 512
B_COLS = 1792
RWKV_W_LORA = 64
RWKV_A_LORA = 64
RWKV_V_LORA = 32
RWKV_G_LORA = 128
RWKV_GN_EPS = 64e-5
DECAY_SCALE = math.exp(-0.5)

C_HEADS = 8
C_QK = 1024

N_GROUPS = 4
EXPERTS_PER_GROUP = 8
N_EXPERTS = 32
TOP_K = 2
EXPERT_HIDDEN = 512
MOE_ROWS = 256

LANES = 128
SCAN_CHUNK = 64
ROW_TILE = 256
Q_TILE = 128
VMEM_LIMIT = 56 * 1024 * 1024

_HI = lax.Precision.HIGHEST
_F32 = jnp.float32
_BF16 = jnp.bfloat16

_NN = (((1,), (0,)), ((), ()))
_NT = (((1,), (1,)), ((), ()))
_TN = (((0,), (0,)), ((), ()))


def _params(n_axes):
    return pltpu.CompilerParams(dimension_semantics=("arbitrary",) * n_axes,
                                vmem_limit_bytes=VMEM_LIMIT)


def _full(shape):
    return pl.BlockSpec(shape, lambda *_: (0,) * len(shape))


def _split2(x):
    hi = x.astype(_BF16)
    return hi, (x - hi.astype(_F32)).astype(_BF16)


def _dot3(a, b, dims=_NN):
    d = lambda u, w: lax.dot_general(u, w, dims, preferred_element_type=_F32)
    return (d(a[1], b[0]) + d(a[0], b[1])) + d(a[0], b[0])


def _dot1(a, b, dims=_NN):
    return lax.dot_general(a.astype(_BF16), b.astype(_BF16), dims, preferred_element_type=_F32)


def _lane_lo(shape):
    return lax.broadcasted_iota(jnp.int32, shape, len(shape) - 1) < HEAD_DIM


def _head_sum_matrix(value):
    row = lax.broadcasted_iota(jnp.int32, (LANES, LANES), 0)
    col = lax.broadcasted_iota(jnp.int32, (LANES, LANES), 1)
    return jnp.where((row // HEAD_DIM) == (col // HEAD_DIM), value, 0.0).astype(_BF16)


def _head_sum(x, mat):
    hi, lo = _split2(x)
    return jnp.dot(lo, mat, preferred_element_type=_F32) + jnp.dot(hi, mat, preferred_element_type=_F32)


def _rms_rows(x, eps=RMS_EPS):
    return x * lax.rsqrt(jnp.mean(x * x, axis=-1, keepdims=True) + eps)


def _rope_slab(x, cos, sin):
    lane = lax.broadcasted_iota(jnp.int32, x.shape, 1)
    partner = jnp.where((lane & 16) == 0, pltpu.roll(x, LANES - 16, axis=1), pltpu.roll(x, 16, axis=1))
    return x * cos + partner * sin


def _mods_index(i):
    tiles = (CTX_LEN + SEQ) // ROW_TILE
    return (i // tiles) * 2 + jnp.minimum(i % tiles, 1)


def _mm_kernel(a_ref, w_ref, o_ref, *, precision):
    o_ref[...] = jnp.dot(a_ref[...], w_ref[...], precision=precision,
                         preferred_element_type=_F32).astype(o_ref.dtype)


def _mm(a, w, *, tn, precision=None):
    m, k = a.shape
    n = w.shape[1]
    return pl.pallas_call(
        functools.partial(_mm_kernel, precision=precision),
        grid=(n // tn,),
        in_specs=[pl.BlockSpec((m, k), lambda j: (0, 0)),
                  pl.BlockSpec((k, tn), lambda j: (0, j))],
        out_specs=pl.BlockSpec((m, tn), lambda j: (0, j)),
        out_shape=jax.ShapeDtypeStruct((m, n), _F32),
        name=f"mm_{k}x{n}",
        compiler_params=_params(1),
    )(a, w)


def _proj_kernel(*refs, combine, rope):
    it = iter(refs)
    x_ref, m_ref = next(it), next(it)
    if combine:
        y1_ref, y2_ref, g_ref, mp_ref = (next(it) for _ in range(4))
    w_ref = next(it)
    if rope:
        cos_ref, sin_ref = next(it), next(it)
    o_ref = next(it)
    x = x_ref[...]
    if combine:
        xo_ref = next(it)
        g = g_ref[...]
        x = x + mp_ref[0, 5:6, :] * (g[:, 0:1] * y1_ref[...] + g[:, 1:2] * y2_ref[...])
        xo_ref[...] = x
    h = _rms_rows(x) * (1.0 + m_ref[0, 1:2, :]) + m_ref[0, 0:1, :]
    p = jnp.dot(h.astype(_BF16), w_ref[...], preferred_element_type=_F32)
    if rope:
        cos, sin = cos_ref[...], sin_ref[...]
        for s in range(2 * C_QK // LANES):
            sl = slice(s * LANES, (s + 1) * LANES)
            slab = _rope_slab(p[:, sl], cos, sin)
            if s < C_QK // LANES:
                slab = slab * HEAD_DIM ** -0.5
            o_ref[:, sl] = slab.astype(o_ref.dtype)
        o_ref[:, 2 * C_QK:] = p[:, 2 * C_QK:].astype(o_ref.dtype)
    else:
        o_ref[...] = p


def _project(x, mods, w, *, moe=None, rope=None):
    n, d = x.shape
    cols = w.shape[1]
    tm = ROW_TILE
    tiles = (CTX_LEN + SEQ) // tm
    row = lambda width: pl.BlockSpec((tm, width), lambda i: (i, 0))
    mod_spec = pl.BlockSpec((1, 6, d), lambda i: (_mods_index(i), 0, 0))
    args, specs = [x, mods], [row(d), mod_spec]
    if moe is not None:
        args += list(moe)
        specs += [row(d), row(d), row(LANES), mod_spec]
    args.append(w)
    specs.append(_full(w.shape))
    if rope is not None:
        args += list(rope)
        specs += [pl.BlockSpec((tm, LANES), lambda i: (i % tiles, 0))] * 2
    out_shape = [jax.ShapeDtypeStruct((n, cols), _BF16 if rope is not None else _F32)]
    out_specs = [row(cols)]
    if moe is not None:
        out_shape.append(jax.ShapeDtypeStruct((n, d), _F32))
        out_specs.append(row(d))
    out = pl.pallas_call(
        functools.partial(_proj_kernel, combine=moe is not None, rope=rope is not None),
        grid=(n // tm,), in_specs=specs, out_specs=out_specs, out_shape=out_shape,
        name=f"project_{cols}", compiler_params=_params(1),
    )(*args)
    return out if moe is not None else (out[0], x)


def _prep_kernel(*refs, has_v_first):
    it = iter(refs)
    p_ref, pp_ref, pn_ref, cos_ref, sin_ref = (next(it) for _ in range(5))
    gain_ref, mu_ref, wa0_ref, kv_ref, wl_ref, g2_ref = (next(it) for _ in range(6))
    if has_v_first:
        v1_ref, v2_ref, vf_ref = next(it), next(it), next(it)
    q_ref, kk2_ref, vv2_ref, r_ref, v_ref, kkn_ref, lw_ref, a_ref, kd_ref, g_ref = it

    tiles = (CTX_LEN + SEQ) // ROW_TILE
    j = pl.program_id(0) % tiles
    p = p_ref[...]
    cos, sin = cos_ref[...], sin_ref[...]
    mean_mat = _head_sum_matrix(1.0 / HEAD_DIM)
    ones_mat = _head_sum_matrix(1.0)
    lo = _lane_lo((ROW_TILE, LANES))

    def normed(x, gain):
        return x * lax.rsqrt(_head_sum(x * x, mean_mat) + RMS_EPS) * gain

    for s in range(A_WIDTH // LANES):
        sl = slice(s * LANES, (s + 1) * LANES)
        q = _rope_slab(normed(p[:, sl], gain_ref[0:1, :]), cos, sin)
        q_ref[:, sl] = (q * HEAD_DIM ** -0.5).astype(_BF16)
    k = _rope_slab(normed(p[:, A_WIDTH:A_WIDTH + LANES], gain_ref[1:2, :]), cos, sin)
    v = p[:, A_WIDTH + LANES:A_COLS]
    for src, dst in ((k, kk2_ref), (v, vv2_ref)):
        swapped = pltpu.roll(src, HEAD_DIM, axis=1)
        dst[:, :LANES] = jnp.where(lo, src, swapped).astype(_BF16)
        dst[:, LANES:] = jnp.where(lo, swapped, src).astype(_BF16)

    u = p[:, A_COLS:]
    rid = lax.broadcasted_iota(jnp.int32, (ROW_TILE, 1), 0)
    has_prev = jnp.where(j >= 2, 1.0, 0.0)
    has_next = jnp.where((j >= 1) & (j < tiles - 1), 1.0, 0.0)
    first = pp_ref[7:8, A_COLS:] * has_prev
    last = pn_ref[0:1, A_COLS:] * has_next
    prev = jnp.where(rid == 0, first, pltpu.roll(u, 1, axis=0))
    nxt = jnp.where(rid == ROW_TILE - 1, last, pltpu.roll(u, ROW_TILE - 1, axis=0))
    u = u + mu_ref[0:1, :] * (prev - u) + mu_ref[1:2, :] * (nxt - u)
    r, k, v = u[:, :B_WIDTH], u[:, B_WIDTH:2 * B_WIDTH], u[:, 2 * B_WIDTH:3 * B_WIDTH]
    wa, glr = u[:, 3 * B_WIDTH:3 * B_WIDTH + LANES], u[:, 3 * B_WIDTH + LANES:]
    if has_v_first:
        mid = jnp.dot(v.astype(_BF16), v1_ref[...], preferred_element_type=_F32)
        lora = jnp.dot(mid.astype(_BF16), v2_ref[...], preferred_element_type=_F32)
        v = v + (vf_ref[...] - v) * jax.nn.sigmoid(kv_ref[2:3, :] + lora)
    r_ref[...] = r
    v_ref[...] = v
    kks = k * kv_ref[0:1, :]
    for s in range(B_WIDTH // LANES):
        sl = slice(s * LANES, (s + 1) * LANES)
        x = kks[:, sl]
        kkn_ref[:, sl] = x * lax.rsqrt(jnp.maximum(_head_sum(x * x, ones_mat), 1e-24))
    xw = jnp.where(lo, jnp.tanh(wa), wa).astype(_BF16)
    for dr in range(2):
        z = jnp.dot(xw, wl_ref[dr], preferred_element_type=_F32) + wa0_ref[dr:dr + 1, :]
        a = jax.nn.sigmoid(z[:, B_WIDTH:])
        lw_ref[dr] = -DECAY_SCALE * jax.nn.sigmoid(z[:, :B_WIDTH])
        a_ref[dr] = a
        kd_ref[dr] = k * (1.0 + (a - 1.0) * kv_ref[1:2, :])
    g_ref[...] = jnp.dot(jax.nn.sigmoid(glr).astype(_BF16), g2_ref[...], preferred_element_type=_F32)


def _prepare_even(proj, cos, sin, p, v_first):
    n, cols = proj.shape
    tm = ROW_TILE
    tiles = (CTX_LEN + SEQ) // tm
    halo = 8
    per_tile = tm // halo
    row = lambda width: pl.BlockSpec((tm, width), lambda i: (i, 0))
    both = pl.BlockSpec((2, tm, B_WIDTH), lambda i: (0, i, 0))
    tab = pl.BlockSpec((tm, LANES), lambda i: (i % tiles, 0))
    pad = lambda w, rows, width: jnp.zeros((rows, width), _BF16).at[:w.shape[0], :w.shape[1]].set(w.astype(_BF16))
    gains = jnp.stack([jnp.tile(p['q_gain'], 2), jnp.tile(p['k_gain'], 2)])
    mu = jnp.stack([p['mu_prev'], p['mu_next']])
    wa0 = jnp.concatenate([p['w0'], p['a0']], axis=1)
    kvec = jnp.stack([p['k_k'], p['k_a'], p.get('v0', jnp.zeros((B_WIDTH,), _F32))])
    zero = jnp.zeros((RWKV_W_LORA, B_WIDTH), _F32)
    wl = jnp.stack([jnp.concatenate([jnp.concatenate([p['w2'][dr], zero], axis=1),
                                     jnp.concatenate([zero, p['a2'][dr]], axis=1)], axis=0)
                    for dr in range(2)]).astype(_BF16)
    args = [proj, proj, proj, cos, sin, gains, mu, wa0, kvec, wl, p['g2'].astype(_BF16)]
    specs = [row(cols),
             pl.BlockSpec((halo, cols), lambda i: (jnp.maximum(i * per_tile - 1, 0), 0)),
             pl.BlockSpec((halo, cols), lambda i: (jnp.minimum((i + 1) * per_tile, n // halo - 1), 0)),
             tab, tab, _full(gains.shape), _full(mu.shape), _full(wa0.shape), _full(kvec.shape),
             _full(wl.shape), _full(p['g2'].shape)]
    if v_first is not None:
        v1 = pad(p['v1'], B_WIDTH, LANES)
        v2 = pad(p['v2'], LANES, B_WIDTH)
        args += [v1, v2, v_first]
        specs += [_full(v1.shape), _full(v2.shape), row(B_WIDTH)]
    f32 = lambda *shape: jax.ShapeDtypeStruct(shape, _F32)
    bf16 = lambda *shape: jax.ShapeDtypeStruct(shape, _BF16)
    out_shape = [bf16(n, A_WIDTH), bf16(n, 2 * LANES), bf16(n, 2 * LANES),
                 f32(n, B_WIDTH), f32(n, B_WIDTH), f32(n, B_WIDTH),
                 f32(2, n, B_WIDTH), f32(2, n, B_WIDTH), f32(2, n, B_WIDTH), f32(n, B_WIDTH)]
    out_specs = [row(A_WIDTH), row(2 * LANES), row(2 * LANES), row(B_WIDTH), row(B_WIDTH), row(B_WIDTH),
                 both, both, both, row(B_WIDTH)]
    return pl.pallas_call(
        functools.partial(_prep_kernel, has_v_first=v_first is not None),
        grid=(n // tm,), in_specs=specs, out_specs=out_specs, out_shape=out_shape,
        name="prepare_even", compiler_params=_params(1),
    )(*args)


def _split_heads(q2):
    lo = _lane_lo(q2.shape)
    zero = jnp.zeros_like(q2)
    return jnp.concatenate([jnp.where(lo, q2, zero), jnp.where(lo, zero, q2)], axis=0)


def _scores(qs, k):
    return lax.dot_general(qs, k, _NT, preferred_element_type=_F32)


def _interleave(chains):
    results = {}
    while len(results) < len(chains):
        for p, chain in enumerate(chains):
            try:
                next(chain)
            except StopIteration as stop:
                results[p] = stop.value
    return [results[p] for p in range(len(chains))]


def _gqa_tile(q, k, v):
    tq = q.shape[0]
    qs = jnp.concatenate([_split_heads(q[:, :LANES]), _split_heads(q[:, LANES:])], axis=0)
    s = _scores(qs, k)
    yield
    p = jnp.exp(s - jnp.max(s, axis=-1, keepdims=True))
    l = jnp.sum(p, axis=-1, keepdims=True)
    p = p.astype(_BF16)
    yield
    o = jnp.dot(p, v, preferred_element_type=_F32) / l
    lo = _lane_lo((tq, LANES))
    return jnp.concatenate([jnp.where(lo, o[:tq], o[tq:2 * tq]),
                            jnp.where(lo, o[2 * tq:3 * tq], o[3 * tq:])], axis=1)


def _attend_halves(tile_fn, q_ref, o_ref):
    halves = [slice(i * Q_TILE, (i + 1) * Q_TILE) for i in range(q_ref.shape[1] // Q_TILE)]
    outs = _interleave([tile_fn(q_ref[0, sl]) for sl in halves])
    for sl, o in zip(halves, outs):
        o_ref[0, sl] = o.astype(o_ref.dtype)


def _gqa_kernel(q_ref, k_ref, v_ref, o_ref):
    is_ctx = pl.program_id(2) < CTX_LEN // q_ref.shape[1]

    @pl.when(is_ctx)
    def _():
        _attend_halves(lambda q: _gqa_tile(q, k_ref[0, :CTX_LEN], v_ref[0, :CTX_LEN]), q_ref, o_ref)

    @pl.when(jnp.logical_not(is_ctx))
    def _():
        _attend_halves(lambda q: _gqa_tile(q, k_ref[0], v_ref[0]), q_ref, o_ref)


def _gqa_attend(q, kk, vv):
    b, t, _ = q.shape
    tq = 2 * Q_TILE
    kv_spec = pl.BlockSpec((1, t, LANES), lambda i, g, j: (i, 0, g))
    return pl.pallas_call(
        _gqa_kernel,
        grid=(b, A_KV_HEADS, t // tq),
        in_specs=[pl.BlockSpec((1, tq, 2 * LANES), lambda i, g, j: (i, j, g)), kv_spec, kv_spec],
        out_specs=pl.BlockSpec((1, tq, 2 * LANES), lambda i, g, j: (i, j, g)),
        out_shape=jax.ShapeDtypeStruct((b, t, A_WIDTH), _BF16),
        name="gqa_attn", compiler_params=_params(3),
    )(q, kk, vv)


def _diff_tile(lam, q, k, v, gain):
    tq = q.shape[0]
    s = _scores(_split_heads(q), k)
    yield
    p = jnp.exp(s - jnp.max(s, axis=-1, keepdims=True))
    l = jnp.sum(p, axis=-1, keepdims=True)
    l1, l2 = l[:tq], l[tq:]
    pr = (p[:tq] - (lam * l1 / l2) * p[tq:]).astype(_BF16)
    yield
    o = jnp.dot(pr, v, preferred_element_type=_F32) / l1
    y = o * lax.rsqrt(jnp.mean(o * o, axis=-1, keepdims=True) + SUBLN_EPS)
    return y * gain


def _diff_kernel(lam_ref, q_ref, k_ref, v_ref, g_ref, o_ref):
    is_ctx = pl.program_id(2) < CTX_LEN // q_ref.shape[1]
    lam, gain = lam_ref[0, 0], g_ref[...]

    @pl.when(is_ctx)
    def _():
        _attend_halves(lambda q: _diff_tile(lam, q, k_ref[0, :CTX_LEN], v_ref[0, :CTX_LEN], gain),
                       q_ref, o_ref)

    @pl.when(jnp.logical_not(is_ctx))
    def _():
        _attend_halves(lambda q: _diff_tile(lam, q, k_ref[0], v_ref[0], gain), q_ref, o_ref)


def _diff_attend(lam, qkv, gain):
    b, t, _ = qkv.shape
    tq = 2 * Q_TILE
    return pl.pallas_call(
        _diff_kernel,
        grid=(b, C_HEADS, t // tq),
        in_specs=[pl.BlockSpec(memory_space=pltpu.SMEM),
                  pl.BlockSpec((1, tq, LANES), lambda i, h, j: (i, j, h)),
                  pl.BlockSpec((1, t, LANES), lambda i, h, j: (i, 0, C_HEADS + h)),
                  pl.BlockSpec((1, t, LANES), lambda i, h, j: (i, 0, 2 * C_HEADS + h)),
                  _full((1, LANES))],
        out_specs=pl.BlockSpec((1, tq, LANES), lambda i, h, j: (i, j, h)),
        out_shape=jax.ShapeDtypeStruct((b, t, C_HEADS * LANES), _BF16),
        name="diff_attn", compiler_params=_params(3),
    )(lam, qkv, qkv, qkv, gain)


def _scan_chunk(h, r, v, kk, lw, cum, cum_end, a, kd, sgn):
    c = SCAN_CHUNK
    lo = _lane_lo((c, LANES))
    row = lax.broadcasted_iota(jnp.int32, (LANES, LANES), 0)
    col = lax.broadcasted_iota(jnp.int32, (LANES, LANES), 1)
    delta = ((col & (c - 1)) - (row & (c - 1))) * sgn
    before = delta < 0
    upto = delta <= 0
    eye = row == col
    zero = jnp.zeros((LANES, LANES), _F32)

    def blockdiag(x):
        z = jnp.zeros_like(x)
        return jnp.concatenate([jnp.where(lo, x, z), jnp.where(lo, z, x)], axis=0)

    p_inv = jnp.exp(-cum)
    p_end = jnp.exp(cum_end - cum)
    kka = kk * a
    a_rs = blockdiag(-kk * jnp.exp(cum - lw))
    r_rs = blockdiag(r * jnp.exp(cum))
    b_rs = blockdiag(kka * p_inv)
    k_rs = blockdiag(kd * p_inv)
    v_rs = blockdiag(v)
    bp_rs = blockdiag(kka * p_end)
    kp_rs = blockdiag(kd * p_end)

    z = _dot3(_split2(jnp.concatenate([a_rs, r_rs], axis=0)),
              _split2(jnp.concatenate([b_rs, k_rs], axis=0)), _NT)
    yield
    low = jnp.where(before, z[:LANES, :LANES], zero)
    g = jnp.where(before, z[:LANES, LANES:], zero)
    rb = jnp.where(upto, z[LANES:, :LANES], zero)
    rk = jnp.where(upto, z[LANES:, LANES:], zero)

    x = _dot1(low, low)
    yield
    m = jnp.where(eye, 1.0, 0.0).astype(_F32) + low
    n = 2
    while 2 * n < c:
        res = _dot1(x, jnp.concatenate([m, x], axis=1))
        m = m + res[:, :LANES]
        x = res[:, LANES:]
        n *= 2
        yield
    m = m + _dot1(x, m)
    yield
    gv = _dot1(g, v_rs)
    yield
    au = _dot1(m, jnp.concatenate([a_rs, gv], axis=1))
    yield
    rhs = _split2(jnp.concatenate(
        [au, jnp.concatenate([zero, v_rs], axis=1)], axis=0))
    tf = lax.dot_general(jnp.concatenate([bp_rs, kp_rs], axis=0).astype(_BF16), rhs[0], _TN,
                         preferred_element_type=_F32)
    qy = _dot3(_split2(jnp.concatenate([rb, rk], axis=1)), rhs)
    yield
    t = jnp.where(eye, jnp.broadcast_to(jnp.exp(cum_end), (LANES, LANES)), zero) + tf[:, :LANES]
    q = r_rs + qy[:, :LANES]
    out = _dot1(jnp.concatenate([q, t], axis=0), h)
    y = out[:LANES] + qy[:, LANES:]
    h_new = out[LANES:] + tf[:, LANES:]
    return y[:c] + y[c:], h_new


def _scan_kernel(r_ref, v_ref, kk_ref, lw_ref, a_ref, kd_ref, y_ref, h_ref):
    c = SCAN_CHUNK

    @pl.when(pl.program_id(2) == 0)
    def _():
        h_ref[...] = jnp.zeros_like(h_ref)

    sgn = 1 - 2 * pl.program_id(0)
    tr = lax.broadcasted_iota(jnp.int32, (c, c), 0)
    tc = lax.broadcasted_iota(jnp.int32, (c, c), 1)
    tri = jnp.where((tc - tr) * sgn <= 0, 1.0, 0.0).astype(_BF16)
    lw = lw_ref[0, 0]
    l1 = lw.astype(_BF16)
    rem = lw - l1.astype(_F32)
    l2 = rem.astype(_BF16)
    l3 = (rem - l2.astype(_F32)).astype(_BF16)
    d = lambda w: jnp.dot(tri, w, preferred_element_type=_F32)
    cum = (d(l3) + d(l2)) + d(l1)
    cum_end = jnp.sum(lw, axis=0, keepdims=True)
    slices = [slice(p * LANES, (p + 1) * LANES) for p in range(lw.shape[1] // LANES)]
    chains = [_scan_chunk(h_ref[p], r_ref[0, :, sl], v_ref[0, :, sl], kk_ref[0, :, sl], lw[:, sl],
                          cum[:, sl], cum_end[:, sl], a_ref[0, 0, :, sl], kd_ref[0, 0, :, sl], sgn)
              for p, sl in enumerate(slices)]
    for p, (sl, result) in enumerate(zip(slices, _interleave(chains))):
        y_ref[0, 0, :, sl], h_ref[p] = result


def _delta_scan(r, v, kk, lw, a, kd):
    b, t, w = r.shape
    c = SCAN_CHUNK
    n_chunks = t // c
    n_ctx = CTX_LEN // c

    def chunk(d, j):
        rev = jnp.where(j < n_ctx, n_ctx - 1 - j, n_chunks + n_ctx - 1 - j)
        return jnp.where(d == 0, j, rev)

    shared = pl.BlockSpec((1, c, w), lambda d, i, j: (i, chunk(d, j), 0))
    per_dir = pl.BlockSpec((1, 1, c, w), lambda d, i, j: (d, i, chunk(d, j), 0))
    return pl.pallas_call(
        _scan_kernel,
        grid=(2, b, n_chunks),
        in_specs=[shared, shared, shared, per_dir, per_dir, per_dir],
        out_specs=per_dir,
        out_shape=jax.ShapeDtypeStruct((2, b, t, w), _F32),
        scratch_shapes=[pltpu.VMEM((w // LANES, LANES, LANES), _F32)],
        name="delta_scan", compiler_params=_params(3),
    )(r, v, kk, lw, a, kd)


def _post_kernel(*refs, even):
    it = iter(refs)
    if even:
        y_ref, r_ref, v_ref, kd_ref, g_ref, gn_ref = (next(it) for _ in range(6))
    att_ref, x_ref, m_ref, wo_ref, wrh_ref, wrl_ref, br_ref = (next(it) for _ in range(7))
    xo_ref, h_ref, route_ref, count_ref, carry_ref = it
    if even:
        mean_mat = _head_sum_matrix(1.0 / HEAD_DIM)
        ones_mat = _head_sum_matrix(1.0)
        parts = [att_ref[...]]
        for s in range(B_WIDTH // LANES):
            sl = slice(s * LANES, (s + 1) * LANES)
            y = y_ref[0, :, sl] + y_ref[1, :, sl]
            dev = y - _head_sum(y, mean_mat)
            var = _head_sum(dev * dev, mean_mat)
            y = dev * lax.rsqrt(var + RWKV_GN_EPS) * gn_ref[0:1, sl] + gn_ref[1:2, sl]
            r, v = r_ref[:, sl], v_ref[:, sl]
            for dr in range(2):
                y = y + _head_sum(r * kd_ref[dr, :, sl] * gn_ref[2:3, sl], ones_mat) * v
            parts.append((y * g_ref[:, sl]).astype(_BF16))
        mixed = jnp.concatenate(parts, axis=1)
    else:
        mixed = att_ref[...]
    out = jnp.dot(mixed, wo_ref[...], preferred_element_type=_F32)
    x = x_ref[...] + m_ref[0, 2:3, :] * out
    xo_ref[...] = x
    h = _rms_rows(x) * (1.0 + m_ref[0, 4:5, :]) + m_ref[0, 3:4, :]
    h_ref[...] = h
    logits = _dot3(_split2(h), (wrh_ref[...], wrl_ref[...])) + br_ref[...]
    _route(logits, route_ref, count_ref, carry_ref)


def _route(logits, route_ref, count_ref, carry_ref):
    tm = logits.shape[0]
    lane = lax.broadcasted_iota(jnp.int32, (tm, LANES), 1)
    neg = -1e30
    first_lane = lambda hit: jnp.min(jnp.where(hit, lane, LANES), axis=-1, keepdims=True)

    is_grp = lane < N_GROUPS
    grp = jnp.where(is_grp, logits, neg)
    g_max = jnp.max(grp, axis=-1, keepdims=True)
    g_sel = first_lane(grp == g_max)
    grp_w = 1.0 / jnp.sum(jnp.where(is_grp, jnp.exp(grp - g_max), 0.0), axis=-1, keepdims=True)

    base = N_GROUPS + EXPERTS_PER_GROUP * g_sel
    in_grp = (lane >= base) & (lane < base + EXPERTS_PER_GROUP)
    ex = jnp.where(in_grp, logits, neg)
    e_max = jnp.max(ex, axis=-1, keepdims=True)
    pe = jnp.where(in_grp, jnp.exp(ex - e_max), 0.0)
    prob = pe / jnp.sum(pe, axis=-1, keepdims=True)
    p1 = jnp.max(prob, axis=-1, keepdims=True)
    l1 = first_lane(in_grp & (prob == p1))
    rest = jnp.where(in_grp & (lane != l1), prob, -1.0)
    p2 = jnp.max(rest, axis=-1, keepdims=True)
    l2 = first_lane(rest == p2)
    scale = grp_w / (p1 + p2)
    e1, e2 = l1 - N_GROUPS, l2 - N_GROUPS

    @pl.when(pl.program_id(0) == 0)
    def _():
        carry_ref[...] = jnp.zeros_like(carry_ref)

    oh1 = jnp.where(lane == e1, 1.0, 0.0)
    oh2 = jnp.where(lane == e2, 1.0, 0.0)
    rr = lax.broadcasted_iota(jnp.int32, (tm, tm), 0)
    cc = lax.broadcasted_iota(jnp.int32, (tm, tm), 1)
    tri = jnp.where(cc < rr, 1.0, 0.0).astype(_BF16)
    before = lambda oh: jnp.dot(tri, oh.astype(_BF16), preferred_element_type=_F32)
    carry = carry_ref[...]
    tot1 = jnp.sum(oh1, axis=0, keepdims=True)
    rank1 = jnp.sum(oh1 * (before(oh1) + carry), axis=-1, keepdims=True)
    rank2 = jnp.sum(oh2 * (before(oh2) + (carry + tot1)), axis=-1, keepdims=True)
    carry = carry + tot1 + jnp.sum(oh2, axis=0, keepdims=True)
    carry_ref[...] = carry
    count_ref[...] = carry

    out = jnp.zeros((tm, LANES), _F32)
    for i, val in enumerate((p1 * scale, p2 * scale, e1.astype(_F32), e2.astype(_F32), rank1, rank2)):
        out = jnp.where(lane == i, val, out)
    route_ref[...] = out


def _mixer_out(att, x, mods, w_out, w_router, b_router, rwkv=None):
    n, d = x.shape
    tm = ROW_TILE
    row = lambda width: pl.BlockSpec((tm, width), lambda i: (i, 0))
    both = pl.BlockSpec((2, tm, B_WIDTH), lambda i: (0, i, 0))
    args, specs = [], []
    if rwkv is not None:
        y, r, v, kd, g, gn = rwkv
        args += [y, r, v, kd, g, gn]
        specs += [both, row(B_WIDTH), row(B_WIDTH), both, row(B_WIDTH), _full(gn.shape)]
    wr_hi, wr_lo = _split2(w_router)
    args += [att, x, mods, w_out, wr_hi, wr_lo, b_router]
    specs += [row(att.shape[1]), row(d), pl.BlockSpec((1, 6, d), lambda i: (_mods_index(i), 0, 0)),
              _full(w_out.shape), _full(wr_hi.shape), _full(wr_lo.shape), _full(b_router.shape)]
    return pl.pallas_call(
        functools.partial(_post_kernel, even=rwkv is not None),
        grid=(n // tm,), in_specs=specs,
        out_specs=[row(d), row(d), row(LANES), _full((1, LANES))],
        out_shape=[jax.ShapeDtypeStruct((n, d), _F32), jax.ShapeDtypeStruct((n, d), _F32),
                   jax.ShapeDtypeStruct((n, LANES), _F32), jax.ShapeDtypeStruct((1, LANES), _F32)],
        scratch_shapes=[pltpu.VMEM((1, LANES), _F32)],
        name="mixer_out_even" if rwkv is not None else "mixer_out_odd", compiler_params=_params(1),
    )(*args)


def _moe_kernel(be_ref, nv_ref, x_ref, wg_ref, wu_ref, wd_ref, o_ref, wg_s, wu_s, wd_s):
    i = pl.program_id(0)
    valid = i < nv_ref[0]
    new_expert = (i == 0) | (be_ref[i] != be_ref[jnp.maximum(i - 1, 0)])

    @pl.when(valid & new_expert)
    def _():
        wg_s[...] = wg_ref[0, 0].astype(_BF16)
        wu_s[...] = wu_ref[0, 0].astype(_BF16)
        wd_s[...] = wd_ref[0, 0].astype(_BF16)

    @pl.when(valid)
    def _():
        x = x_ref[...].astype(_BF16)
        gate = jnp.dot(x, wg_s[...], preferred_element_type=_F32)
        up = jnp.dot(x, wu_s[...], preferred_element_type=_F32)
        hid = (gate * jax.nn.sigmoid(gate) * up).astype(_BF16)
        o_ref[...] = jnp.dot(hid, wd_s[...], preferred_element_type=_F32)

    @pl.when(jnp.logical_not(valid))
    def _():
        o_ref[...] = jnp.zeros_like(o_ref)


def _moe_blocks(block_expert, n_valid, xs, wg, wu, wd, layer):
    rows, d = xs.shape
    nb = rows // MOE_ROWS
    w_spec = lambda a, b: pl.BlockSpec((1, 1, a, b), lambda i, be, nv: (layer, be[i], 0, 0))
    grid_spec = pltpu.PrefetchScalarGridSpec(
        num_scalar_prefetch=2,
        grid=(nb,),
        in_specs=[pl.BlockSpec((MOE_ROWS, d), lambda i, be, nv: (i, 0)),
                  w_spec(d, EXPERT_HIDDEN), w_spec(d, EXPERT_HIDDEN), w_spec(EXPERT_HIDDEN, d)],
        out_specs=pl.BlockSpec((MOE_ROWS, d), lambda i, be, nv: (i, 0)),
        scratch_shapes=[pltpu.VMEM((d, EXPERT_HIDDEN), _BF16), pltpu.VMEM((d, EXPERT_HIDDEN), _BF16),
                        pltpu.VMEM((EXPERT_HIDDEN, d), _BF16)],
    )
    return pl.pallas_call(
        _moe_kernel,
        grid_spec=grid_spec,
        out_shape=jax.ShapeDtypeStruct((rows, d), _F32),
        name="moe_ffn", compiler_params=_params(1),
    )(block_expert, n_valid, xs, wg, wu, wd)


def _moe(h, route, counts, w_gate, w_up, w_down, layer):
    n, d = h.shape
    eid = route[:, 2:4].astype(jnp.int32)
    rank = route[:, 4:6].astype(jnp.int32)
    cnt = counts[0, :N_EXPERTS].astype(jnp.int32)
    padded = (cnt + MOE_ROWS - 1) // MOE_ROWS * MOE_ROWS
    pad_end = jnp.cumsum(padded)
    experts = jnp.arange(N_EXPERTS, dtype=jnp.int32)
    pad_start = jnp.sum(jnp.where(eid[:, :, None] == experts, pad_end - padded, 0), axis=-1)
    slot = pad_start + rank
    n_blocks = -(-(n * TOP_K + N_EXPERTS * (MOE_ROWS - 1)) // MOE_ROWS)
    tok = jnp.broadcast_to(jnp.arange(n, dtype=jnp.int32)[:, None], (n, TOP_K))
    slot_tok = jnp.zeros((n_blocks * MOE_ROWS,), jnp.int32).at[slot.reshape(-1)].set(tok.reshape(-1))
    block_start = jnp.arange(n_blocks, dtype=jnp.int32) * MOE_ROWS
    block_expert = jnp.minimum(jnp.sum(pad_end[None, :] <= block_start[:, None], axis=1),
                               N_EXPERTS - 1).astype(jnp.int32)
    n_valid = (pad_end[-1:] // MOE_ROWS).astype(jnp.int32)
    yb = _moe_blocks(block_expert, n_valid, h[slot_tok], w_gate, w_up, w_down, layer)
    return yb[slot[:, 0]], yb[slot[:, 1]]


def _final_kernel(x_ref, y1_ref, y2_ref, g_ref, m_ref, gain_ref, o_ref):
    g = g_ref[...]
    x = x_ref[...] + m_ref[0, 5:6, :] * (g[:, 0:1] * y1_ref[...] + g[:, 1:2] * y2_ref[...])
    o_ref[...] = _rms_rows(x) * gain_ref[...]


def _final(x, y1, y2, route, mods, gain, bsz, n_lat):
    d = x.shape[1]
    tm = ROW_TILE
    lat_tiles = n_lat // tm
    tiles = (CTX_LEN + n_lat) // tm
    src = lambda i: (i // lat_tiles) * tiles + CTX_LEN // tm + i % lat_tiles
    row = lambda width: pl.BlockSpec((tm, width), lambda i: (src(i), 0))
    return pl.pallas_call(
        _final_kernel,
        grid=(bsz * lat_tiles,),
        in_specs=[row(d), row(d), row(d), row(LANES),
                  pl.BlockSpec((1, 6, d), lambda i: ((i // lat_tiles) * 2 + 1, 0, 0)), _full((1, d))],
        out_specs=pl.BlockSpec((tm, d), lambda i: (i, 0)),
        out_shape=jax.ShapeDtypeStruct((bsz * n_lat, d), _F32),
        name="final_norm", compiler_params=_params(1),
    )(x, y1, y2, route, mods, gain.reshape(1, d))


def _rope_tables(n_lat):
    nf = HEAD_DIM // 4
    inv = ROPE_THETA ** (-jnp.arange(nf, dtype=_F32) / nf)
    rows = n_lat // GRID_W
    r_ang = jnp.repeat(jnp.arange(rows, dtype=_F32), GRID_W)[:, None] * inv
    c_ang = jnp.tile(jnp.arange(GRID_W, dtype=_F32), rows)[:, None] * inv
    cos = jnp.concatenate([jnp.cos(r_ang)] * 2 + [jnp.cos(c_ang)] * 2, axis=-1)
    sin = jnp.concatenate([-jnp.sin(r_ang), jnp.sin(r_ang), -jnp.sin(c_ang), jnp.sin(c_ang)], axis=-1)
    cos = jnp.concatenate([jnp.ones((CTX_LEN, HEAD_DIM), _F32), cos], axis=0)
    sin = jnp.concatenate([jnp.zeros((CTX_LEN, HEAD_DIM), _F32), sin], axis=0)
    return jnp.tile(cos, (1, 2)), jnp.tile(sin, (1, 2))


def kernel(x, c, ctx, c_ctx, mod_w, mod_b, ev_w_in, ev_w_out, ev_q_gain, ev_k_gain, ev_mu_prev, ev_mu_next, ev_w0, ev_w2, ev_a0, ev_a2, ev_g2, ev_k_k, ev_k_a, ev_r_k, ev_gn_w, ev_gn_b, ev_v0, ev_v1, ev_v2, od_w_in, od_w_out, od_lq1, od_lk1, od_lq2, od_lk2, od_subln, moe_w_grp, moe_b_grp, moe_w_rt, moe_b_rt, moe_w_gate, moe_w_up, moe_w_down, final_gain):
    bsz, n_lat, d = x.shape
    t = CTX_LEN + n_lat
    n = bsz * t
    cos, sin = _rope_tables(n_lat)
    cond = jax.nn.silu(jnp.concatenate([c, c_ctx[None, :], jnp.zeros((16 - bsz - 1, d), _F32)], axis=0))
    xs = jnp.concatenate([ctx, x], axis=1).reshape(n, d)
    v_first = None
    moe = None
    for layer in range(DEPTH):
        m = _mm(cond, mod_w[layer], tn=1536, precision=_HI) + mod_b[layer]
        m = m.reshape(16, 6, d)
        mods = jnp.stack([jnp.broadcast_to(m[bsz], (bsz, 6, d)), m[:bsz]], axis=1).reshape(2 * bsz, 6, d)
        w_router = jnp.concatenate(
            [moe_w_grp[layer], moe_w_rt[layer], jnp.zeros((d, LANES - N_GROUPS - N_EXPERTS), _F32)], axis=1)
        b_router = jnp.concatenate(
            [moe_b_grp[layer], moe_b_rt[layer], jnp.zeros((LANES - N_GROUPS - N_EXPERTS,), _F32)])[None, :]
        if layer % 2 == 0:
            e = layer // 2
            p = {'q_gain': ev_q_gain[e], 'k_gain': ev_k_gain[e], 'mu_prev': ev_mu_prev[e],
                 'mu_next': ev_mu_next[e], 'w0': ev_w0[e], 'w2': ev_w2[e], 'a0': ev_a0[e],
                 'a2': ev_a2[e], 'g2': ev_g2[e], 'k_k': ev_k_k[e], 'k_a': ev_k_a[e]}
            if e > 0:
                p['v0'], p['v1'], p['v2'] = ev_v0[e - 1], ev_v1[e - 1], ev_v2[e - 1]
            proj, xs = _project(xs, mods, ev_w_in[e].astype(_BF16), moe=moe)
            q, kk2, vv2, r, v, kkn, lw, a, kd, g = _prepare_even(proj, cos, sin, p, v_first)
            if v_first is None:
                v_first = v
            att = _gqa_attend(q.reshape(bsz, t, -1), kk2.reshape(bsz, t, -1), vv2.reshape(bsz, t, -1))
            seq = lambda z: z.reshape(*z.shape[:-2], bsz, t, B_WIDTH)
            y = _delta_scan(seq(r), seq(v), seq(kkn), seq(lw), seq(a), seq(kd)).reshape(2, n, B_WIDTH)
            gn = jnp.stack([ev_gn_w[e], ev_gn_b[e], ev_r_k[e].reshape(B_WIDTH)])
            xs, h, route, counts = _mixer_out(att.reshape(n, -1), xs, mods, ev_w_out[e].astype(_BF16),
                                       w_router, b_router, rwkv=(y, r, v, kd, g, gn))
        else:
            o = layer // 2
            lambda_init = 0.8 - 0.6 * math.exp(-0.3 * layer)
            qkv, xs = _project(xs, mods, od_w_in[o].astype(_BF16), moe=moe, rope=(cos, sin))
            lam = (jnp.exp(jnp.sum(od_lq1[o] * od_lk1[o])) - jnp.exp(jnp.sum(od_lq2[o] * od_lk2[o]))
                   + lambda_init).reshape(1, 1)
            gain = (od_subln[o] * (1.0 - lambda_init)).reshape(1, LANES)
            att = _diff_attend(lam, qkv.reshape(bsz, t, -1), gain)
            xs, h, route, counts = _mixer_out(att.reshape(n, -1), xs, mods, od_w_out[o].astype(_BF16),
                                       w_router, b_router)
        y1, y2 = _moe(h, route, counts, moe_w_gate, moe_w_up, moe_w_down, layer)
        moe = (y1, y2, route, mods)
    return _final(xs, y1, y2, route, mods, final_gain, bsz, n_lat).reshape(bsz, n_lat, d)
```

```python
import functools
import math

import jax
import jax.numpy as jnp
from jax import lax
from jax.experimental import pallas as pl
from jax.experimental.pallas import tpu as pltpu

D_MODEL = 1024
DEPTH = 4
SEQ = 2048
GRID_W = 64
CTX_LEN = 256
HEAD_DIM = 64
ROPE_THETA = 10000.0
RMS_EPS = 1e-6
SUBLN_EPS = 1e-5

A_HEADS = 8
A_KV_HEADS = 2
A_WIDTH = 512
A_KV_WIDTH = 128
A_COLS = 768

B_HEADS = 8
B_WIDTH = 512
B_COLS = 1792
RWKV_W_LORA = 64
RWKV_A_LORA = 64
RWKV_V_LORA = 32
RWKV_G_LORA = 128
RWKV_GN_EPS = 64e-5
DECAY_SCALE = math.exp(-0.5)

C_HEADS = 8
C_QK = 1024

N_GROUPS = 4
EXPERTS_PER_GROUP = 8
N_EXPERTS = 32
TOP_K = 2
EXPERT_HIDDEN = 512
MOE_ROWS = 256

LANES = 128
SCAN_CHUNK = 64
ROW_TILE = 256
Q_TILE = 128
VMEM_LIMIT = 56 * 1024 * 1024

_HI = lax.Precision.HIGHEST
_F32 = jnp.float32
_BF16 = jnp.bfloat16

_NN = (((1,), (0,)), ((), ()))
_NT = (((1,), (1,)), ((), ()))
_TN = (((0,), (0,)), ((), ()))


def _params(n_axes):
    return pltpu.CompilerParams(dimension_semantics=("arbitrary",) * n_axes,
                                vmem_limit_bytes=VMEM_LIMIT)


def _full(shape):
    return pl.BlockSpec(shape, lambda *_: (0,) * len(shape))


def _split2(x):
    hi = x.astype(_BF16)
    return hi, (x - hi.astype(_F32)).astype(_BF16)


def _dot3(a, b, dims=_NN):
    d = lambda u, w: lax.dot_general(u, w, dims, preferred_element_type=_F32)
    return (d(a[1], b[0]) + d(a[0], b[1])) + d(a[0], b[0])


def _dot1(a, b, dims=_NN):
    return lax.dot_general(a.astype(_BF16), b.astype(_BF16), dims, preferred_element_type=_F32)


def _pack_bf16_pairs(x):
    w = x.shape[1] // 2
    bits = pltpu.bitcast(x.astype(_BF16).astype(_F32), jnp.uint32)
    return (bits[:, :w] >> 16) | (bits[:, w:] & jnp.uint32(0xFFFF0000))


def _unpack_bf16_pairs(packed):
    lo = pltpu.bitcast(packed << 16, _F32)
    hi = pltpu.bitcast(packed & jnp.uint32(0xFFFF0000), _F32)
    return jnp.concatenate([lo, hi], axis=1).astype(_BF16)


def _lane_lo(shape):
    return lax.broadcasted_iota(jnp.int32, shape, len(shape) - 1) < HEAD_DIM


def _head_sum_matrix(value):
    row = lax.broadcasted_iota(jnp.int32, (LANES, LANES), 0)
    col = lax.broadcasted_iota(jnp.int32, (LANES, LANES), 1)
    return jnp.where((row // HEAD_DIM) == (col // HEAD_DIM), value, 0.0).astype(_BF16)


def _head_sum(x, mat):
    hi, lo = _split2(x)
    return jnp.dot(lo, mat, preferred_element_type=_F32) + jnp.dot(hi, mat, preferred_element_type=_F32)


def _rms_rows(x, eps=RMS_EPS):
    return x * lax.rsqrt(jnp.mean(x * x, axis=-1, keepdims=True) + eps)


def _rope_slab(x, cos, sin):
    lane = lax.broadcasted_iota(jnp.int32, x.shape, 1)
    partner = jnp.where((lane & 16) == 0, pltpu.roll(x, LANES - 16, axis=1), pltpu.roll(x, 16, axis=1))
    return x * cos + partner * sin


def _mods_index(i):
    tiles = (CTX_LEN + SEQ) // ROW_TILE
    return (i // tiles) * 2 + jnp.minimum(i % tiles, 1)


def _mm_kernel(a_ref, w_ref, o_ref, *, precision):
    o_ref[...] = jnp.dot(a_ref[...], w_ref[...], precision=precision,
                         preferred_element_type=_F32).astype(o_ref.dtype)


def _mm(a, w, *, tn, precision=None):
    m, k = a.shape
    n = w.shape[1]
    return pl.pallas_call(
        functools.partial(_mm_kernel, precision=precision),
        grid=(n // tn,),
        in_specs=[pl.BlockSpec((m, k), lambda j: (0, 0)),
                  pl.BlockSpec((k, tn), lambda j: (0, j))],
        out_specs=pl.BlockSpec((m, tn), lambda j: (0, j)),
        out_shape=jax.ShapeDtypeStruct((m, n), _F32),
        name=f"mm_{k}x{n}",
        compiler_params=_params(1),
    )(a, w)


def _proj_kernel(*refs, combine, rope):
    it = iter(refs)
    x_ref, m_ref = next(it), next(it)
    if combine:
        y1_ref, y2_ref, g_ref, mp_ref = (next(it) for _ in range(4))
    w_ref = next(it)
    if rope:
        cos_ref, sin_ref = next(it), next(it)
    o_ref = next(it)
    x = x_ref[...]
    if combine:
        xo_ref = next(it)
        g = g_ref[...]
        x = x + mp_ref[0, 5:6, :] * (g[:, 0:1] * y1_ref[...] + g[:, 1:2] * y2_ref[...])
        xo_ref[...] = x
    h = _rms_rows(x) * (1.0 + m_ref[0, 1:2, :]) + m_ref[0, 0:1, :]
    p = jnp.dot(h.astype(_BF16), w_ref[...], preferred_element_type=_F32)
    if rope:
        cos, sin = cos_ref[...], sin_ref[...]
        for s in range(2 * C_QK // LANES):
            sl = slice(s * LANES, (s + 1) * LANES)
            slab = _rope_slab(p[:, sl], cos, sin)
            if s < C_QK // LANES:
                slab = slab * HEAD_DIM ** -0.5
            o_ref[:, sl] = slab.astype(o_ref.dtype)
        o_ref[:, 2 * C_QK:] = p[:, 2 * C_QK:].astype(o_ref.dtype)
    else:
        o_ref[...] = p


def _project(x, mods, w, *, moe=None, rope=None):
    n, d = x.shape
    cols = w.shape[1]
    tm = ROW_TILE
    tiles = (CTX_LEN + SEQ) // tm
    row = lambda width: pl.BlockSpec((tm, width), lambda i: (i, 0))
    mod_spec = pl.BlockSpec((1, 6, d), lambda i: (_mods_index(i), 0, 0))
    args, specs = [x, mods], [row(d), mod_spec]
    if moe is not None:
        args += list(moe)
        specs += [row(d), row(d), row(LANES), mod_spec]
    args.append(w)
    specs.append(_full(w.shape))
    if rope is not None:
        args += list(rope)
        specs += [pl.BlockSpec((tm, LANES), lambda i: (i % tiles, 0))] * 2
    out_shape = [jax.ShapeDtypeStruct((n, cols), _BF16 if rope is not None else _F32)]
    out_specs = [row(cols)]
    if moe is not None:
        out_shape.append(jax.ShapeDtypeStruct((n, d), _F32))
        out_specs.append(row(d))
    out = pl.pallas_call(
        functools.partial(_proj_kernel, combine=moe is not None, rope=rope is not None),
        grid=(n // tm,), in_specs=specs, out_specs=out_specs, out_shape=out_shape,
        name=f"project_{cols}", compiler_params=_params(1),
    )(*args)
    return out if moe is not None else (out[0], x)


def _prep_kernel(*refs, has_v_first):
    it = iter(refs)
    p_ref, pp_ref, pn_ref, cos_ref, sin_ref = (next(it) for _ in range(5))
    gain_ref, mu_ref, wa0_ref, kv_ref, wl_ref, g2_ref = (next(it) for _ in range(6))
    if has_v_first:
        v1_ref, v2_ref, vf_ref = next(it), next(it), next(it)
    q_ref, kk2_ref, vv2_ref, r_ref, v_ref, kkn_ref, lw_ref, a_ref, kd_ref, g_ref = it

    tiles = (CTX_LEN + SEQ) // ROW_TILE
    j = pl.program_id(0) % tiles
    p = p_ref[...]
    cos, sin = cos_ref[...], sin_ref[...]
    mean_mat = _head_sum_matrix(1.0 / HEAD_DIM)
    ones_mat = _head_sum_matrix(1.0)
    lo = _lane_lo((ROW_TILE, LANES))

    def normed(x, gain):
        return x * lax.rsqrt(_head_sum(x * x, mean_mat) + RMS_EPS) * gain

    for s in range(A_WIDTH // LANES):
        sl = slice(s * LANES, (s + 1) * LANES)
        q = _rope_slab(normed(p[:, sl], gain_ref[0:1, :]), cos, sin)
        q_ref[:, sl] = (q * HEAD_DIM ** -0.5).astype(_BF16)
    k = _rope_slab(normed(p[:, A_WIDTH:A_WIDTH + LANES], gain_ref[1:2, :]), cos, sin)
    v = p[:, A_WIDTH + LANES:A_COLS]
    for src, dst in ((k, kk2_ref), (v, vv2_ref)):
        swapped = pltpu.roll(src, HEAD_DIM, axis=1)
        dst[:, :LANES] = jnp.where(lo, src, swapped).astype(_BF16)
        dst[:, LANES:] = jnp.where(lo, swapped, src).astype(_BF16)

    u = p[:, A_COLS:]
    rid = lax.broadcasted_iota(jnp.int32, (ROW_TILE, 1), 0)
    has_prev = jnp.where(j >= 2, 1.0, 0.0)
    has_next = jnp.where((j >= 1) & (j < tiles - 1), 1.0, 0.0)
    first = pp_ref[7:8, A_COLS:] * has_prev
    last = pn_ref[0:1, A_COLS:] * has_next
    prev = jnp.where(rid == 0, first, pltpu.roll(u, 1, axis=0))
    nxt = jnp.where(rid == ROW_TILE - 1, last, pltpu.roll(u, ROW_TILE - 1, axis=0))
    u = u + mu_ref[0:1, :] * (prev - u) + mu_ref[1:2, :] * (nxt - u)
    r, k, v = u[:, :B_WIDTH], u[:, B_WIDTH:2 * B_WIDTH], u[:, 2 * B_WIDTH:3 * B_WIDTH]
    wa, glr = u[:, 3 * B_WIDTH:3 * B_WIDTH + LANES], u[:, 3 * B_WIDTH + LANES:]
    if has_v_first:
        mid = jnp.dot(v.astype(_BF16), v1_ref[...], preferred_element_type=_F32)
        lora = jnp.dot(mid.astype(_BF16), v2_ref[...], preferred_element_type=_F32)
        v = v + (vf_ref[...] - v) * jax.nn.sigmoid(kv_ref[2:3, :] + lora)
    r_ref[...] = r
    v_ref[...] = v
    kks = k * kv_ref[0:1, :]
    for s in range(B_WIDTH // LANES):
        sl = slice(s * LANES, (s + 1) * LANES)
        x = kks[:, sl]
        kkn_ref[:, sl] = x * lax.rsqrt(jnp.maximum(_head_sum(x * x, ones_mat), 1e-24))
    xw = jnp.where(lo, jnp.tanh(wa), wa).astype(_BF16)
    for dr in range(2):
        z = jnp.dot(xw, wl_ref[dr], preferred_element_type=_F32) + wa0_ref[dr:dr + 1, :]
        a = jax.nn.sigmoid(z[:, B_WIDTH:])
        lw_ref[dr] = -DECAY_SCALE * jax.nn.sigmoid(z[:, :B_WIDTH])
        a_ref[dr] = a
        kd_ref[dr] = k * (1.0 + (a - 1.0) * kv_ref[1:2, :])
    g_ref[...] = jnp.dot(jax.nn.sigmoid(glr).astype(_BF16), g2_ref[...], preferred_element_type=_F32)


def _prepare_even(proj, cos, sin, p, v_first):
    n, cols = proj.shape
    tm = ROW_TILE
    tiles = (CTX_LEN + SEQ) // tm
    halo = 8
    per_tile = tm // halo
    row = lambda width: pl.BlockSpec((tm, width), lambda i: (i, 0))
    both = pl.BlockSpec((2, tm, B_WIDTH), lambda i: (0, i, 0))
    tab = pl.BlockSpec((tm, LANES), lambda i: (i % tiles, 0))
    pad = lambda w, rows, width: jnp.zeros((rows, width), _BF16).at[:w.shape[0], :w.shape[1]].set(w.astype(_BF16))
    gains = jnp.stack([jnp.tile(p['q_gain'], 2), jnp.tile(p['k_gain'], 2)])
    mu = jnp.stack([p['mu_prev'], p['mu_next']])
    wa0 = jnp.concatenate([p['w0'], p['a0']], axis=1)
    kvec = jnp.stack([p['k_k'], p['k_a'], p.get('v0', jnp.zeros((B_WIDTH,), _F32))])
    zero = jnp.zeros((RWKV_W_LORA, B_WIDTH), _F32)
    wl = jnp.stack([jnp.concatenate([jnp.concatenate([p['w2'][dr], zero], axis=1),
                                     jnp.concatenate([zero, p['a2'][dr]], axis=1)], axis=0)
                    for dr in range(2)]).astype(_BF16)
    args = [proj, proj, proj, cos, sin, gains, mu, wa0, kvec, wl, p['g2'].astype(_BF16)]
    specs = [row(cols),
             pl.BlockSpec((halo, cols), lambda i: (jnp.maximum(i * per_tile - 1, 0), 0)),
             pl.BlockSpec((halo, cols), lambda i: (jnp.minimum((i + 1) * per_tile, n // halo - 1), 0)),
             tab, tab, _full(gains.shape), _full(mu.shape), _full(wa0.shape), _full(kvec.shape),
             _full(wl.shape), _full(p['g2'].shape)]
    if v_first is not None:
        v1 = pad(p['v1'], B_WIDTH, LANES)
        v2 = pad(p['v2'], LANES, B_WIDTH)
        args += [v1, v2, v_first]
        specs += [_full(v1.shape), _full(v2.shape), row(B_WIDTH)]
    f32 = lambda *shape: jax.ShapeDtypeStruct(shape, _F32)
    bf16 = lambda *shape: jax.ShapeDtypeStruct(shape, _BF16)
    out_shape = [bf16(n, A_WIDTH), bf16(n, 2 * LANES), bf16(n, 2 * LANES),
                 f32(n, B_WIDTH), f32(n, B_WIDTH), f32(n, B_WIDTH),
                 f32(2, n, B_WIDTH), f32(2, n, B_WIDTH), f32(2, n, B_WIDTH), f32(n, B_WIDTH)]
    out_specs = [row(A_WIDTH), row(2 * LANES), row(2 * LANES), row(B_WIDTH), row(B_WIDTH), row(B_WIDTH),
                 both, both, both, row(B_WIDTH)]
    return pl.pallas_call(
        functools.partial(_prep_kernel, has_v_first=v_first is not None),
        grid=(n // tm,), in_specs=specs, out_specs=out_specs, out_shape=out_shape,
        name="prepare_even", compiler_params=_params(1),
    )(*args)


def _split_heads(q2):
    lo = _lane_lo(q2.shape)
    zero = jnp.zeros_like(q2)
    return jnp.concatenate([jnp.where(lo, q2, zero), jnp.where(lo, zero, q2)], axis=0)


def _scores(qs, k):
    return lax.dot_general(qs, k, _NT, preferred_element_type=_F32)


def _interleave(chains):
    results = {}
    while len(results) < len(chains):
        for p, chain in enumerate(chains):
            try:
                next(chain)
            except StopIteration as stop:
                results[p] = stop.value
    return [results[p] for p in range(len(chains))]


def _gqa_tile(q, k, v):
    tq = q.shape[0]
    qs = jnp.concatenate([_split_heads(q[:, :LANES]), _split_heads(q[:, LANES:])], axis=0)
    s = _scores(qs, k)
    yield
    p = jnp.exp(s - jnp.max(s, axis=-1, keepdims=True))
    l = jnp.sum(p, axis=-1, keepdims=True)
    p = p.astype(_BF16)
    yield
    o = jnp.dot(p, v, preferred_element_type=_F32) / l
    lo = _lane_lo((tq, LANES))
    return jnp.concatenate([jnp.where(lo, o[:tq], o[tq:2 * tq]),
                            jnp.where(lo, o[2 * tq:3 * tq], o[3 * tq:])], axis=1)


def _attend_halves(tile_fn, q_ref, o_ref):
    halves = [slice(i * Q_TILE, (i + 1) * Q_TILE) for i in range(q_ref.shape[1] // Q_TILE)]
    outs = _interleave([tile_fn(q_ref[0, sl]) for sl in halves])
    for sl, o in zip(halves, outs):
        o_ref[0, sl] = o.astype(o_ref.dtype)


def _gqa_kernel(q_ref, k_ref, v_ref, o_ref):
    is_ctx = pl.program_id(2) < CTX_LEN // q_ref.shape[1]

    @pl.when(is_ctx)
    def _():
        _attend_halves(lambda q: _gqa_tile(q, k_ref[0, :CTX_LEN], v_ref[0, :CTX_LEN]), q_ref, o_ref)

    @pl.when(jnp.logical_not(is_ctx))
    def _():
        _attend_halves(lambda q: _gqa_tile(q, k_ref[0], v_ref[0]), q_ref, o_ref)


def _gqa_attend(q, kk, vv):
    b, t, _ = q.shape
    tq = 2 * Q_TILE
    kv_spec = pl.BlockSpec((1, t, LANES), lambda i, g, j: (i, 0, g))
    return pl.pallas_call(
        _gqa_kernel,
        grid=(b, A_KV_HEADS, t // tq),
        in_specs=[pl.BlockSpec((1, tq, 2 * LANES), lambda i, g, j: (i, j, g)), kv_spec, kv_spec],
        out_specs=pl.BlockSpec((1, tq, 2 * LANES), lambda i, g, j: (i, j, g)),
        out_shape=jax.ShapeDtypeStruct((b, t, A_WIDTH), _BF16),
        name="gqa_attn", compiler_params=_params(3),
    )(q, kk, vv)


def _diff_tile(lam, q, k, v, gain):
    tq = q.shape[0]
    s = _scores(_split_heads(q), k)
    yield
    p = jnp.exp(s - jnp.max(s, axis=-1, keepdims=True))
    l = jnp.sum(p, axis=-1, keepdims=True)
    l1, l2 = l[:tq], l[tq:]
    pr = (p[:tq] - (lam * l1 / l2) * p[tq:]).astype(_BF16)
    yield
    o = jnp.dot(pr, v, preferred_element_type=_F32) / l1
    y = o * lax.rsqrt(jnp.mean(o * o, axis=-1, keepdims=True) + SUBLN_EPS)
    return y * gain


def _diff_kernel(lam_ref, q_ref, k_ref, v_ref, g_ref, o_ref):
    is_ctx = pl.program_id(2) < CTX_LEN // q_ref.shape[1]
    lam, gain = lam_ref[0, 0], g_ref[...]

    @pl.when(is_ctx)
    def _():
        _attend_halves(lambda q: _diff_tile(lam, q, k_ref[0, :CTX_LEN], v_ref[0, :CTX_LEN], gain),
                       q_ref, o_ref)

    @pl.when(jnp.logical_not(is_ctx))
    def _():
        _attend_halves(lambda q: _diff_tile(lam, q, k_ref[0], v_ref[0], gain), q_ref, o_ref)


def _diff_attend(lam, qkv, gain):
    b, t, _ = qkv.shape
    tq = 2 * Q_TILE
    return pl.pallas_call(
        _diff_kernel,
        grid=(b, C_HEADS, t // tq),
        in_specs=[pl.BlockSpec(memory_space=pltpu.SMEM),
                  pl.BlockSpec((1, tq, LANES), lambda i, h, j: (i, j, h)),
                  pl.BlockSpec((1, t, LANES), lambda i, h, j: (i, 0, C_HEADS + h)),
                  pl.BlockSpec((1, t, LANES), lambda i, h, j: (i, 0, 2 * C_HEADS + h)),
                  _full((1, LANES))],
        out_specs=pl.BlockSpec((1, tq, LANES), lambda i, h, j: (i, j, h)),
        out_shape=jax.ShapeDtypeStruct((b, t, C_HEADS * LANES), _BF16),
        name="diff_attn", compiler_params=_params(3),
    )(lam, qkv, qkv, qkv, gain)


def _scan_chunk(h, r, v, kk, lw, cum, cum_end, a, kd, sgn):
    c = SCAN_CHUNK
    lo = _lane_lo((c, LANES))
    row = lax.broadcasted_iota(jnp.int32, (LANES, LANES), 0)
    col = lax.broadcasted_iota(jnp.int32, (LANES, LANES), 1)
    delta = ((col & (c - 1)) - (row & (c - 1))) * sgn
    before = delta < 0
    upto = delta <= 0
    eye = row == col
    zero = jnp.zeros((LANES, LANES), _F32)

    def blockdiag(x):
        z = jnp.zeros_like(x)
        return jnp.concatenate([jnp.where(lo, x, z), jnp.where(lo, z, x)], axis=0)

    p_inv = jnp.exp(-cum)
    p_end = jnp.exp(cum_end - cum)
    kka = kk * a
    a_rs = blockdiag(-kk * jnp.exp(cum - lw))
    r_rs = blockdiag(r * jnp.exp(cum))
    b_rs = blockdiag(kka * p_inv)
    k_rs = blockdiag(kd * p_inv)
    v_rs = blockdiag(v)
    bp_rs = blockdiag(kka * p_end)
    kp_rs = blockdiag(kd * p_end)

    z = _dot3(_split2(jnp.concatenate([a_rs, r_rs], axis=0)),
              _split2(jnp.concatenate([b_rs, k_rs], axis=0)), _NT)
    yield
    low = jnp.where(before, z[:LANES, :LANES], zero)
    g = jnp.where(before, z[:LANES, LANES:], zero)
    rb = jnp.where(upto, z[LANES:, :LANES], zero)
    rk = jnp.where(upto, z[LANES:, LANES:], zero)

    x = _dot1(low, low)
    yield
    m = jnp.where(eye, 1.0, 0.0).astype(_F32) + low
    n = 2
    while 2 * n < c:
        res = _dot1(x, jnp.concatenate([m, x], axis=1))
        m = m + res[:, :LANES]
        x = res[:, LANES:]
        n *= 2
        yield
    m = m + _dot1(x, m)
    yield
    gv = _dot1(g, v_rs)
    yield
    au = _dot1(m, jnp.concatenate([a_rs, gv], axis=1))
    yield
    rhs = _split2(jnp.concatenate(
        [au, jnp.concatenate([zero, v_rs], axis=1)], axis=0))
    tf = lax.dot_general(jnp.concatenate([bp_rs, kp_rs], axis=0).astype(_BF16), rhs[0], _TN,
                         preferred_element_type=_F32)
    qy = _dot3(_split2(jnp.concatenate([rb, rk], axis=1)), rhs)
    yield
    t = jnp.where(eye, jnp.broadcast_to(jnp.exp(cum_end), (LANES, LANES)), zero) + tf[:, :LANES]
    q = r_rs + qy[:, :LANES]
    out = _dot1(jnp.concatenate([q, t], axis=0), h)
    y = out[:LANES] + qy[:, LANES:]
    h_new = out[LANES:] + tf[:, LANES:]
    return y[:c] + y[c:], h_new


def _scan_kernel(*refs):
    c = SCAN_CHUNK
    ins, y_refs, h_ref = refs[:12], refs[12:14], refs[14]

    @pl.when(pl.program_id(1) == 0)
    def _():
        h_ref[...] = jnp.zeros_like(h_ref)

    tr = lax.broadcasted_iota(jnp.int32, (c, c), 0)
    tc = lax.broadcasted_iota(jnp.int32, (c, c), 1)
    chains, outs = [], []
    for d, sgn in enumerate((1, -1)):
        r_ref, v_ref, kk_ref, lw_ref, a_ref, kd_ref = ins[6 * d:6 * d + 6]
        tri = jnp.where((tc - tr) * sgn <= 0, 1.0, 0.0).astype(_BF16)
        lw = lw_ref[0, 0]
        l1 = lw.astype(_BF16)
        rem = lw - l1.astype(_F32)
        l2 = rem.astype(_BF16)
        l3 = (rem - l2.astype(_F32)).astype(_BF16)
        dot = lambda w, tri=tri: jnp.dot(tri, w, preferred_element_type=_F32)
        cum = (dot(l3) + dot(l2)) + dot(l1)
        cum_end = jnp.sum(lw, axis=0, keepdims=True)
        pairs = lw.shape[1] // LANES
        for p in range(pairs):
            sl = slice(p * LANES, (p + 1) * LANES)
            chains.append(_scan_chunk(h_ref[d * pairs + p], r_ref[0, :, sl], v_ref[0, :, sl],
                                      kk_ref[0, :, sl], lw[:, sl], cum[:, sl], cum_end[:, sl],
                                      a_ref[0, 0, :, sl], kd_ref[0, 0, :, sl], sgn))
            outs.append((y_refs[d], sl, d * pairs + p))
    for (y_ref, sl, slot), (y, h) in zip(outs, _interleave(chains)):
        y_ref[0, :, sl] = y
        h_ref[slot] = h


def _delta_scan(r, v, kk, lw, a, kd):
    b, t, w = r.shape
    c = SCAN_CHUNK
    n_chunks = t // c
    n_ctx = CTX_LEN // c

    def chunk(d, j):
        return j if d == 0 else jnp.where(j < n_ctx, n_ctx - 1 - j, n_chunks + n_ctx - 1 - j)

    specs, args = [], []
    for d in range(2):
        shared = pl.BlockSpec((1, c, w), lambda i, j, d=d: (i, chunk(d, j), 0))
        per_dir = pl.BlockSpec((1, 1, c, w), lambda i, j, d=d: (d, i, chunk(d, j), 0))
        specs += [shared, shared, shared, per_dir, per_dir, per_dir]
        args += [r, v, kk, lw, a, kd]
    out_specs = [pl.BlockSpec((1, c, w), lambda i, j, d=d: (i, chunk(d, j), 0)) for d in range(2)]
    return pl.pallas_call(
        _scan_kernel,
        grid=(b, n_chunks),
        in_specs=specs,
        out_specs=out_specs,
        out_shape=[jax.ShapeDtypeStruct((b, t, w), _F32)] * 2,
        scratch_shapes=[pltpu.VMEM((2 * w // LANES, LANES, LANES), _F32)],
        name="delta_scan", compiler_params=_params(2),
    )(*args)


def _post_kernel(*refs, even):
    it = iter(refs)
    if even:
        yf_ref, yb_ref, r_ref, v_ref, kd_ref, g_ref, gn_ref = (next(it) for _ in range(7))
    att_ref, x_ref, m_ref, wo_ref, wrh_ref, wrl_ref, br_ref = (next(it) for _ in range(7))
    xo_ref, h_ref, route_ref, count_ref, carry_ref = it
    if even:
        mean_mat = _head_sum_matrix(1.0 / HEAD_DIM)
        ones_mat = _head_sum_matrix(1.0)
        parts = [att_ref[...]]
        for s in range(B_WIDTH // LANES):
            sl = slice(s * LANES, (s + 1) * LANES)
            y = yf_ref[:, sl] + yb_ref[:, sl]
            dev = y - _head_sum(y, mean_mat)
            var = _head_sum(dev * dev, mean_mat)
            y = dev * lax.rsqrt(var + RWKV_GN_EPS) * gn_ref[0:1, sl] + gn_ref[1:2, sl]
            r, v = r_ref[:, sl], v_ref[:, sl]
            for dr in range(2):
                y = y + _head_sum(r * kd_ref[dr, :, sl] * gn_ref[2:3, sl], ones_mat) * v
            parts.append((y * g_ref[:, sl]).astype(_BF16))
        mixed = jnp.concatenate(parts, axis=1)
    else:
        mixed = att_ref[...]
    out = jnp.dot(mixed, wo_ref[...], preferred_element_type=_F32)
    x = x_ref[...] + m_ref[0, 2:3, :] * out
    xo_ref[...] = x
    h = _rms_rows(x) * (1.0 + m_ref[0, 4:5, :]) + m_ref[0, 3:4, :]
    h_ref[...] = _pack_bf16_pairs(h)
    logits = _dot3(_split2(h), (wrh_ref[...], wrl_ref[...])) + br_ref[...]
    _route(logits, route_ref, count_ref, carry_ref)


def _route(logits, route_ref, count_ref, carry_ref):
    tm = logits.shape[0]
    lane = lax.broadcasted_iota(jnp.int32, (tm, LANES), 1)
    neg = -1e30
    first_lane = lambda hit: jnp.min(jnp.where(hit, lane, LANES), axis=-1, keepdims=True)

    is_grp = lane < N_GROUPS
    grp = jnp.where(is_grp, logits, neg)
    g_max = jnp.max(grp, axis=-1, keepdims=True)
    g_sel = first_lane(grp == g_max)
    grp_w = 1.0 / jnp.sum(jnp.where(is_grp, jnp.exp(grp - g_max), 0.0), axis=-1, keepdims=True)

    base = N_GROUPS + EXPERTS_PER_GROUP * g_sel
    in_grp = (lane >= base) & (lane < base + EXPERTS_PER_GROUP)
    ex = jnp.where(in_grp, logits, neg)
    e_max = jnp.max(ex, axis=-1, keepdims=True)
    pe = jnp.where(in_grp, jnp.exp(ex - e_max), 0.0)
    prob = pe / jnp.sum(pe, axis=-1, keepdims=True)
    p1 = jnp.max(prob, axis=-1, keepdims=True)
    l1 = first_lane(in_grp & (prob == p1))
    rest = jnp.where(in_grp & (lane != l1), prob, -1.0)
    p2 = jnp.max(rest, axis=-1, keepdims=True)
    l2 = first_lane(rest == p2)
    scale = grp_w / (p1 + p2)
    e1, e2 = l1 - N_GROUPS, l2 - N_GROUPS

    @pl.when(pl.program_id(0) == 0)
    def _():
        carry_ref[...] = jnp.zeros_like(carry_ref)

    oh1 = jnp.where(lane == e1, 1.0, 0.0)
    oh2 = jnp.where(lane == e2, 1.0, 0.0)
    rr = lax.broadcasted_iota(jnp.int32, (tm, tm), 0)
    cc = lax.broadcasted_iota(jnp.int32, (tm, tm), 1)
    tri = jnp.where(cc < rr, 1.0, 0.0).astype(_BF16)
    before = lambda oh: jnp.dot(tri, oh.astype(_BF16), preferred_element_type=_F32)
    carry = carry_ref[...]
    tot1 = jnp.sum(oh1, axis=0, keepdims=True)
    rank1 = jnp.sum(oh1 * (before(oh1) + carry), axis=-1, keepdims=True)
    rank2 = jnp.sum(oh2 * (before(oh2) + (carry + tot1)), axis=-1, keepdims=True)
    carry = carry + tot1 + jnp.sum(oh2, axis=0, keepdims=True)
    carry_ref[...] = carry
    count_ref[...] = carry

    out = jnp.zeros((tm, LANES), _F32)
    for i, val in enumerate((p1 * scale, p2 * scale, e1.astype(_F32), e2.astype(_F32), rank1, rank2)):
        out = jnp.where(lane == i, val, out)
    route_ref[...] = out


def _mixer_out(att, x, mods, w_out, w_router, b_router, rwkv=None):
    n, d = x.shape
    tm = ROW_TILE
    row = lambda width: pl.BlockSpec((tm, width), lambda i: (i, 0))
    both = pl.BlockSpec((2, tm, B_WIDTH), lambda i: (0, i, 0))
    args, specs = [], []
    if rwkv is not None:
        y_fwd, y_bwd, r, v, kd, g, gn = rwkv
        args += [y_fwd, y_bwd, r, v, kd, g, gn]
        specs += [row(B_WIDTH)] * 4 + [both, row(B_WIDTH), _full(gn.shape)]
    wr_hi, wr_lo = _split2(w_router)
    args += [att, x, mods, w_out, wr_hi, wr_lo, b_router]
    specs += [row(att.shape[1]), row(d), pl.BlockSpec((1, 6, d), lambda i: (_mods_index(i), 0, 0)),
              _full(w_out.shape), _full(wr_hi.shape), _full(wr_lo.shape), _full(b_router.shape)]
    return pl.pallas_call(
        functools.partial(_post_kernel, even=rwkv is not None),
        grid=(n // tm,), in_specs=specs,
        out_specs=[row(d), row(d // 2), row(LANES), _full((1, LANES))],
        out_shape=[jax.ShapeDtypeStruct((n, d), _F32), jax.ShapeDtypeStruct((n, d // 2), jnp.uint32),
                   jax.ShapeDtypeStruct((n, LANES), _F32), jax.ShapeDtypeStruct((1, LANES), _F32)],
        scratch_shapes=[pltpu.VMEM((1, LANES), _F32)],
        name="mixer_out_even" if rwkv is not None else "mixer_out_odd", compiler_params=_params(1),
    )(*args)


def _moe_kernel(be_ref, nv_ref, x_ref, wg_ref, wu_ref, wd_ref, o_ref, wg_s, wu_s, wd_s):
    i = pl.program_id(0)
    valid = i < nv_ref[0]
    new_expert = (i == 0) | (be_ref[i] != be_ref[jnp.maximum(i - 1, 0)])

    @pl.when(valid & new_expert)
    def _():
        wg_s[...] = wg_ref[0, 0].astype(_BF16)
        wu_s[...] = wu_ref[0, 0].astype(_BF16)
        wd_s[...] = wd_ref[0, 0].astype(_BF16)

    @pl.when(valid)
    def _():
        x = _unpack_bf16_pairs(x_ref[...])
        gate = jnp.dot(x, wg_s[...], preferred_element_type=_F32)
        up = jnp.dot(x, wu_s[...], preferred_element_type=_F32)
        hid = (gate * jax.nn.sigmoid(gate) * up).astype(_BF16)
        o_ref[...] = jnp.dot(hid, wd_s[...], preferred_element_type=_F32)

    @pl.when(jnp.logical_not(valid))
    def _():
        o_ref[...] = jnp.zeros_like(o_ref)


def _moe_blocks(block_expert, n_valid, xs, wg, wu, wd, layer):
    rows, d = xs.shape[0], 2 * xs.shape[1]
    nb = rows // MOE_ROWS
    w_spec = lambda a, b: pl.BlockSpec((1, 1, a, b), lambda i, be, nv: (layer, be[i], 0, 0))
    grid_spec = pltpu.PrefetchScalarGridSpec(
        num_scalar_prefetch=2,
        grid=(nb,),
        in_specs=[pl.BlockSpec((MOE_ROWS, d // 2), lambda i, be, nv: (i, 0)),
                  w_spec(d, EXPERT_HIDDEN), w_spec(d, EXPERT_HIDDEN), w_spec(EXPERT_HIDDEN, d)],
        out_specs=pl.BlockSpec((MOE_ROWS, d), lambda i, be, nv: (i, 0)),
        scratch_shapes=[pltpu.VMEM((d, EXPERT_HIDDEN), _BF16), pltpu.VMEM((d, EXPERT_HIDDEN), _BF16),
                        pltpu.VMEM((EXPERT_HIDDEN, d), _BF16)],
    )
    return pl.pallas_call(
        _moe_kernel,
        grid_spec=grid_spec,
        out_shape=jax.ShapeDtypeStruct((rows, d), _F32),
        name="moe_ffn", compiler_params=_params(1),
    )(block_expert, n_valid, xs, wg, wu, wd)


def _moe(h, route, counts, w_gate, w_up, w_down, layer):
    n = h.shape[0]
    eid = route[:, 2:4].astype(jnp.int32)
    rank = route[:, 4:6].astype(jnp.int32)
    cnt = counts[0, :N_EXPERTS].astype(jnp.int32)
    padded = (cnt + MOE_ROWS - 1) // MOE_ROWS * MOE_ROWS
    pad_end = jnp.cumsum(padded)
    experts = jnp.arange(N_EXPERTS, dtype=jnp.int32)
    pad_start = jnp.sum(jnp.where(eid[:, :, None] == experts, pad_end - padded, 0), axis=-1)
    slot = pad_start + rank
    n_blocks = -(-(n * TOP_K + N_EXPERTS * (MOE_ROWS - 1)) // MOE_ROWS)
    tok = jnp.broadcast_to(jnp.arange(n, dtype=jnp.int32)[:, None], (n, TOP_K))
    slot_tok = jnp.zeros((n_blocks * MOE_ROWS,), jnp.int32).at[slot.reshape(-1)].set(tok.reshape(-1))
    block_start = jnp.arange(n_blocks, dtype=jnp.int32) * MOE_ROWS
    block_expert = jnp.minimum(jnp.sum(pad_end[None, :] <= block_start[:, None], axis=1),
                               N_EXPERTS - 1).astype(jnp.int32)
    n_valid = (pad_end[-1:] // MOE_ROWS).astype(jnp.int32)
    yb = _moe_blocks(block_expert, n_valid, h[slot_tok], w_gate, w_up, w_down, layer)
    return yb[slot[:, 0]], yb[slot[:, 1]]


def _final_kernel(x_ref, y1_ref, y2_ref, g_ref, m_ref, gain_ref, o_ref):
    g = g_ref[...]
    x = x_ref[...] + m_ref[0, 5:6, :] * (g[:, 0:1] * y1_ref[...] + g[:, 1:2] * y2_ref[...])
    o_ref[...] = _rms_rows(x) * gain_ref[...]


def _final(x, y1, y2, route, mods, gain, bsz, n_lat):
    d = x.shape[1]
    tm = ROW_TILE
    lat_tiles = n_lat // tm
    tiles = (CTX_LEN + n_lat) // tm
    src = lambda i: (i // lat_tiles) * tiles + CTX_LEN // tm + i % lat_tiles
    row = lambda width: pl.BlockSpec((tm, width), lambda i: (src(i), 0))
    return pl.pallas_call(
        _final_kernel,
        grid=(bsz * lat_tiles,),
        in_specs=[row(d), row(d), row(d), row(LANES),
                  pl.BlockSpec((1, 6, d), lambda i: ((i // lat_tiles) * 2 + 1, 0, 0)), _full((1, d))],
        out_specs=pl.BlockSpec((tm, d), lambda i: (i, 0)),
        out_shape=jax.ShapeDtypeStruct((bsz * n_lat, d), _F32),
        name="final_norm", compiler_params=_params(1),
    )(x, y1, y2, route, mods, gain.reshape(1, d))


def _rope_tables(n_lat):
    nf = HEAD_DIM // 4
    inv = ROPE_THETA ** (-jnp.arange(nf, dtype=_F32) / nf)
    rows = n_lat // GRID_W
    r_ang = jnp.repeat(jnp.arange(rows, dtype=_F32), GRID_W)[:, None] * inv
    c_ang = jnp.tile(jnp.arange(GRID_W, dtype=_F32), rows)[:, None] * inv
    cos = jnp.concatenate([jnp.cos(r_ang)] * 2 + [jnp.cos(c_ang)] * 2, axis=-1)
    sin = jnp.concatenate([-jnp.sin(r_ang), jnp.sin(r_ang), -jnp.sin(c_ang), jnp.sin(c_ang)], axis=-1)
    cos = jnp.concatenate([jnp.ones((CTX_LEN, HEAD_DIM), _F32), cos], axis=0)
    sin = jnp.concatenate([jnp.zeros((CTX_LEN, HEAD_DIM), _F32), sin], axis=0)
    return jnp.tile(cos, (1, 2)), jnp.tile(sin, (1, 2))


def kernel(x, c, ctx, c_ctx, mod_w, mod_b, ev_w_in, ev_w_out, ev_q_gain, ev_k_gain, ev_mu_prev, ev_mu_next, ev_w0, ev_w2, ev_a0, ev_a2, ev_g2, ev_k_k, ev_k_a, ev_r_k, ev_gn_w, ev_gn_b, ev_v0, ev_v1, ev_v2, od_w_in, od_w_out, od_lq1, od_lk1, od_lq2, od_lk2, od_subln, moe_w_grp, moe_b_grp, moe_w_rt, moe_b_rt, moe_w_gate, moe_w_up, moe_w_down, final_gain):
    bsz, n_lat, d = x.shape
    t = CTX_LEN + n_lat
    n = bsz * t
    cos, sin = _rope_tables(n_lat)
    cond = jax.nn.silu(jnp.concatenate([c, c_ctx[None, :], jnp.zeros((16 - bsz - 1, d), _F32)], axis=0))
    xs = jnp.concatenate([ctx, x], axis=1).reshape(n, d)
    v_first = None
    moe = None
    for layer in range(DEPTH):
        m = _mm(cond, mod_w[layer], tn=1536, precision=_HI) + mod_b[layer]
        m = m.reshape(16, 6, d)
        mods = jnp.stack([jnp.broadcast_to(m[bsz], (bsz, 6, d)), m[:bsz]], axis=1).reshape(2 * bsz, 6, d)
        w_router = jnp.concatenate(
            [moe_w_grp[layer], moe_w_rt[layer], jnp.zeros((d, LANES - N_GROUPS - N_EXPERTS), _F32)], axis=1)
        b_router = jnp.concatenate(
            [moe_b_grp[layer], moe_b_rt[layer], jnp.zeros((LANES - N_GROUPS - N_EXPERTS,), _F32)])[None, :]
        if layer % 2 == 0:
            e = layer // 2
            p = {'q_gain': ev_q_gain[e], 'k_gain': ev_k_gain[e], 'mu_prev': ev_mu_prev[e],
                 'mu_next': ev_mu_next[e], 'w0': ev_w0[e], 'w2': ev_w2[e], 'a0': ev_a0[e],
                 'a2': ev_a2[e], 'g2': ev_g2[e], 'k_k': ev_k_k[e], 'k_a': ev_k_a[e]}
            if e > 0:
                p['v0'], p['v1'], p['v2'] = ev_v0[e - 1], ev_v1[e - 1], ev_v2[e - 1]
            proj, xs = _project(xs, mods, ev_w_in[e].astype(_BF16), moe=moe)
            q, kk2, vv2, r, v, kkn, lw, a, kd, g = _prepare_even(proj, cos, sin, p, v_first)
            if v_first is None:
                v_first = v
            att = _gqa_attend(q.reshape(bsz, t, -1), kk2.reshape(bsz, t, -1), vv2.reshape(bsz, t, -1))
            seq = lambda z: z.reshape(*z.shape[:-2], bsz, t, B_WIDTH)
            y_fwd, y_bwd = (y.reshape(n, B_WIDTH) for y in
                            _delta_scan(seq(r), seq(v), seq(kkn), seq(lw), seq(a), seq(kd)))
            gn = jnp.stack([ev_gn_w[e], ev_gn_b[e], ev_r_k[e].reshape(B_WIDTH)])
            xs, h, route, counts = _mixer_out(att.reshape(n, -1), xs, mods, ev_w_out[e].astype(_BF16),
                                       w_router, b_router, rwkv=(y_fwd, y_bwd, r, v, kd, g, gn))
        else:
            o = layer // 2
            lambda_init = 0.8 - 0.6 * math.exp(-0.3 * layer)
            qkv, xs = _project(xs, mods, od_w_in[o].astype(_BF16), moe=moe, rope=(cos, sin))
            lam = (jnp.exp(jnp.sum(od_lq1[o] * od_lk1[o])) - jnp.exp(jnp.sum(od_lq2[o] * od_lk2[o]))
                   + lambda_init).reshape(1, 1)
            gain = (od_subln[o] * (1.0 - lambda_init)).reshape(1, LANES)
            att = _diff_attend(lam, qkv.reshape(bsz, t, -1), gain)
            xs, h, route, counts = _mixer_out(att.reshape(n, -1), xs, mods, od_w_out[o].astype(_BF16),
                                       w_router, b_router)
        y1, y2 = _moe(h, route, counts, moe_w_gate, moe_w_up, moe_w_down, layer)
        moe = (y1, y2, route, mods)
    return _final(xs, y1, y2, route, mods, final_gain, bsz, n_lat).reshape(bsz, n_lat, d)
```

```python
import functools
import math

import jax
import jax.numpy as jnp
from jax import lax
from jax.experimental import pallas as pl
from jax.experimental.pallas import tpu as pltpu

D_MODEL = 1024
DEPTH = 4
SEQ = 2048
GRID_W = 64
CTX_LEN = 256
HEAD_DIM = 64
ROPE_THETA = 10000.0
RMS_EPS = 1e-6
SUBLN_EPS = 1e-5
Q_SCALE = HEAD_DIM ** -0.5 * math.log2(math.e)

A_HEADS = 8
A_KV_HEADS = 2
A_WIDTH = 512
A_KV_WIDTH = 128
A_COLS = 768

B_HEADS = 8
B_WIDTH = 512
B_COLS = 1792
RWKV_W_LORA = 64
RWKV_A_LORA = 64
RWKV_V_LORA = 32
RWKV_G_LORA = 128
RWKV_GN_EPS = 64e-5
DECAY_SCALE = math.exp(-0.5)

C_HEADS = 8
C_QK = 1024

N_GROUPS = 4
EXPERTS_PER_GROUP = 8
N_EXPERTS = 32
TOP_K = 2
EXPERT_HIDDEN = 512
MOE_ROWS = 256

LANES = 128
SCAN_CHUNK = 64
ROW_TILE = 256
Q_TILE = 128
VMEM_LIMIT = 56 * 1024 * 1024

_HI = lax.Precision.HIGHEST
_F32 = jnp.float32
_BF16 = jnp.bfloat16

_NN = (((1,), (0,)), ((), ()))
_NT = (((1,), (1,)), ((), ()))
_TN = (((0,), (0,)), ((), ()))


def _params(n_axes):
    return pltpu.CompilerParams(dimension_semantics=("arbitrary",) * n_axes,
                                vmem_limit_bytes=VMEM_LIMIT)


def _full(shape):
    return pl.BlockSpec(shape, lambda *_: (0,) * len(shape))


def _split2(x):
    hi = x.astype(_BF16)
    return hi, (x - hi.astype(_F32)).astype(_BF16)


def _dot3(a, b, dims=_NN):
    d = lambda u, w: lax.dot_general(u, w, dims, preferred_element_type=_F32)
    return (d(a[1], b[0]) + d(a[0], b[1])) + d(a[0], b[0])


def _dot1(a, b, dims=_NN):
    return lax.dot_general(a.astype(_BF16), b.astype(_BF16), dims, preferred_element_type=_F32)


def _pack_bf16_pairs(x):
    w = x.shape[1] // 2
    bits = pltpu.bitcast(x.astype(_BF16).astype(_F32), jnp.uint32)
    return pltpu.bitcast((bits[:, :w] >> 16) | (bits[:, w:] & jnp.uint32(0xFFFF0000)), _F32)


def _unpack_bf16_pairs(words):
    packed = pltpu.bitcast(words, jnp.uint32)
    lo = pltpu.bitcast(packed << 16, _F32)
    hi = pltpu.bitcast(packed & jnp.uint32(0xFFFF0000), _F32)
    return jnp.concatenate([lo, hi], axis=1).astype(_BF16)


def _lane_lo(shape):
    return lax.broadcasted_iota(jnp.int32, shape, len(shape) - 1) < HEAD_DIM


def _head_sum_matrix(value):
    row = lax.broadcasted_iota(jnp.int32, (LANES, LANES), 0)
    col = lax.broadcasted_iota(jnp.int32, (LANES, LANES), 1)
    return jnp.where((row // HEAD_DIM) == (col // HEAD_DIM), value, 0.0).astype(_BF16)


def _head_sum(x, mat):
    hi, lo = _split2(x)
    return jnp.dot(lo, mat, preferred_element_type=_F32) + jnp.dot(hi, mat, preferred_element_type=_F32)


def _rms_rows(x, eps=RMS_EPS):
    return x * lax.rsqrt(jnp.mean(x * x, axis=-1, keepdims=True) + eps)


def _rope_slab(x, cos, sin):
    lane = lax.broadcasted_iota(jnp.int32, x.shape, 1)
    partner = jnp.where((lane & 16) == 0, pltpu.roll(x, LANES - 16, axis=1), pltpu.roll(x, 16, axis=1))
    return x * cos + partner * sin


def _mods_index(i):
    tiles = (CTX_LEN + SEQ) // ROW_TILE
    return (i // tiles) * 2 + jnp.minimum(i % tiles, 1)


def _mm_kernel(a_ref, w_ref, o_ref, *, precision):
    o_ref[...] = jnp.dot(a_ref[...], w_ref[...], precision=precision,
                         preferred_element_type=_F32).astype(o_ref.dtype)


def _mm(a, w, *, tn, precision=None):
    m, k = a.shape
    n = w.shape[1]
    return pl.pallas_call(
        functools.partial(_mm_kernel, precision=precision),
        grid=(n // tn,),
        in_specs=[pl.BlockSpec((m, k), lambda j: (0, 0)),
                  pl.BlockSpec((k, tn), lambda j: (0, j))],
        out_specs=pl.BlockSpec((m, tn), lambda j: (0, j)),
        out_shape=jax.ShapeDtypeStruct((m, n), _F32),
        name=f"mm_{k}x{n}",
        compiler_params=_params(1),
    )(a, w)


def _proj_kernel(*refs, combine, rope):
    it = iter(refs)
    x_ref, m_ref = next(it), next(it)
    if combine:
        y1_ref, y2_ref, g_ref, mp_ref = (next(it) for _ in range(4))
    w_ref = next(it)
    if rope:
        cos_ref, sin_ref = next(it), next(it)
    o_ref = next(it)
    x = x_ref[...]
    if combine:
        xo_ref = next(it)
        g = g_ref[...]
        x = x + mp_ref[0, 5:6, :] * (g[:, 0:1] * y1_ref[...] + g[:, 1:2] * y2_ref[...])
        xo_ref[...] = x
    h = _rms_rows(x) * (1.0 + m_ref[0, 1:2, :]) + m_ref[0, 0:1, :]
    p = jnp.dot(h.astype(_BF16), w_ref[...], preferred_element_type=_F32)
    if rope:
        cos, sin = cos_ref[...], sin_ref[...]
        for s in range(2 * C_QK // LANES):
            sl = slice(s * LANES, (s + 1) * LANES)
            slab = _rope_slab(p[:, sl], cos, sin)
            if s < C_QK // LANES:
                slab = slab * Q_SCALE
            o_ref[:, sl] = slab.astype(o_ref.dtype)
        o_ref[:, 2 * C_QK:] = p[:, 2 * C_QK:].astype(o_ref.dtype)
    else:
        o_ref[...] = p


def _project(x, mods, w, *, moe=None, rope=None):
    n, d = x.shape
    cols = w.shape[1]
    tm = ROW_TILE
    tiles = (CTX_LEN + SEQ) // tm
    row = lambda width: pl.BlockSpec((tm, width), lambda i: (i, 0))
    mod_spec = pl.BlockSpec((1, 6, d), lambda i: (_mods_index(i), 0, 0))
    args, specs = [x, mods], [row(d), mod_spec]
    if moe is not None:
        args += list(moe)
        specs += [row(d), row(d), row(LANES), mod_spec]
    args.append(w)
    specs.append(_full(w.shape))
    if rope is not None:
        args += list(rope)
        specs += [pl.BlockSpec((tm, LANES), lambda i: (i % tiles, 0))] * 2
    out_shape = [jax.ShapeDtypeStruct((n, cols), _BF16 if rope is not None else _F32)]
    out_specs = [row(cols)]
    if moe is not None:
        out_shape.append(jax.ShapeDtypeStruct((n, d), _F32))
        out_specs.append(row(d))
    out = pl.pallas_call(
        functools.partial(_proj_kernel, combine=moe is not None, rope=rope is not None),
        grid=(n // tm,), in_specs=specs, out_specs=out_specs, out_shape=out_shape,
        name=f"project_{cols}", compiler_params=_params(1),
    )(*args)
    return out if moe is not None else (out[0], x)


def _prep_kernel(*refs, has_v_first):
    it = iter(refs)
    p_ref, pp_ref, pn_ref, cos_ref, sin_ref = (next(it) for _ in range(5))
    gain_ref, mu_ref, wa0_ref, kv_ref, wl_ref, g2_ref = (next(it) for _ in range(6))
    if has_v_first:
        v1_ref, v2_ref, vf_ref = next(it), next(it), next(it)
    q_ref, kk2_ref, vv2_ref, r_ref, v_ref, kkn_ref, lw_ref, a_ref, kd_ref, g_ref = it

    tiles = (CTX_LEN + SEQ) // ROW_TILE
    j = pl.program_id(0) % tiles
    p = p_ref[...]
    cos, sin = cos_ref[...], sin_ref[...]
    mean_mat = _head_sum_matrix(1.0 / HEAD_DIM)
    ones_mat = _head_sum_matrix(1.0)
    lo = _lane_lo((ROW_TILE, LANES))

    def normed(x, gain):
        return x * lax.rsqrt(_head_sum(x * x, mean_mat) + RMS_EPS) * gain

    for s in range(A_WIDTH // LANES):
        sl = slice(s * LANES, (s + 1) * LANES)
        q = _rope_slab(normed(p[:, sl], gain_ref[0:1, :]), cos, sin)
        q_ref[:, sl] = (q * Q_SCALE).astype(_BF16)
    k = _rope_slab(normed(p[:, A_WIDTH:A_WIDTH + LANES], gain_ref[1:2, :]), cos, sin)
    v = p[:, A_WIDTH + LANES:A_COLS]
    for src, dst in ((k, kk2_ref), (v, vv2_ref)):
        swapped = pltpu.roll(src, HEAD_DIM, axis=1)
        dst[:, :LANES] = jnp.where(lo, src, swapped).astype(_BF16)
        dst[:, LANES:] = jnp.where(lo, swapped, src).astype(_BF16)

    u = p[:, A_COLS:]
    rid = lax.broadcasted_iota(jnp.int32, (ROW_TILE, 1), 0)
    has_prev = jnp.where(j >= 2, 1.0, 0.0)
    has_next = jnp.where((j >= 1) & (j < tiles - 1), 1.0, 0.0)
    first = pp_ref[7:8, A_COLS:] * has_prev
    last = pn_ref[0:1, A_COLS:] * has_next
    prev = jnp.where(rid == 0, first, pltpu.roll(u, 1, axis=0))
    nxt = jnp.where(rid == ROW_TILE - 1, last, pltpu.roll(u, ROW_TILE - 1, axis=0))
    u = u + mu_ref[0:1, :] * (prev - u) + mu_ref[1:2, :] * (nxt - u)
    r, k, v = u[:, :B_WIDTH], u[:, B_WIDTH:2 * B_WIDTH], u[:, 2 * B_WIDTH:3 * B_WIDTH]
    wa, glr = u[:, 3 * B_WIDTH:3 * B_WIDTH + LANES], u[:, 3 * B_WIDTH + LANES:]
    if has_v_first:
        mid = jnp.dot(v.astype(_BF16), v1_ref[...], preferred_element_type=_F32)
        lora = jnp.dot(mid.astype(_BF16), v2_ref[...], preferred_element_type=_F32)
        v = v + (vf_ref[...] - v) * jax.nn.sigmoid(kv_ref[2:3, :] + lora)
    r_ref[...] = r
    v_ref[...] = v
    kks = k * kv_ref[0:1, :]
    for s in range(B_WIDTH // LANES):
        sl = slice(s * LANES, (s + 1) * LANES)
        x = kks[:, sl]
        kkn_ref[:, sl] = x * lax.rsqrt(jnp.maximum(_head_sum(x * x, ones_mat), 1e-24))
    xw = jnp.where(lo, jnp.tanh(wa), wa).astype(_BF16)
    for dr in range(2):
        z = jnp.dot(xw, wl_ref[dr], preferred_element_type=_F32) + wa0_ref[dr:dr + 1, :]
        a = jax.nn.sigmoid(z[:, B_WIDTH:])
        lw_ref[dr] = -DECAY_SCALE * jax.nn.sigmoid(z[:, :B_WIDTH])
        a_ref[dr] = a
        kd_ref[dr] = k * (1.0 + (a - 1.0) * kv_ref[1:2, :])
    g_ref[...] = jnp.dot(jax.nn.sigmoid(glr).astype(_BF16), g2_ref[...], preferred_element_type=_F32)


def _prepare_even(proj, cos, sin, p, v_first):
    n, cols = proj.shape
    tm = ROW_TILE
    tiles = (CTX_LEN + SEQ) // tm
    halo = 8
    per_tile = tm // halo
    row = lambda width: pl.BlockSpec((tm, width), lambda i: (i, 0))
    both = pl.BlockSpec((2, tm, B_WIDTH), lambda i: (0, i, 0))
    tab = pl.BlockSpec((tm, LANES), lambda i: (i % tiles, 0))
    pad = lambda w, rows, width: jnp.zeros((rows, width), _BF16).at[:w.shape[0], :w.shape[1]].set(w.astype(_BF16))
    gains = jnp.stack([jnp.tile(p['q_gain'], 2), jnp.tile(p['k_gain'], 2)])
    mu = jnp.stack([p['mu_prev'], p['mu_next']])
    wa0 = jnp.concatenate([p['w0'], p['a0']], axis=1)
    kvec = jnp.stack([p['k_k'], p['k_a'], p.get('v0', jnp.zeros((B_WIDTH,), _F32))])
    zero = jnp.zeros((RWKV_W_LORA, B_WIDTH), _F32)
    wl = jnp.stack([jnp.concatenate([jnp.concatenate([p['w2'][dr], zero], axis=1),
                                     jnp.concatenate([zero, p['a2'][dr]], axis=1)], axis=0)
                    for dr in range(2)]).astype(_BF16)
    args = [proj, proj, proj, cos, sin, gains, mu, wa0, kvec, wl, p['g2'].astype(_BF16)]
    specs = [row(cols),
             pl.BlockSpec((halo, cols), lambda i: (jnp.maximum(i * per_tile - 1, 0), 0)),
             pl.BlockSpec((halo, cols), lambda i: (jnp.minimum((i + 1) * per_tile, n // halo - 1), 0)),
             tab, tab, _full(gains.shape), _full(mu.shape), _full(wa0.shape), _full(kvec.shape),
             _full(wl.shape), _full(p['g2'].shape)]
    if v_first is not None:
        v1 = pad(p['v1'], B_WIDTH, LANES)
        v2 = pad(p['v2'], LANES, B_WIDTH)
        args += [v1, v2, v_first]
        specs += [_full(v1.shape), _full(v2.shape), row(B_WIDTH)]
    f32 = lambda *shape: jax.ShapeDtypeStruct(shape, _F32)
    bf16 = lambda *shape: jax.ShapeDtypeStruct(shape, _BF16)
    out_shape = [bf16(n, A_WIDTH), bf16(n, 2 * LANES), bf16(n, 2 * LANES),
                 f32(n, B_WIDTH), f32(n, B_WIDTH), f32(n, B_WIDTH),
                 f32(2, n, B_WIDTH), f32(2, n, B_WIDTH), f32(2, n, B_WIDTH), f32(n, B_WIDTH)]
    out_specs = [row(A_WIDTH), row(2 * LANES), row(2 * LANES), row(B_WIDTH), row(B_WIDTH), row(B_WIDTH),
                 both, both, both, row(B_WIDTH)]
    return pl.pallas_call(
        functools.partial(_prep_kernel, has_v_first=v_first is not None),
        grid=(n // tm,), in_specs=specs, out_specs=out_specs, out_shape=out_shape,
        name="prepare_even", compiler_params=_params(1),
    )(*args)


def _split_heads(q2):
    lo = _lane_lo(q2.shape)
    zero = jnp.zeros_like(q2)
    return jnp.concatenate([jnp.where(lo, q2, zero), jnp.where(lo, zero, q2)], axis=0)


def _scores(qs, k):
    return lax.dot_general(qs, k, _NT, preferred_element_type=_F32)


def _interleave(chains):
    results = {}
    while len(results) < len(chains):
        for p, chain in enumerate(chains):
            try:
                next(chain)
            except StopIteration as stop:
                results[p] = stop.value
    return [results[p] for p in range(len(chains))]


def _gqa_tile(q, k, v):
    tq = q.shape[0]
    qs = jnp.concatenate([_split_heads(q[:, :LANES]), _split_heads(q[:, LANES:])], axis=0)
    s = _scores(qs, k)
    yield
    p = jnp.exp2(s - jnp.max(s, axis=-1, keepdims=True))
    l = jnp.sum(p, axis=-1, keepdims=True)
    p = p.astype(_BF16)
    yield
    o = jnp.dot(p, v, preferred_element_type=_F32) / l
    lo = _lane_lo((tq, LANES))
    return jnp.concatenate([jnp.where(lo, o[:tq], o[tq:2 * tq]),
                            jnp.where(lo, o[2 * tq:3 * tq], o[3 * tq:])], axis=1)


def _attend_halves(tile_fn, q_ref, o_ref):
    halves = [slice(i * Q_TILE, (i + 1) * Q_TILE) for i in range(q_ref.shape[1] // Q_TILE)]
    outs = _interleave([tile_fn(q_ref[0, sl]) for sl in halves])
    for sl, o in zip(halves, outs):
        o_ref[0, sl] = o.astype(o_ref.dtype)


def _gqa_kernel(q_ref, k_ref, v_ref, o_ref):
    is_ctx = pl.program_id(2) < CTX_LEN // q_ref.shape[1]

    @pl.when(is_ctx)
    def _():
        _attend_halves(lambda q: _gqa_tile(q, k_ref[0, :CTX_LEN], v_ref[0, :CTX_LEN]), q_ref, o_ref)

    @pl.when(jnp.logical_not(is_ctx))
    def _():
        _attend_halves(lambda q: _gqa_tile(q, k_ref[0], v_ref[0]), q_ref, o_ref)


def _gqa_attend(q, kk, vv):
    b, t, _ = q.shape
    tq = 2 * Q_TILE
    kv_spec = pl.BlockSpec((1, t, LANES), lambda i, g, j: (i, 0, g))
    return pl.pallas_call(
        _gqa_kernel,
        grid=(b, A_KV_HEADS, t // tq),
        in_specs=[pl.BlockSpec((1, tq, 2 * LANES), lambda i, g, j: (i, j, g)), kv_spec, kv_spec],
        out_specs=pl.BlockSpec((1, tq, 2 * LANES), lambda i, g, j: (i, j, g)),
        out_shape=jax.ShapeDtypeStruct((b, t, A_WIDTH), _BF16),
        name="gqa_attn", compiler_params=_params(3),
    )(q, kk, vv)


def _diff_tile(lam, q, k, v, gain):
    tq = q.shape[0]
    s = _scores(_split_heads(q), k)
    yield
    p = jnp.exp2(s - jnp.max(s, axis=-1, keepdims=True))
    l = jnp.sum(p, axis=-1, keepdims=True)
    p = p.astype(_BF16)
    yield
    o = jnp.dot(p, v, preferred_element_type=_F32) / l
    o = o[:tq] - lam * o[tq:]
    y = o * lax.rsqrt(jnp.mean(o * o, axis=-1, keepdims=True) + SUBLN_EPS)
    return y * gain


def _diff_kernel(lam_ref, q_ref, k_ref, v_ref, g_ref, o_ref):
    is_ctx = pl.program_id(2) < CTX_LEN // q_ref.shape[1]
    lam, gain = lam_ref[0, 0], g_ref[...]

    @pl.when(is_ctx)
    def _():
        _attend_halves(lambda q: _diff_tile(lam, q, k_ref[0, :CTX_LEN], v_ref[0, :CTX_LEN], gain),
                       q_ref, o_ref)

    @pl.when(jnp.logical_not(is_ctx))
    def _():
        _attend_halves(lambda q: _diff_tile(lam, q, k_ref[0], v_ref[0], gain), q_ref, o_ref)


def _diff_attend(lam, qkv, gain):
    b, t, _ = qkv.shape
    tq = 2 * Q_TILE
    return pl.pallas_call(
        _diff_kernel,
        grid=(b, C_HEADS, t // tq),
        in_specs=[pl.BlockSpec(memory_space=pltpu.SMEM),
                  pl.BlockSpec((1, tq, LANES), lambda i, h, j: (i, j, h)),
                  pl.BlockSpec((1, t, LANES), lambda i, h, j: (i, 0, C_HEADS + h)),
                  pl.BlockSpec((1, t, LANES), lambda i, h, j: (i, 0, 2 * C_HEADS + h)),
                  _full((1, LANES))],
        out_specs=pl.BlockSpec((1, tq, LANES), lambda i, h, j: (i, j, h)),
        out_shape=jax.ShapeDtypeStruct((b, t, C_HEADS * LANES), _BF16),
        name="diff_attn", compiler_params=_params(3),
    )(lam, qkv, qkv, qkv, gain)


def _scan_chunk(h, r, v, kk, lw, cum, cum_end, a, kd, sgn):
    c = SCAN_CHUNK
    lo = _lane_lo((c, LANES))
    row = lax.broadcasted_iota(jnp.int32, (LANES, LANES), 0)
    col = lax.broadcasted_iota(jnp.int32, (LANES, LANES), 1)
    delta = ((col & (c - 1)) - (row & (c - 1))) * sgn
    before = delta < 0
    upto = delta <= 0
    eye = row == col
    zero = jnp.zeros((LANES, LANES), _F32)

    def blockdiag(x):
        z = jnp.zeros_like(x)
        return jnp.concatenate([jnp.where(lo, x, z), jnp.where(lo, z, x)], axis=0)

    p_inv = jnp.exp(-cum)
    p_end = jnp.exp(cum_end - cum)
    kka = kk * a
    a_rs = blockdiag(-kk * jnp.exp(cum - lw))
    r_rs = blockdiag(r * jnp.exp(cum))
    b_rs = blockdiag(kka * p_inv)
    k_rs = blockdiag(kd * p_inv)
    v_rs = blockdiag(v)
    bp_rs = blockdiag(kka * p_end)
    kp_rs = blockdiag(kd * p_end)

    z = _dot3(_split2(jnp.concatenate([a_rs, r_rs], axis=0)),
              _split2(jnp.concatenate([b_rs, k_rs], axis=0)), _NT)
    yield
    low = jnp.where(before, z[:LANES, :LANES], zero)
    g = jnp.where(before, z[:LANES, LANES:], zero)
    rb = jnp.where(upto, z[LANES:, :LANES], zero)
    rk = jnp.where(upto, z[LANES:, LANES:], zero)

    x = _dot1(low, low)
    yield
    m = jnp.where(eye, 1.0, 0.0).astype(_F32) + low
    n = 2
    while 2 * n < c:
        res = _dot1(x, jnp.concatenate([m, x], axis=1))
        m = m + res[:, :LANES]
        x = res[:, LANES:]
        n *= 2
        yield
    m = m + _dot1(x, m)
    yield
    gv = _dot1(g, v_rs)
    yield
    au = _dot1(m, jnp.concatenate([a_rs, gv], axis=1))
    yield
    rhs = _split2(jnp.concatenate(
        [au, jnp.concatenate([zero, v_rs], axis=1)], axis=0))
    tf = lax.dot_general(jnp.concatenate([bp_rs, kp_rs], axis=0).astype(_BF16), rhs[0], _TN,
                         preferred_element_type=_F32)
    qy = _dot3(_split2(jnp.concatenate([rb, rk], axis=1)), rhs)
    yield
    t = jnp.where(eye, jnp.broadcast_to(jnp.exp(cum_end), (LANES, LANES)), zero) + tf[:, :LANES]
    q = r_rs + qy[:, :LANES]
    out = _dot1(jnp.concatenate([q, t], axis=0), h)
    y = out[:LANES] + qy[:, LANES:]
    h_new = out[LANES:] + tf[:, LANES:]
    return y[:c] + y[c:], h_new


def _scan_kernel(*refs):
    c = SCAN_CHUNK
    ins, y_refs, h_ref = refs[:12], refs[12:14], refs[14]

    @pl.when(pl.program_id(1) == 0)
    def _():
        h_ref[...] = jnp.zeros_like(h_ref)

    tr = lax.broadcasted_iota(jnp.int32, (c, c), 0)
    tc = lax.broadcasted_iota(jnp.int32, (c, c), 1)
    chains, outs = [], []
    for d, sgn in enumerate((1, -1)):
        r_ref, v_ref, kk_ref, lw_ref, a_ref, kd_ref = ins[6 * d:6 * d + 6]
        tri = jnp.where((tc - tr) * sgn <= 0, 1.0, 0.0).astype(_BF16)
        lw = lw_ref[0, 0]
        l1 = lw.astype(_BF16)
        rem = lw - l1.astype(_F32)
        l2 = rem.astype(_BF16)
        l3 = (rem - l2.astype(_F32)).astype(_BF16)
        dot = lambda w, tri=tri: jnp.dot(tri, w, preferred_element_type=_F32)
        cum = (dot(l3) + dot(l2)) + dot(l1)
        cum_end = jnp.sum(lw, axis=0, keepdims=True)
        pairs = lw.shape[1] // LANES
        for p in range(pairs):
            sl = slice(p * LANES, (p + 1) * LANES)
            chains.append(_scan_chunk(h_ref[d * pairs + p], r_ref[0, :, sl], v_ref[0, :, sl],
                                      kk_ref[0, :, sl], lw[:, sl], cum[:, sl], cum_end[:, sl],
                                      a_ref[0, 0, :, sl], kd_ref[0, 0, :, sl], sgn))
            outs.append((y_refs[d], sl, d * pairs + p))
    for (y_ref, sl, slot), (y, h) in zip(outs, _interleave(chains)):
        y_ref[0, :, sl] = y
        h_ref[slot] = h


def _delta_scan(r, v, kk, lw, a, kd):
    b, t, w = r.shape
    c = SCAN_CHUNK
    n_chunks = t // c
    n_ctx = CTX_LEN // c

    def chunk(d, j):
        return j if d == 0 else jnp.where(j < n_ctx, n_ctx - 1 - j, n_chunks + n_ctx - 1 - j)

    specs, args = [], []
    for d in range(2):
        shared = pl.BlockSpec((1, c, w), lambda i, j, d=d: (i, chunk(d, j), 0))
        per_dir = pl.BlockSpec((1, 1, c, w), lambda i, j, d=d: (d, i, chunk(d, j), 0))
        specs += [shared, shared, shared, per_dir, per_dir, per_dir]
        args += [r, v, kk, lw, a, kd]
    out_specs = [pl.BlockSpec((1, c, w), lambda i, j, d=d: (i, chunk(d, j), 0)) for d in range(2)]
    return pl.pallas_call(
        _scan_kernel,
        grid=(b, n_chunks),
        in_specs=specs,
        out_specs=out_specs,
        out_shape=[jax.ShapeDtypeStruct((b, t, w), _F32)] * 2,
        scratch_shapes=[pltpu.VMEM((2 * w // LANES, LANES, LANES), _F32)],
        name="delta_scan", compiler_params=_params(2),
    )(*args)


def _post_kernel(*refs, even):
    it = iter(refs)
    if even:
        yf_ref, yb_ref, r_ref, v_ref, kd_ref, g_ref, gn_ref = (next(it) for _ in range(7))
    att_ref, x_ref, m_ref, wo_ref, wrh_ref, wrl_ref, br_ref = (next(it) for _ in range(7))
    xo_ref, h_ref, route_ref, count_ref, carry_ref = it
    if even:
        mean_mat = _head_sum_matrix(1.0 / HEAD_DIM)
        ones_mat = _head_sum_matrix(1.0)
        parts = [att_ref[...]]
        for s in range(B_WIDTH // LANES):
            sl = slice(s * LANES, (s + 1) * LANES)
            y = yf_ref[:, sl] + yb_ref[:, sl]
            dev = y - _head_sum(y, mean_mat)
            var = _head_sum(dev * dev, mean_mat)
            y = dev * lax.rsqrt(var + RWKV_GN_EPS) * gn_ref[0:1, sl] + gn_ref[1:2, sl]
            r, v = r_ref[:, sl], v_ref[:, sl]
            for dr in range(2):
                y = y + _head_sum(r * kd_ref[dr, :, sl] * gn_ref[2:3, sl], ones_mat) * v
            parts.append((y * g_ref[:, sl]).astype(_BF16))
        mixed = jnp.concatenate(parts, axis=1)
    else:
        mixed = att_ref[...]
    out = jnp.dot(mixed, wo_ref[...], preferred_element_type=_F32)
    x = x_ref[...] + m_ref[0, 2:3, :] * out
    xo_ref[...] = x
    h = _rms_rows(x) * (1.0 + m_ref[0, 4:5, :]) + m_ref[0, 3:4, :]
    h_ref[...] = _pack_bf16_pairs(h)
    logits = _dot3(_split2(h), (wrh_ref[...], wrl_ref[...])) + br_ref[...]
    _route(logits, route_ref, count_ref, carry_ref)


def _route(logits, route_ref, count_ref, carry_ref):
    tm = logits.shape[0]
    lane = lax.broadcasted_iota(jnp.int32, (tm, LANES), 1)
    neg = -1e30
    first_lane = lambda hit: jnp.min(jnp.where(hit, lane, LANES), axis=-1, keepdims=True)

    is_grp = lane < N_GROUPS
    grp = jnp.where(is_grp, logits, neg)
    g_max = jnp.max(grp, axis=-1, keepdims=True)
    g_sel = first_lane(grp == g_max)
    grp_w = 1.0 / jnp.sum(jnp.where(is_grp, jnp.exp(grp - g_max), 0.0), axis=-1, keepdims=True)

    base = N_GROUPS + EXPERTS_PER_GROUP * g_sel
    in_grp = (lane >= base) & (lane < base + EXPERTS_PER_GROUP)
    ex = jnp.where(in_grp, logits, neg)
    e_max = jnp.max(ex, axis=-1, keepdims=True)
    pe = jnp.where(in_grp, jnp.exp(ex - e_max), 0.0)
    prob = pe / jnp.sum(pe, axis=-1, keepdims=True)
    p1 = jnp.max(prob, axis=-1, keepdims=True)
    l1 = first_lane(in_grp & (prob == p1))
    rest = jnp.where(in_grp & (lane != l1), prob, -1.0)
    p2 = jnp.max(rest, axis=-1, keepdims=True)
    l2 = first_lane(rest == p2)
    scale = grp_w / (p1 + p2)
    e1, e2 = l1 - N_GROUPS, l2 - N_GROUPS

    @pl.when(pl.program_id(0) == 0)
    def _():
        carry_ref[...] = jnp.zeros_like(carry_ref)

    oh1 = jnp.where(lane == e1, 1.0, 0.0)
    oh2 = jnp.where(lane == e2, 1.0, 0.0)
    rr = lax.broadcasted_iota(jnp.int32, (tm, tm), 0)
    cc = lax.broadcasted_iota(jnp.int32, (tm, tm), 1)
    tri = jnp.where(cc < rr, 1.0, 0.0).astype(_BF16)
    before = lambda oh: jnp.dot(tri, oh.astype(_BF16), preferred_element_type=_F32)
    carry = carry_ref[...]
    tot1 = jnp.sum(oh1, axis=0, keepdims=True)
    rank1 = jnp.sum(oh1 * (before(oh1) + carry), axis=-1, keepdims=True)
    rank2 = jnp.sum(oh2 * (before(oh2) + (carry + tot1)), axis=-1, keepdims=True)
    carry = carry + tot1 + jnp.sum(oh2, axis=0, keepdims=True)
    carry_ref[...] = carry
    count_ref[...] = carry

    out = jnp.zeros((tm, LANES), _F32)
    for i, val in enumerate((p1 * scale, p2 * scale, e1.astype(_F32), e2.astype(_F32), rank1, rank2)):
        out = jnp.where(lane == i, val, out)
    route_ref[...] = out


def _mixer_out(att, x, mods, w_out, w_router, b_router, rwkv=None):
    n, d = x.shape
    tm = ROW_TILE
    row = lambda width: pl.BlockSpec((tm, width), lambda i: (i, 0))
    both = pl.BlockSpec((2, tm, B_WIDTH), lambda i: (0, i, 0))
    args, specs = [], []
    if rwkv is not None:
        y_fwd, y_bwd, r, v, kd, g, gn = rwkv
        args += [y_fwd, y_bwd, r, v, kd, g, gn]
        specs += [row(B_WIDTH)] * 4 + [both, row(B_WIDTH), _full(gn.shape)]
    wr_hi, wr_lo = _split2(w_router)
    args += [att, x, mods, w_out, wr_hi, wr_lo, b_router]
    specs += [row(att.shape[1]), row(d), pl.BlockSpec((1, 6, d), lambda i: (_mods_index(i), 0, 0)),
              _full(w_out.shape), _full(wr_hi.shape), _full(wr_lo.shape), _full(b_router.shape)]
    return pl.pallas_call(
        functools.partial(_post_kernel, even=rwkv is not None),
        grid=(n // tm,), in_specs=specs,
        out_specs=[row(d), row(d // 2), row(LANES), _full((1, LANES))],
        out_shape=[jax.ShapeDtypeStruct((n, d), _F32), jax.ShapeDtypeStruct((n, d // 2), _F32),
                   jax.ShapeDtypeStruct((n, LANES), _F32), jax.ShapeDtypeStruct((1, LANES), _F32)],
        scratch_shapes=[pltpu.VMEM((1, LANES), _F32)],
        name="mixer_out_even" if rwkv is not None else "mixer_out_odd", compiler_params=_params(1),
    )(*args)


def _moe_kernel(be_ref, nv_ref, x_ref, wg_ref, wu_ref, wd_ref, o_ref, wg_s, wu_s, wd_s):
    i = pl.program_id(0)
    valid = i < nv_ref[0]
    new_expert = (i == 0) | (be_ref[i] != be_ref[jnp.maximum(i - 1, 0)])

    @pl.when(valid & new_expert)
    def _():
        wg_s[...] = wg_ref[0, 0].astype(_BF16)
        wu_s[...] = wu_ref[0, 0].astype(_BF16)
        wd_s[...] = wd_ref[0, 0].astype(_BF16)

    @pl.when(valid)
    def _():
        x = _unpack_bf16_pairs(x_ref[...])
        gate = jnp.dot(x, wg_s[...], preferred_element_type=_F32)
        up = jnp.dot(x, wu_s[...], preferred_element_type=_F32)
        hid = (gate * jax.nn.sigmoid(gate) * up).astype(_BF16)
        o_ref[...] = jnp.dot(hid, wd_s[...], preferred_element_type=_F32)

    @pl.when(jnp.logical_not(valid))
    def _():
        o_ref[...] = jnp.zeros_like(o_ref)


def _moe_blocks(block_expert, n_valid, xs, wg, wu, wd, layer):
    rows, d = xs.shape[0], 2 * xs.shape[1]
    nb = rows // MOE_ROWS
    w_spec = lambda a, b: pl.BlockSpec((1, 1, a, b), lambda i, be, nv: (layer, be[i], 0, 0))
    grid_spec = pltpu.PrefetchScalarGridSpec(
        num_scalar_prefetch=2,
        grid=(nb,),
        in_specs=[pl.BlockSpec((MOE_ROWS, d // 2), lambda i, be, nv: (i, 0)),
                  w_spec(d, EXPERT_HIDDEN), w_spec(d, EXPERT_HIDDEN), w_spec(EXPERT_HIDDEN, d)],
        out_specs=pl.BlockSpec((MOE_ROWS, d), lambda i, be, nv: (i, 0)),
        scratch_shapes=[pltpu.VMEM((d, EXPERT_HIDDEN), _BF16), pltpu.VMEM((d, EXPERT_HIDDEN), _BF16),
                        pltpu.VMEM((EXPERT_HIDDEN, d), _BF16)],
    )
    return pl.pallas_call(
        _moe_kernel,
        grid_spec=grid_spec,
        out_shape=jax.ShapeDtypeStruct((rows, d), _F32),
        name="moe_ffn", compiler_params=_params(1),
    )(block_expert, n_valid, xs, wg, wu, wd)


def _moe(h, route, counts, w_gate, w_up, w_down, layer):
    n = h.shape[0]
    eid = route[:, 2:4].astype(jnp.int32)
    rank = route[:, 4:6].astype(jnp.int32)
    cnt = counts[0, :N_EXPERTS].astype(jnp.int32)
    padded = (cnt + MOE_ROWS - 1) // MOE_ROWS * MOE_ROWS
    pad_end = jnp.cumsum(padded)
    experts = jnp.arange(N_EXPERTS, dtype=jnp.int32)
    pad_start = jnp.sum(jnp.where(eid[:, :, None] == experts, pad_end - padded, 0), axis=-1)
    slot = pad_start + rank
    n_blocks = -(-(n * TOP_K + N_EXPERTS * (MOE_ROWS - 1)) // MOE_ROWS)
    block_start = jnp.arange(n_blocks, dtype=jnp.int32) * MOE_ROWS
    block_expert = jnp.minimum(jnp.sum(pad_end[None, :] <= block_start[:, None], axis=1),
                               N_EXPERTS - 1).astype(jnp.int32)
    n_valid = (pad_end[-1:] // MOE_ROWS).astype(jnp.int32)
    tok = jnp.broadcast_to(jnp.arange(n, dtype=jnp.int32)[:, None], (n, TOP_K))
    _, tok_by_slot = lax.sort_key_val(slot.reshape(-1), tok.reshape(-1))
    run_start = (jnp.cumsum(cnt) - cnt)[block_expert]
    run_pos = block_start - (pad_end - padded)[block_expert]
    pos = (run_pos[:, None] + jnp.arange(MOE_ROWS, dtype=jnp.int32)).reshape(-1)
    occupied = pos < jnp.repeat(cnt[block_expert], MOE_ROWS)
    src = jnp.minimum(jnp.repeat(run_start, MOE_ROWS) + pos, n * TOP_K - 1)
    slot_tok = jnp.where(occupied, tok_by_slot[src], 0)
    yb = _moe_blocks(block_expert, n_valid, h[slot_tok], w_gate, w_up, w_down, layer)
    return yb[slot[:, 0]], yb[slot[:, 1]]


def _final_kernel(x_ref, y1_ref, y2_ref, g_ref, m_ref, gain_ref, o_ref):
    g = g_ref[...]
    x = x_ref[...] + m_ref[0, 5:6, :] * (g[:, 0:1] * y1_ref[...] + g[:, 1:2] * y2_ref[...])
    o_ref[...] = _rms_rows(x) * gain_ref[...]


def _final(x, y1, y2, route, mods, gain, bsz, n_lat):
    d = x.shape[1]
    tm = ROW_TILE
    lat_tiles = n_lat // tm
    tiles = (CTX_LEN + n_lat) // tm
    src = lambda i: (i // lat_tiles) * tiles + CTX_LEN // tm + i % lat_tiles
    row = lambda width: pl.BlockSpec((tm, width), lambda i: (src(i), 0))
    return pl.pallas_call(
        _final_kernel,
        grid=(bsz * lat_tiles,),
        in_specs=[row(d), row(d), row(d), row(LANES),
                  pl.BlockSpec((1, 6, d), lambda i: ((i // lat_tiles) * 2 + 1, 0, 0)), _full((1, d))],
        out_specs=pl.BlockSpec((tm, d), lambda i: (i, 0)),
        out_shape=jax.ShapeDtypeStruct((bsz * n_lat, d), _F32),
        name="final_norm", compiler_params=_params(1),
    )(x, y1, y2, route, mods, gain.reshape(1, d))


def _rope_tables(n_lat):
    nf = HEAD_DIM // 4
    inv = ROPE_THETA ** (-jnp.arange(nf, dtype=_F32) / nf)
    rows = n_lat // GRID_W
    r_ang = jnp.repeat(jnp.arange(rows, dtype=_F32), GRID_W)[:, None] * inv
    c_ang = jnp.tile(jnp.arange(GRID_W, dtype=_F32), rows)[:, None] * inv
    cos = jnp.concatenate([jnp.cos(r_ang)] * 2 + [jnp.cos(c_ang)] * 2, axis=-1)
    sin = jnp.concatenate([-jnp.sin(r_ang), jnp.sin(r_ang), -jnp.sin(c_ang), jnp.sin(c_ang)], axis=-1)
    cos = jnp.concatenate([jnp.ones((CTX_LEN, HEAD_DIM), _F32), cos], axis=0)
    sin = jnp.concatenate([jnp.zeros((CTX_LEN, HEAD_DIM), _F32), sin], axis=0)
    return jnp.tile(cos, (1, 2)), jnp.tile(sin, (1, 2))


def kernel(x, c, ctx, c_ctx, mod_w, mod_b, ev_w_in, ev_w_out, ev_q_gain, ev_k_gain, ev_mu_prev, ev_mu_next, ev_w0, ev_w2, ev_a0, ev_a2, ev_g2, ev_k_k, ev_k_a, ev_r_k, ev_gn_w, ev_gn_b, ev_v0, ev_v1, ev_v2, od_w_in, od_w_out, od_lq1, od_lk1, od_lq2, od_lk2, od_subln, moe_w_grp, moe_b_grp, moe_w_rt, moe_b_rt, moe_w_gate, moe_w_up, moe_w_down, final_gain):
    bsz, n_lat, d = x.shape
    t = CTX_LEN + n_lat
    n = bsz * t
    cos, sin = _rope_tables(n_lat)
    cond = jax.nn.silu(jnp.concatenate([c, c_ctx[None, :], jnp.zeros((16 - bsz - 1, d), _F32)], axis=0))
    xs = jnp.concatenate([ctx, x], axis=1).reshape(n, d)
    v_first = None
    moe = None
    for layer in range(DEPTH):
        m = _mm(cond, mod_w[layer], tn=1536, precision=_HI) + mod_b[layer]
        m = m.reshape(16, 6, d)
        mods = jnp.stack([jnp.broadcast_to(m[bsz], (bsz, 6, d)), m[:bsz]], axis=1).reshape(2 * bsz, 6, d)
        w_router = jnp.concatenate(
            [moe_w_grp[layer], moe_w_rt[layer], jnp.zeros((d, LANES - N_GROUPS - N_EXPERTS), _F32)], axis=1)
        b_router = jnp.concatenate(
            [moe_b_grp[layer], moe_b_rt[layer], jnp.zeros((LANES - N_GROUPS - N_EXPERTS,), _F32)])[None, :]
        if layer % 2 == 0:
            e = layer // 2
            p = {'q_gain': ev_q_gain[e], 'k_gain': ev_k_gain[e], 'mu_prev': ev_mu_prev[e],
                 'mu_next': ev_mu_next[e], 'w0': ev_w0[e], 'w2': ev_w2[e], 'a0': ev_a0[e],
                 'a2': ev_a2[e], 'g2': ev_g2[e], 'k_k': ev_k_k[e], 'k_a': ev_k_a[e]}
            if e > 0:
                p['v0'], p['v1'], p['v2'] = ev_v0[e - 1], ev_v1[e - 1], ev_v2[e - 1]
            proj, xs = _project(xs, mods, ev_w_in[e].astype(_BF16), moe=moe)
            q, kk2, vv2, r, v, kkn, lw, a, kd, g = _prepare_even(proj, cos, sin, p, v_first)
            if v_first is None:
                v_first = v
            att = _gqa_attend(q.reshape(bsz, t, -1), kk2.reshape(bsz, t, -1), vv2.reshape(bsz, t, -1))
            seq = lambda z: z.reshape(*z.shape[:-2], bsz, t, B_WIDTH)
            y_fwd, y_bwd = (y.reshape(n, B_WIDTH) for y in
                            _delta_scan(seq(r), seq(v), seq(kkn), seq(lw), seq(a), seq(kd)))
            gn = jnp.stack([ev_gn_w[e], ev_gn_b[e], ev_r_k[e].reshape(B_WIDTH)])
            xs, h, route, counts = _mixer_out(att.reshape(n, -1), xs, mods, ev_w_out[e].astype(_BF16),
                                       w_router, b_router, rwkv=(y_fwd, y_bwd, r, v, kd, g, gn))
        else:
            o = layer // 2
            lambda_init = 0.8 - 0.6 * math.exp(-0.3 * layer)
            qkv, xs = _project(xs, mods, od_w_in[o].astype(_BF16), moe=moe, rope=(cos, sin))
            lam = (jnp.exp(jnp.sum(od_lq1[o] * od_lk1[o])) - jnp.exp(jnp.sum(od_lq2[o] * od_lk2[o]))
                   + lambda_init).reshape(1, 1)
            gain = (od_subln[o] * (1.0 - lambda_init)).reshape(1, LANES)
            att = _diff_attend(lam, qkv.reshape(bsz, t, -1), gain)
            xs, h, route, counts = _mixer_out(att.reshape(n, -1), xs, mods, od_w_out[o].astype(_BF16),
                                       w_router, b_router)
        y1, y2 = _moe(h, route, counts, moe_w_gate, moe_w_up, moe_w_down, layer)
        moe = (y1, y2, route, mods)
    return _final(xs, y1, y2, route, mods, final_gain, bsz, n_lat).reshape(bsz, n_lat, d)
```

```python
import functools
import math

import jax
import jax.numpy as jnp
from jax import lax
from jax.experimental import pallas as pl
from jax.experimental.pallas import tpu as pltpu

D_MODEL = 1024
DEPTH = 4
SEQ = 2048
GRID_W = 64
CTX_LEN = 256
HEAD_DIM = 64
ROPE_THETA = 10000.0
RMS_EPS = 1e-6
SUBLN_EPS = 1e-5
Q_SCALE = HEAD_DIM ** -0.5 * math.log2(math.e)

A_HEADS = 8
A_KV_HEADS = 2
A_WIDTH = 512
A_KV_WIDTH = 128
A_COLS = 768

B_HEADS = 8
B_WIDTH = 512
B_COLS = 1792
RWKV_W_LORA = 64
RWKV_A_LORA = 64
RWKV_V_LORA = 32
RWKV_G_LORA = 128
RWKV_GN_EPS = 64e-5
DECAY_SCALE = math.exp(-0.5)

C_HEADS = 8
C_QK = 1024

N_GROUPS = 4
EXPERTS_PER_GROUP = 8
N_EXPERTS = 32
TOP_K = 2
EXPERT_HIDDEN = 512
MOE_ROWS = 256

LANES = 128
SCAN_CHUNK = 64
ROW_TILE = 256
Q_TILE = 128
VMEM_LIMIT = 56 * 1024 * 1024

_HI = lax.Precision.HIGHEST
_F32 = jnp.float32
_BF16 = jnp.bfloat16

_NN = (((1,), (0,)), ((), ()))
_NT = (((1,), (1,)), ((), ()))
_TN = (((0,), (0,)), ((), ()))


def _params(n_axes):
    return pltpu.CompilerParams(dimension_semantics=("arbitrary",) * n_axes,
                                vmem_limit_bytes=VMEM_LIMIT)


def _full(shape):
    return pl.BlockSpec(shape, lambda *_: (0,) * len(shape))


def _split2(x):
    hi = x.astype(_BF16)
    return hi, (x - hi.astype(_F32)).astype(_BF16)


def _dot3(a, b, dims=_NN):
    d = lambda u, w: lax.dot_general(u, w, dims, preferred_element_type=_F32)
    return (d(a[1], b[0]) + d(a[0], b[1])) + d(a[0], b[0])


def _dot1(a, b, dims=_NN):
    return lax.dot_general(a.astype(_BF16), b.astype(_BF16), dims, preferred_element_type=_F32)


def _pack_bf16_pairs(x):
    w = x.shape[1] // 2
    bits = pltpu.bitcast(x.astype(_BF16).astype(_F32), jnp.uint32)
    return pltpu.bitcast((bits[:, :w] >> 16) | (bits[:, w:] & jnp.uint32(0xFFFF0000)), _F32)


def _unpack_bf16_pairs(words):
    packed = pltpu.bitcast(words, jnp.uint32)
    lo = pltpu.bitcast(packed << 16, _F32)
    hi = pltpu.bitcast(packed & jnp.uint32(0xFFFF0000), _F32)
    return jnp.concatenate([lo, hi], axis=1).astype(_BF16)


def _lane_lo(shape):
    return lax.broadcasted_iota(jnp.int32, shape, len(shape) - 1) < HEAD_DIM


def _head_sum_matrix(value):
    row = lax.broadcasted_iota(jnp.int32, (LANES, LANES), 0)
    col = lax.broadcasted_iota(jnp.int32, (LANES, LANES), 1)
    return jnp.where((row // HEAD_DIM) == (col // HEAD_DIM), value, 0.0).astype(_BF16)


def _head_sum(x, mat):
    hi, lo = _split2(x)
    return jnp.dot(lo, mat, preferred_element_type=_F32) + jnp.dot(hi, mat, preferred_element_type=_F32)


def _rms_rows(x, eps=RMS_EPS):
    return x * lax.rsqrt(jnp.mean(x * x, axis=-1, keepdims=True) + eps)


def _rope_slab(x, cos, sin):
    lane = lax.broadcasted_iota(jnp.int32, x.shape, 1)
    partner = jnp.where((lane & 16) == 0, pltpu.roll(x, LANES - 16, axis=1), pltpu.roll(x, 16, axis=1))
    return x * cos + partner * sin


def _mods_index(i):
    tiles = (CTX_LEN + SEQ) // ROW_TILE
    return (i // tiles) * 2 + jnp.minimum(i % tiles, 1)


def _mm_kernel(a_ref, w_ref, o_ref, *, precision):
    o_ref[...] = jnp.dot(a_ref[...], w_ref[...], precision=precision,
                         preferred_element_type=_F32).astype(o_ref.dtype)


def _mm(a, w, *, tn, precision=None):
    m, k = a.shape
    n = w.shape[1]
    return pl.pallas_call(
        functools.partial(_mm_kernel, precision=precision),
        grid=(n // tn,),
        in_specs=[pl.BlockSpec((m, k), lambda j: (0, 0)),
                  pl.BlockSpec((k, tn), lambda j: (0, j))],
        out_specs=pl.BlockSpec((m, tn), lambda j: (0, j)),
        out_shape=jax.ShapeDtypeStruct((m, n), _F32),
        name=f"mm_{k}x{n}",
        compiler_params=_params(1),
    )(a, w)


def _proj_kernel(*refs, combine, rope):
    it = iter(refs)
    x_ref, m_ref = next(it), next(it)
    if combine:
        y1_ref, y2_ref, g_ref, mp_ref = (next(it) for _ in range(4))
    w_ref = next(it)
    if rope:
        cos_ref, sin_ref = next(it), next(it)
    o_ref = next(it)
    x = x_ref[...]
    if combine:
        xo_ref = next(it)
        g = g_ref[...]
        x = x + mp_ref[0, 5:6, :] * (g[:, 0:1] * y1_ref[...] + g[:, 1:2] * y2_ref[...])
        xo_ref[...] = x
    h = _rms_rows(x) * (1.0 + m_ref[0, 1:2, :]) + m_ref[0, 0:1, :]
    p = jnp.dot(h.astype(_BF16), w_ref[...], preferred_element_type=_F32)
    if rope:
        cos, sin = cos_ref[...], sin_ref[...]
        for s in range(2 * C_QK // LANES):
            sl = slice(s * LANES, (s + 1) * LANES)
            slab = _rope_slab(p[:, sl], cos, sin)
            if s < C_QK // LANES:
                slab = slab * Q_SCALE
            o_ref[:, sl] = slab.astype(o_ref.dtype)
        o_ref[:, 2 * C_QK:] = p[:, 2 * C_QK:].astype(o_ref.dtype)
    else:
        o_ref[...] = p


def _project(x, mods, w, *, moe=None, rope=None):
    n, d = x.shape
    cols = w.shape[1]
    tm = ROW_TILE
    tiles = (CTX_LEN + SEQ) // tm
    row = lambda width: pl.BlockSpec((tm, width), lambda i: (i, 0))
    mod_spec = pl.BlockSpec((1, 6, d), lambda i: (_mods_index(i), 0, 0))
    args, specs = [x, mods], [row(d), mod_spec]
    if moe is not None:
        args += list(moe)
        specs += [row(d), row(d), row(LANES), mod_spec]
    args.append(w)
    specs.append(_full(w.shape))
    if rope is not None:
        args += list(rope)
        specs += [pl.BlockSpec((tm, LANES), lambda i: (i % tiles, 0))] * 2
    out_shape = [jax.ShapeDtypeStruct((n, cols), _BF16 if rope is not None else _F32)]
    out_specs = [row(cols)]
    if moe is not None:
        out_shape.append(jax.ShapeDtypeStruct((n, d), _F32))
        out_specs.append(row(d))
    out = pl.pallas_call(
        functools.partial(_proj_kernel, combine=moe is not None, rope=rope is not None),
        grid=(n // tm,), in_specs=specs, out_specs=out_specs, out_shape=out_shape,
        name=f"project_{cols}", compiler_params=_params(1),
    )(*args)
    return out if moe is not None else (out[0], x)


def _prep_kernel(*refs, has_v_first):
    it = iter(refs)
    p_ref, pp_ref, pn_ref, cos_ref, sin_ref = (next(it) for _ in range(5))
    gain_ref, mu_ref, wa0_ref, kv_ref, wl_ref, g2_ref = (next(it) for _ in range(6))
    if has_v_first:
        v1_ref, v2_ref, vf_ref = next(it), next(it), next(it)
    q_ref, kk2_ref, vv2_ref, r_ref, v_ref, kkn_ref, lw_ref, a_ref, kd_ref, g_ref = it

    tiles = (CTX_LEN + SEQ) // ROW_TILE
    j = pl.program_id(0) % tiles
    p = p_ref[...]
    cos, sin = cos_ref[...], sin_ref[...]
    mean_mat = _head_sum_matrix(1.0 / HEAD_DIM)
    ones_mat = _head_sum_matrix(1.0)
    lo = _lane_lo((ROW_TILE, LANES))

    def normed(x, gain):
        return x * lax.rsqrt(_head_sum(x * x, mean_mat) + RMS_EPS) * gain

    for s in range(A_WIDTH // LANES):
        sl = slice(s * LANES, (s + 1) * LANES)
        q = _rope_slab(normed(p[:, sl], gain_ref[0:1, :]), cos, sin)
        q_ref[:, sl] = (q * Q_SCALE).astype(_BF16)
    k = _rope_slab(normed(p[:, A_WIDTH:A_WIDTH + LANES], gain_ref[1:2, :]), cos, sin)
    v = p[:, A_WIDTH + LANES:A_COLS]
    for src, dst in ((k, kk2_ref), (v, vv2_ref)):
        swapped = pltpu.roll(src, HEAD_DIM, axis=1)
        dst[:, :LANES] = jnp.where(lo, src, swapped).astype(_BF16)
        dst[:, LANES:] = jnp.where(lo, swapped, src).astype(_BF16)

    u = p[:, A_COLS:]
    rid = lax.broadcasted_iota(jnp.int32, (ROW_TILE, 1), 0)
    has_prev = jnp.where(j >= 2, 1.0, 0.0)
    has_next = jnp.where((j >= 1) & (j < tiles - 1), 1.0, 0.0)
    first = pp_ref[7:8, A_COLS:] * has_prev
    last = pn_ref[0:1, A_COLS:] * has_next
    prev = jnp.where(rid == 0, first, pltpu.roll(u, 1, axis=0))
    nxt = jnp.where(rid == ROW_TILE - 1, last, pltpu.roll(u, ROW_TILE - 1, axis=0))
    u = u + mu_ref[0:1, :] * (prev - u) + mu_ref[1:2, :] * (nxt - u)
    r, k, v = u[:, :B_WIDTH], u[:, B_WIDTH:2 * B_WIDTH], u[:, 2 * B_WIDTH:3 * B_WIDTH]
    wa, glr = u[:, 3 * B_WIDTH:3 * B_WIDTH + LANES], u[:, 3 * B_WIDTH + LANES:]
    if has_v_first:
        mid = jnp.dot(v.astype(_BF16), v1_ref[...], preferred_element_type=_F32)
        lora = jnp.dot(mid.astype(_BF16), v2_ref[...], preferred_element_type=_F32)
        v = v + (vf_ref[...] - v) * jax.nn.sigmoid(kv_ref[2:3, :] + lora)
    r_ref[...] = r
    v_ref[...] = v
    kks = k * kv_ref[0:1, :]
    for s in range(B_WIDTH // LANES):
        sl = slice(s * LANES, (s + 1) * LANES)
        x = kks[:, sl]
        kkn_ref[:, sl] = x * lax.rsqrt(jnp.maximum(_head_sum(x * x, ones_mat), 1e-24))
    xw = jnp.where(lo, jnp.tanh(wa), wa).astype(_BF16)
    for dr in range(2):
        z = jnp.dot(xw, wl_ref[dr], preferred_element_type=_F32) + wa0_ref[dr:dr + 1, :]
        a = jax.nn.sigmoid(z[:, B_WIDTH:])
        lw_ref[dr] = -DECAY_SCALE * jax.nn.sigmoid(z[:, :B_WIDTH])
        a_ref[dr] = a
        kd_ref[dr] = k * (1.0 + (a - 1.0) * kv_ref[1:2, :])
    g_ref[...] = jnp.dot(jax.nn.sigmoid(glr).astype(_BF16), g2_ref[...], preferred_element_type=_F32)


def _prepare_even(proj, cos, sin, p, v_first):
    n, cols = proj.shape
    tm = ROW_TILE
    tiles = (CTX_LEN + SEQ) // tm
    halo = 8
    per_tile = tm // halo
    row = lambda width: pl.BlockSpec((tm, width), lambda i: (i, 0))
    both = pl.BlockSpec((2, tm, B_WIDTH), lambda i: (0, i, 0))
    tab = pl.BlockSpec((tm, LANES), lambda i: (i % tiles, 0))
    pad = lambda w, rows, width: jnp.zeros((rows, width), _BF16).at[:w.shape[0], :w.shape[1]].set(w.astype(_BF16))
    gains = jnp.stack([jnp.tile(p['q_gain'], 2), jnp.tile(p['k_gain'], 2)])
    mu = jnp.stack([p['mu_prev'], p['mu_next']])
    wa0 = jnp.concatenate([p['w0'], p['a0']], axis=1)
    kvec = jnp.stack([p['k_k'], p['k_a'], p.get('v0', jnp.zeros((B_WIDTH,), _F32))])
    zero = jnp.zeros((RWKV_W_LORA, B_WIDTH), _F32)
    wl = jnp.stack([jnp.concatenate([jnp.concatenate([p['w2'][dr], zero], axis=1),
                                     jnp.concatenate([zero, p['a2'][dr]], axis=1)], axis=0)
                    for dr in range(2)]).astype(_BF16)
    args = [proj, proj, proj, cos, sin, gains, mu, wa0, kvec, wl, p['g2'].astype(_BF16)]
    specs = [row(cols),
             pl.BlockSpec((halo, cols), lambda i: (jnp.maximum(i * per_tile - 1, 0), 0)),
             pl.BlockSpec((halo, cols), lambda i: (jnp.minimum((i + 1) * per_tile, n // halo - 1), 0)),
             tab, tab, _full(gains.shape), _full(mu.shape), _full(wa0.shape), _full(kvec.shape),
             _full(wl.shape), _full(p['g2'].shape)]
    if v_first is not None:
        v1 = pad(p['v1'], B_WIDTH, LANES)
        v2 = pad(p['v2'], LANES, B_WIDTH)
        args += [v1, v2, v_first]
        specs += [_full(v1.shape), _full(v2.shape), row(B_WIDTH)]
    f32 = lambda *shape: jax.ShapeDtypeStruct(shape, _F32)
    bf16 = lambda *shape: jax.ShapeDtypeStruct(shape, _BF16)
    out_shape = [bf16(n, A_WIDTH), bf16(n, 2 * LANES), bf16(n, 2 * LANES),
                 f32(n, B_WIDTH), f32(n, B_WIDTH), f32(n, B_WIDTH),
                 f32(2, n, B_WIDTH), f32(2, n, B_WIDTH), f32(2, n, B_WIDTH), f32(n, B_WIDTH)]
    out_specs = [row(A_WIDTH), row(2 * LANES), row(2 * LANES), row(B_WIDTH), row(B_WIDTH), row(B_WIDTH),
                 both, both, both, row(B_WIDTH)]
    return pl.pallas_call(
        functools.partial(_prep_kernel, has_v_first=v_first is not None),
        grid=(n // tm,), in_specs=specs, out_specs=out_specs, out_shape=out_shape,
        name="prepare_even", compiler_params=_params(1),
    )(*args)


def _split_heads(q2):
    lo = _lane_lo(q2.shape)
    zero = jnp.zeros_like(q2)
    return jnp.concatenate([jnp.where(lo, q2, zero), jnp.where(lo, zero, q2)], axis=0)


def _scores(qs, k):
    return lax.dot_general(qs, k, _NT, preferred_element_type=_F32)


def _interleave(chains):
    results = {}
    while len(results) < len(chains):
        for p, chain in enumerate(chains):
            try:
                next(chain)
            except StopIteration as stop:
                results[p] = stop.value
    return [results[p] for p in range(len(chains))]


def _softmax_pv(s, v):
    p = jnp.exp2(s - jnp.max(s, axis=-1, keepdims=True))
    l = jnp.sum(p, axis=-1, keepdims=True)
    p = p.astype(_BF16)
    yield
    return jnp.dot(p, v, preferred_element_type=_F32) / l


def _gqa_tile(q, k, v):
    tq = q.shape[0]
    s = _scores(_split_heads(q), k)
    yield
    o = yield from _softmax_pv(s, v)
    return jnp.where(_lane_lo((tq, LANES)), o[:tq], o[tq:])


def _attend_chains(make_chain, q_ref, o_ref):
    pieces = [(slice(i * Q_TILE, (i + 1) * Q_TILE), j, slice(j * LANES, (j + 1) * LANES))
              for i in range(q_ref.shape[1] // Q_TILE) for j in range(q_ref.shape[2] // LANES)]
    outs = _interleave([make_chain(q_ref[0, rows, lanes], j) for rows, j, lanes in pieces])
    for (rows, _, lanes), o in zip(pieces, outs):
        o_ref[0, rows, lanes] = o.astype(o_ref.dtype)


def _gqa_kernel(q_ref, k_ref, v_ref, o_ref):
    is_ctx = pl.program_id(2) < CTX_LEN // q_ref.shape[1]

    def run(tk):
        _attend_chains(lambda q, slab: _gqa_tile(q, k_ref[0, :tk], v_ref[0, :tk]), q_ref, o_ref)

    pl.when(is_ctx)(lambda: run(CTX_LEN))
    pl.when(jnp.logical_not(is_ctx))(lambda: run(k_ref.shape[1]))


def _gqa_attend(q, kk, vv):
    b, t, _ = q.shape
    tq = 2 * Q_TILE
    kv_spec = pl.BlockSpec((1, t, LANES), lambda i, g, j: (i, 0, g))
    return pl.pallas_call(
        _gqa_kernel,
        grid=(b, A_KV_HEADS, t // tq),
        in_specs=[pl.BlockSpec((1, tq, 2 * LANES), lambda i, g, j: (i, j, g)), kv_spec, kv_spec],
        out_specs=pl.BlockSpec((1, tq, 2 * LANES), lambda i, g, j: (i, j, g)),
        out_shape=jax.ShapeDtypeStruct((b, t, A_WIDTH), _BF16),
        name="gqa_attn", compiler_params=_params(3),
    )(q, kk, vv)


def _diff_tile(lam, q, k, v, gain):
    tq = q.shape[0]
    s = _scores(_split_heads(q), k)
    yield
    o = yield from _softmax_pv(s, v)
    o = o[:tq] - lam * o[tq:]
    y = o * lax.rsqrt(jnp.mean(o * o, axis=-1, keepdims=True) + SUBLN_EPS)
    return y * gain


def _diff_kernel(lam_ref, q_ref, k_ref, v_ref, g_ref, o_ref):
    is_ctx = pl.program_id(2) < CTX_LEN // q_ref.shape[1]
    lam, gain = lam_ref[0, 0], g_ref[...]

    def run(tk):
        def chain(q, slab):
            lanes = slice(slab * LANES, (slab + 1) * LANES)
            return _diff_tile(lam, q, k_ref[0, :tk, lanes], v_ref[0, :tk, lanes], gain)
        _attend_chains(chain, q_ref, o_ref)

    pl.when(is_ctx)(lambda: run(CTX_LEN))
    pl.when(jnp.logical_not(is_ctx))(lambda: run(k_ref.shape[1]))


def _diff_attend(lam, qkv, gain):
    b, t, _ = qkv.shape
    tq = 2 * Q_TILE
    pairs = C_HEADS // 2
    return pl.pallas_call(
        _diff_kernel,
        grid=(b, pairs, t // tq),
        in_specs=[pl.BlockSpec(memory_space=pltpu.SMEM),
                  pl.BlockSpec((1, tq, 2 * LANES), lambda i, h, j: (i, j, h)),
                  pl.BlockSpec((1, t, 2 * LANES), lambda i, h, j: (i, 0, pairs + h)),
                  pl.BlockSpec((1, t, 2 * LANES), lambda i, h, j: (i, 0, 2 * pairs + h)),
                  _full((1, LANES))],
        out_specs=pl.BlockSpec((1, tq, 2 * LANES), lambda i, h, j: (i, j, h)),
        out_shape=jax.ShapeDtypeStruct((b, t, C_HEADS * LANES), _BF16),
        name="diff_attn", compiler_params=_params(3),
    )(lam, qkv, qkv, qkv, gain)


def _scan_chunk(h, r, v, kk, lw, cum, cum_end, a, kd, sgn):
    c = SCAN_CHUNK
    lo = _lane_lo((c, LANES))
    row = lax.broadcasted_iota(jnp.int32, (LANES, LANES), 0)
    col = lax.broadcasted_iota(jnp.int32, (LANES, LANES), 1)
    delta = ((col & (c - 1)) - (row & (c - 1))) * sgn
    before = delta < 0
    upto = delta <= 0
    eye = row == col
    zero = jnp.zeros((LANES, LANES), _F32)

    def blockdiag(x):
        z = jnp.zeros_like(x)
        return jnp.concatenate([jnp.where(lo, x, z), jnp.where(lo, z, x)], axis=0)

    p_inv = jnp.exp(-cum)
    p_end = jnp.exp(cum_end - cum)
    kka = kk * a
    a_rs = blockdiag(-kk * jnp.exp(cum - lw))
    r_rs = blockdiag(r * jnp.exp(cum))
    b_rs = blockdiag(kka * p_inv)
    k_rs = blockdiag(kd * p_inv)
    v_rs = blockdiag(v)
    bp_rs = blockdiag(kka * p_end)
    kp_rs = blockdiag(kd * p_end)

    z = _dot3(_split2(jnp.concatenate([a_rs, r_rs], axis=0)),
              _split2(jnp.concatenate([b_rs, k_rs], axis=0)), _NT)
    yield
    low = jnp.where(before, z[:LANES, :LANES], zero)
    g = jnp.where(before, z[:LANES, LANES:], zero)
    rb = jnp.where(upto, z[LANES:, :LANES], zero)
    rk = jnp.where(upto, z[LANES:, LANES:], zero)

    x = _dot1(low, low)
    yield
    m = jnp.where(eye, 1.0, 0.0).astype(_F32) + low
    n = 2
    while 2 * n < c:
        res = _dot1(x, jnp.concatenate([m, x], axis=1))
        m = m + res[:, :LANES]
        x = res[:, LANES:]
        n *= 2
        yield
    m = m + _dot1(x, m)
    yield
    gv = _dot1(g, v_rs)
    yield
    au = _dot1(m, jnp.concatenate([a_rs, gv], axis=1))
    yield
    rhs = _split2(jnp.concatenate(
        [au, jnp.concatenate([zero, v_rs], axis=1)], axis=0))
    tf = lax.dot_general(jnp.concatenate([bp_rs, kp_rs], axis=0).astype(_BF16), rhs[0], _TN,
                         preferred_element_type=_F32)
    qy = _dot3(_split2(jnp.concatenate([rb, rk], axis=1)), rhs)
    yield
    t = jnp.where(eye, jnp.broadcast_to(jnp.exp(cum_end), (LANES, LANES)), zero) + tf[:, :LANES]
    q = r_rs + qy[:, :LANES]
    out = _dot1(jnp.concatenate([q, t], axis=0), h)
    y = out[:LANES] + qy[:, LANES:]
    h_new = out[LANES:] + tf[:, LANES:]
    return y[:c] + y[c:], h_new


def _scan_kernel(*refs):
    c = SCAN_CHUNK
    ins, y_refs, h_ref = refs[:12], refs[12:14], refs[14]

    @pl.when(pl.program_id(1) == 0)
    def _():
        h_ref[...] = jnp.zeros_like(h_ref)

    tr = lax.broadcasted_iota(jnp.int32, (c, c), 0)
    tc = lax.broadcasted_iota(jnp.int32, (c, c), 1)
    chains, outs = [], []
    for d, sgn in enumerate((1, -1)):
        r_ref, v_ref, kk_ref, lw_ref, a_ref, kd_ref = ins[6 * d:6 * d + 6]
        tri = jnp.where((tc - tr) * sgn <= 0, 1.0, 0.0).astype(_BF16)
        lw = lw_ref[0, 0]
        l1 = lw.astype(_BF16)
        rem = lw - l1.astype(_F32)
        l2 = rem.astype(_BF16)
        l3 = (rem - l2.astype(_F32)).astype(_BF16)
        dot = lambda w, tri=tri: jnp.dot(tri, w, preferred_element_type=_F32)
        cum = (dot(l3) + dot(l2)) + dot(l1)
        cum_end = jnp.sum(lw, axis=0, keepdims=True)
        pairs = lw.shape[1] // LANES
        for p in range(pairs):
            sl = slice(p * LANES, (p + 1) * LANES)
            chains.append(_scan_chunk(h_ref[d * pairs + p], r_ref[0, :, sl], v_ref[0, :, sl],
                                      kk_ref[0, :, sl], lw[:, sl], cum[:, sl], cum_end[:, sl],
                                      a_ref[0, 0, :, sl], kd_ref[0, 0, :, sl], sgn))
            outs.append((y_refs[d], sl, d * pairs + p))
    for (y_ref, sl, slot), (y, h) in zip(outs, _interleave(chains)):
        y_ref[0, :, sl] = y
        h_ref[slot] = h


def _delta_scan(r, v, kk, lw, a, kd):
    b, t, w = r.shape
    c = SCAN_CHUNK
    n_chunks = t // c
    n_ctx = CTX_LEN // c

    def chunk(d, j):
        return j if d == 0 else jnp.where(j < n_ctx, n_ctx - 1 - j, n_chunks + n_ctx - 1 - j)

    specs, args = [], []
    for d in range(2):
        shared = pl.BlockSpec((1, c, w), lambda i, j, d=d: (i, chunk(d, j), 0))
        per_dir = pl.BlockSpec((1, 1, c, w), lambda i, j, d=d: (d, i, chunk(d, j), 0))
        specs += [shared, shared, shared, per_dir, per_dir, per_dir]
        args += [r, v, kk, lw, a, kd]
    out_specs = [pl.BlockSpec((1, c, w), lambda i, j, d=d: (i, chunk(d, j), 0)) for d in range(2)]
    return pl.pallas_call(
        _scan_kernel,
        grid=(b, n_chunks),
        in_specs=specs,
        out_specs=out_specs,
        out_shape=[jax.ShapeDtypeStruct((b, t, w), _F32)] * 2,
        scratch_shapes=[pltpu.VMEM((2 * w // LANES, LANES, LANES), _F32)],
        name="delta_scan", compiler_params=_params(2),
    )(*args)


def _post_kernel(*refs, even):
    it = iter(refs)
    if even:
        yf_ref, yb_ref, r_ref, v_ref, kd_ref, g_ref, gn_ref = (next(it) for _ in range(7))
    att_ref, x_ref, m_ref, wo_ref, wrh_ref, wrl_ref, br_ref = (next(it) for _ in range(7))
    xo_ref, h_ref, route_ref, count_ref, carry_ref = it
    if even:
        mean_mat = _head_sum_matrix(1.0 / HEAD_DIM)
        ones_mat = _head_sum_matrix(1.0)
        parts = [att_ref[...]]
        for s in range(B_WIDTH // LANES):
            sl = slice(s * LANES, (s + 1) * LANES)
            y = yf_ref[:, sl] + yb_ref[:, sl]
            dev = y - _head_sum(y, mean_mat)
            var = _head_sum(dev * dev, mean_mat)
            y = dev * lax.rsqrt(var + RWKV_GN_EPS) * gn_ref[0:1, sl] + gn_ref[1:2, sl]
            r, v = r_ref[:, sl], v_ref[:, sl]
            for dr in range(2):
                y = y + _head_sum(r * kd_ref[dr, :, sl] * gn_ref[2:3, sl], ones_mat) * v
            parts.append((y * g_ref[:, sl]).astype(_BF16))
        mixed = jnp.concatenate(parts, axis=1)
    else:
        mixed = att_ref[...]
    out = jnp.dot(mixed, wo_ref[...], preferred_element_type=_F32)
    x = x_ref[...] + m_ref[0, 2:3, :] * out
    xo_ref[...] = x
    h = _rms_rows(x) * (1.0 + m_ref[0, 4:5, :]) + m_ref[0, 3:4, :]
    h_ref[...] = _pack_bf16_pairs(h)
    logits = _dot3(_split2(h), (wrh_ref[...], wrl_ref[...])) + br_ref[...]
    _route(logits, route_ref, count_ref, carry_ref)


def _route(logits, route_ref, count_ref, carry_ref):
    tm = logits.shape[0]
    lane = lax.broadcasted_iota(jnp.int32, (tm, LANES), 1)
    neg = -1e30
    first_lane = lambda hit: jnp.min(jnp.where(hit, lane, LANES), axis=-1, keepdims=True)

    is_grp = lane < N_GROUPS
    grp = jnp.where(is_grp, logits, neg)
    g_max = jnp.max(grp, axis=-1, keepdims=True)
    g_sel = first_lane(grp == g_max)
    grp_w = 1.0 / jnp.sum(jnp.where(is_grp, jnp.exp(grp - g_max), 0.0), axis=-1, keepdims=True)

    base = N_GROUPS + EXPERTS_PER_GROUP * g_sel
    in_grp = (lane >= base) & (lane < base + EXPERTS_PER_GROUP)
    ex = jnp.where(in_grp, logits, neg)
    e_max = jnp.max(ex, axis=-1, keepdims=True)
    pe = jnp.where(in_grp, jnp.exp(ex - e_max), 0.0)
    prob = pe / jnp.sum(pe, axis=-1, keepdims=True)
    p1 = jnp.max(prob, axis=-1, keepdims=True)
    l1 = first_lane(in_grp & (prob == p1))
    rest = jnp.where(in_grp & (lane != l1), prob, -1.0)
    p2 = jnp.max(rest, axis=-1, keepdims=True)
    l2 = first_lane(rest == p2)
    scale = grp_w / (p1 + p2)
    e1, e2 = l1 - N_GROUPS, l2 - N_GROUPS

    @pl.when(pl.program_id(0) == 0)
    def _():
        carry_ref[...] = jnp.zeros_like(carry_ref)

    oh1 = jnp.where(lane == e1, 1.0, 0.0)
    oh2 = jnp.where(lane == e2, 1.0, 0.0)
    rr = lax.broadcasted_iota(jnp.int32, (tm, tm), 0)
    cc = lax.broadcasted_iota(jnp.int32, (tm, tm), 1)
    tri = jnp.where(cc < rr, 1.0, 0.0).astype(_BF16)
    before = lambda oh: jnp.dot(tri, oh.astype(_BF16), preferred_element_type=_F32)
    carry = carry_ref[...]
    tot1 = jnp.sum(oh1, axis=0, keepdims=True)
    rank1 = jnp.sum(oh1 * (before(oh1) + carry), axis=-1, keepdims=True)
    rank2 = jnp.sum(oh2 * (before(oh2) + (carry + tot1)), axis=-1, keepdims=True)
    carry = carry + tot1 + jnp.sum(oh2, axis=0, keepdims=True)
    carry_ref[...] = carry
    count_ref[...] = carry

    out = jnp.zeros((tm, LANES), _F32)
    for i, val in enumerate((p1 * scale, p2 * scale, e1.astype(_F32), e2.astype(_F32), rank1, rank2)):
        out = jnp.where(lane == i, val, out)
    route_ref[...] = out


def _mixer_out(att, x, mods, w_out, w_router, b_router, rwkv=None):
    n, d = x.shape
    tm = ROW_TILE
    row = lambda width: pl.BlockSpec((tm, width), lambda i: (i, 0))
    both = pl.BlockSpec((2, tm, B_WIDTH), lambda i: (0, i, 0))
    args, specs = [], []
    if rwkv is not None:
        y_fwd, y_bwd, r, v, kd, g, gn = rwkv
        args += [y_fwd, y_bwd, r, v, kd, g, gn]
        specs += [row(B_WIDTH)] * 4 + [both, row(B_WIDTH), _full(gn.shape)]
    wr_hi, wr_lo = _split2(w_router)
    args += [att, x, mods, w_out, wr_hi, wr_lo, b_router]
    specs += [row(att.shape[1]), row(d), pl.BlockSpec((1, 6, d), lambda i: (_mods_index(i), 0, 0)),
              _full(w_out.shape), _full(wr_hi.shape), _full(wr_lo.shape), _full(b_router.shape)]
    return pl.pallas_call(
        functools.partial(_post_kernel, even=rwkv is not None),
        grid=(n // tm,), in_specs=specs,
        out_specs=[row(d), row(d // 2), row(LANES), _full((1, LANES))],
        out_shape=[jax.ShapeDtypeStruct((n, d), _F32), jax.ShapeDtypeStruct((n, d // 2), _F32),
                   jax.ShapeDtypeStruct((n, LANES), _F32), jax.ShapeDtypeStruct((1, LANES), _F32)],
        scratch_shapes=[pltpu.VMEM((1, LANES), _F32)],
        name="mixer_out_even" if rwkv is not None else "mixer_out_odd", compiler_params=_params(1),
    )(*args)


def _moe_kernel(be_ref, nv_ref, x_ref, wg_ref, wu_ref, wd_ref, o_ref, wg_s, wu_s, wd_s):
    i = pl.program_id(0)
    valid = i < nv_ref[0]
    new_expert = (i == 0) | (be_ref[i] != be_ref[jnp.maximum(i - 1, 0)])

    @pl.when(valid & new_expert)
    def _():
        wg_s[...] = wg_ref[0, 0].astype(_BF16)
        wu_s[...] = wu_ref[0, 0].astype(_BF16)
        wd_s[...] = wd_ref[0, 0].astype(_BF16)

    @pl.when(valid)
    def _():
        x = _unpack_bf16_pairs(x_ref[...])
        gate = jnp.dot(x, wg_s[...], preferred_element_type=_F32)
        up = jnp.dot(x, wu_s[...], preferred_element_type=_F32)
        hid = (gate * jax.nn.sigmoid(gate) * up).astype(_BF16)
        o_ref[...] = jnp.dot(hid, wd_s[...], preferred_element_type=_F32)

    @pl.when(jnp.logical_not(valid))
    def _():
        o_ref[...] = jnp.zeros_like(o_ref)


def _moe_blocks(block_expert, n_valid, xs, wg, wu, wd, layer):
    rows, d = xs.shape[0], 2 * xs.shape[1]
    nb = rows // MOE_ROWS
    w_spec = lambda a, b: pl.BlockSpec((1, 1, a, b), lambda i, be, nv: (layer, be[i], 0, 0))
    grid_spec = pltpu.PrefetchScalarGridSpec(
        num_scalar_prefetch=2,
        grid=(nb,),
        in_specs=[pl.BlockSpec((MOE_ROWS, d // 2), lambda i, be, nv: (i, 0)),
                  w_spec(d, EXPERT_HIDDEN), w_spec(d, EXPERT_HIDDEN), w_spec(EXPERT_HIDDEN, d)],
        out_specs=pl.BlockSpec((MOE_ROWS, d), lambda i, be, nv: (i, 0)),
        scratch_shapes=[pltpu.VMEM((d, EXPERT_HIDDEN), _BF16), pltpu.VMEM((d, EXPERT_HIDDEN), _BF16),
                        pltpu.VMEM((EXPERT_HIDDEN, d), _BF16)],
    )
    return pl.pallas_call(
        _moe_kernel,
        grid_spec=grid_spec,
        out_shape=jax.ShapeDtypeStruct((rows, d), _F32),
        name="moe_ffn", compiler_params=_params(1),
    )(block_expert, n_valid, xs, wg, wu, wd)


def _moe(h, route, counts, w_gate, w_up, w_down, layer):
    n = h.shape[0]
    eid = route[:, 2:4].astype(jnp.int32)
    rank = route[:, 4:6].astype(jnp.int32)
    cnt = counts[0, :N_EXPERTS].astype(jnp.int32)
    padded = (cnt + MOE_ROWS - 1) // MOE_ROWS * MOE_ROWS
    pad_end = jnp.cumsum(padded)
    experts = jnp.arange(N_EXPERTS, dtype=jnp.int32)
    pad_start = jnp.sum(jnp.where(eid[:, :, None] == experts, pad_end - padded, 0), axis=-1)
    slot = pad_start + rank
    n_blocks = -(-(n * TOP_K + N_EXPERTS * (MOE_ROWS - 1)) // MOE_ROWS)
    block_start = jnp.arange(n_blocks, dtype=jnp.int32) * MOE_ROWS
    block_expert = jnp.minimum(jnp.sum(pad_end[None, :] <= block_start[:, None], axis=1),
                               N_EXPERTS - 1).astype(jnp.int32)
    n_valid = (pad_end[-1:] // MOE_ROWS).astype(jnp.int32)
    tok = jnp.broadcast_to(jnp.arange(n, dtype=jnp.int32)[:, None], (n, TOP_K))
    _, tok_by_slot = lax.sort_key_val(slot.reshape(-1), tok.reshape(-1))
    run_start = (jnp.cumsum(cnt) - cnt)[block_expert]
    run_pos = block_start - (pad_end - padded)[block_expert]
    pos = (run_pos[:, None] + jnp.arange(MOE_ROWS, dtype=jnp.int32)).reshape(-1)
    occupied = pos < jnp.repeat(cnt[block_expert], MOE_ROWS)
    src = jnp.minimum(jnp.repeat(run_start, MOE_ROWS) + pos, n * TOP_K - 1)
    slot_tok = jnp.where(occupied, tok_by_slot[src], 0)
    yb = _moe_blocks(block_expert, n_valid, h[slot_tok], w_gate, w_up, w_down, layer)
    return yb[slot[:, 0]], yb[slot[:, 1]]


def _final_kernel(x_ref, y1_ref, y2_ref, g_ref, m_ref, gain_ref, o_ref):
    g = g_ref[...]
    x = x_ref[...] + m_ref[0, 5:6, :] * (g[:, 0:1] * y1_ref[...] + g[:, 1:2] * y2_ref[...])
    o_ref[...] = _rms_rows(x) * gain_ref[...]


def _final(x, y1, y2, route, mods, gain, bsz, n_lat):
    d = x.shape[1]
    tm = ROW_TILE
    lat_tiles = n_lat // tm
    tiles = (CTX_LEN + n_lat) // tm
    src = lambda i: (i // lat_tiles) * tiles + CTX_LEN // tm + i % lat_tiles
    row = lambda width: pl.BlockSpec((tm, width), lambda i: (src(i), 0))
    return pl.pallas_call(
        _final_kernel,
        grid=(bsz * lat_tiles,),
        in_specs=[row(d), row(d), row(d), row(LANES),
                  pl.BlockSpec((1, 6, d), lambda i: ((i // lat_tiles) * 2 + 1, 0, 0)), _full((1, d))],
        out_specs=pl.BlockSpec((tm, d), lambda i: (i, 0)),
        out_shape=jax.ShapeDtypeStruct((bsz * n_lat, d), _F32),
        name="final_norm", compiler_params=_params(1),
    )(x, y1, y2, route, mods, gain.reshape(1, d))


def _rope_tables(n_lat):
    nf = HEAD_DIM // 4
    inv = ROPE_THETA ** (-jnp.arange(nf, dtype=_F32) / nf)
    rows = n_lat // GRID_W
    r_ang = jnp.repeat(jnp.arange(rows, dtype=_F32), GRID_W)[:, None] * inv
    c_ang = jnp.tile(jnp.arange(GRID_W, dtype=_F32), rows)[:, None] * inv
    cos = jnp.concatenate([jnp.cos(r_ang)] * 2 + [jnp.cos(c_ang)] * 2, axis=-1)
    sin = jnp.concatenate([-jnp.sin(r_ang), jnp.sin(r_ang), -jnp.sin(c_ang), jnp.sin(c_ang)], axis=-1)
    cos = jnp.concatenate([jnp.ones((CTX_LEN, HEAD_DIM), _F32), cos], axis=0)
    sin = jnp.concatenate([jnp.zeros((CTX_LEN, HEAD_DIM), _F32), sin], axis=0)
    return jnp.tile(cos, (1, 2)), jnp.tile(sin, (1, 2))


def kernel(x, c, ctx, c_ctx, mod_w, mod_b, ev_w_in, ev_w_out, ev_q_gain, ev_k_gain, ev_mu_prev, ev_mu_next, ev_w0, ev_w2, ev_a0, ev_a2, ev_g2, ev_k_k, ev_k_a, ev_r_k, ev_gn_w, ev_gn_b, ev_v0, ev_v1, ev_v2, od_w_in, od_w_out, od_lq1, od_lk1, od_lq2, od_lk2, od_subln, moe_w_grp, moe_b_grp, moe_w_rt, moe_b_rt, moe_w_gate, moe_w_up, moe_w_down, final_gain):
    bsz, n_lat, d = x.shape
    t = CTX_LEN + n_lat
    n = bsz * t
    cos, sin = _rope_tables(n_lat)
    cond = jax.nn.silu(jnp.concatenate([c, c_ctx[None, :], jnp.zeros((16 - bsz - 1, d), _F32)], axis=0))
    xs = jnp.concatenate([ctx, x], axis=1).reshape(n, d)
    v_first = None
    moe = None
    for layer in range(DEPTH):
        m = _mm(cond, mod_w[layer], tn=1536, precision=_HI) + mod_b[layer]
        m = m.reshape(16, 6, d)
        mods = jnp.stack([jnp.broadcast_to(m[bsz], (bsz, 6, d)), m[:bsz]], axis=1).reshape(2 * bsz, 6, d)
        w_router = jnp.concatenate(
            [moe_w_grp[layer], moe_w_rt[layer], jnp.zeros((d, LANES - N_GROUPS - N_EXPERTS), _F32)], axis=1)
        b_router = jnp.concatenate(
            [moe_b_grp[layer], moe_b_rt[layer], jnp.zeros((LANES - N_GROUPS - N_EXPERTS,), _F32)])[None, :]
        if layer % 2 == 0:
            e = layer // 2
            p = {'q_gain': ev_q_gain[e], 'k_gain': ev_k_gain[e], 'mu_prev': ev_mu_prev[e],
                 'mu_next': ev_mu_next[e], 'w0': ev_w0[e], 'w2': ev_w2[e], 'a0': ev_a0[e],
                 'a2': ev_a2[e], 'g2': ev_g2[e], 'k_k': ev_k_k[e], 'k_a': ev_k_a[e]}
            if e > 0:
                p['v0'], p['v1'], p['v2'] = ev_v0[e - 1], ev_v1[e - 1], ev_v2[e - 1]
            proj, xs = _project(xs, mods, ev_w_in[e].astype(_BF16), moe=moe)
            q, kk2, vv2, r, v, kkn, lw, a, kd, g = _prepare_even(proj, cos, sin, p, v_first)
            if v_first is None:
                v_first = v
            att = _gqa_attend(q.reshape(bsz, t, -1), kk2.reshape(bsz, t, -1), vv2.reshape(bsz, t, -1))
            seq = lambda z: z.reshape(*z.shape[:-2], bsz, t, B_WIDTH)
            y_fwd, y_bwd = (y.reshape(n, B_WIDTH) for y in
                            _delta_scan(seq(r), seq(v), seq(kkn), seq(lw), seq(a), seq(kd)))
            gn = jnp.stack([ev_gn_w[e], ev_gn_b[e], ev_r_k[e].reshape(B_WIDTH)])
            xs, h, route, counts = _mixer_out(att.reshape(n, -1), xs, mods, ev_w_out[e].astype(_BF16),
                                       w_router, b_router, rwkv=(y_fwd, y_bwd, r, v, kd, g, gn))
        else:
            o = layer // 2
            lambda_init = 0.8 - 0.6 * math.exp(-0.3 * layer)
            qkv, xs = _project(xs, mods, od_w_in[o].astype(_BF16), moe=moe, rope=(cos, sin))
            lam = (jnp.exp(jnp.sum(od_lq1[o] * od_lk1[o])) - jnp.exp(jnp.sum(od_lq2[o] * od_lk2[o]))
                   + lambda_init).reshape(1, 1)
            gain = (od_subln[o] * (1.0 - lambda_init)).reshape(1, LANES)
            att = _diff_attend(lam, qkv.reshape(bsz, t, -1), gain)
            xs, h, route, counts = _mixer_out(att.reshape(n, -1), xs, mods, od_w_out[o].astype(_BF16),
                                       w_router, b_router)
        y1, y2 = _moe(h, route, counts, moe_w_gate, moe_w_up, moe_w_down, layer)
        moe = (y1, y2, route, mods)
    return _final(xs, y1, y2, route, mods, final_gain, bsz, n_lat).reshape(bsz, n_lat, d)
```

```python
import functools
import math

import jax
import jax.numpy as jnp
from jax import lax
from jax.experimental import pallas as pl
from jax.experimental.pallas import tpu as pltpu

D_MODEL = 1024
DEPTH = 4
SEQ = 2048
GRID_W = 64
CTX_LEN = 256
HEAD_DIM = 64
ROPE_THETA = 10000.0
RMS_EPS = 1e-6
SUBLN_EPS = 1e-5
Q_SCALE = HEAD_DIM ** -0.5 * math.log2(math.e)

A_HEADS = 8
A_KV_HEADS = 2
A_WIDTH = 512
A_KV_WIDTH = 128
A_COLS = 768

B_HEADS = 8
B_WIDTH = 512
B_COLS = 1792
RWKV_W_LORA = 64
RWKV_A_LORA = 64
RWKV_V_LORA = 32
RWKV_G_LORA = 128
RWKV_GN_EPS = 64e-5
DECAY_SCALE = math.exp(-0.5)

C_HEADS = 8
C_QK = 1024

N_GROUPS = 4
EXPERTS_PER_GROUP = 8
N_EXPERTS = 32
TOP_K = 2
EXPERT_HIDDEN = 512
MOE_ROWS = 256

LANES = 128
SCAN_CHUNK = 64
SCAN_STEP_CHUNKS = 2
ROW_TILE = 256
Q_TILE = 128
VMEM_LIMIT = 56 * 1024 * 1024

_HI = lax.Precision.HIGHEST
_F32 = jnp.float32
_BF16 = jnp.bfloat16

_NN = (((1,), (0,)), ((), ()))
_NT = (((1,), (1,)), ((), ()))
_TN = (((0,), (0,)), ((), ()))


def _params(n_axes):
    return pltpu.CompilerParams(dimension_semantics=("arbitrary",) * n_axes,
                                vmem_limit_bytes=VMEM_LIMIT)


def _full(shape):
    return pl.BlockSpec(shape, lambda *_: (0,) * len(shape))


def _split2(x):
    hi = x.astype(_BF16)
    return hi, (x - hi.astype(_F32)).astype(_BF16)


def _dot3(a, b, dims=_NN):
    d = lambda u, w: lax.dot_general(u, w, dims, preferred_element_type=_F32)
    return (d(a[1], b[0]) + d(a[0], b[1])) + d(a[0], b[0])


def _dot1(a, b, dims=_NN):
    return lax.dot_general(a.astype(_BF16), b.astype(_BF16), dims, preferred_element_type=_F32)


def _pack_bf16_pairs(x):
    w = x.shape[1] // 2
    bits = pltpu.bitcast(x.astype(_BF16).astype(_F32), jnp.uint32)
    return pltpu.bitcast((bits[:, :w] >> 16) | (bits[:, w:] & jnp.uint32(0xFFFF0000)), _F32)


def _unpack_bf16_pairs(words):
    packed = pltpu.bitcast(words, jnp.uint32)
    lo = pltpu.bitcast(packed << 16, _F32)
    hi = pltpu.bitcast(packed & jnp.uint32(0xFFFF0000), _F32)
    return jnp.concatenate([lo, hi], axis=1).astype(_BF16)


def _lane_lo(shape):
    return lax.broadcasted_iota(jnp.int32, shape, len(shape) - 1) < HEAD_DIM


def _head_sum_matrix(value):
    row = lax.broadcasted_iota(jnp.int32, (LANES, LANES), 0)
    col = lax.broadcasted_iota(jnp.int32, (LANES, LANES), 1)
    return jnp.where((row // HEAD_DIM) == (col // HEAD_DIM), value, 0.0).astype(_BF16)


def _head_sum(x, mat):
    hi, lo = _split2(x)
    return jnp.dot(lo, mat, preferred_element_type=_F32) + jnp.dot(hi, mat, preferred_element_type=_F32)


def _rms_rows(x, eps=RMS_EPS):
    return x * lax.rsqrt(jnp.mean(x * x, axis=-1, keepdims=True) + eps)


def _rope_slab(x, cos, sin):
    lane = lax.broadcasted_iota(jnp.int32, x.shape, 1)
    partner = jnp.where((lane & 16) == 0, pltpu.roll(x, LANES - 16, axis=1), pltpu.roll(x, 16, axis=1))
    return x * cos + partner * sin


def _mods_index(i):
    tiles = (CTX_LEN + SEQ) // ROW_TILE
    return (i // tiles) * 2 + jnp.minimum(i % tiles, 1)


def _mm_kernel(a_ref, w_ref, o_ref, *, precision):
    o_ref[...] = jnp.dot(a_ref[...], w_ref[...], precision=precision,
                         preferred_element_type=_F32).astype(o_ref.dtype)


def _mm(a, w, *, tn, precision=None):
    m, k = a.shape
    n = w.shape[1]
    return pl.pallas_call(
        functools.partial(_mm_kernel, precision=precision),
        grid=(n // tn,),
        in_specs=[pl.BlockSpec((m, k), lambda j: (0, 0)),
                  pl.BlockSpec((k, tn), lambda j: (0, j))],
        out_specs=pl.BlockSpec((m, tn), lambda j: (0, j)),
        out_shape=jax.ShapeDtypeStruct((m, n), _F32),
        name=f"mm_{k}x{n}",
        compiler_params=_params(1),
    )(a, w)


def _proj_kernel(*refs, combine, rope):
    it = iter(refs)
    x_ref, m_ref = next(it), next(it)
    if combine:
        y1_ref, y2_ref, g_ref, mp_ref = (next(it) for _ in range(4))
    w_ref = next(it)
    if rope:
        cos_ref, sin_ref = next(it), next(it)
    o_ref = next(it)
    x = x_ref[...]
    if combine:
        xo_ref = next(it)
        g = g_ref[...]
        x = x + mp_ref[0, 5:6, :] * (g[:, 0:1] * y1_ref[...] + g[:, 1:2] * y2_ref[...])
        xo_ref[...] = x
    h = _rms_rows(x) * (1.0 + m_ref[0, 1:2, :]) + m_ref[0, 0:1, :]
    p = jnp.dot(h.astype(_BF16), w_ref[...], preferred_element_type=_F32)
    if rope:
        cos, sin = cos_ref[...], sin_ref[...]
        for s in range(2 * C_QK // LANES):
            sl = slice(s * LANES, (s + 1) * LANES)
            slab = _rope_slab(p[:, sl], cos, sin)
            if s < C_QK // LANES:
                slab = slab * Q_SCALE
            o_ref[:, sl] = slab.astype(o_ref.dtype)
        o_ref[:, 2 * C_QK:] = p[:, 2 * C_QK:].astype(o_ref.dtype)
    else:
        o_ref[...] = p


def _project(x, mods, w, *, moe=None, rope=None):
    n, d = x.shape
    cols = w.shape[1]
    tm = ROW_TILE
    tiles = (CTX_LEN + SEQ) // tm
    row = lambda width: pl.BlockSpec((tm, width), lambda i: (i, 0))
    mod_spec = pl.BlockSpec((1, 6, d), lambda i: (_mods_index(i), 0, 0))
    args, specs = [x, mods], [row(d), mod_spec]
    if moe is not None:
        args += list(moe)
        specs += [row(d), row(d), row(LANES), mod_spec]
    args.append(w)
    specs.append(_full(w.shape))
    if rope is not None:
        args += list(rope)
        specs += [pl.BlockSpec((tm, LANES), lambda i: (i % tiles, 0))] * 2
    out_shape = [jax.ShapeDtypeStruct((n, cols), _BF16 if rope is not None else _F32)]
    out_specs = [row(cols)]
    if moe is not None:
        out_shape.append(jax.ShapeDtypeStruct((n, d), _F32))
        out_specs.append(row(d))
    out = pl.pallas_call(
        functools.partial(_proj_kernel, combine=moe is not None, rope=rope is not None),
        grid=(n // tm,), in_specs=specs, out_specs=out_specs, out_shape=out_shape,
        name=f"project_{cols}", compiler_params=_params(1),
    )(*args)
    return out if moe is not None else (out[0], x)


def _prep_kernel(*refs, has_v_first):
    it = iter(refs)
    p_ref, pp_ref, pn_ref, cos_ref, sin_ref = (next(it) for _ in range(5))
    gain_ref, mu_ref, wa0_ref, kv_ref, wl_ref, g2_ref = (next(it) for _ in range(6))
    if has_v_first:
        v1_ref, v2_ref, vf_ref = next(it), next(it), next(it)
    q_ref, kk2_ref, vv2_ref, r_ref, v_ref, kkn_ref, lw_ref, a_ref, kd_ref, g_ref = it

    tiles = (CTX_LEN + SEQ) // ROW_TILE
    j = pl.program_id(0) % tiles
    p = p_ref[...]
    cos, sin = cos_ref[...], sin_ref[...]
    mean_mat = _head_sum_matrix(1.0 / HEAD_DIM)
    ones_mat = _head_sum_matrix(1.0)
    lo = _lane_lo((ROW_TILE, LANES))

    def normed(x, gain):
        return x * lax.rsqrt(_head_sum(x * x, mean_mat) + RMS_EPS) * gain

    for s in range(A_WIDTH // LANES):
        sl = slice(s * LANES, (s + 1) * LANES)
        q = _rope_slab(normed(p[:, sl], gain_ref[0:1, :]), cos, sin)
        q_ref[:, sl] = (q * Q_SCALE).astype(_BF16)
    k = _rope_slab(normed(p[:, A_WIDTH:A_WIDTH + LANES], gain_ref[1:2, :]), cos, sin)
    v = p[:, A_WIDTH + LANES:A_COLS]
    for src, dst in ((k, kk2_ref), (v, vv2_ref)):
        swapped = pltpu.roll(src, HEAD_DIM, axis=1)
        dst[:, :LANES] = jnp.where(lo, src, swapped).astype(_BF16)
        dst[:, LANES:] = jnp.where(lo, swapped, src).astype(_BF16)

    u = p[:, A_COLS:]
    rid = lax.broadcasted_iota(jnp.int32, (ROW_TILE, 1), 0)
    has_prev = jnp.where(j >= 2, 1.0, 0.0)
    has_next = jnp.where((j >= 1) & (j < tiles - 1), 1.0, 0.0)
    first = pp_ref[7:8, A_COLS:] * has_prev
    last = pn_ref[0:1, A_COLS:] * has_next
    prev = jnp.where(rid == 0, first, pltpu.roll(u, 1, axis=0))
    nxt = jnp.where(rid == ROW_TILE - 1, last, pltpu.roll(u, ROW_TILE - 1, axis=0))
    u = u + mu_ref[0:1, :] * (prev - u) + mu_ref[1:2, :] * (nxt - u)
    r, k, v = u[:, :B_WIDTH], u[:, B_WIDTH:2 * B_WIDTH], u[:, 2 * B_WIDTH:3 * B_WIDTH]
    wa, glr = u[:, 3 * B_WIDTH:3 * B_WIDTH + LANES], u[:, 3 * B_WIDTH + LANES:]
    if has_v_first:
        mid = jnp.dot(v.astype(_BF16), v1_ref[...], preferred_element_type=_F32)
        lora = jnp.dot(mid.astype(_BF16), v2_ref[...], preferred_element_type=_F32)
        v = v + (vf_ref[...] - v) * jax.nn.sigmoid(kv_ref[2:3, :] + lora)
    r_ref[...] = r
    v_ref[...] = v
    kks = k * kv_ref[0:1, :]
    for s in range(B_WIDTH // LANES):
        sl = slice(s * LANES, (s + 1) * LANES)
        x = kks[:, sl]
        kkn_ref[:, sl] = x * lax.rsqrt(jnp.maximum(_head_sum(x * x, ones_mat), 1e-24))
    xw = jnp.where(lo, jnp.tanh(wa), wa).astype(_BF16)
    for dr in range(2):
        z = jnp.dot(xw, wl_ref[dr], preferred_element_type=_F32) + wa0_ref[dr:dr + 1, :]
        a = jax.nn.sigmoid(z[:, B_WIDTH:])
        lw_ref[dr] = -DECAY_SCALE * jax.nn.sigmoid(z[:, :B_WIDTH])
        a_ref[dr] = a
        kd_ref[dr] = k * (1.0 + (a - 1.0) * kv_ref[1:2, :])
    g_ref[...] = jnp.dot(jax.nn.sigmoid(glr).astype(_BF16), g2_ref[...], preferred_element_type=_F32)


def _prepare_even(proj, cos, sin, p, v_first):
    n, cols = proj.shape
    tm = ROW_TILE
    tiles = (CTX_LEN + SEQ) // tm
    halo = 8
    per_tile = tm // halo
    row = lambda width: pl.BlockSpec((tm, width), lambda i: (i, 0))
    both = pl.BlockSpec((2, tm, B_WIDTH), lambda i: (0, i, 0))
    tab = pl.BlockSpec((tm, LANES), lambda i: (i % tiles, 0))
    pad = lambda w, rows, width: jnp.zeros((rows, width), _BF16).at[:w.shape[0], :w.shape[1]].set(w.astype(_BF16))
    gains = jnp.stack([jnp.tile(p['q_gain'], 2), jnp.tile(p['k_gain'], 2)])
    mu = jnp.stack([p['mu_prev'], p['mu_next']])
    wa0 = jnp.concatenate([p['w0'], p['a0']], axis=1)
    kvec = jnp.stack([p['k_k'], p['k_a'], p.get('v0', jnp.zeros((B_WIDTH,), _F32))])
    zero = jnp.zeros((RWKV_W_LORA, B_WIDTH), _F32)
    wl = jnp.stack([jnp.concatenate([jnp.concatenate([p['w2'][dr], zero], axis=1),
                                     jnp.concatenate([zero, p['a2'][dr]], axis=1)], axis=0)
                    for dr in range(2)]).astype(_BF16)
    args = [proj, proj, proj, cos, sin, gains, mu, wa0, kvec, wl, p['g2'].astype(_BF16)]
    specs = [row(cols),
             pl.BlockSpec((halo, cols), lambda i: (jnp.maximum(i * per_tile - 1, 0), 0)),
             pl.BlockSpec((halo, cols), lambda i: (jnp.minimum((i + 1) * per_tile, n // halo - 1), 0)),
             tab, tab, _full(gains.shape), _full(mu.shape), _full(wa0.shape), _full(kvec.shape),
             _full(wl.shape), _full(p['g2'].shape)]
    if v_first is not None:
        v1 = pad(p['v1'], B_WIDTH, LANES)
        v2 = pad(p['v2'], LANES, B_WIDTH)
        args += [v1, v2, v_first]
        specs += [_full(v1.shape), _full(v2.shape), row(B_WIDTH)]
    f32 = lambda *shape: jax.ShapeDtypeStruct(shape, _F32)
    bf16 = lambda *shape: jax.ShapeDtypeStruct(shape, _BF16)
    out_shape = [bf16(n, A_WIDTH), bf16(n, 2 * LANES), bf16(n, 2 * LANES),
                 f32(n, B_WIDTH), f32(n, B_WIDTH), f32(n, B_WIDTH),
                 f32(2, n, B_WIDTH), f32(2, n, B_WIDTH), f32(2, n, B_WIDTH), f32(n, B_WIDTH)]
    out_specs = [row(A_WIDTH), row(2 * LANES), row(2 * LANES), row(B_WIDTH), row(B_WIDTH), row(B_WIDTH),
                 both, both, both, row(B_WIDTH)]
    return pl.pallas_call(
        functools.partial(_prep_kernel, has_v_first=v_first is not None),
        grid=(n // tm,), in_specs=specs, out_specs=out_specs, out_shape=out_shape,
        name="prepare_even", compiler_params=_params(1),
    )(*args)


def _split_heads(q2):
    lo = _lane_lo(q2.shape)
    zero = jnp.zeros_like(q2)
    return jnp.concatenate([jnp.where(lo, q2, zero), jnp.where(lo, zero, q2)], axis=0)


def _scores(qs, k):
    return lax.dot_general(qs, k, _NT, preferred_element_type=_F32)


def _interleave(chains):
    results = {}
    while len(results) < len(chains):
        for p, chain in enumerate(chains):
            try:
                next(chain)
            except StopIteration as stop:
                results[p] = stop.value
    return [results[p] for p in range(len(chains))]


def _softmax_pv(s, v):
    p = jnp.exp2(s - jnp.max(s, axis=-1, keepdims=True))
    l = jnp.sum(p, axis=-1, keepdims=True)
    p = p.astype(_BF16)
    yield
    return jnp.dot(p, v, preferred_element_type=_F32) / l


def _gqa_tile(q, k, v):
    tq = q.shape[0]
    s = _scores(_split_heads(q), k)
    yield
    o = yield from _softmax_pv(s, v)
    return jnp.where(_lane_lo((tq, LANES)), o[:tq], o[tq:])


def _attend_chains(make_chain, q_ref, o_ref):
    pieces = [(slice(i * Q_TILE, (i + 1) * Q_TILE), j, slice(j * LANES, (j + 1) * LANES))
              for i in range(q_ref.shape[1] // Q_TILE) for j in range(q_ref.shape[2] // LANES)]
    outs = _interleave([make_chain(q_ref[0, rows, lanes], j) for rows, j, lanes in pieces])
    for (rows, _, lanes), o in zip(pieces, outs):
        o_ref[0, rows, lanes] = o.astype(o_ref.dtype)


def _gqa_kernel(q_ref, k_ref, v_ref, o_ref):
    is_ctx = pl.program_id(2) < CTX_LEN // q_ref.shape[1]

    def run(tk):
        _attend_chains(lambda q, slab: _gqa_tile(q, k_ref[0, :tk], v_ref[0, :tk]), q_ref, o_ref)

    pl.when(is_ctx)(lambda: run(CTX_LEN))
    pl.when(jnp.logical_not(is_ctx))(lambda: run(k_ref.shape[1]))


def _gqa_attend(q, kk, vv):
    b, t, _ = q.shape
    tq = 2 * Q_TILE
    kv_spec = pl.BlockSpec((1, t, LANES), lambda i, g, j: (i, 0, g))
    return pl.pallas_call(
        _gqa_kernel,
        grid=(b, A_KV_HEADS, t // tq),
        in_specs=[pl.BlockSpec((1, tq, 2 * LANES), lambda i, g, j: (i, j, g)), kv_spec, kv_spec],
        out_specs=pl.BlockSpec((1, tq, 2 * LANES), lambda i, g, j: (i, j, g)),
        out_shape=jax.ShapeDtypeStruct((b, t, A_WIDTH), _BF16),
        name="gqa_attn", compiler_params=_params(3),
    )(q, kk, vv)


def _diff_tile(lam, q, k, v, gain):
    tq = q.shape[0]
    s = _scores(_split_heads(q), k)
    yield
    o = yield from _softmax_pv(s, v)
    o = o[:tq] - lam * o[tq:]
    y = o * lax.rsqrt(jnp.mean(o * o, axis=-1, keepdims=True) + SUBLN_EPS)
    return y * gain


def _diff_kernel(lam_ref, q_ref, k_ref, v_ref, g_ref, o_ref):
    is_ctx = pl.program_id(2) < CTX_LEN // q_ref.shape[1]
    lam, gain = lam_ref[0, 0], g_ref[...]

    def run(tk):
        def chain(q, slab):
            lanes = slice(slab * LANES, (slab + 1) * LANES)
            return _diff_tile(lam, q, k_ref[0, :tk, lanes], v_ref[0, :tk, lanes], gain)
        _attend_chains(chain, q_ref, o_ref)

    pl.when(is_ctx)(lambda: run(CTX_LEN))
    pl.when(jnp.logical_not(is_ctx))(lambda: run(k_ref.shape[1]))


def _diff_attend(lam, qkv, gain):
    b, t, _ = qkv.shape
    tq = 2 * Q_TILE
    pairs = C_HEADS // 2
    return pl.pallas_call(
        _diff_kernel,
        grid=(b, pairs, t // tq),
        in_specs=[pl.BlockSpec(memory_space=pltpu.SMEM),
                  pl.BlockSpec((1, tq, 2 * LANES), lambda i, h, j: (i, j, h)),
                  pl.BlockSpec((1, t, 2 * LANES), lambda i, h, j: (i, 0, pairs + h)),
                  pl.BlockSpec((1, t, 2 * LANES), lambda i, h, j: (i, 0, 2 * pairs + h)),
                  _full((1, LANES))],
        out_specs=pl.BlockSpec((1, tq, 2 * LANES), lambda i, h, j: (i, j, h)),
        out_shape=jax.ShapeDtypeStruct((b, t, C_HEADS * LANES), _BF16),
        name="diff_attn", compiler_params=_params(3),
    )(lam, qkv, qkv, qkv, gain)


def _scan_chunk(get_h, r, v, kk, lw, cum, cum_end, a, kd, sgn):
    c = SCAN_CHUNK
    lo = _lane_lo((c, LANES))
    row = lax.broadcasted_iota(jnp.int32, (LANES, LANES), 0)
    col = lax.broadcasted_iota(jnp.int32, (LANES, LANES), 1)
    delta = ((col & (c - 1)) - (row & (c - 1))) * sgn
    before = delta < 0
    upto = delta <= 0
    eye = row == col
    zero = jnp.zeros((LANES, LANES), _F32)

    def blockdiag(x):
        z = jnp.zeros_like(x)
        return jnp.concatenate([jnp.where(lo, x, z), jnp.where(lo, z, x)], axis=0)

    p_inv = jnp.exp(-cum)
    p_end = jnp.exp(cum_end - cum)
    kka = kk * a
    a_rs = blockdiag(-kk * jnp.exp(cum - lw))
    r_rs = blockdiag(r * jnp.exp(cum))
    b_rs = blockdiag(kka * p_inv)
    k_rs = blockdiag(kd * p_inv)
    v_rs = blockdiag(v)
    bp_rs = blockdiag(kka * p_end)
    kp_rs = blockdiag(kd * p_end)

    z = _dot3(_split2(jnp.concatenate([a_rs, r_rs], axis=0)),
              _split2(jnp.concatenate([b_rs, k_rs], axis=0)), _NT)
    yield
    low = jnp.where(before, z[:LANES, :LANES], zero)
    g = jnp.where(before, z[:LANES, LANES:], zero)
    rb = jnp.where(upto, z[LANES:, :LANES], zero)
    rk = jnp.where(upto, z[LANES:, LANES:], zero)

    x = _dot1(low, low)
    yield
    m = jnp.where(eye, 1.0, 0.0).astype(_F32) + low
    n = 2
    while 2 * n < c:
        res = _dot1(x, jnp.concatenate([m, x], axis=1))
        m = m + res[:, :LANES]
        x = res[:, LANES:]
        n *= 2
        yield
    m = m + _dot1(x, m)
    yield
    gv = _dot1(g, v_rs)
    yield
    au = _dot1(m, jnp.concatenate([a_rs, gv], axis=1))
    yield
    rhs = _split2(jnp.concatenate(
        [au, jnp.concatenate([zero, v_rs], axis=1)], axis=0))
    tf = lax.dot_general(jnp.concatenate([bp_rs, kp_rs], axis=0).astype(_BF16), rhs[0], _TN,
                         preferred_element_type=_F32)
    qy = _dot3(_split2(jnp.concatenate([rb, rk], axis=1)), rhs)
    yield
    t = jnp.where(eye, jnp.broadcast_to(jnp.exp(cum_end), (LANES, LANES)), zero) + tf[:, :LANES]
    q = r_rs + qy[:, :LANES]
    out = _dot1(jnp.concatenate([q, t], axis=0), get_h())
    y = out[:LANES] + qy[:, LANES:]
    h_new = out[LANES:] + tf[:, LANES:]
    return y[:c] + y[c:], h_new


def _recording(chain, key, store):
    store[key] = yield from chain
    return store[key]


def _scan_kernel(*refs):
    c = SCAN_CHUNK
    ins, y_refs, h_ref = refs[:12], refs[12:14], refs[14]

    @pl.when(pl.program_id(1) == 0)
    def _():
        h_ref[...] = jnp.zeros_like(h_ref)

    tr = lax.broadcasted_iota(jnp.int32, (c, c), 0)
    tc = lax.broadcasted_iota(jnp.int32, (c, c), 1)
    chains, done = [], {}
    for step in range(SCAN_STEP_CHUNKS):
        for d, sgn in enumerate((1, -1)):
            r_ref, v_ref, kk_ref, lw_ref, a_ref, kd_ref = ins[6 * d:6 * d + 6]
            sub = step if d == 0 else SCAN_STEP_CHUNKS - 1 - step
            rows = slice(sub * c, (sub + 1) * c)
            tri = jnp.where((tc - tr) * sgn <= 0, 1.0, 0.0).astype(_BF16)
            lw = lw_ref[0, 0, rows]
            l1 = lw.astype(_BF16)
            rem = lw - l1.astype(_F32)
            l2 = rem.astype(_BF16)
            l3 = (rem - l2.astype(_F32)).astype(_BF16)
            dot = lambda w, tri=tri: jnp.dot(tri, w, preferred_element_type=_F32)
            cum = (dot(l3) + dot(l2)) + dot(l1)
            cum_end = jnp.sum(lw, axis=0, keepdims=True)
            pairs = lw.shape[1] // LANES
            for p in range(pairs):
                sl = slice(p * LANES, (p + 1) * LANES)
                slot = d * pairs + p
                if step == 0:
                    get_h = lambda slot=slot: h_ref[slot]
                else:
                    get_h = lambda key=(step - 1, slot): done[key][1]
                chain = _scan_chunk(get_h, r_ref[0, rows, sl], v_ref[0, rows, sl], kk_ref[0, rows, sl],
                                    lw[:, sl], cum[:, sl], cum_end[:, sl],
                                    a_ref[0, 0, rows, sl], kd_ref[0, 0, rows, sl], sgn)
                chains.append(_recording(chain, (step, slot), done))
    _interleave(chains)
    for (step, slot), (y, h) in done.items():
        d, p = divmod(slot, h_ref.shape[0] // 2)
        sub = step if d == 0 else SCAN_STEP_CHUNKS - 1 - step
        y_refs[d][0, sub * c:(sub + 1) * c, p * LANES:(p + 1) * LANES] = y
        if step == SCAN_STEP_CHUNKS - 1:
            h_ref[slot] = h


def _delta_scan(r, v, kk, lw, a, kd):
    b, t, w = r.shape
    rows = SCAN_CHUNK * SCAN_STEP_CHUNKS
    n_blocks = t // rows
    n_ctx = CTX_LEN // rows

    def block(d, j):
        return j if d == 0 else jnp.where(j < n_ctx, n_ctx - 1 - j, n_blocks + n_ctx - 1 - j)

    specs, args = [], []
    for d in range(2):
        shared = pl.BlockSpec((1, rows, w), lambda i, j, d=d: (i, block(d, j), 0))
        per_dir = pl.BlockSpec((1, 1, rows, w), lambda i, j, d=d: (d, i, block(d, j), 0))
        specs += [shared, shared, shared, per_dir, per_dir, per_dir]
        args += [r, v, kk, lw, a, kd]
    out_specs = [pl.BlockSpec((1, rows, w), lambda i, j, d=d: (i, block(d, j), 0)) for d in range(2)]
    return pl.pallas_call(
        _scan_kernel,
        grid=(b, n_blocks),
        in_specs=specs,
        out_specs=out_specs,
        out_shape=[jax.ShapeDtypeStruct((b, t, w), _F32)] * 2,
        scratch_shapes=[pltpu.VMEM((2 * w // LANES, LANES, LANES), _F32)],
        name="delta_scan", compiler_params=_params(2),
    )(*args)


def _post_kernel(*refs, even):
    it = iter(refs)
    if even:
        yf_ref, yb_ref, r_ref, v_ref, kd_ref, g_ref, gn_ref = (next(it) for _ in range(7))
    att_ref, x_ref, m_ref, wo_ref, wrh_ref, wrl_ref, br_ref = (next(it) for _ in range(7))
    xo_ref, h_ref, route_ref, count_ref, carry_ref = it
    if even:
        mean_mat = _head_sum_matrix(1.0 / HEAD_DIM)
        ones_mat = _head_sum_matrix(1.0)
        parts = [att_ref[...]]
        for s in range(B_WIDTH // LANES):
            sl = slice(s * LANES, (s + 1) * LANES)
            y = yf_ref[:, sl] + yb_ref[:, sl]
            dev = y - _head_sum(y, mean_mat)
            var = _head_sum(dev * dev, mean_mat)
            y = dev * lax.rsqrt(var + RWKV_GN_EPS) * gn_ref[0:1, sl] + gn_ref[1:2, sl]
            r, v = r_ref[:, sl], v_ref[:, sl]
            for dr in range(2):
                y = y + _head_sum(r * kd_ref[dr, :, sl] * gn_ref[2:3, sl], ones_mat) * v
            parts.append((y * g_ref[:, sl]).astype(_BF16))
        mixed = jnp.concatenate(parts, axis=1)
    else:
        mixed = att_ref[...]
    out = jnp.dot(mixed, wo_ref[...], preferred_element_type=_F32)
    x = x_ref[...] + m_ref[0, 2:3, :] * out
    xo_ref[...] = x
    h = _rms_rows(x) * (1.0 + m_ref[0, 4:5, :]) + m_ref[0, 3:4, :]
    h_ref[...] = _pack_bf16_pairs(h)
    logits = _dot3(_split2(h), (wrh_ref[...], wrl_ref[...])) + br_ref[...]
    _route(logits, route_ref, count_ref, carry_ref)


def _route(logits, route_ref, count_ref, carry_ref):
    tm = logits.shape[0]
    lane = lax.broadcasted_iota(jnp.int32, (tm, LANES), 1)
    neg = -1e30
    first_lane = lambda hit: jnp.min(jnp.where(hit, lane, LANES), axis=-1, keepdims=True)

    is_grp = lane < N_GROUPS
    grp = jnp.where(is_grp, logits, neg)
    g_max = jnp.max(grp, axis=-1, keepdims=True)
    g_sel = first_lane(grp == g_max)
    grp_w = 1.0 / jnp.sum(jnp.where(is_grp, jnp.exp(grp - g_max), 0.0), axis=-1, keepdims=True)

    base = N_GROUPS + EXPERTS_PER_GROUP * g_sel
    in_grp = (lane >= base) & (lane < base + EXPERTS_PER_GROUP)
    ex = jnp.where(in_grp, logits, neg)
    e_max = jnp.max(ex, axis=-1, keepdims=True)
    pe = jnp.where(in_grp, jnp.exp(ex - e_max), 0.0)
    prob = pe / jnp.sum(pe, axis=-1, keepdims=True)
    p1 = jnp.max(prob, axis=-1, keepdims=True)
    l1 = first_lane(in_grp & (prob == p1))
    rest = jnp.where(in_grp & (lane != l1), prob, -1.0)
    p2 = jnp.max(rest, axis=-1, keepdims=True)
    l2 = first_lane(rest == p2)
    scale = grp_w / (p1 + p2)
    e1, e2 = l1 - N_GROUPS, l2 - N_GROUPS

    @pl.when(pl.program_id(0) == 0)
    def _():
        carry_ref[...] = jnp.zeros_like(carry_ref)

    oh1 = jnp.where(lane == e1, 1.0, 0.0)
    oh2 = jnp.where(lane == e2, 1.0, 0.0)
    rr = lax.broadcasted_iota(jnp.int32, (tm, tm), 0)
    cc = lax.broadcasted_iota(jnp.int32, (tm, tm), 1)
    tri = jnp.where(cc < rr, 1.0, 0.0).astype(_BF16)
    before = lambda oh: jnp.dot(tri, oh.astype(_BF16), preferred_element_type=_F32)
    carry = carry_ref[...]
    tot1 = jnp.sum(oh1, axis=0, keepdims=True)
    rank1 = jnp.sum(oh1 * (before(oh1) + carry), axis=-1, keepdims=True)
    rank2 = jnp.sum(oh2 * (before(oh2) + (carry + tot1)), axis=-1, keepdims=True)
    carry = carry + tot1 + jnp.sum(oh2, axis=0, keepdims=True)
    carry_ref[...] = carry
    count_ref[...] = carry

    out = jnp.zeros((tm, LANES), _F32)
    for i, val in enumerate((p1 * scale, p2 * scale, e1.astype(_F32), e2.astype(_F32), rank1, rank2)):
        out = jnp.where(lane == i, val, out)
    route_ref[...] = out


def _mixer_out(att, x, mods, w_out, w_router, b_router, rwkv=None):
    n, d = x.shape
    tm = ROW_TILE
    row = lambda width: pl.BlockSpec((tm, width), lambda i: (i, 0))
    both = pl.BlockSpec((2, tm, B_WIDTH), lambda i: (0, i, 0))
    args, specs = [], []
    if rwkv is not None:
        y_fwd, y_bwd, r, v, kd, g, gn = rwkv
        args += [y_fwd, y_bwd, r, v, kd, g, gn]
        specs += [row(B_WIDTH)] * 4 + [both, row(B_WIDTH), _full(gn.shape)]
    wr_hi, wr_lo = _split2(w_router)
    args += [att, x, mods, w_out, wr_hi, wr_lo, b_router]
    specs += [row(att.shape[1]), row(d), pl.BlockSpec((1, 6, d), lambda i: (_mods_index(i), 0, 0)),
              _full(w_out.shape), _full(wr_hi.shape), _full(wr_lo.shape), _full(b_router.shape)]
    return pl.pallas_call(
        functools.partial(_post_kernel, even=rwkv is not None),
        grid=(n // tm,), in_specs=specs,
        out_specs=[row(d), row(d // 2), row(LANES), _full((1, LANES))],
        out_shape=[jax.ShapeDtypeStruct((n, d), _F32), jax.ShapeDtypeStruct((n, d // 2), _F32),
                   jax.ShapeDtypeStruct((n, LANES), _F32), jax.ShapeDtypeStruct((1, LANES), _F32)],
        scratch_shapes=[pltpu.VMEM((1, LANES), _F32)],
        name="mixer_out_even" if rwkv is not None else "mixer_out_odd", compiler_params=_params(1),
    )(*args)


def _moe_kernel(be_ref, nv_ref, x_ref, wg_ref, wu_ref, wd_ref, o_ref, wg_s, wu_s, wd_s):
    i = pl.program_id(0)
    valid = i < nv_ref[0]
    new_expert = (i == 0) | (be_ref[i] != be_ref[jnp.maximum(i - 1, 0)])

    @pl.when(valid & new_expert)
    def _():
        wg_s[...] = wg_ref[0, 0].astype(_BF16)
        wu_s[...] = wu_ref[0, 0].astype(_BF16)
        wd_s[...] = wd_ref[0, 0].astype(_BF16)

    @pl.when(valid)
    def _():
        x = _unpack_bf16_pairs(x_ref[...])
        gate = jnp.dot(x, wg_s[...], preferred_element_type=_F32)
        up = jnp.dot(x, wu_s[...], preferred_element_type=_F32)
        hid = (gate * jax.nn.sigmoid(gate) * up).astype(_BF16)
        o_ref[...] = jnp.dot(hid, wd_s[...], preferred_element_type=_F32)

    @pl.when(jnp.logical_not(valid))
    def _():
        o_ref[...] = jnp.zeros_like(o_ref)


def _moe_blocks(block_expert, n_valid, xs, wg, wu, wd, layer):
    rows, d = xs.shape[0], 2 * xs.shape[1]
    nb = rows // MOE_ROWS
    w_spec = lambda a, b: pl.BlockSpec((1, 1, a, b), lambda i, be, nv: (layer, be[i], 0, 0))
    grid_spec = pltpu.PrefetchScalarGridSpec(
        num_scalar_prefetch=2,
        grid=(nb,),
        in_specs=[pl.BlockSpec((MOE_ROWS, d // 2), lambda i, be, nv: (i, 0)),
                  w_spec(d, EXPERT_HIDDEN), w_spec(d, EXPERT_HIDDEN), w_spec(EXPERT_HIDDEN, d)],
        out_specs=pl.BlockSpec((MOE_ROWS, d), lambda i, be, nv: (i, 0)),
        scratch_shapes=[pltpu.VMEM((d, EXPERT_HIDDEN), _BF16), pltpu.VMEM((d, EXPERT_HIDDEN), _BF16),
                        pltpu.VMEM((EXPERT_HIDDEN, d), _BF16)],
    )
    return pl.pallas_call(
        _moe_kernel,
        grid_spec=grid_spec,
        out_shape=jax.ShapeDtypeStruct((rows, d), _F32),
        name="moe_ffn", compiler_params=_params(1),
    )(block_expert, n_valid, xs, wg, wu, wd)


def _moe(h, route, counts, w_gate, w_up, w_down, layer):
    n = h.shape[0]
    eid = route[:, 2:4].astype(jnp.int32)
    rank = route[:, 4:6].astype(jnp.int32)
    cnt = counts[0, :N_EXPERTS].astype(jnp.int32)
    padded = (cnt + MOE_ROWS - 1) // MOE_ROWS * MOE_ROWS
    pad_end = jnp.cumsum(padded)
    experts = jnp.arange(N_EXPERTS, dtype=jnp.int32)
    pad_start = jnp.sum(jnp.where(eid[:, :, None] == experts, pad_end - padded, 0), axis=-1)
    slot = pad_start + rank
    n_blocks = -(-(n * TOP_K + N_EXPERTS * (MOE_ROWS - 1)) // MOE_ROWS)
    block_start = jnp.arange(n_blocks, dtype=jnp.int32) * MOE_ROWS
    block_expert = jnp.minimum(jnp.sum(pad_end[None, :] <= block_start[:, None], axis=1),
                               N_EXPERTS - 1).astype(jnp.int32)
    n_valid = (pad_end[-1:] // MOE_ROWS).astype(jnp.int32)
    tok = jnp.broadcast_to(jnp.arange(n, dtype=jnp.int32)[:, None], (n, TOP_K))
    _, tok_by_slot = lax.sort_key_val(slot.reshape(-1), tok.reshape(-1))
    run_start = (jnp.cumsum(cnt) - cnt)[block_expert]
    run_pos = block_start - (pad_end - padded)[block_expert]
    pos = (run_pos[:, None] + jnp.arange(MOE_ROWS, dtype=jnp.int32)).reshape(-1)
    occupied = pos < jnp.repeat(cnt[block_expert], MOE_ROWS)
    src = jnp.minimum(jnp.repeat(run_start, MOE_ROWS) + pos, n * TOP_K - 1)
    slot_tok = jnp.where(occupied, tok_by_slot[src], 0)
    yb = _moe_blocks(block_expert, n_valid, h[slot_tok], w_gate, w_up, w_down, layer)
    return yb[slot[:, 0]], yb[slot[:, 1]]


def _final_kernel(x_ref, y1_ref, y2_ref, g_ref, m_ref, gain_ref, o_ref):
    g = g_ref[...]
    x = x_ref[...] + m_ref[0, 5:6, :] * (g[:, 0:1] * y1_ref[...] + g[:, 1:2] * y2_ref[...])
    o_ref[...] = _rms_rows(x) * gain_ref[...]


def _final(x, y1, y2, route, mods, gain, bsz, n_lat):
    d = x.shape[1]
    tm = ROW_TILE
    lat_tiles = n_lat // tm
    tiles = (CTX_LEN + n_lat) // tm
    src = lambda i: (i // lat_tiles) * tiles + CTX_LEN // tm + i % lat_tiles
    row = lambda width: pl.BlockSpec((tm, width), lambda i: (src(i), 0))
    return pl.pallas_call(
        _final_kernel,
        grid=(bsz * lat_tiles,),
        in_specs=[row(d), row(d), row(d), row(LANES),
                  pl.BlockSpec((1, 6, d), lambda i: ((i // lat_tiles) * 2 + 1, 0, 0)), _full((1, d))],
        out_specs=pl.BlockSpec((tm, d), lambda i: (i, 0)),
        out_shape=jax.ShapeDtypeStruct((bsz * n_lat, d), _F32),
        name="final_norm", compiler_params=_params(1),
    )(x, y1, y2, route, mods, gain.reshape(1, d))


def _rope_tables(n_lat):
    nf = HEAD_DIM // 4
    inv = ROPE_THETA ** (-jnp.arange(nf, dtype=_F32) / nf)
    rows = n_lat // GRID_W
    r_ang = jnp.repeat(jnp.arange(rows, dtype=_F32), GRID_W)[:, None] * inv
    c_ang = jnp.tile(jnp.arange(GRID_W, dtype=_F32), rows)[:, None] * inv
    cos = jnp.concatenate([jnp.cos(r_ang)] * 2 + [jnp.cos(c_ang)] * 2, axis=-1)
    sin = jnp.concatenate([-jnp.sin(r_ang), jnp.sin(r_ang), -jnp.sin(c_ang), jnp.sin(c_ang)], axis=-1)
    cos = jnp.concatenate([jnp.ones((CTX_LEN, HEAD_DIM), _F32), cos], axis=0)
    sin = jnp.concatenate([jnp.zeros((CTX_LEN, HEAD_DIM), _F32), sin], axis=0)
    return jnp.tile(cos, (1, 2)), jnp.tile(sin, (1, 2))


def kernel(x, c, ctx, c_ctx, mod_w, mod_b, ev_w_in, ev_w_out, ev_q_gain, ev_k_gain, ev_mu_prev, ev_mu_next, ev_w0, ev_w2, ev_a0, ev_a2, ev_g2, ev_k_k, ev_k_a, ev_r_k, ev_gn_w, ev_gn_b, ev_v0, ev_v1, ev_v2, od_w_in, od_w_out, od_lq1, od_lk1, od_lq2, od_lk2, od_subln, moe_w_grp, moe_b_grp, moe_w_rt, moe_b_rt, moe_w_gate, moe_w_up, moe_w_down, final_gain):
    bsz, n_lat, d = x.shape
    t = CTX_LEN + n_lat
    n = bsz * t
    cos, sin = _rope_tables(n_lat)
    cond = jax.nn.silu(jnp.concatenate([c, c_ctx[None, :], jnp.zeros((16 - bsz - 1, d), _F32)], axis=0))
    xs = jnp.concatenate([ctx, x], axis=1).reshape(n, d)
    v_first = None
    moe = None
    for layer in range(DEPTH):
        m = _mm(cond, mod_w[layer], tn=1536, precision=_HI) + mod_b[layer]
        m = m.reshape(16, 6, d)
        mods = jnp.stack([jnp.broadcast_to(m[bsz], (bsz, 6, d)), m[:bsz]], axis=1).reshape(2 * bsz, 6, d)
        w_router = jnp.concatenate(
            [moe_w_grp[layer], moe_w_rt[layer], jnp.zeros((d, LANES - N_GROUPS - N_EXPERTS), _F32)], axis=1)
        b_router = jnp.concatenate(
            [moe_b_grp[layer], moe_b_rt[layer], jnp.zeros((LANES - N_GROUPS - N_EXPERTS,), _F32)])[None, :]
        if layer % 2 == 0:
            e = layer // 2
            p = {'q_gain': ev_q_gain[e], 'k_gain': ev_k_gain[e], 'mu_prev': ev_mu_prev[e],
                 'mu_next': ev_mu_next[e], 'w0': ev_w0[e], 'w2': ev_w2[e], 'a0': ev_a0[e],
                 'a2': ev_a2[e], 'g2': ev_g2[e], 'k_k': ev_k_k[e], 'k_a': ev_k_a[e]}
            if e > 0:
                p['v0'], p['v1'], p['v2'] = ev_v0[e - 1], ev_v1[e - 1], ev_v2[e - 1]
            proj, xs = _project(xs, mods, ev_w_in[e].astype(_BF16), moe=moe)
            q, kk2, vv2, r, v, kkn, lw, a, kd, g = _prepare_even(proj, cos, sin, p, v_first)
            if v_first is None:
                v_first = v
            att = _gqa_attend(q.reshape(bsz, t, -1), kk2.reshape(bsz, t, -1), vv2.reshape(bsz, t, -1))
            seq = lambda z: z.reshape(*z.shape[:-2], bsz, t, B_WIDTH)
            y_fwd, y_bwd = (y.reshape(n, B_WIDTH) for y in
                            _delta_scan(seq(r), seq(v), seq(kkn), seq(lw), seq(a), seq(kd)))
            gn = jnp.stack([ev_gn_w[e], ev_gn_b[e], ev_r_k[e].reshape(B_WIDTH)])
            xs, h, route, counts = _mixer_out(att.reshape(n, -1), xs, mods, ev_w_out[e].astype(_BF16),
                                       w_router, b_router, rwkv=(y_fwd, y_bwd, r, v, kd, g, gn))
        else:
            o = layer // 2
            lambda_init = 0.8 - 0.6 * math.exp(-0.3 * layer)
            qkv, xs = _project(xs, mods, od_w_in[o].astype(_BF16), moe=moe, rope=(cos, sin))
            lam = (jnp.exp(jnp.sum(od_lq1[o] * od_lk1[o])) - jnp.exp(jnp.sum(od_lq2[o] * od_lk2[o]))
                   + lambda_init).reshape(1, 1)
            gain = (od_subln[o] * (1.0 - lambda_init)).reshape(1, LANES)
            att = _diff_attend(lam, qkv.reshape(bsz, t, -1), gain)
            xs, h, route, counts = _mixer_out(att.reshape(n, -1), xs, mods, od_w_out[o].astype(_BF16),
                                       w_router, b_router)
        y1, y2 = _moe(h, route, counts, moe_w_gate, moe_w_up, moe_w_down, layer)
        moe = (y1, y2, route, mods)
    return _final(xs, y1, y2, route, mods, final_gain, bsz, n_lat).reshape(bsz, n_lat, d)
```

```python
import functools
import math

import jax
import jax.numpy as jnp
from jax import lax
from jax.experimental import pallas as pl
from jax.experimental.pallas import tpu as pltpu

D_MODEL = 1024
DEPTH = 4
SEQ = 2048
GRID_W = 64
CTX_LEN = 256
HEAD_DIM = 64
ROPE_THETA = 10000.0
RMS_EPS = 1e-6
SUBLN_EPS = 1e-5
Q_SCALE = HEAD_DIM ** -0.5 * math.log2(math.e)

A_HEADS = 8
A_KV_HEADS = 2
A_WIDTH = 512
A_KV_WIDTH = 128
A_COLS = 768

B_HEADS = 8
B_WIDTH = 512
B_COLS = 1792
RWKV_W_LORA = 64
RWKV_A_LORA = 64
RWKV_V_LORA = 32
RWKV_G_LORA = 128
RWKV_GN_EPS = 64e-5
DECAY_SCALE = math.exp(-0.5)

C_HEADS = 8
C_QK = 1024

N_GROUPS = 4
EXPERTS_PER_GROUP = 8
N_EXPERTS = 32
TOP_K = 2
EXPERT_HIDDEN = 512
MOE_ROWS = 256

LANES = 128
SCAN_CHUNK = 64
SCAN_STEP_CHUNKS = 2
ROW_TILE = 256
Q_TILE = 128
VMEM_LIMIT = 56 * 1024 * 1024

_HI = lax.Precision.HIGHEST
_F32 = jnp.float32
_BF16 = jnp.bfloat16

_NN = (((1,), (0,)), ((), ()))
_NT = (((1,), (1,)), ((), ()))
_TN = (((0,), (0,)), ((), ()))


def _params(n_axes):
    return pltpu.CompilerParams(dimension_semantics=("arbitrary",) * n_axes,
                                vmem_limit_bytes=VMEM_LIMIT)


def _full(shape):
    return pl.BlockSpec(shape, lambda *_: (0,) * len(shape))


def _split2(x):
    hi = x.astype(_BF16)
    return hi, (x - hi.astype(_F32)).astype(_BF16)


def _dot3(a, b, dims=_NN):
    d = lambda u, w: lax.dot_general(u, w, dims, preferred_element_type=_F32)
    return (d(a[1], b[0]) + d(a[0], b[1])) + d(a[0], b[0])


def _dot1(a, b, dims=_NN):
    return lax.dot_general(a.astype(_BF16), b.astype(_BF16), dims, preferred_element_type=_F32)


def _pack_bf16_pairs(x):
    w = x.shape[1] // 2
    bits = pltpu.bitcast(x.astype(_BF16).astype(_F32), jnp.uint32)
    return pltpu.bitcast((bits[:, :w] >> 16) | (bits[:, w:] & jnp.uint32(0xFFFF0000)), _F32)


def _unpack_bf16_pairs(words):
    packed = pltpu.bitcast(words, jnp.uint32)
    lo = pltpu.bitcast(packed << 16, _F32)
    hi = pltpu.bitcast(packed & jnp.uint32(0xFFFF0000), _F32)
    return jnp.concatenate([lo, hi], axis=1).astype(_BF16)


def _lane_lo(shape):
    return lax.broadcasted_iota(jnp.int32, shape, len(shape) - 1) < HEAD_DIM


def _head_sum_matrix(value):
    row = lax.broadcasted_iota(jnp.int32, (LANES, LANES), 0)
    col = lax.broadcasted_iota(jnp.int32, (LANES, LANES), 1)
    return jnp.where((row // HEAD_DIM) == (col // HEAD_DIM), value, 0.0).astype(_BF16)


def _head_sum(x, mat):
    hi, lo = _split2(x)
    return jnp.dot(lo, mat, preferred_element_type=_F32) + jnp.dot(hi, mat, preferred_element_type=_F32)


def _rms_rows(x, eps=RMS_EPS):
    return x * lax.rsqrt(jnp.mean(x * x, axis=-1, keepdims=True) + eps)


def _rope_slab(x, cos, sin):
    lane = lax.broadcasted_iota(jnp.int32, x.shape, 1)
    partner = jnp.where((lane & 16) == 0, pltpu.roll(x, LANES - 16, axis=1), pltpu.roll(x, 16, axis=1))
    return x * cos + partner * sin


def _mods_index(i):
    tiles = (CTX_LEN + SEQ) // ROW_TILE
    return (i // tiles) * 2 + jnp.minimum(i % tiles, 1)


def _mm_kernel(a_ref, w_ref, o_ref, *, precision):
    o_ref[...] = jnp.dot(a_ref[...], w_ref[...], precision=precision,
                         preferred_element_type=_F32).astype(o_ref.dtype)


def _mm(a, w, *, tn, precision=None):
    m, k = a.shape
    n = w.shape[1]
    return pl.pallas_call(
        functools.partial(_mm_kernel, precision=precision),
        grid=(n // tn,),
        in_specs=[pl.BlockSpec((m, k), lambda j: (0, 0)),
                  pl.BlockSpec((k, tn), lambda j: (0, j))],
        out_specs=pl.BlockSpec((m, tn), lambda j: (0, j)),
        out_shape=jax.ShapeDtypeStruct((m, n), _F32),
        name=f"mm_{k}x{n}",
        compiler_params=_params(1),
    )(a, w)


def _proj_kernel(*refs, combine, rope):
    it = iter(refs)
    x_ref, m_ref = next(it), next(it)
    if combine:
        y1_ref, y2_ref, g_ref, mp_ref = (next(it) for _ in range(4))
    w_ref = next(it)
    if rope:
        cos_ref, sin_ref = next(it), next(it)
    o_ref = next(it)
    x = x_ref[...]
    if combine:
        xo_ref = next(it)
        g = g_ref[...]
        x = x + mp_ref[0, 5:6, :] * (g[:, 0:1] * y1_ref[...] + g[:, 1:2] * y2_ref[...])
        xo_ref[...] = x
    h = _rms_rows(x) * (1.0 + m_ref[0, 1:2, :]) + m_ref[0, 0:1, :]
    p = jnp.dot(h.astype(_BF16), w_ref[...], preferred_element_type=_F32)
    if rope:
        cos, sin = cos_ref[...], sin_ref[...]
        for s in range(2 * C_QK // LANES):
            sl = slice(s * LANES, (s + 1) * LANES)
            slab = _rope_slab(p[:, sl], cos, sin)
            if s < C_QK // LANES:
                slab = slab * Q_SCALE
            o_ref[:, sl] = slab.astype(o_ref.dtype)
        ones = jnp.ones((p.shape[0], LANES), o_ref.dtype)
        for h in range(C_HEADS):
            base = 2 * C_QK + 2 * h * LANES
            o_ref[:, base:base + LANES] = p[:, 2 * C_QK + h * LANES:2 * C_QK + (h + 1) * LANES].astype(o_ref.dtype)
            o_ref[:, base + LANES:base + 2 * LANES] = ones
    else:
        o_ref[...] = p


def _project(x, mods, w, *, moe=None, rope=None):
    n, d = x.shape
    cols = w.shape[1]
    tm = ROW_TILE
    tiles = (CTX_LEN + SEQ) // tm
    row = lambda width: pl.BlockSpec((tm, width), lambda i: (i, 0))
    mod_spec = pl.BlockSpec((1, 6, d), lambda i: (_mods_index(i), 0, 0))
    args, specs = [x, mods], [row(d), mod_spec]
    if moe is not None:
        args += list(moe)
        specs += [row(d), row(d), row(LANES), mod_spec]
    args.append(w)
    specs.append(_full(w.shape))
    if rope is not None:
        args += list(rope)
        specs += [pl.BlockSpec((tm, LANES), lambda i: (i % tiles, 0))] * 2
    if rope is not None:
        out_cols = cols + C_HEADS * LANES
        out_shape = [jax.ShapeDtypeStruct((n, out_cols), _BF16)]
    else:
        out_cols = cols
        out_shape = [jax.ShapeDtypeStruct((n, cols), _F32)]
    out_specs = [row(out_cols)]
    if moe is not None:
        out_shape.append(jax.ShapeDtypeStruct((n, d), _F32))
        out_specs.append(row(d))
    out = pl.pallas_call(
        functools.partial(_proj_kernel, combine=moe is not None, rope=rope is not None),
        grid=(n // tm,), in_specs=specs, out_specs=out_specs, out_shape=out_shape,
        name=f"project_{cols}", compiler_params=_params(1),
    )(*args)
    return out if moe is not None else (out[0], x)


def _prep_kernel(*refs, has_v_first):
    it = iter(refs)
    p_ref, pp_ref, pn_ref, cos_ref, sin_ref = (next(it) for _ in range(5))
    gain_ref, mu_ref, wa0_ref, kv_ref, wl_ref, g2_ref = (next(it) for _ in range(6))
    if has_v_first:
        v1_ref, v2_ref, vf_ref = next(it), next(it), next(it)
    q_ref, kk2_ref, vv2_ref, r_ref, v_ref, kkn_ref, lw_ref, a_ref, kd_ref, g_ref = it

    tiles = (CTX_LEN + SEQ) // ROW_TILE
    j = pl.program_id(0) % tiles
    p = p_ref[...]
    cos, sin = cos_ref[...], sin_ref[...]
    mean_mat = _head_sum_matrix(1.0 / HEAD_DIM)
    ones_mat = _head_sum_matrix(1.0)
    lo = _lane_lo((ROW_TILE, LANES))

    def normed(x, gain):
        return x * lax.rsqrt(_head_sum(x * x, mean_mat) + RMS_EPS) * gain

    for s in range(A_WIDTH // LANES):
        sl = slice(s * LANES, (s + 1) * LANES)
        q = _rope_slab(normed(p[:, sl], gain_ref[0:1, :]), cos, sin)
        q_ref[:, sl] = (q * Q_SCALE).astype(_BF16)
    k = _rope_slab(normed(p[:, A_WIDTH:A_WIDTH + LANES], gain_ref[1:2, :]), cos, sin)
    v = p[:, A_WIDTH + LANES:A_COLS]
    swapped = pltpu.roll(k, HEAD_DIM, axis=1)
    kk2_ref[:, :LANES] = jnp.where(lo, k, swapped).astype(_BF16)
    kk2_ref[:, LANES:] = jnp.where(lo, swapped, k).astype(_BF16)
    vv2_ref[:, :LANES] = jnp.where(lo, v, 1.0).astype(_BF16)
    vv2_ref[:, LANES:] = jnp.where(lo, pltpu.roll(v, HEAD_DIM, axis=1), 1.0).astype(_BF16)

    u = p[:, A_COLS:]
    rid = lax.broadcasted_iota(jnp.int32, (ROW_TILE, 1), 0)
    has_prev = jnp.where(j >= 2, 1.0, 0.0)
    has_next = jnp.where((j >= 1) & (j < tiles - 1), 1.0, 0.0)
    first = pp_ref[7:8, A_COLS:] * has_prev
    last = pn_ref[0:1, A_COLS:] * has_next
    prev = jnp.where(rid == 0, first, pltpu.roll(u, 1, axis=0))
    nxt = jnp.where(rid == ROW_TILE - 1, last, pltpu.roll(u, ROW_TILE - 1, axis=0))
    u = u + mu_ref[0:1, :] * (prev - u) + mu_ref[1:2, :] * (nxt - u)
    r, k, v = u[:, :B_WIDTH], u[:, B_WIDTH:2 * B_WIDTH], u[:, 2 * B_WIDTH:3 * B_WIDTH]
    wa, glr = u[:, 3 * B_WIDTH:3 * B_WIDTH + LANES], u[:, 3 * B_WIDTH + LANES:]
    if has_v_first:
        mid = jnp.dot(v.astype(_BF16), v1_ref[...], preferred_element_type=_F32)
        lora = jnp.dot(mid.astype(_BF16), v2_ref[...], preferred_element_type=_F32)
        v = v + (vf_ref[...] - v) * jax.nn.sigmoid(kv_ref[2:3, :] + lora)
    r_ref[...] = r
    v_ref[...] = v
    kks = k * kv_ref[0:1, :]
    for s in range(B_WIDTH // LANES):
        sl = slice(s * LANES, (s + 1) * LANES)
        x = kks[:, sl]
        kkn_ref[:, sl] = x * lax.rsqrt(jnp.maximum(_head_sum(x * x, ones_mat), 1e-24))
    xw = jnp.where(lo, jnp.tanh(wa), wa).astype(_BF16)
    for dr in range(2):
        z = jnp.dot(xw, wl_ref[dr], preferred_element_type=_F32) + wa0_ref[dr:dr + 1, :]
        a = jax.nn.sigmoid(z[:, B_WIDTH:])
        lw_ref[dr] = -DECAY_SCALE * jax.nn.sigmoid(z[:, :B_WIDTH])
        a_ref[dr] = a
        kd_ref[dr] = k * (1.0 + (a - 1.0) * kv_ref[1:2, :])
    g_ref[...] = jnp.dot(jax.nn.sigmoid(glr).astype(_BF16), g2_ref[...], preferred_element_type=_F32)


def _prepare_even(proj, cos, sin, p, v_first):
    n, cols = proj.shape
    tm = ROW_TILE
    tiles = (CTX_LEN + SEQ) // tm
    halo = 8
    per_tile = tm // halo
    row = lambda width: pl.BlockSpec((tm, width), lambda i: (i, 0))
    both = pl.BlockSpec((2, tm, B_WIDTH), lambda i: (0, i, 0))
    tab = pl.BlockSpec((tm, LANES), lambda i: (i % tiles, 0))
    pad = lambda w, rows, width: jnp.zeros((rows, width), _BF16).at[:w.shape[0], :w.shape[1]].set(w.astype(_BF16))
    gains = jnp.stack([jnp.tile(p['q_gain'], 2), jnp.tile(p['k_gain'], 2)])
    mu = jnp.stack([p['mu_prev'], p['mu_next']])
    wa0 = jnp.concatenate([p['w0'], p['a0']], axis=1)
    kvec = jnp.stack([p['k_k'], p['k_a'], p.get('v0', jnp.zeros((B_WIDTH,), _F32))])
    zero = jnp.zeros((RWKV_W_LORA, B_WIDTH), _F32)
    wl = jnp.stack([jnp.concatenate([jnp.concatenate([p['w2'][dr], zero], axis=1),
                                     jnp.concatenate([zero, p['a2'][dr]], axis=1)], axis=0)
                    for dr in range(2)]).astype(_BF16)
    args = [proj, proj, proj, cos, sin, gains, mu, wa0, kvec, wl, p['g2'].astype(_BF16)]
    specs = [row(cols),
             pl.BlockSpec((halo, cols), lambda i: (jnp.maximum(i * per_tile - 1, 0), 0)),
             pl.BlockSpec((halo, cols), lambda i: (jnp.minimum((i + 1) * per_tile, n // halo - 1), 0)),
             tab, tab, _full(gains.shape), _full(mu.shape), _full(wa0.shape), _full(kvec.shape),
             _full(wl.shape), _full(p['g2'].shape)]
    if v_first is not None:
        v1 = pad(p['v1'], B_WIDTH, LANES)
        v2 = pad(p['v2'], LANES, B_WIDTH)
        args += [v1, v2, v_first]
        specs += [_full(v1.shape), _full(v2.shape), row(B_WIDTH)]
    f32 = lambda *shape: jax.ShapeDtypeStruct(shape, _F32)
    bf16 = lambda *shape: jax.ShapeDtypeStruct(shape, _BF16)
    out_shape = [bf16(n, A_WIDTH), bf16(n, 2 * LANES), bf16(n, 2 * LANES),
                 f32(n, B_WIDTH), f32(n, B_WIDTH), f32(n, B_WIDTH),
                 f32(2, n, B_WIDTH), f32(2, n, B_WIDTH), f32(2, n, B_WIDTH), f32(n, B_WIDTH)]
    out_specs = [row(A_WIDTH), row(2 * LANES), row(2 * LANES), row(B_WIDTH), row(B_WIDTH), row(B_WIDTH),
                 both, both, both, row(B_WIDTH)]
    return pl.pallas_call(
        functools.partial(_prep_kernel, has_v_first=v_first is not None),
        grid=(n // tm,), in_specs=specs, out_specs=out_specs, out_shape=out_shape,
        name="prepare_even", compiler_params=_params(1),
    )(*args)


def _split_heads(q2):
    lo = _lane_lo(q2.shape)
    zero = jnp.zeros_like(q2)
    return jnp.concatenate([jnp.where(lo, q2, zero), jnp.where(lo, zero, q2)], axis=0)


def _scores(qs, k):
    return lax.dot_general(qs, k, _NT, preferred_element_type=_F32)


def _interleave(chains):
    results = {}
    while len(results) < len(chains):
        for p, chain in enumerate(chains):
            try:
                next(chain)
            except StopIteration as stop:
                results[p] = stop.value
    return [results[p] for p in range(len(chains))]


def _softmax_pv(s, v_ones):
    p = jnp.exp2(s - jnp.max(s, axis=-1, keepdims=True)).astype(_BF16)
    yield
    return jnp.dot(p, v_ones, preferred_element_type=_F32)


def _gqa_tile(q, k, v):
    tq = q.shape[0]
    s = _scores(_split_heads(q), k)
    yield
    o = yield from _softmax_pv(s, v)
    o = o / pltpu.roll(o, HEAD_DIM, axis=1)
    return jnp.where(_lane_lo((tq, LANES)), o[:tq], pltpu.roll(o[tq:], HEAD_DIM, axis=1))


def _attend_chains(make_chain, q_ref, o_ref):
    pieces = [(slice(i * Q_TILE, (i + 1) * Q_TILE), j, slice(j * LANES, (j + 1) * LANES))
              for i in range(q_ref.shape[1] // Q_TILE) for j in range(q_ref.shape[2] // LANES)]
    outs = _interleave([make_chain(q_ref[0, rows, lanes], j) for rows, j, lanes in pieces])
    for (rows, _, lanes), o in zip(pieces, outs):
        o_ref[0, rows, lanes] = o.astype(o_ref.dtype)


def _gqa_kernel(q_ref, k_ref, v_ref, o_ref):
    is_ctx = pl.program_id(2) < CTX_LEN // q_ref.shape[1]

    def run(tk):
        _attend_chains(lambda q, slab: _gqa_tile(q, k_ref[0, :tk], v_ref[0, :tk]), q_ref, o_ref)

    pl.when(is_ctx)(lambda: run(CTX_LEN))
    pl.when(jnp.logical_not(is_ctx))(lambda: run(k_ref.shape[1]))


def _gqa_attend(q, kk, vv):
    b, t, _ = q.shape
    tq = 2 * Q_TILE
    kv_spec = pl.BlockSpec((1, t, LANES), lambda i, g, j: (i, 0, g))
    return pl.pallas_call(
        _gqa_kernel,
        grid=(b, A_KV_HEADS, t // tq),
        in_specs=[pl.BlockSpec((1, tq, 2 * LANES), lambda i, g, j: (i, j, g)), kv_spec, kv_spec],
        out_specs=pl.BlockSpec((1, tq, 2 * LANES), lambda i, g, j: (i, j, g)),
        out_shape=jax.ShapeDtypeStruct((b, t, A_WIDTH), _BF16),
        name="gqa_attn", compiler_params=_params(3),
    )(q, kk, vv)


def _diff_tile(lam, q, k, v, gain):
    tq = q.shape[0]
    s = _scores(_split_heads(q), k)
    yield
    o = yield from _softmax_pv(s, v)
    o = o[:, :LANES] / o[:, LANES:]
    o = o[:tq] - lam * o[tq:]
    y = o * lax.rsqrt(jnp.mean(o * o, axis=-1, keepdims=True) + SUBLN_EPS)
    return y * gain


def _diff_kernel(lam_ref, q_ref, k_ref, v_ref, g_ref, o_ref):
    is_ctx = pl.program_id(2) < CTX_LEN // q_ref.shape[1]
    lam, gain = lam_ref[0, 0], g_ref[...]

    def run(tk):
        def chain(q, slab):
            lanes = slice(slab * LANES, (slab + 1) * LANES)
            v_lanes = slice(2 * slab * LANES, 2 * (slab + 1) * LANES)
            return _diff_tile(lam, q, k_ref[0, :tk, lanes], v_ref[0, :tk, v_lanes], gain)
        _attend_chains(chain, q_ref, o_ref)

    pl.when(is_ctx)(lambda: run(CTX_LEN))
    pl.when(jnp.logical_not(is_ctx))(lambda: run(k_ref.shape[1]))


def _diff_attend(lam, qkv, gain):
    b, t, _ = qkv.shape
    tq = 2 * Q_TILE
    pairs = C_HEADS // 2
    return pl.pallas_call(
        _diff_kernel,
        grid=(b, pairs, t // tq),
        in_specs=[pl.BlockSpec(memory_space=pltpu.SMEM),
                  pl.BlockSpec((1, tq, 2 * LANES), lambda i, h, j: (i, j, h)),
                  pl.BlockSpec((1, t, 2 * LANES), lambda i, h, j: (i, 0, pairs + h)),
                  pl.BlockSpec((1, t, 4 * LANES), lambda i, h, j: (i, 0, pairs + h)),
                  _full((1, LANES))],
        out_specs=pl.BlockSpec((1, tq, 2 * LANES), lambda i, h, j: (i, j, h)),
        out_shape=jax.ShapeDtypeStruct((b, t, C_HEADS * LANES), _BF16),
        name="diff_attn", compiler_params=_params(3),
    )(lam, qkv, qkv, qkv, gain)


def _scan_chunk(get_h, r, v, kk, lw, cum, cum_end, a, kd, sgn):
    c = SCAN_CHUNK
    lo = _lane_lo((c, LANES))
    row = lax.broadcasted_iota(jnp.int32, (LANES, LANES), 0)
    col = lax.broadcasted_iota(jnp.int32, (LANES, LANES), 1)
    delta = ((col & (c - 1)) - (row & (c - 1))) * sgn
    before = delta < 0
    upto = delta <= 0
    eye = row == col
    zero = jnp.zeros((LANES, LANES), _F32)

    def blockdiag(x):
        z = jnp.zeros_like(x)
        return jnp.concatenate([jnp.where(lo, x, z), jnp.where(lo, z, x)], axis=0)

    p_inv = jnp.exp(-cum)
    p_end = jnp.exp(cum_end - cum)
    kka = kk * a
    a_rs = blockdiag(-kk * jnp.exp(cum - lw))
    r_rs = blockdiag(r * jnp.exp(cum))
    b_rs = blockdiag(kka * p_inv)
    k_rs = blockdiag(kd * p_inv)
    v_rs = blockdiag(v)
    bp_rs = blockdiag(kka * p_end)
    kp_rs = blockdiag(kd * p_end)

    z = _dot3(_split2(jnp.concatenate([a_rs, r_rs], axis=0)),
              _split2(jnp.concatenate([b_rs, k_rs], axis=0)), _NT)
    yield
    low = jnp.where(before, z[:LANES, :LANES], zero)
    g = jnp.where(before, z[:LANES, LANES:], zero)
    rb = jnp.where(upto, z[LANES:, :LANES], zero)
    rk = jnp.where(upto, z[LANES:, LANES:], zero)

    x = _dot1(low, low)
    yield
    m = jnp.where(eye, 1.0, 0.0).astype(_F32) + low
    n = 2
    while 2 * n < c:
        res = _dot1(x, jnp.concatenate([m, x], axis=1))
        m = m + res[:, :LANES]
        x = res[:, LANES:]
        n *= 2
        yield
    m = m + _dot1(x, m)
    yield
    gv = _dot1(g, v_rs)
    yield
    au = _dot1(m, jnp.concatenate([a_rs, gv], axis=1))
    yield
    rhs = _split2(jnp.concatenate(
        [au, jnp.concatenate([zero, v_rs], axis=1)], axis=0))
    tf = lax.dot_general(jnp.concatenate([bp_rs, kp_rs], axis=0).astype(_BF16), rhs[0], _TN,
                         preferred_element_type=_F32)
    qy = _dot3(_split2(jnp.concatenate([rb, rk], axis=1)), rhs)
    yield
    t = jnp.where(eye, jnp.broadcast_to(jnp.exp(cum_end), (LANES, LANES)), zero) + tf[:, :LANES]
    q = r_rs + qy[:, :LANES]
    out = _dot1(jnp.concatenate([q, t], axis=0), get_h())
    y = out[:LANES] + qy[:, LANES:]
    h_new = out[LANES:] + tf[:, LANES:]
    return y[:c] + y[c:], h_new


def _recording(chain, key, store):
    store[key] = yield from chain
    return store[key]


def _scan_kernel(*refs):
    c = SCAN_CHUNK
    ins, y_refs, h_ref = refs[:12], refs[12:14], refs[14]

    @pl.when(pl.program_id(1) == 0)
    def _():
        h_ref[...] = jnp.zeros_like(h_ref)

    tr = lax.broadcasted_iota(jnp.int32, (c, c), 0)
    tc = lax.broadcasted_iota(jnp.int32, (c, c), 1)
    chains, done = [], {}
    for step in range(SCAN_STEP_CHUNKS):
        for d, sgn in enumerate((1, -1)):
            r_ref, v_ref, kk_ref, lw_ref, a_ref, kd_ref = ins[6 * d:6 * d + 6]
            sub = step if d == 0 else SCAN_STEP_CHUNKS - 1 - step
            rows = slice(sub * c, (sub + 1) * c)
            tri = jnp.where((tc - tr) * sgn <= 0, 1.0, 0.0).astype(_BF16)
            lw = lw_ref[0, 0, rows]
            l1 = lw.astype(_BF16)
            rem = lw - l1.astype(_F32)
            l2 = rem.astype(_BF16)
            l3 = (rem - l2.astype(_F32)).astype(_BF16)
            dot = lambda w, tri=tri: jnp.dot(tri, w, preferred_element_type=_F32)
            cum = (dot(l3) + dot(l2)) + dot(l1)
            cum_end = jnp.sum(lw, axis=0, keepdims=True)
            pairs = lw.shape[1] // LANES
            for p in range(pairs):
                sl = slice(p * LANES, (p + 1) * LANES)
                slot = d * pairs + p
                if step == 0:
                    get_h = lambda slot=slot: h_ref[slot]
                else:
                    get_h = lambda key=(step - 1, slot): done[key][1]
                chain = _scan_chunk(get_h, r_ref[0, rows, sl], v_ref[0, rows, sl], kk_ref[0, rows, sl],
                                    lw[:, sl], cum[:, sl], cum_end[:, sl],
                                    a_ref[0, 0, rows, sl], kd_ref[0, 0, rows, sl], sgn)
                chains.append(_recording(chain, (step, slot), done))
    _interleave(chains)
    for (step, slot), (y, h) in done.items():
        d, p = divmod(slot, h_ref.shape[0] // 2)
        sub = step if d == 0 else SCAN_STEP_CHUNKS - 1 - step
        y_refs[d][0, sub * c:(sub + 1) * c, p * LANES:(p + 1) * LANES] = y
        if step == SCAN_STEP_CHUNKS - 1:
            h_ref[slot] = h


def _delta_scan(r, v, kk, lw, a, kd):
    b, t, w = r.shape
    rows = SCAN_CHUNK * SCAN_STEP_CHUNKS
    n_blocks = t // rows
    n_ctx = CTX_LEN // rows

    def block(d, j):
        return j if d == 0 else jnp.where(j < n_ctx, n_ctx - 1 - j, n_blocks + n_ctx - 1 - j)

    specs, args = [], []
    for d in range(2):
        shared = pl.BlockSpec((1, rows, w), lambda i, j, d=d: (i, block(d, j), 0))
        per_dir = pl.BlockSpec((1, 1, rows, w), lambda i, j, d=d: (d, i, block(d, j), 0))
        specs += [shared, shared, shared, per_dir, per_dir, per_dir]
        args += [r, v, kk, lw, a, kd]
    out_specs = [pl.BlockSpec((1, rows, w), lambda i, j, d=d: (i, block(d, j), 0)) for d in range(2)]
    return pl.pallas_call(
        _scan_kernel,
        grid=(b, n_blocks),
        in_specs=specs,
        out_specs=out_specs,
        out_shape=[jax.ShapeDtypeStruct((b, t, w), _F32)] * 2,
        scratch_shapes=[pltpu.VMEM((2 * w // LANES, LANES, LANES), _F32)],
        name="delta_scan", compiler_params=_params(2),
    )(*args)


def _post_kernel(*refs, even):
    it = iter(refs)
    if even:
        yf_ref, yb_ref, r_ref, v_ref, kd_ref, g_ref, gn_ref = (next(it) for _ in range(7))
    att_ref, x_ref, m_ref, wo_ref, wrh_ref, wrl_ref, br_ref = (next(it) for _ in range(7))
    xo_ref, h_ref, route_ref, count_ref, carry_ref = it
    if even:
        mean_mat = _head_sum_matrix(1.0 / HEAD_DIM)
        ones_mat = _head_sum_matrix(1.0)
        parts = [att_ref[...]]
        for s in range(B_WIDTH // LANES):
            sl = slice(s * LANES, (s + 1) * LANES)
            y = yf_ref[:, sl] + yb_ref[:, sl]
            dev = y - _head_sum(y, mean_mat)
            var = _head_sum(dev * dev, mean_mat)
            y = dev * lax.rsqrt(var + RWKV_GN_EPS) * gn_ref[0:1, sl] + gn_ref[1:2, sl]
            r, v = r_ref[:, sl], v_ref[:, sl]
            for dr in range(2):
                y = y + _head_sum(r * kd_ref[dr, :, sl] * gn_ref[2:3, sl], ones_mat) * v
            parts.append((y * g_ref[:, sl]).astype(_BF16))
        mixed = jnp.concatenate(parts, axis=1)
    else:
        mixed = att_ref[...]
    out = jnp.dot(mixed, wo_ref[...], preferred_element_type=_F32)
    x = x_ref[...] + m_ref[0, 2:3, :] * out
    xo_ref[...] = x
    h = _rms_rows(x) * (1.0 + m_ref[0, 4:5, :]) + m_ref[0, 3:4, :]
    h_ref[...] = _pack_bf16_pairs(h)
    logits = _dot3(_split2(h), (wrh_ref[...], wrl_ref[...])) + br_ref[...]
    _route(logits, route_ref, count_ref, carry_ref)


def _route(logits, route_ref, count_ref, carry_ref):
    tm = logits.shape[0]
    lane = lax.broadcasted_iota(jnp.int32, (tm, LANES), 1)
    neg = -1e30
    first_lane = lambda hit: jnp.min(jnp.where(hit, lane, LANES), axis=-1, keepdims=True)

    is_grp = lane < N_GROUPS
    grp = jnp.where(is_grp, logits, neg)
    g_max = jnp.max(grp, axis=-1, keepdims=True)
    g_sel = first_lane(grp == g_max)
    grp_w = 1.0 / jnp.sum(jnp.where(is_grp, jnp.exp(grp - g_max), 0.0), axis=-1, keepdims=True)

    base = N_GROUPS + EXPERTS_PER_GROUP * g_sel
    in_grp = (lane >= base) & (lane < base + EXPERTS_PER_GROUP)
    ex = jnp.where(in_grp, logits, neg)
    e_max = jnp.max(ex, axis=-1, keepdims=True)
    pe = jnp.where(in_grp, jnp.exp(ex - e_max), 0.0)
    prob = pe / jnp.sum(pe, axis=-1, keepdims=True)
    p1 = jnp.max(prob, axis=-1, keepdims=True)
    l1 = first_lane(in_grp & (prob == p1))
    rest = jnp.where(in_grp & (lane != l1), prob, -1.0)
    p2 = jnp.max(rest, axis=-1, keepdims=True)
    l2 = first_lane(rest == p2)
    scale = grp_w / (p1 + p2)
    e1, e2 = l1 - N_GROUPS, l2 - N_GROUPS

    @pl.when(pl.program_id(0) == 0)
    def _():
        carry_ref[...] = jnp.zeros_like(carry_ref)

    oh1 = jnp.where(lane == e1, 1.0, 0.0)
    oh2 = jnp.where(lane == e2, 1.0, 0.0)
    rr = lax.broadcasted_iota(jnp.int32, (tm, tm), 0)
    cc = lax.broadcasted_iota(jnp.int32, (tm, tm), 1)
    tri = jnp.where(cc < rr, 1.0, 0.0).astype(_BF16)
    before = lambda oh: jnp.dot(tri, oh.astype(_BF16), preferred_element_type=_F32)
    carry = carry_ref[...]
    tot1 = jnp.sum(oh1, axis=0, keepdims=True)
    rank1 = jnp.sum(oh1 * (before(oh1) + carry), axis=-1, keepdims=True)
    rank2 = jnp.sum(oh2 * (before(oh2) + (carry + tot1)), axis=-1, keepdims=True)
    carry = carry + tot1 + jnp.sum(oh2, axis=0, keepdims=True)
    carry_ref[...] = carry
    count_ref[...] = carry

    out = jnp.zeros((tm, LANES), _F32)
    for i, val in enumerate((p1 * scale, p2 * scale, e1.astype(_F32), e2.astype(_F32), rank1, rank2)):
        out = jnp.where(lane == i, val, out)
    route_ref[...] = out


def _mixer_out(att, x, mods, w_out, w_router, b_router, rwkv=None):
    n, d = x.shape
    tm = ROW_TILE
    row = lambda width: pl.BlockSpec((tm, width), lambda i: (i, 0))
    both = pl.BlockSpec((2, tm, B_WIDTH), lambda i: (0, i, 0))
    args, specs = [], []
    if rwkv is not None:
        y_fwd, y_bwd, r, v, kd, g, gn = rwkv
        args += [y_fwd, y_bwd, r, v, kd, g, gn]
        specs += [row(B_WIDTH)] * 4 + [both, row(B_WIDTH), _full(gn.shape)]
    wr_hi, wr_lo = _split2(w_router)
    args += [att, x, mods, w_out, wr_hi, wr_lo, b_router]
    specs += [row(att.shape[1]), row(d), pl.BlockSpec((1, 6, d), lambda i: (_mods_index(i), 0, 0)),
              _full(w_out.shape), _full(wr_hi.shape), _full(wr_lo.shape), _full(b_router.shape)]
    return pl.pallas_call(
        functools.partial(_post_kernel, even=rwkv is not None),
        grid=(n // tm,), in_specs=specs,
        out_specs=[row(d), row(d // 2), row(LANES), _full((1, LANES))],
        out_shape=[jax.ShapeDtypeStruct((n, d), _F32), jax.ShapeDtypeStruct((n, d // 2), _F32),
                   jax.ShapeDtypeStruct((n, LANES), _F32), jax.ShapeDtypeStruct((1, LANES), _F32)],
        scratch_shapes=[pltpu.VMEM((1, LANES), _F32)],
        name="mixer_out_even" if rwkv is not None else "mixer_out_odd", compiler_params=_params(1),
    )(*args)


def _moe_kernel(be_ref, nv_ref, x_ref, wg_ref, wu_ref, wd_ref, o_ref, wg_s, wu_s, wd_s):
    i = pl.program_id(0)
    valid = i < nv_ref[0]
    new_expert = (i == 0) | (be_ref[i] != be_ref[jnp.maximum(i - 1, 0)])

    @pl.when(valid & new_expert)
    def _():
        wg_s[...] = wg_ref[0, 0].astype(_BF16)
        wu_s[...] = wu_ref[0, 0].astype(_BF16)
        wd_s[...] = wd_ref[0, 0].astype(_BF16)

    @pl.when(valid)
    def _():
        x = _unpack_bf16_pairs(x_ref[...])
        gate = jnp.dot(x, wg_s[...], preferred_element_type=_F32)
        up = jnp.dot(x, wu_s[...], preferred_element_type=_F32)
        hid = (gate * jax.nn.sigmoid(gate) * up).astype(_BF16)
        o_ref[...] = jnp.dot(hid, wd_s[...], preferred_element_type=_F32)

    @pl.when(jnp.logical_not(valid))
    def _():
        o_ref[...] = jnp.zeros_like(o_ref)


def _moe_blocks(block_expert, n_valid, xs, wg, wu, wd, layer):
    rows, d = xs.shape[0], 2 * xs.shape[1]
    nb = rows // MOE_ROWS
    w_spec = lambda a, b: pl.BlockSpec((1, 1, a, b), lambda i, be, nv: (layer, be[i], 0, 0))
    grid_spec = pltpu.PrefetchScalarGridSpec(
        num_scalar_prefetch=2,
        grid=(nb,),
        in_specs=[pl.BlockSpec((MOE_ROWS, d // 2), lambda i, be, nv: (i, 0)),
                  w_spec(d, EXPERT_HIDDEN), w_spec(d, EXPERT_HIDDEN), w_spec(EXPERT_HIDDEN, d)],
        out_specs=pl.BlockSpec((MOE_ROWS, d), lambda i, be, nv: (i, 0)),
        scratch_shapes=[pltpu.VMEM((d, EXPERT_HIDDEN), _BF16), pltpu.VMEM((d, EXPERT_HIDDEN), _BF16),
                        pltpu.VMEM((EXPERT_HIDDEN, d), _BF16)],
    )
    return pl.pallas_call(
        _moe_kernel,
        grid_spec=grid_spec,
        out_shape=jax.ShapeDtypeStruct((rows, d), _F32),
        name="moe_ffn", compiler_params=_params(1),
    )(block_expert, n_valid, xs, wg, wu, wd)


def _moe(h, route, counts, w_gate, w_up, w_down, layer):
    n = h.shape[0]
    eid = route[:, 2:4].astype(jnp.int32)
    rank = route[:, 4:6].astype(jnp.int32)
    cnt = counts[0, :N_EXPERTS].astype(jnp.int32)
    padded = (cnt + MOE_ROWS - 1) // MOE_ROWS * MOE_ROWS
    pad_end = jnp.cumsum(padded)
    experts = jnp.arange(N_EXPERTS, dtype=jnp.int32)
    pad_start = jnp.sum(jnp.where(eid[:, :, None] == experts, pad_end - padded, 0), axis=-1)
    slot = pad_start + rank
    n_blocks = -(-(n * TOP_K + N_EXPERTS * (MOE_ROWS - 1)) // MOE_ROWS)
    block_start = jnp.arange(n_blocks, dtype=jnp.int32) * MOE_ROWS
    block_expert = jnp.minimum(jnp.sum(pad_end[None, :] <= block_start[:, None], axis=1),
                               N_EXPERTS - 1).astype(jnp.int32)
    n_valid = (pad_end[-1:] // MOE_ROWS).astype(jnp.int32)
    tok = jnp.broadcast_to(jnp.arange(n, dtype=jnp.int32)[:, None], (n, TOP_K))
    _, tok_by_slot = lax.sort_key_val(slot.reshape(-1), tok.reshape(-1))
    run_start = (jnp.cumsum(cnt) - cnt)[block_expert]
    run_pos = block_start - (pad_end - padded)[block_expert]
    pos = (run_pos[:, None] + jnp.arange(MOE_ROWS, dtype=jnp.int32)).reshape(-1)
    occupied = pos < jnp.repeat(cnt[block_expert], MOE_ROWS)
    src = jnp.minimum(jnp.repeat(run_start, MOE_ROWS) + pos, n * TOP_K - 1)
    slot_tok = jnp.where(occupied, tok_by_slot[src], 0)
    yb = _moe_blocks(block_expert, n_valid, h[slot_tok], w_gate, w_up, w_down, layer)
    return yb[slot[:, 0]], yb[slot[:, 1]]


def _final_kernel(x_ref, y1_ref, y2_ref, g_ref, m_ref, gain_ref, o_ref):
    g = g_ref[...]
    x = x_ref[...] + m_ref[0, 5:6, :] * (g[:, 0:1] * y1_ref[...] + g[:, 1:2] * y2_ref[...])
    o_ref[...] = _rms_rows(x) * gain_ref[...]


def _final(x, y1, y2, route, mods, gain, bsz, n_lat):
    d = x.shape[1]
    tm = ROW_TILE
    lat_tiles = n_lat // tm
    tiles = (CTX_LEN + n_lat) // tm
    src = lambda i: (i // lat_tiles) * tiles + CTX_LEN // tm + i % lat_tiles
    row = lambda width: pl.BlockSpec((tm, width), lambda i: (src(i), 0))
    return pl.pallas_call(
        _final_kernel,
        grid=(bsz * lat_tiles,),
        in_specs=[row(d), row(d), row(d), row(LANES),
                  pl.BlockSpec((1, 6, d), lambda i: ((i // lat_tiles) * 2 + 1, 0, 0)), _full((1, d))],
        out_specs=pl.BlockSpec((tm, d), lambda i: (i, 0)),
        out_shape=jax.ShapeDtypeStruct((bsz * n_lat, d), _F32),
        name="final_norm", compiler_params=_params(1),
    )(x, y1, y2, route, mods, gain.reshape(1, d))


def _rope_tables(n_lat):
    nf = HEAD_DIM // 4
    inv = ROPE_THETA ** (-jnp.arange(nf, dtype=_F32) / nf)
    rows = n_lat // GRID_W
    r_ang = jnp.repeat(jnp.arange(rows, dtype=_F32), GRID_W)[:, None] * inv
    c_ang = jnp.tile(jnp.arange(GRID_W, dtype=_F32), rows)[:, None] * inv
    cos = jnp.concatenate([jnp.cos(r_ang)] * 2 + [jnp.cos(c_ang)] * 2, axis=-1)
    sin = jnp.concatenate([-jnp.sin(r_ang), jnp.sin(r_ang), -jnp.sin(c_ang), jnp.sin(c_ang)], axis=-1)
    cos = jnp.concatenate([jnp.ones((CTX_LEN, HEAD_DIM), _F32), cos], axis=0)
    sin = jnp.concatenate([jnp.zeros((CTX_LEN, HEAD_DIM), _F32), sin], axis=0)
    return jnp.tile(cos, (1, 2)), jnp.tile(sin, (1, 2))


def kernel(x, c, ctx, c_ctx, mod_w, mod_b, ev_w_in, ev_w_out, ev_q_gain, ev_k_gain, ev_mu_prev, ev_mu_next, ev_w0, ev_w2, ev_a0, ev_a2, ev_g2, ev_k_k, ev_k_a, ev_r_k, ev_gn_w, ev_gn_b, ev_v0, ev_v1, ev_v2, od_w_in, od_w_out, od_lq1, od_lk1, od_lq2, od_lk2, od_subln, moe_w_grp, moe_b_grp, moe_w_rt, moe_b_rt, moe_w_gate, moe_w_up, moe_w_down, final_gain):
    bsz, n_lat, d = x.shape
    t = CTX_LEN + n_lat
    n = bsz * t
    cos, sin = _rope_tables(n_lat)
    cond = jax.nn.silu(jnp.concatenate([c, c_ctx[None, :], jnp.zeros((16 - bsz - 1, d), _F32)], axis=0))
    xs = jnp.concatenate([ctx, x], axis=1).reshape(n, d)
    v_first = None
    moe = None
    for layer in range(DEPTH):
        m = _mm(cond, mod_w[layer], tn=1536, precision=_HI) + mod_b[layer]
        m = m.reshape(16, 6, d)
        mods = jnp.stack([jnp.broadcast_to(m[bsz], (bsz, 6, d)), m[:bsz]], axis=1).reshape(2 * bsz, 6, d)
        w_router = jnp.concatenate(
            [moe_w_grp[layer], moe_w_rt[layer], jnp.zeros((d, LANES - N_GROUPS - N_EXPERTS), _F32)], axis=1)
        b_router = jnp.concatenate(
            [moe_b_grp[layer], moe_b_rt[layer], jnp.zeros((LANES - N_GROUPS - N_EXPERTS,), _F32)])[None, :]
        if layer % 2 == 0:
            e = layer // 2
            p = {'q_gain': ev_q_gain[e], 'k_gain': ev_k_gain[e], 'mu_prev': ev_mu_prev[e],
                 'mu_next': ev_mu_next[e], 'w0': ev_w0[e], 'w2': ev_w2[e], 'a0': ev_a0[e],
                 'a2': ev_a2[e], 'g2': ev_g2[e], 'k_k': ev_k_k[e], 'k_a': ev_k_a[e]}
            if e > 0:
                p['v0'], p['v1'], p['v2'] = ev_v0[e - 1], ev_v1[e - 1], ev_v2[e - 1]
            proj, xs = _project(xs, mods, ev_w_in[e].astype(_BF16), moe=moe)
            q, kk2, vv2, r, v, kkn, lw, a, kd, g = _prepare_even(proj, cos, sin, p, v_first)
            if v_first is None:
                v_first = v
            att = _gqa_attend(q.reshape(bsz, t, -1), kk2.reshape(bsz, t, -1), vv2.reshape(bsz, t, -1))
            seq = lambda z: z.reshape(*z.shape[:-2], bsz, t, B_WIDTH)
            y_fwd, y_bwd = (y.reshape(n, B_WIDTH) for y in
                            _delta_scan(seq(r), seq(v), seq(kkn), seq(lw), seq(a), seq(kd)))
            gn = jnp.stack([ev_gn_w[e], ev_gn_b[e], ev_r_k[e].reshape(B_WIDTH)])
            xs, h, route, counts = _mixer_out(att.reshape(n, -1), xs, mods, ev_w_out[e].astype(_BF16),
                                       w_router, b_router, rwkv=(y_fwd, y_bwd, r, v, kd, g, gn))
        else:
            o = layer // 2
            lambda_init = 0.8 - 0.6 * math.exp(-0.3 * layer)
            qkv, xs = _project(xs, mods, od_w_in[o].astype(_BF16), moe=moe, rope=(cos, sin))
            lam = (jnp.exp(jnp.sum(od_lq1[o] * od_lk1[o])) - jnp.exp(jnp.sum(od_lq2[o] * od_lk2[o]))
                   + lambda_init).reshape(1, 1)
            gain = (od_subln[o] * (1.0 - lambda_init)).reshape(1, LANES)
            att = _diff_attend(lam, qkv.reshape(bsz, t, -1), gain)
            xs, h, route, counts = _mixer_out(att.reshape(n, -1), xs, mods, od_w_out[o].astype(_BF16),
                                       w_router, b_router)
        y1, y2 = _moe(h, route, counts, moe_w_gate, moe_w_up, moe_w_down, layer)
        moe = (y1, y2, route, mods)
    return _final(xs, y1, y2, route, mods, final_gain, bsz, n_lat).reshape(bsz, n_lat, d)
```

```python
import functools
import math

import jax
import jax.numpy as jnp
from jax import lax
from jax.experimental import pallas as pl
from jax.experimental.pallas import tpu as pltpu

D_MODEL = 1024
DEPTH = 4
SEQ = 2048
GRID_W = 64
CTX_LEN = 256
HEAD_DIM = 64
ROPE_THETA = 10000.0
RMS_EPS = 1e-6
SUBLN_EPS = 1e-5
Q_SCALE = HEAD_DIM ** -0.5 * math.log2(math.e)

A_HEADS = 8
A_KV_HEADS = 2
A_WIDTH = 512
A_KV_WIDTH = 128
A_COLS = 768

B_HEADS = 8
B_WIDTH = 512
B_COLS = 1792
RWKV_W_LORA = 64
RWKV_A_LORA = 64
RWKV_V_LORA = 32
RWKV_G_LORA = 128
RWKV_GN_EPS = 64e-5
DECAY_SCALE = math.exp(-0.5)

C_HEADS = 8
C_QK = 1024

N_GROUPS = 4
EXPERTS_PER_GROUP = 8
N_EXPERTS = 32
TOP_K = 2
EXPERT_HIDDEN = 512
MOE_ROWS = 256

LANES = 128
SCAN_CHUNK = 64
SCAN_STEP_CHUNKS = 2
ROW_TILE = 256
Q_TILE = 128
VMEM_LIMIT = 56 * 1024 * 1024

_HI = lax.Precision.HIGHEST
_F32 = jnp.float32
_BF16 = jnp.bfloat16

_NN = (((1,), (0,)), ((), ()))
_NT = (((1,), (1,)), ((), ()))
_TN = (((0,), (0,)), ((), ()))


def _params(n_axes):
    return pltpu.CompilerParams(dimension_semantics=("arbitrary",) * n_axes,
                                vmem_limit_bytes=VMEM_LIMIT)


def _full(shape):
    return pl.BlockSpec(shape, lambda *_: (0,) * len(shape))


def _split2(x):
    hi = x.astype(_BF16)
    return hi, (x - hi.astype(_F32)).astype(_BF16)


def _dot3(a, b, dims=_NN):
    d = lambda u, w: lax.dot_general(u, w, dims, preferred_element_type=_F32)
    return (d(a[1], b[0]) + d(a[0], b[1])) + d(a[0], b[0])


def _dot1(a, b, dims=_NN):
    return lax.dot_general(a.astype(_BF16), b.astype(_BF16), dims, preferred_element_type=_F32)


def _pack_bf16_pairs(x):
    w = x.shape[1] // 2
    bits = pltpu.bitcast(x.astype(_BF16).astype(_F32), jnp.uint32)
    return pltpu.bitcast((bits[:, :w] >> 16) | (bits[:, w:] & jnp.uint32(0xFFFF0000)), _F32)


def _unpack_bf16_pairs(words):
    packed = pltpu.bitcast(words, jnp.uint32)
    lo = pltpu.bitcast(packed << 16, _F32)
    hi = pltpu.bitcast(packed & jnp.uint32(0xFFFF0000), _F32)
    return jnp.concatenate([lo, hi], axis=1).astype(_BF16)


def _lane_lo(shape):
    return lax.broadcasted_iota(jnp.int32, shape, len(shape) - 1) < HEAD_DIM


def _head_sum_matrix(value):
    row = lax.broadcasted_iota(jnp.int32, (LANES, LANES), 0)
    col = lax.broadcasted_iota(jnp.int32, (LANES, LANES), 1)
    return jnp.where((row // HEAD_DIM) == (col // HEAD_DIM), value, 0.0).astype(_BF16)


def _head_sum(x, mat):
    hi, lo = _split2(x)
    return jnp.dot(lo, mat, preferred_element_type=_F32) + jnp.dot(hi, mat, preferred_element_type=_F32)


def _rms_rows(x, eps=RMS_EPS):
    return x * lax.rsqrt(jnp.mean(x * x, axis=-1, keepdims=True) + eps)


def _rope_slab(x, cos, sin):
    lane = lax.broadcasted_iota(jnp.int32, x.shape, 1)
    partner = jnp.where((lane & 16) == 0, pltpu.roll(x, LANES - 16, axis=1), pltpu.roll(x, 16, axis=1))
    return x * cos + partner * sin


def _mods_index(i):
    tiles = (CTX_LEN + SEQ) // ROW_TILE
    return (i // tiles) * 2 + jnp.minimum(i % tiles, 1)


def _mm_kernel(a_ref, w_ref, o_ref, *, precision):
    o_ref[...] = jnp.dot(a_ref[...], w_ref[...], precision=precision,
                         preferred_element_type=_F32).astype(o_ref.dtype)


def _mm(a, w, *, tn, precision=None):
    m, k = a.shape
    n = w.shape[1]
    return pl.pallas_call(
        functools.partial(_mm_kernel, precision=precision),
        grid=(n // tn,),
        in_specs=[pl.BlockSpec((m, k), lambda j: (0, 0)),
                  pl.BlockSpec((k, tn), lambda j: (0, j))],
        out_specs=pl.BlockSpec((m, tn), lambda j: (0, j)),
        out_shape=jax.ShapeDtypeStruct((m, n), _F32),
        name=f"mm_{k}x{n}",
        compiler_params=_params(1),
    )(a, w)


def _proj_kernel(*refs, combine, rope):
    it = iter(refs)
    x_ref, m_ref = next(it), next(it)
    if combine:
        y1_ref, y2_ref, g_ref, mp_ref = (next(it) for _ in range(4))
    w_ref = next(it)
    if rope:
        cos_ref, sin_ref = next(it), next(it)
    o_ref = next(it)
    x = x_ref[...]
    if combine:
        xo_ref = next(it)
        g = g_ref[...]
        x = x + mp_ref[0, 5:6, :] * (g[:, 0:1] * y1_ref[...] + g[:, 1:2] * y2_ref[...])
        xo_ref[...] = x
    h = _rms_rows(x) * (1.0 + m_ref[0, 1:2, :]) + m_ref[0, 0:1, :]
    p = jnp.dot(h.astype(_BF16), w_ref[...], preferred_element_type=_F32)
    if rope:
        cos, sin = cos_ref[...], sin_ref[...]
        for s in range(2 * C_QK // LANES):
            sl = slice(s * LANES, (s + 1) * LANES)
            slab = _rope_slab(p[:, sl], cos, sin)
            if s < C_QK // LANES:
                slab = slab * Q_SCALE
            o_ref[:, sl] = slab.astype(o_ref.dtype)
        ones = jnp.ones((p.shape[0], LANES), o_ref.dtype)
        for h in range(C_HEADS):
            base = 2 * C_QK + 2 * h * LANES
            o_ref[:, base:base + LANES] = p[:, 2 * C_QK + h * LANES:2 * C_QK + (h + 1) * LANES].astype(o_ref.dtype)
            o_ref[:, base + LANES:base + 2 * LANES] = ones
    else:
        o_ref[...] = p


def _project(x, mods, w, *, moe=None, rope=None):
    n, d = x.shape
    cols = w.shape[1]
    tm = ROW_TILE
    tiles = (CTX_LEN + SEQ) // tm
    row = lambda width: pl.BlockSpec((tm, width), lambda i: (i, 0))
    mod_spec = pl.BlockSpec((1, 6, d), lambda i: (_mods_index(i), 0, 0))
    args, specs = [x, mods], [row(d), mod_spec]
    if moe is not None:
        args += list(moe)
        specs += [row(d), row(d), row(LANES), mod_spec]
    args.append(w)
    specs.append(_full(w.shape))
    if rope is not None:
        args += list(rope)
        specs += [pl.BlockSpec((tm, LANES), lambda i: (i % tiles, 0))] * 2
    if rope is not None:
        out_cols = cols + C_HEADS * LANES
        out_shape = [jax.ShapeDtypeStruct((n, out_cols), _BF16)]
    else:
        out_cols = cols
        out_shape = [jax.ShapeDtypeStruct((n, cols), _F32)]
    out_specs = [row(out_cols)]
    if moe is not None:
        out_shape.append(jax.ShapeDtypeStruct((n, d), _F32))
        out_specs.append(row(d))
    out = pl.pallas_call(
        functools.partial(_proj_kernel, combine=moe is not None, rope=rope is not None),
        grid=(n // tm,), in_specs=specs, out_specs=out_specs, out_shape=out_shape,
        name=f"project_{cols}", compiler_params=_params(1),
    )(*args)
    return out if moe is not None else (out[0], x)


def _prep_kernel(*refs, has_v_first):
    it = iter(refs)
    p_ref, pp_ref, pn_ref, cos_ref, sin_ref = (next(it) for _ in range(5))
    gain_ref, mu_ref, wa0_ref, kv_ref, wl_ref, g2_ref = (next(it) for _ in range(6))
    if has_v_first:
        v1_ref, v2_ref, vf_ref = next(it), next(it), next(it)
    q_ref, kk2_ref, vv2_ref, r_ref, v_ref, kkn_ref, lw_ref, a_ref, kd_ref, g_ref = it

    tiles = (CTX_LEN + SEQ) // ROW_TILE
    j = pl.program_id(0) % tiles
    p = p_ref[...]
    cos, sin = cos_ref[...], sin_ref[...]
    mean_mat = _head_sum_matrix(1.0 / HEAD_DIM)
    ones_mat = _head_sum_matrix(1.0)
    lo = _lane_lo((ROW_TILE, LANES))

    def normed(x, gain):
        return x * lax.rsqrt(_head_sum(x * x, mean_mat) + RMS_EPS) * gain

    for s in range(A_WIDTH // LANES):
        sl = slice(s * LANES, (s + 1) * LANES)
        q = _rope_slab(normed(p[:, sl], gain_ref[0:1, :]), cos, sin)
        q_ref[:, sl] = (q * Q_SCALE).astype(_BF16)
    k = _rope_slab(normed(p[:, A_WIDTH:A_WIDTH + LANES], gain_ref[1:2, :]), cos, sin)
    v = p[:, A_WIDTH + LANES:A_COLS]
    swapped = pltpu.roll(k, HEAD_DIM, axis=1)
    kk2_ref[:, :LANES] = jnp.where(lo, k, swapped).astype(_BF16)
    kk2_ref[:, LANES:] = jnp.where(lo, swapped, k).astype(_BF16)
    vv2_ref[:, :LANES] = jnp.where(lo, v, 1.0).astype(_BF16)
    vv2_ref[:, LANES:] = jnp.where(lo, pltpu.roll(v, HEAD_DIM, axis=1), 1.0).astype(_BF16)

    u = p[:, A_COLS:]
    rid = lax.broadcasted_iota(jnp.int32, (ROW_TILE, 1), 0)
    has_prev = jnp.where(j >= 2, 1.0, 0.0)
    has_next = jnp.where((j >= 1) & (j < tiles - 1), 1.0, 0.0)
    first = pp_ref[7:8, A_COLS:] * has_prev
    last = pn_ref[0:1, A_COLS:] * has_next
    prev = jnp.where(rid == 0, first, pltpu.roll(u, 1, axis=0))
    nxt = jnp.where(rid == ROW_TILE - 1, last, pltpu.roll(u, ROW_TILE - 1, axis=0))
    u = u + mu_ref[0:1, :] * (prev - u) + mu_ref[1:2, :] * (nxt - u)
    r, k, v = u[:, :B_WIDTH], u[:, B_WIDTH:2 * B_WIDTH], u[:, 2 * B_WIDTH:3 * B_WIDTH]
    wa, glr = u[:, 3 * B_WIDTH:3 * B_WIDTH + LANES], u[:, 3 * B_WIDTH + LANES:]
    if has_v_first:
        mid = jnp.dot(v.astype(_BF16), v1_ref[...], preferred_element_type=_F32)
        lora = jnp.dot(mid.astype(_BF16), v2_ref[...], preferred_element_type=_F32)
        v = v + (vf_ref[...] - v) * jax.nn.sigmoid(kv_ref[2:3, :] + lora)
    r_ref[...] = r
    v_ref[...] = v
    kks = k * kv_ref[0:1, :]
    for s in range(B_WIDTH // LANES):
        sl = slice(s * LANES, (s + 1) * LANES)
        x = kks[:, sl]
        kkn_ref[:, sl] = x * lax.rsqrt(jnp.maximum(_head_sum(x * x, ones_mat), 1e-24))
    xw = jnp.where(lo, jnp.tanh(wa), wa).astype(_BF16)
    for dr in range(2):
        z = jnp.dot(xw, wl_ref[dr], preferred_element_type=_F32) + wa0_ref[dr:dr + 1, :]
        a = jax.nn.sigmoid(z[:, B_WIDTH:])
        lw_ref[dr] = -DECAY_SCALE * jax.nn.sigmoid(z[:, :B_WIDTH])
        a_ref[dr] = a
        kd_ref[dr] = k * (1.0 + (a - 1.0) * kv_ref[1:2, :])
    g_ref[...] = jnp.dot(jax.nn.sigmoid(glr).astype(_BF16), g2_ref[...], preferred_element_type=_F32)


def _prepare_even(proj, cos, sin, p, v_first):
    n, cols = proj.shape
    tm = ROW_TILE
    tiles = (CTX_LEN + SEQ) // tm
    halo = 8
    per_tile = tm // halo
    row = lambda width: pl.BlockSpec((tm, width), lambda i: (i, 0))
    both = pl.BlockSpec((2, tm, B_WIDTH), lambda i: (0, i, 0))
    tab = pl.BlockSpec((tm, LANES), lambda i: (i % tiles, 0))
    pad = lambda w, rows, width: jnp.zeros((rows, width), _BF16).at[:w.shape[0], :w.shape[1]].set(w.astype(_BF16))
    gains = jnp.stack([jnp.tile(p['q_gain'], 2), jnp.tile(p['k_gain'], 2)])
    mu = jnp.stack([p['mu_prev'], p['mu_next']])
    wa0 = jnp.concatenate([p['w0'], p['a0']], axis=1)
    kvec = jnp.stack([p['k_k'], p['k_a'], p.get('v0', jnp.zeros((B_WIDTH,), _F32))])
    zero = jnp.zeros((RWKV_W_LORA, B_WIDTH), _F32)
    wl = jnp.stack([jnp.concatenate([jnp.concatenate([p['w2'][dr], zero], axis=1),
                                     jnp.concatenate([zero, p['a2'][dr]], axis=1)], axis=0)
                    for dr in range(2)]).astype(_BF16)
    args = [proj, proj, proj, cos, sin, gains, mu, wa0, kvec, wl, p['g2'].astype(_BF16)]
    specs = [row(cols),
             pl.BlockSpec((halo, cols), lambda i: (jnp.maximum(i * per_tile - 1, 0), 0)),
             pl.BlockSpec((halo, cols), lambda i: (jnp.minimum((i + 1) * per_tile, n // halo - 1), 0)),
             tab, tab, _full(gains.shape), _full(mu.shape), _full(wa0.shape), _full(kvec.shape),
             _full(wl.shape), _full(p['g2'].shape)]
    if v_first is not None:
        v1 = pad(p['v1'], B_WIDTH, LANES)
        v2 = pad(p['v2'], LANES, B_WIDTH)
        args += [v1, v2, v_first]
        specs += [_full(v1.shape), _full(v2.shape), row(B_WIDTH)]
    f32 = lambda *shape: jax.ShapeDtypeStruct(shape, _F32)
    bf16 = lambda *shape: jax.ShapeDtypeStruct(shape, _BF16)
    out_shape = [bf16(n, A_WIDTH), bf16(n, 2 * LANES), bf16(n, 2 * LANES),
                 f32(n, B_WIDTH), f32(n, B_WIDTH), f32(n, B_WIDTH),
                 f32(2, n, B_WIDTH), f32(2, n, B_WIDTH), f32(2, n, B_WIDTH), f32(n, B_WIDTH)]
    out_specs = [row(A_WIDTH), row(2 * LANES), row(2 * LANES), row(B_WIDTH), row(B_WIDTH), row(B_WIDTH),
                 both, both, both, row(B_WIDTH)]
    return pl.pallas_call(
        functools.partial(_prep_kernel, has_v_first=v_first is not None),
        grid=(n // tm,), in_specs=specs, out_specs=out_specs, out_shape=out_shape,
        name="prepare_even", compiler_params=_params(1),
    )(*args)


def _split_heads(q2):
    lo = _lane_lo(q2.shape)
    zero = jnp.zeros_like(q2)
    return jnp.concatenate([jnp.where(lo, q2, zero), jnp.where(lo, zero, q2)], axis=0)


def _scores(qs, k):
    return lax.dot_general(qs, k, _NT, preferred_element_type=_F32)


def _interleave(chains):
    results = {}
    while len(results) < len(chains):
        for p, chain in enumerate(chains):
            try:
                next(chain)
            except StopIteration as stop:
                results[p] = stop.value
    return [results[p] for p in range(len(chains))]


def _softmax_pv(s, v_ones):
    p = jnp.exp2(s - jnp.max(s, axis=-1, keepdims=True)).astype(_BF16)
    yield
    return jnp.dot(p, v_ones, preferred_element_type=_F32)


def _gqa_tile(q, k, v):
    tq = q.shape[0]
    s = _scores(_split_heads(q), k)
    yield
    o = yield from _softmax_pv(s, v)
    o = o / pltpu.roll(o, HEAD_DIM, axis=1)
    return jnp.where(_lane_lo((tq, LANES)), o[:tq], pltpu.roll(o[tq:], HEAD_DIM, axis=1))


def _attend_chains(make_chain, q_ref, o_ref):
    pieces = [(slice(i * Q_TILE, (i + 1) * Q_TILE), j, slice(j * LANES, (j + 1) * LANES))
              for i in range(q_ref.shape[1] // Q_TILE) for j in range(q_ref.shape[2] // LANES)]
    outs = _interleave([make_chain(q_ref[0, rows, lanes], j) for rows, j, lanes in pieces])
    for (rows, _, lanes), o in zip(pieces, outs):
        o_ref[0, rows, lanes] = o.astype(o_ref.dtype)


def _gqa_kernel(q_ref, k_ref, v_ref, o_ref):
    is_ctx = pl.program_id(2) < CTX_LEN // q_ref.shape[1]

    def run(tk):
        _attend_chains(lambda q, slab: _gqa_tile(q, k_ref[0, :tk], v_ref[0, :tk]), q_ref, o_ref)

    pl.when(is_ctx)(lambda: run(CTX_LEN))
    pl.when(jnp.logical_not(is_ctx))(lambda: run(k_ref.shape[1]))


def _gqa_attend(q, kk, vv):
    b, t, _ = q.shape
    tq = 2 * Q_TILE
    kv_spec = pl.BlockSpec((1, t, LANES), lambda i, g, j: (i, 0, g))
    return pl.pallas_call(
        _gqa_kernel,
        grid=(b, A_KV_HEADS, t // tq),
        in_specs=[pl.BlockSpec((1, tq, 2 * LANES), lambda i, g, j: (i, j, g)), kv_spec, kv_spec],
        out_specs=pl.BlockSpec((1, tq, 2 * LANES), lambda i, g, j: (i, j, g)),
        out_shape=jax.ShapeDtypeStruct((b, t, A_WIDTH), _BF16),
        name="gqa_attn", compiler_params=_params(3),
    )(q, kk, vv)


def _diff_tile(lam, q, k, v, gain):
    tq = q.shape[0]
    s = _scores(_split_heads(q), k)
    yield
    o = yield from _softmax_pv(s, v)
    o = o[:, :LANES] / o[:, LANES:]
    o = o[:tq] - lam * o[tq:]
    y = o * lax.rsqrt(jnp.mean(o * o, axis=-1, keepdims=True) + SUBLN_EPS)
    return y * gain


def _diff_kernel(lam_ref, q_ref, k_ref, v_ref, g_ref, o_ref):
    is_ctx = pl.program_id(2) < CTX_LEN // q_ref.shape[1]
    lam, gain = lam_ref[0, 0], g_ref[...]

    def run(tk):
        def chain(q, slab):
            lanes = slice(slab * LANES, (slab + 1) * LANES)
            v_lanes = slice(2 * slab * LANES, 2 * (slab + 1) * LANES)
            return _diff_tile(lam, q, k_ref[0, :tk, lanes], v_ref[0, :tk, v_lanes], gain)
        _attend_chains(chain, q_ref, o_ref)

    pl.when(is_ctx)(lambda: run(CTX_LEN))
    pl.when(jnp.logical_not(is_ctx))(lambda: run(k_ref.shape[1]))


def _diff_attend(lam, qkv, gain):
    b, t, _ = qkv.shape
    tq = 2 * Q_TILE
    pairs = C_HEADS // 2
    return pl.pallas_call(
        _diff_kernel,
        grid=(b, pairs, t // tq),
        in_specs=[pl.BlockSpec(memory_space=pltpu.SMEM),
                  pl.BlockSpec((1, tq, 2 * LANES), lambda i, h, j: (i, j, h)),
                  pl.BlockSpec((1, t, 2 * LANES), lambda i, h, j: (i, 0, pairs + h)),
                  pl.BlockSpec((1, t, 4 * LANES), lambda i, h, j: (i, 0, pairs + h)),
                  _full((1, LANES))],
        out_specs=pl.BlockSpec((1, tq, 2 * LANES), lambda i, h, j: (i, j, h)),
        out_shape=jax.ShapeDtypeStruct((b, t, C_HEADS * LANES), _BF16),
        name="diff_attn", compiler_params=_params(3),
    )(lam, qkv, qkv, qkv, gain)


def _scan_chunk(get_h, r, v, kk, lw, cum, cum_end, a, kd, sgn):
    c = SCAN_CHUNK
    lo = _lane_lo((c, LANES))
    row = lax.broadcasted_iota(jnp.int32, (LANES, LANES), 0)
    col = lax.broadcasted_iota(jnp.int32, (LANES, LANES), 1)
    delta = ((col & (c - 1)) - (row & (c - 1))) * sgn
    before = delta < 0
    upto = delta <= 0
    eye = row == col
    zero = jnp.zeros((LANES, LANES), _F32)

    def blockdiag(x):
        z = jnp.zeros_like(x)
        return jnp.concatenate([jnp.where(lo, x, z), jnp.where(lo, z, x)], axis=0)

    p_inv = jnp.exp(-cum)
    p_end = jnp.exp(cum_end - cum)
    kka = kk * a
    a_rs = blockdiag(-kk * jnp.exp(cum - lw))
    r_rs = blockdiag(r * jnp.exp(cum))
    b_rs = blockdiag(kka * p_inv)
    k_rs = blockdiag(kd * p_inv)
    v_rs = blockdiag(v)
    bp_rs = blockdiag(kka * p_end)
    kp_rs = blockdiag(kd * p_end)

    bk = _split2(jnp.concatenate([b_rs, k_rs], axis=0))
    za = lax.dot_general(a_rs.astype(_BF16), bk[0], _NT, preferred_element_type=_F32)
    zr = _dot3(_split2(r_rs), bk, _NT)
    yield
    low = jnp.where(before, za[:, :LANES], zero)
    g = jnp.where(before, za[:, LANES:], zero)
    rb = jnp.where(upto, zr[:, :LANES], zero)
    rk = jnp.where(upto, zr[:, LANES:], zero)

    x = _dot1(low, low)
    yield
    m = jnp.where(eye, 1.0, 0.0).astype(_F32) + low
    n = 2
    while 2 * n < c:
        res = _dot1(x, jnp.concatenate([m, x], axis=1))
        m = m + res[:, :LANES]
        x = res[:, LANES:]
        n *= 2
        yield
    m = m + _dot1(x, m)
    yield
    gv = _dot1(g, v_rs)
    yield
    au = _dot1(m, jnp.concatenate([a_rs, gv], axis=1))
    yield
    rhs = _split2(jnp.concatenate(
        [au, jnp.concatenate([zero, v_rs], axis=1)], axis=0))
    tf = lax.dot_general(jnp.concatenate([bp_rs, kp_rs], axis=0).astype(_BF16), rhs[0], _TN,
                         preferred_element_type=_F32)
    qy = _dot3(_split2(jnp.concatenate([rb, rk], axis=1)), rhs)
    yield
    t = jnp.where(eye, jnp.broadcast_to(jnp.exp(cum_end), (LANES, LANES)), zero) + tf[:, :LANES]
    q = r_rs + qy[:, :LANES]
    out = _dot1(jnp.concatenate([q, t], axis=0), get_h())
    y = out[:LANES] + qy[:, LANES:]
    h_new = out[LANES:] + tf[:, LANES:]
    return y[:c] + y[c:], h_new


def _recording(chain, key, store):
    store[key] = yield from chain
    return store[key]


def _scan_kernel(*refs):
    c = SCAN_CHUNK
    ins, y_refs, h_ref = refs[:12], refs[12:14], refs[14]

    @pl.when(pl.program_id(1) == 0)
    def _():
        h_ref[...] = jnp.zeros_like(h_ref)

    tr = lax.broadcasted_iota(jnp.int32, (c, c), 0)
    tc = lax.broadcasted_iota(jnp.int32, (c, c), 1)
    chains, done = [], {}
    for step in range(SCAN_STEP_CHUNKS):
        for d, sgn in enumerate((1, -1)):
            r_ref, v_ref, kk_ref, lw_ref, a_ref, kd_ref = ins[6 * d:6 * d + 6]
            sub = step if d == 0 else SCAN_STEP_CHUNKS - 1 - step
            rows = slice(sub * c, (sub + 1) * c)
            tri = jnp.where((tc - tr) * sgn <= 0, 1.0, 0.0).astype(_BF16)
            lw = lw_ref[0, 0, rows]
            l1 = lw.astype(_BF16)
            rem = lw - l1.astype(_F32)
            l2 = rem.astype(_BF16)
            l3 = (rem - l2.astype(_F32)).astype(_BF16)
            dot = lambda w, tri=tri: jnp.dot(tri, w, preferred_element_type=_F32)
            cum = (dot(l3) + dot(l2)) + dot(l1)
            cum_end = jnp.sum(lw, axis=0, keepdims=True)
            pairs = lw.shape[1] // LANES
            for p in range(pairs):
                sl = slice(p * LANES, (p + 1) * LANES)
                slot = d * pairs + p
                if step == 0:
                    get_h = lambda slot=slot: h_ref[slot]
                else:
                    get_h = lambda key=(step - 1, slot): done[key][1]
                chain = _scan_chunk(get_h, r_ref[0, rows, sl], v_ref[0, rows, sl], kk_ref[0, rows, sl],
                                    lw[:, sl], cum[:, sl], cum_end[:, sl],
                                    a_ref[0, 0, rows, sl], kd_ref[0, 0, rows, sl], sgn)
                chains.append(_recording(chain, (step, slot), done))
    _interleave(chains)
    for (step, slot), (y, h) in done.items():
        d, p = divmod(slot, h_ref.shape[0] // 2)
        sub = step if d == 0 else SCAN_STEP_CHUNKS - 1 - step
        y_refs[d][0, sub * c:(sub + 1) * c, p * LANES:(p + 1) * LANES] = y
        if step == SCAN_STEP_CHUNKS - 1:
            h_ref[slot] = h


def _delta_scan(r, v, kk, lw, a, kd):
    b, t, w = r.shape
    rows = SCAN_CHUNK * SCAN_STEP_CHUNKS
    n_blocks = t // rows
    n_ctx = CTX_LEN // rows

    def block(d, j):
        return j if d == 0 else jnp.where(j < n_ctx, n_ctx - 1 - j, n_blocks + n_ctx - 1 - j)

    specs, args = [], []
    for d in range(2):
        shared = pl.BlockSpec((1, rows, w), lambda i, j, d=d: (i, block(d, j), 0))
        per_dir = pl.BlockSpec((1, 1, rows, w), lambda i, j, d=d: (d, i, block(d, j), 0))
        specs += [shared, shared, shared, per_dir, per_dir, per_dir]
        args += [r, v, kk, lw, a, kd]
    out_specs = [pl.BlockSpec((1, rows, w), lambda i, j, d=d: (i, block(d, j), 0)) for d in range(2)]
    return pl.pallas_call(
        _scan_kernel,
        grid=(b, n_blocks),
        in_specs=specs,
        out_specs=out_specs,
        out_shape=[jax.ShapeDtypeStruct((b, t, w), _F32)] * 2,
        scratch_shapes=[pltpu.VMEM((2 * w // LANES, LANES, LANES), _F32)],
        name="delta_scan", compiler_params=_params(2),
    )(*args)


def _post_kernel(*refs, even):
    it = iter(refs)
    if even:
        yf_ref, yb_ref, r_ref, v_ref, kd_ref, g_ref, gn_ref = (next(it) for _ in range(7))
    att_ref, x_ref, m_ref, wo_ref, wrh_ref, wrl_ref, br_ref = (next(it) for _ in range(7))
    xo_ref, h_ref, route_ref, count_ref, carry_ref = it
    if even:
        mean_mat = _head_sum_matrix(1.0 / HEAD_DIM)
        ones_mat = _head_sum_matrix(1.0)
        parts = [att_ref[...]]
        for s in range(B_WIDTH // LANES):
            sl = slice(s * LANES, (s + 1) * LANES)
            y = yf_ref[:, sl] + yb_ref[:, sl]
            dev = y - _head_sum(y, mean_mat)
            var = _head_sum(dev * dev, mean_mat)
            y = dev * lax.rsqrt(var + RWKV_GN_EPS) * gn_ref[0:1, sl] + gn_ref[1:2, sl]
            r, v = r_ref[:, sl], v_ref[:, sl]
            for dr in range(2):
                y = y + _head_sum(r * kd_ref[dr, :, sl] * gn_ref[2:3, sl], ones_mat) * v
            parts.append((y * g_ref[:, sl]).astype(_BF16))
        mixed = jnp.concatenate(parts, axis=1)
    else:
        mixed = att_ref[...]
    out = jnp.dot(mixed, wo_ref[...], preferred_element_type=_F32)
    x = x_ref[...] + m_ref[0, 2:3, :] * out
    xo_ref[...] = x
    h = _rms_rows(x) * (1.0 + m_ref[0, 4:5, :]) + m_ref[0, 3:4, :]
    h_ref[...] = _pack_bf16_pairs(h)
    logits = _dot3(_split2(h), (wrh_ref[...], wrl_ref[...])) + br_ref[...]
    _route(logits, route_ref, count_ref, carry_ref)


def _route(logits, route_ref, count_ref, carry_ref):
    tm = logits.shape[0]
    lane = lax.broadcasted_iota(jnp.int32, (tm, LANES), 1)
    neg = -1e30
    first_lane = lambda hit: jnp.min(jnp.where(hit, lane, LANES), axis=-1, keepdims=True)

    is_grp = lane < N_GROUPS
    grp = jnp.where(is_grp, logits, neg)
    g_max = jnp.max(grp, axis=-1, keepdims=True)
    g_sel = first_lane(grp == g_max)
    grp_w = 1.0 / jnp.sum(jnp.where(is_grp, jnp.exp(grp - g_max), 0.0), axis=-1, keepdims=True)

    base = N_GROUPS + EXPERTS_PER_GROUP * g_sel
    in_grp = (lane >= base) & (lane < base + EXPERTS_PER_GROUP)
    ex = jnp.where(in_grp, logits, neg)
    e_max = jnp.max(ex, axis=-1, keepdims=True)
    pe = jnp.where(in_grp, jnp.exp(ex - e_max), 0.0)
    prob = pe / jnp.sum(pe, axis=-1, keepdims=True)
    p1 = jnp.max(prob, axis=-1, keepdims=True)
    l1 = first_lane(in_grp & (prob == p1))
    rest = jnp.where(in_grp & (lane != l1), prob, -1.0)
    p2 = jnp.max(rest, axis=-1, keepdims=True)
    l2 = first_lane(rest == p2)
    scale = grp_w / (p1 + p2)
    e1, e2 = l1 - N_GROUPS, l2 - N_GROUPS

    @pl.when(pl.program_id(0) == 0)
    def _():
        carry_ref[...] = jnp.zeros_like(carry_ref)

    oh1 = jnp.where(lane == e1, 1.0, 0.0)
    oh2 = jnp.where(lane == e2, 1.0, 0.0)
    rr = lax.broadcasted_iota(jnp.int32, (tm, tm), 0)
    cc = lax.broadcasted_iota(jnp.int32, (tm, tm), 1)
    tri = jnp.where(cc < rr, 1.0, 0.0).astype(_BF16)
    before = lambda oh: jnp.dot(tri, oh.astype(_BF16), preferred_element_type=_F32)
    carry = carry_ref[...]
    tot1 = jnp.sum(oh1, axis=0, keepdims=True)
    rank1 = jnp.sum(oh1 * (before(oh1) + carry), axis=-1, keepdims=True)
    rank2 = jnp.sum(oh2 * (before(oh2) + (carry + tot1)), axis=-1, keepdims=True)
    carry = carry + tot1 + jnp.sum(oh2, axis=0, keepdims=True)
    carry_ref[...] = carry
    count_ref[...] = carry

    out = jnp.zeros((tm, LANES), _F32)
    for i, val in enumerate((p1 * scale, p2 * scale, e1.astype(_F32), e2.astype(_F32), rank1, rank2)):
        out = jnp.where(lane == i, val, out)
    route_ref[...] = out


def _mixer_out(att, x, mods, w_out, w_router, b_router, rwkv=None):
    n, d = x.shape
    tm = ROW_TILE
    row = lambda width: pl.BlockSpec((tm, width), lambda i: (i, 0))
    both = pl.BlockSpec((2, tm, B_WIDTH), lambda i: (0, i, 0))
    args, specs = [], []
    if rwkv is not None:
        y_fwd, y_bwd, r, v, kd, g, gn = rwkv
        args += [y_fwd, y_bwd, r, v, kd, g, gn]
        specs += [row(B_WIDTH)] * 4 + [both, row(B_WIDTH), _full(gn.shape)]
    wr_hi, wr_lo = _split2(w_router)
    args += [att, x, mods, w_out, wr_hi, wr_lo, b_router]
    specs += [row(att.shape[1]), row(d), pl.BlockSpec((1, 6, d), lambda i: (_mods_index(i), 0, 0)),
              _full(w_out.shape), _full(wr_hi.shape), _full(wr_lo.shape), _full(b_router.shape)]
    return pl.pallas_call(
        functools.partial(_post_kernel, even=rwkv is not None),
        grid=(n // tm,), in_specs=specs,
        out_specs=[row(d), row(d // 2), row(LANES), _full((1, LANES))],
        out_shape=[jax.ShapeDtypeStruct((n, d), _F32), jax.ShapeDtypeStruct((n, d // 2), _F32),
                   jax.ShapeDtypeStruct((n, LANES), _F32), jax.ShapeDtypeStruct((1, LANES), _F32)],
        scratch_shapes=[pltpu.VMEM((1, LANES), _F32)],
        name="mixer_out_even" if rwkv is not None else "mixer_out_odd", compiler_params=_params(1),
    )(*args)


def _moe_kernel(be_ref, nv_ref, x_ref, wg_ref, wu_ref, wd_ref, o_ref, wg_s, wu_s, wd_s):
    i = pl.program_id(0)
    valid = i < nv_ref[0]
    new_expert = (i == 0) | (be_ref[i] != be_ref[jnp.maximum(i - 1, 0)])

    @pl.when(valid & new_expert)
    def _():
        wg_s[...] = wg_ref[0, 0].astype(_BF16)
        wu_s[...] = wu_ref[0, 0].astype(_BF16)
        wd_s[...] = wd_ref[0, 0].astype(_BF16)

    @pl.when(valid)
    def _():
        x = _unpack_bf16_pairs(x_ref[...])

        def hidden_half(cols):
            gate = jnp.dot(x, wg_s[:, cols], preferred_element_type=_F32)
            up = jnp.dot(x, wu_s[:, cols], preferred_element_type=_F32)
            yield
            hid = (gate * jax.nn.sigmoid(gate) * up).astype(_BF16)
            yield
            return jnp.dot(hid, wd_s[cols, :], preferred_element_type=_F32)

        half = EXPERT_HIDDEN // 2
        lo, hi = _interleave([hidden_half(slice(0, half)), hidden_half(slice(half, EXPERT_HIDDEN))])
        o_ref[...] = lo + hi

    @pl.when(jnp.logical_not(valid))
    def _():
        o_ref[...] = jnp.zeros_like(o_ref)


def _moe_blocks(block_expert, n_valid, xs, wg, wu, wd, layer):
    rows, d = xs.shape[0], 2 * xs.shape[1]
    nb = rows // MOE_ROWS
    w_spec = lambda a, b: pl.BlockSpec((1, 1, a, b), lambda i, be, nv: (layer, be[i], 0, 0))
    grid_spec = pltpu.PrefetchScalarGridSpec(
        num_scalar_prefetch=2,
        grid=(nb,),
        in_specs=[pl.BlockSpec((MOE_ROWS, d // 2), lambda i, be, nv: (i, 0)),
                  w_spec(d, EXPERT_HIDDEN), w_spec(d, EXPERT_HIDDEN), w_spec(EXPERT_HIDDEN, d)],
        out_specs=pl.BlockSpec((MOE_ROWS, d), lambda i, be, nv: (i, 0)),
        scratch_shapes=[pltpu.VMEM((d, EXPERT_HIDDEN), _BF16), pltpu.VMEM((d, EXPERT_HIDDEN), _BF16),
                        pltpu.VMEM((EXPERT_HIDDEN, d), _BF16)],
    )
    return pl.pallas_call(
        _moe_kernel,
        grid_spec=grid_spec,
        out_shape=jax.ShapeDtypeStruct((rows, d), _F32),
        name="moe_ffn", compiler_params=_params(1),
    )(block_expert, n_valid, xs, wg, wu, wd)


def _moe(h, route, counts, w_gate, w_up, w_down, layer):
    n = h.shape[0]
    eid = route[:, 2:4].astype(jnp.int32)
    rank = route[:, 4:6].astype(jnp.int32)
    cnt = counts[0, :N_EXPERTS].astype(jnp.int32)
    padded = (cnt + MOE_ROWS - 1) // MOE_ROWS * MOE_ROWS
    pad_end = jnp.cumsum(padded)
    experts = jnp.arange(N_EXPERTS, dtype=jnp.int32)
    pad_start = jnp.sum(jnp.where(eid[:, :, None] == experts, pad_end - padded, 0), axis=-1)
    slot = pad_start + rank
    n_blocks = -(-(n * TOP_K + N_EXPERTS * (MOE_ROWS - 1)) // MOE_ROWS)
    block_start = jnp.arange(n_blocks, dtype=jnp.int32) * MOE_ROWS
    block_expert = jnp.minimum(jnp.sum(pad_end[None, :] <= block_start[:, None], axis=1),
                               N_EXPERTS - 1).astype(jnp.int32)
    n_valid = (pad_end[-1:] // MOE_ROWS).astype(jnp.int32)
    tok = jnp.broadcast_to(jnp.arange(n, dtype=jnp.int32)[:, None], (n, TOP_K))
    _, tok_by_slot = lax.sort_key_val(slot.reshape(-1), tok.reshape(-1))
    run_start = (jnp.cumsum(cnt) - cnt)[block_expert]
    run_pos = block_start - (pad_end - padded)[block_expert]
    pos = (run_pos[:, None] + jnp.arange(MOE_ROWS, dtype=jnp.int32)).reshape(-1)
    occupied = pos < jnp.repeat(cnt[block_expert], MOE_ROWS)
    src = jnp.minimum(jnp.repeat(run_start, MOE_ROWS) + pos, n * TOP_K - 1)
    slot_tok = jnp.where(occupied, tok_by_slot[src], 0)
    yb = _moe_blocks(block_expert, n_valid, h[slot_tok], w_gate, w_up, w_down, layer)
    return yb[slot[:, 0]], yb[slot[:, 1]]


def _final_kernel(x_ref, y1_ref, y2_ref, g_ref, m_ref, gain_ref, o_ref):
    g = g_ref[...]
    x = x_ref[...] + m_ref[0, 5:6, :] * (g[:, 0:1] * y1_ref[...] + g[:, 1:2] * y2_ref[...])
    o_ref[...] = _rms_rows(x) * gain_ref[...]


def _final(x, y1, y2, route, mods, gain, bsz, n_lat):
    d = x.shape[1]
    tm = ROW_TILE
    lat_tiles = n_lat // tm
    tiles = (CTX_LEN + n_lat) // tm
    src = lambda i: (i // lat_tiles) * tiles + CTX_LEN // tm + i % lat_tiles
    row = lambda width: pl.BlockSpec((tm, width), lambda i: (src(i), 0))
    return pl.pallas_call(
        _final_kernel,
        grid=(bsz * lat_tiles,),
        in_specs=[row(d), row(d), row(d), row(LANES),
                  pl.BlockSpec((1, 6, d), lambda i: ((i // lat_tiles) * 2 + 1, 0, 0)), _full((1, d))],
        out_specs=pl.BlockSpec((tm, d), lambda i: (i, 0)),
        out_shape=jax.ShapeDtypeStruct((bsz * n_lat, d), _F32),
        name="final_norm", compiler_params=_params(1),
    )(x, y1, y2, route, mods, gain.reshape(1, d))


def _rope_tables(n_lat):
    nf = HEAD_DIM // 4
    inv = ROPE_THETA ** (-jnp.arange(nf, dtype=_F32) / nf)
    rows = n_lat // GRID_W
    r_ang = jnp.repeat(jnp.arange(rows, dtype=_F32), GRID_W)[:, None] * inv
    c_ang = jnp.tile(jnp.arange(GRID_W, dtype=_F32), rows)[:, None] * inv
    cos = jnp.concatenate([jnp.cos(r_ang)] * 2 + [jnp.cos(c_ang)] * 2, axis=-1)
    sin = jnp.concatenate([-jnp.sin(r_ang), jnp.sin(r_ang), -jnp.sin(c_ang), jnp.sin(c_ang)], axis=-1)
    cos = jnp.concatenate([jnp.ones((CTX_LEN, HEAD_DIM), _F32), cos], axis=0)
    sin = jnp.concatenate([jnp.zeros((CTX_LEN, HEAD_DIM), _F32), sin], axis=0)
    return jnp.tile(cos, (1, 2)), jnp.tile(sin, (1, 2))


def kernel(x, c, ctx, c_ctx, mod_w, mod_b, ev_w_in, ev_w_out, ev_q_gain, ev_k_gain, ev_mu_prev, ev_mu_next, ev_w0, ev_w2, ev_a0, ev_a2, ev_g2, ev_k_k, ev_k_a, ev_r_k, ev_gn_w, ev_gn_b, ev_v0, ev_v1, ev_v2, od_w_in, od_w_out, od_lq1, od_lk1, od_lq2, od_lk2, od_subln, moe_w_grp, moe_b_grp, moe_w_rt, moe_b_rt, moe_w_gate, moe_w_up, moe_w_down, final_gain):
    bsz, n_lat, d = x.shape
    t = CTX_LEN + n_lat
    n = bsz * t
    cos, sin = _rope_tables(n_lat)
    cond = jax.nn.silu(jnp.concatenate([c, c_ctx[None, :], jnp.zeros((16 - bsz - 1, d), _F32)], axis=0))
    xs = jnp.concatenate([ctx, x], axis=1).reshape(n, d)
    v_first = None
    moe = None
    for layer in range(DEPTH):
        m = _mm(cond, mod_w[layer], tn=1536, precision=_HI) + mod_b[layer]
        m = m.reshape(16, 6, d)
        mods = jnp.stack([jnp.broadcast_to(m[bsz], (bsz, 6, d)), m[:bsz]], axis=1).reshape(2 * bsz, 6, d)
        w_router = jnp.concatenate(
            [moe_w_grp[layer], moe_w_rt[layer], jnp.zeros((d, LANES - N_GROUPS - N_EXPERTS), _F32)], axis=1)
        b_router = jnp.concatenate(
            [moe_b_grp[layer], moe_b_rt[layer], jnp.zeros((LANES - N_GROUPS - N_EXPERTS,), _F32)])[None, :]
        if layer % 2 == 0:
            e = layer // 2
            p = {'q_gain': ev_q_gain[e], 'k_gain': ev_k_gain[e], 'mu_prev': ev_mu_prev[e],
                 'mu_next': ev_mu_next[e], 'w0': ev_w0[e], 'w2': ev_w2[e], 'a0': ev_a0[e],
                 'a2': ev_a2[e], 'g2': ev_g2[e], 'k_k': ev_k_k[e], 'k_a': ev_k_a[e]}
            if e > 0:
                p['v0'], p['v1'], p['v2'] = ev_v0[e - 1], ev_v1[e - 1], ev_v2[e - 1]
            proj, xs = _project(xs, mods, ev_w_in[e].astype(_BF16), moe=moe)
            q, kk2, vv2, r, v, kkn, lw, a, kd, g = _prepare_even(proj, cos, sin, p, v_first)
            if v_first is None:
                v_first = v
            att = _gqa_attend(q.reshape(bsz, t, -1), kk2.reshape(bsz, t, -1), vv2.reshape(bsz, t, -1))
            seq = lambda z: z.reshape(*z.shape[:-2], bsz, t, B_WIDTH)
            y_fwd, y_bwd = (y.reshape(n, B_WIDTH) for y in
                            _delta_scan(seq(r), seq(v), seq(kkn), seq(lw), seq(a), seq(kd)))
            gn = jnp.stack([ev_gn_w[e], ev_gn_b[e], ev_r_k[e].reshape(B_WIDTH)])
            xs, h, route, counts = _mixer_out(att.reshape(n, -1), xs, mods, ev_w_out[e].astype(_BF16),
                                       w_router, b_router, rwkv=(y_fwd, y_bwd, r, v, kd, g, gn))
        else:
            o = layer // 2
            lambda_init = 0.8 - 0.6 * math.exp(-0.3 * layer)
            qkv, xs = _project(xs, mods, od_w_in[o].astype(_BF16), moe=moe, rope=(cos, sin))
            lam = (jnp.exp(jnp.sum(od_lq1[o] * od_lk1[o])) - jnp.exp(jnp.sum(od_lq2[o] * od_lk2[o]))
                   + lambda_init).reshape(1, 1)
            gain = (od_subln[o] * (1.0 - lambda_init)).reshape(1, LANES)
            att = _diff_attend(lam, qkv.reshape(bsz, t, -1), gain)
            xs, h, route, counts = _mixer_out(att.reshape(n, -1), xs, mods, od_w_out[o].astype(_BF16),
                                       w_router, b_router)
        y1, y2 = _moe(h, route, counts, moe_w_gate, moe_w_up, moe_w_down, layer)
        moe = (y1, y2, route, mods)
    return _final(xs, y1, y2, route, mods, final_gain, bsz, n_lat).reshape(bsz, n_lat, d)
```

```python
import functools
import math

import jax
import jax.numpy as jnp
from jax import lax
from jax.experimental import pallas as pl
from jax.experimental.pallas import tpu as pltpu

D_MODEL = 1024
DEPTH = 4
SEQ = 2048
GRID_W = 64
CTX_LEN = 256
HEAD_DIM = 64
ROPE_THETA = 10000.0
RMS_EPS = 1e-6
SUBLN_EPS = 1e-5
Q_SCALE = HEAD_DIM ** -0.5 * math.log2(math.e)

A_HEADS = 8
A_KV_HEADS = 2
A_WIDTH = 512
A_KV_WIDTH = 128
A_COLS = 768

B_HEADS = 8
B_WIDTH = 512
B_COLS = 1792
RWKV_W_LORA = 64
RWKV_A_LORA = 64
RWKV_V_LORA = 32
RWKV_G_LORA = 128
RWKV_GN_EPS = 64e-5
DECAY_SCALE = math.exp(-0.5)

C_HEADS = 8
C_QK = 1024

N_GROUPS = 4
EXPERTS_PER_GROUP = 8
N_EXPERTS = 32
TOP_K = 2
EXPERT_HIDDEN = 512
MOE_ROWS = 256

LANES = 128
SCAN_CHUNK = 64
SCAN_STEP_CHUNKS = 2
ROW_TILE = 256
Q_TILE = 128
VMEM_LIMIT = 56 * 1024 * 1024

_HI = lax.Precision.HIGHEST
_F32 = jnp.float32
_BF16 = jnp.bfloat16

_NN = (((1,), (0,)), ((), ()))
_NT = (((1,), (1,)), ((), ()))
_TN = (((0,), (0,)), ((), ()))


def _params(n_axes):
    return pltpu.CompilerParams(dimension_semantics=("arbitrary",) * n_axes,
                                vmem_limit_bytes=VMEM_LIMIT)


def _full(shape):
    return pl.BlockSpec(shape, lambda *_: (0,) * len(shape))


def _split2(x):
    hi = x.astype(_BF16)
    return hi, (x - hi.astype(_F32)).astype(_BF16)


def _dot3(a, b, dims=_NN):
    d = lambda u, w: lax.dot_general(u, w, dims, preferred_element_type=_F32)
    return (d(a[1], b[0]) + d(a[0], b[1])) + d(a[0], b[0])


def _dot1(a, b, dims=_NN):
    return lax.dot_general(a.astype(_BF16), b.astype(_BF16), dims, preferred_element_type=_F32)


def _pack_bf16_pairs(x):
    w = x.shape[1] // 2
    bits = pltpu.bitcast(x.astype(_BF16).astype(_F32), jnp.uint32)
    return pltpu.bitcast((bits[:, :w] >> 16) | (bits[:, w:] & jnp.uint32(0xFFFF0000)), _F32)


def _unpack_bf16_pairs(words):
    packed = pltpu.bitcast(words, jnp.uint32)
    lo = pltpu.bitcast(packed << 16, _F32)
    hi = pltpu.bitcast(packed & jnp.uint32(0xFFFF0000), _F32)
    return jnp.concatenate([lo, hi], axis=1).astype(_BF16)


def _lane_lo(shape):
    return lax.broadcasted_iota(jnp.int32, shape, len(shape) - 1) < HEAD_DIM


def _head_sum_matrix(value):
    row = lax.broadcasted_iota(jnp.int32, (LANES, LANES), 0)
    col = lax.broadcasted_iota(jnp.int32, (LANES, LANES), 1)
    return jnp.where((row // HEAD_DIM) == (col // HEAD_DIM), value, 0.0).astype(_BF16)


def _head_sum(x, mat):
    hi, lo = _split2(x)
    return jnp.dot(lo, mat, preferred_element_type=_F32) + jnp.dot(hi, mat, preferred_element_type=_F32)


def _rms_rows(x, eps=RMS_EPS):
    return x * lax.rsqrt(jnp.mean(x * x, axis=-1, keepdims=True) + eps)


def _rope_slab(x, cos, sin):
    lane = lax.broadcasted_iota(jnp.int32, x.shape, 1)
    partner = jnp.where((lane & 16) == 0, pltpu.roll(x, LANES - 16, axis=1), pltpu.roll(x, 16, axis=1))
    return x * cos + partner * sin


def _mods_index(i):
    tiles = (CTX_LEN + SEQ) // ROW_TILE
    return (i // tiles) * 2 + jnp.minimum(i % tiles, 1)


def _mm_kernel(a_ref, w_ref, o_ref, *, precision):
    o_ref[...] = jnp.dot(a_ref[...], w_ref[...], precision=precision,
                         preferred_element_type=_F32).astype(o_ref.dtype)


def _mm(a, w, *, tn, precision=None):
    m, k = a.shape
    n = w.shape[1]
    return pl.pallas_call(
        functools.partial(_mm_kernel, precision=precision),
        grid=(n // tn,),
        in_specs=[pl.BlockSpec((m, k), lambda j: (0, 0)),
                  pl.BlockSpec((k, tn), lambda j: (0, j))],
        out_specs=pl.BlockSpec((m, tn), lambda j: (0, j)),
        out_shape=jax.ShapeDtypeStruct((m, n), _F32),
        name=f"mm_{k}x{n}",
        compiler_params=_params(1),
    )(a, w)


def _proj_kernel(*refs, combine, rope):
    it = iter(refs)
    x_ref, m_ref = next(it), next(it)
    if combine:
        y1_ref, y2_ref, g_ref, mp_ref = (next(it) for _ in range(4))
    w_ref = next(it)
    if rope:
        cos_ref, sin_ref = next(it), next(it)
    o_ref = next(it)
    x = x_ref[...]
    if combine:
        xo_ref = next(it)
        g = g_ref[...]
        x = x + mp_ref[0, 5:6, :] * (g[:, 0:1] * y1_ref[...] + g[:, 1:2] * y2_ref[...])
        xo_ref[...] = x
    h = _rms_rows(x) * (1.0 + m_ref[0, 1:2, :]) + m_ref[0, 0:1, :]
    p = jnp.dot(h.astype(_BF16), w_ref[...], preferred_element_type=_F32)
    if rope:
        cos, sin = cos_ref[...], sin_ref[...]
        for s in range(2 * C_QK // LANES):
            sl = slice(s * LANES, (s + 1) * LANES)
            slab = _rope_slab(p[:, sl], cos, sin)
            if s < C_QK // LANES:
                slab = slab * Q_SCALE
            o_ref[:, sl] = slab.astype(o_ref.dtype)
        ones = jnp.ones((p.shape[0], LANES), o_ref.dtype)
        for h in range(C_HEADS):
            base = 2 * C_QK + 2 * h * LANES
            o_ref[:, base:base + LANES] = p[:, 2 * C_QK + h * LANES:2 * C_QK + (h + 1) * LANES].astype(o_ref.dtype)
            o_ref[:, base + LANES:base + 2 * LANES] = ones
    else:
        o_ref[...] = p


def _project(x, mods, w, *, moe=None, rope=None):
    n, d = x.shape
    cols = w.shape[1]
    tm = ROW_TILE
    tiles = (CTX_LEN + SEQ) // tm
    row = lambda width: pl.BlockSpec((tm, width), lambda i: (i, 0))
    mod_spec = pl.BlockSpec((1, 6, d), lambda i: (_mods_index(i), 0, 0))
    args, specs = [x, mods], [row(d), mod_spec]
    if moe is not None:
        args += list(moe)
        specs += [row(d), row(d), row(LANES), mod_spec]
    args.append(w)
    specs.append(_full(w.shape))
    if rope is not None:
        args += list(rope)
        specs += [pl.BlockSpec((tm, LANES), lambda i: (i % tiles, 0))] * 2
    if rope is not None:
        out_cols = cols + C_HEADS * LANES
        out_shape = [jax.ShapeDtypeStruct((n, out_cols), _BF16)]
    else:
        out_cols = cols
        out_shape = [jax.ShapeDtypeStruct((n, cols), _F32)]
    out_specs = [row(out_cols)]
    if moe is not None:
        out_shape.append(jax.ShapeDtypeStruct((n, d), _F32))
        out_specs.append(row(d))
    out = pl.pallas_call(
        functools.partial(_proj_kernel, combine=moe is not None, rope=rope is not None),
        grid=(n // tm,), in_specs=specs, out_specs=out_specs, out_shape=out_shape,
        name=f"project_{cols}", compiler_params=_params(1),
    )(*args)
    return out if moe is not None else (out[0], x)


def _prep_kernel(*refs, has_v_first):
    it = iter(refs)
    p_ref, pp_ref, pn_ref, cos_ref, sin_ref = (next(it) for _ in range(5))
    gain_ref, mu_ref, wa0_ref, kv_ref, wl_ref, g2_ref = (next(it) for _ in range(6))
    if has_v_first:
        v1_ref, v2_ref, vf_ref = next(it), next(it), next(it)
    q_ref, kk2_ref, vv2_ref, r_ref, v_ref, kkn_ref, k_ref, xw_ref, bonus_ref, g_ref = it

    tiles = (CTX_LEN + SEQ) // ROW_TILE
    j = pl.program_id(0) % tiles
    p = p_ref[...]
    cos, sin = cos_ref[...], sin_ref[...]
    mean_mat = _head_sum_matrix(1.0 / HEAD_DIM)
    ones_mat = _head_sum_matrix(1.0)
    lo = _lane_lo((ROW_TILE, LANES))

    def normed(x, gain):
        return x * lax.rsqrt(_head_sum(x * x, mean_mat) + RMS_EPS) * gain

    for s in range(A_WIDTH // LANES):
        sl = slice(s * LANES, (s + 1) * LANES)
        q = _rope_slab(normed(p[:, sl], gain_ref[0:1, :]), cos, sin)
        q_ref[:, sl] = (q * Q_SCALE).astype(_BF16)
    k = _rope_slab(normed(p[:, A_WIDTH:A_WIDTH + LANES], gain_ref[1:2, :]), cos, sin)
    v = p[:, A_WIDTH + LANES:A_COLS]
    swapped = pltpu.roll(k, HEAD_DIM, axis=1)
    kk2_ref[:, :LANES] = jnp.where(lo, k, swapped).astype(_BF16)
    kk2_ref[:, LANES:] = jnp.where(lo, swapped, k).astype(_BF16)
    vv2_ref[:, :LANES] = jnp.where(lo, v, 1.0).astype(_BF16)
    vv2_ref[:, LANES:] = jnp.where(lo, pltpu.roll(v, HEAD_DIM, axis=1), 1.0).astype(_BF16)

    u = p[:, A_COLS:]
    rid = lax.broadcasted_iota(jnp.int32, (ROW_TILE, 1), 0)
    has_prev = jnp.where(j >= 2, 1.0, 0.0)
    has_next = jnp.where((j >= 1) & (j < tiles - 1), 1.0, 0.0)
    first = pp_ref[7:8, A_COLS:] * has_prev
    last = pn_ref[0:1, A_COLS:] * has_next
    prev = jnp.where(rid == 0, first, pltpu.roll(u, 1, axis=0))
    nxt = jnp.where(rid == ROW_TILE - 1, last, pltpu.roll(u, ROW_TILE - 1, axis=0))
    u = u + mu_ref[0:1, :] * (prev - u) + mu_ref[1:2, :] * (nxt - u)
    r, k, v = u[:, :B_WIDTH], u[:, B_WIDTH:2 * B_WIDTH], u[:, 2 * B_WIDTH:3 * B_WIDTH]
    wa, glr = u[:, 3 * B_WIDTH:3 * B_WIDTH + LANES], u[:, 3 * B_WIDTH + LANES:]
    if has_v_first:
        mid = jnp.dot(v.astype(_BF16), v1_ref[...], preferred_element_type=_F32)
        lora = jnp.dot(mid.astype(_BF16), v2_ref[...], preferred_element_type=_F32)
        v = v + (vf_ref[...] - v) * jax.nn.sigmoid(kv_ref[2:3, :] + lora)
    r_ref[...] = r
    v_ref[...] = v
    kks = k * kv_ref[0:1, :]
    for s in range(B_WIDTH // LANES):
        sl = slice(s * LANES, (s + 1) * LANES)
        x = kks[:, sl]
        kkn_ref[:, sl] = x * lax.rsqrt(jnp.maximum(_head_sum(x * x, ones_mat), 1e-24))
    xw = jnp.where(lo, jnp.tanh(wa), wa).astype(_BF16)
    k_ref[...] = k
    xw_ref[...] = xw
    k_both = (_rwkv_gates(xw, k, wl_ref, wa0_ref, kv_ref[1:2, :], 0)[2]
              + _rwkv_gates(xw, k, wl_ref, wa0_ref, kv_ref[1:2, :], 1)[2])
    rk = r * k_both * kv_ref[3:4, :]
    for s in range(B_WIDTH // LANES):
        sl = slice(s * LANES, (s + 1) * LANES)
        bonus_ref[:, sl] = _head_sum(rk[:, sl], ones_mat) * v[:, sl]
    g_ref[...] = jnp.dot(jax.nn.sigmoid(glr).astype(_BF16), g2_ref[...], preferred_element_type=_F32)


def _rwkv_gates(xw, k, wl_ref, wa0_ref, k_a, direction):
    z = jnp.dot(xw, wl_ref[direction], preferred_element_type=_F32) + wa0_ref[direction:direction + 1, :]
    a = jax.nn.sigmoid(z[:, B_WIDTH:])
    lw = -DECAY_SCALE * jax.nn.sigmoid(z[:, :B_WIDTH])
    return lw, a, k * (1.0 + (a - 1.0) * k_a)


def _prepare_even(proj, cos, sin, p, v_first):
    n, cols = proj.shape
    tm = ROW_TILE
    tiles = (CTX_LEN + SEQ) // tm
    halo = 8
    per_tile = tm // halo
    row = lambda width: pl.BlockSpec((tm, width), lambda i: (i, 0))
    tab = pl.BlockSpec((tm, LANES), lambda i: (i % tiles, 0))
    pad = lambda w, rows, width: jnp.zeros((rows, width), _BF16).at[:w.shape[0], :w.shape[1]].set(w.astype(_BF16))
    gains = jnp.stack([jnp.tile(p['q_gain'], 2), jnp.tile(p['k_gain'], 2)])
    mu = jnp.stack([p['mu_prev'], p['mu_next']])
    wa0 = jnp.concatenate([p['w0'], p['a0']], axis=1)
    kvec = jnp.stack([p['k_k'], p['k_a'], p.get('v0', jnp.zeros((B_WIDTH,), _F32)), p['r_k']])
    zero = jnp.zeros((RWKV_W_LORA, B_WIDTH), _F32)
    wl = jnp.stack([jnp.concatenate([jnp.concatenate([p['w2'][dr], zero], axis=1),
                                     jnp.concatenate([zero, p['a2'][dr]], axis=1)], axis=0)
                    for dr in range(2)]).astype(_BF16)
    args = [proj, proj, proj, cos, sin, gains, mu, wa0, kvec, wl, p['g2'].astype(_BF16)]
    specs = [row(cols),
             pl.BlockSpec((halo, cols), lambda i: (jnp.maximum(i * per_tile - 1, 0), 0)),
             pl.BlockSpec((halo, cols), lambda i: (jnp.minimum((i + 1) * per_tile, n // halo - 1), 0)),
             tab, tab, _full(gains.shape), _full(mu.shape), _full(wa0.shape), _full(kvec.shape),
             _full(wl.shape), _full(p['g2'].shape)]
    if v_first is not None:
        v1 = pad(p['v1'], B_WIDTH, LANES)
        v2 = pad(p['v2'], LANES, B_WIDTH)
        args += [v1, v2, v_first]
        specs += [_full(v1.shape), _full(v2.shape), row(B_WIDTH)]
    f32 = lambda *shape: jax.ShapeDtypeStruct(shape, _F32)
    bf16 = lambda *shape: jax.ShapeDtypeStruct(shape, _BF16)
    out_shape = [bf16(n, A_WIDTH), bf16(n, 2 * LANES), bf16(n, 2 * LANES),
                 f32(n, B_WIDTH), f32(n, B_WIDTH), f32(n, B_WIDTH),
                 f32(n, B_WIDTH), bf16(n, LANES), f32(n, B_WIDTH), f32(n, B_WIDTH)]
    out_specs = [row(A_WIDTH), row(2 * LANES), row(2 * LANES), row(B_WIDTH), row(B_WIDTH), row(B_WIDTH),
                 row(B_WIDTH), row(LANES), row(B_WIDTH), row(B_WIDTH)]
    outs = pl.pallas_call(
        functools.partial(_prep_kernel, has_v_first=v_first is not None),
        grid=(n // tm,), in_specs=specs, out_specs=out_specs, out_shape=out_shape,
        name="prepare_even", compiler_params=_params(1),
    )(*args)
    return outs, (wl, wa0, kvec[1:2])


def _split_heads(q2):
    lo = _lane_lo(q2.shape)
    zero = jnp.zeros_like(q2)
    return jnp.concatenate([jnp.where(lo, q2, zero), jnp.where(lo, zero, q2)], axis=0)


def _scores(qs, k):
    return lax.dot_general(qs, k, _NT, preferred_element_type=_F32)


def _interleave(chains):
    results = {}
    while len(results) < len(chains):
        for p, chain in enumerate(chains):
            try:
                next(chain)
            except StopIteration as stop:
                results[p] = stop.value
    return [results[p] for p in range(len(chains))]


def _softmax_pv(s, v_ones):
    p = jnp.exp2(s - jnp.max(s, axis=-1, keepdims=True)).astype(_BF16)
    yield
    return jnp.dot(p, v_ones, preferred_element_type=_F32)


def _gqa_tile(q, k, v):
    tq = q.shape[0]
    s = _scores(_split_heads(q), k)
    yield
    o = yield from _softmax_pv(s, v)
    o = o / pltpu.roll(o, HEAD_DIM, axis=1)
    return jnp.where(_lane_lo((tq, LANES)), o[:tq], pltpu.roll(o[tq:], HEAD_DIM, axis=1))


def _attend_chains(make_chain, q_ref, o_ref):
    pieces = [(slice(i * Q_TILE, (i + 1) * Q_TILE), j, slice(j * LANES, (j + 1) * LANES))
              for i in range(q_ref.shape[1] // Q_TILE) for j in range(q_ref.shape[2] // LANES)]
    outs = _interleave([make_chain(q_ref[0, rows, lanes], j) for rows, j, lanes in pieces])
    for (rows, _, lanes), o in zip(pieces, outs):
        o_ref[0, rows, lanes] = o.astype(o_ref.dtype)


def _gqa_kernel(q_ref, k_ref, v_ref, o_ref):
    is_ctx = pl.program_id(2) < CTX_LEN // q_ref.shape[1]

    def run(tk):
        _attend_chains(lambda q, slab: _gqa_tile(q, k_ref[0, :tk], v_ref[0, :tk]), q_ref, o_ref)

    pl.when(is_ctx)(lambda: run(CTX_LEN))
    pl.when(jnp.logical_not(is_ctx))(lambda: run(k_ref.shape[1]))


def _gqa_attend(q, kk, vv):
    b, t, _ = q.shape
    tq = 2 * Q_TILE
    kv_spec = pl.BlockSpec((1, t, LANES), lambda i, g, j: (i, 0, g))
    return pl.pallas_call(
        _gqa_kernel,
        grid=(b, A_KV_HEADS, t // tq),
        in_specs=[pl.BlockSpec((1, tq, 2 * LANES), lambda i, g, j: (i, j, g)), kv_spec, kv_spec],
        out_specs=pl.BlockSpec((1, tq, 2 * LANES), lambda i, g, j: (i, j, g)),
        out_shape=jax.ShapeDtypeStruct((b, t, A_WIDTH), _BF16),
        name="gqa_attn", compiler_params=_params(3),
    )(q, kk, vv)


def _diff_tile(lam, q, k, v, gain):
    tq = q.shape[0]
    s = _scores(_split_heads(q), k)
    yield
    o = yield from _softmax_pv(s, v)
    o = o[:, :LANES] / o[:, LANES:]
    o = o[:tq] - lam * o[tq:]
    y = o * lax.rsqrt(jnp.mean(o * o, axis=-1, keepdims=True) + SUBLN_EPS)
    return y * gain


def _diff_kernel(lam_ref, q_ref, k_ref, v_ref, g_ref, o_ref):
    is_ctx = pl.program_id(2) < CTX_LEN // q_ref.shape[1]
    lam, gain = lam_ref[0, 0], g_ref[...]

    def run(tk):
        def chain(q, slab):
            lanes = slice(slab * LANES, (slab + 1) * LANES)
            v_lanes = slice(2 * slab * LANES, 2 * (slab + 1) * LANES)
            return _diff_tile(lam, q, k_ref[0, :tk, lanes], v_ref[0, :tk, v_lanes], gain)
        _attend_chains(chain, q_ref, o_ref)

    pl.when(is_ctx)(lambda: run(CTX_LEN))
    pl.when(jnp.logical_not(is_ctx))(lambda: run(k_ref.shape[1]))


def _diff_attend(lam, qkv, gain):
    b, t, _ = qkv.shape
    tq = 2 * Q_TILE
    pairs = C_HEADS // 2
    return pl.pallas_call(
        _diff_kernel,
        grid=(b, pairs, t // tq),
        in_specs=[pl.BlockSpec(memory_space=pltpu.SMEM),
                  pl.BlockSpec((1, tq, 2 * LANES), lambda i, h, j: (i, j, h)),
                  pl.BlockSpec((1, t, 2 * LANES), lambda i, h, j: (i, 0, pairs + h)),
                  pl.BlockSpec((1, t, 4 * LANES), lambda i, h, j: (i, 0, pairs + h)),
                  _full((1, LANES))],
        out_specs=pl.BlockSpec((1, tq, 2 * LANES), lambda i, h, j: (i, j, h)),
        out_shape=jax.ShapeDtypeStruct((b, t, C_HEADS * LANES), _BF16),
        name="diff_attn", compiler_params=_params(3),
    )(lam, qkv, qkv, qkv, gain)


def _scan_chunk(get_h, r, v, kk, lw, cum, cum_end, a, kd, sgn):
    c = SCAN_CHUNK
    lo = _lane_lo((c, LANES))
    row = lax.broadcasted_iota(jnp.int32, (LANES, LANES), 0)
    col = lax.broadcasted_iota(jnp.int32, (LANES, LANES), 1)
    delta = ((col & (c - 1)) - (row & (c - 1))) * sgn
    before = delta < 0
    upto = delta <= 0
    eye = row == col
    zero = jnp.zeros((LANES, LANES), _F32)

    def blockdiag(x):
        z = jnp.zeros_like(x)
        return jnp.concatenate([jnp.where(lo, x, z), jnp.where(lo, z, x)], axis=0)

    p_inv = jnp.exp(-cum)
    p_end = jnp.exp(cum_end - cum)
    kka = kk * a
    a_rs = blockdiag(-kk * jnp.exp(cum - lw))
    r_rs = blockdiag(r * jnp.exp(cum))
    b_rs = blockdiag(kka * p_inv)
    k_rs = blockdiag(kd * p_inv)
    v_rs = blockdiag(v)
    bp_rs = blockdiag(kka * p_end)
    kp_rs = blockdiag(kd * p_end)

    bk = _split2(jnp.concatenate([b_rs, k_rs], axis=0))
    za = lax.dot_general(a_rs.astype(_BF16), bk[0], _NT, preferred_element_type=_F32)
    zr = _dot3(_split2(r_rs), bk, _NT)
    yield
    low = jnp.where(before, za[:, :LANES], zero)
    g = jnp.where(before, za[:, LANES:], zero)
    rb = jnp.where(upto, zr[:, :LANES], zero)
    rk = jnp.where(upto, zr[:, LANES:], zero)

    x = _dot1(low, low)
    yield
    m = jnp.where(eye, 1.0, 0.0).astype(_F32) + low
    n = 2
    while 2 * n < c:
        res = _dot1(x, jnp.concatenate([m, x], axis=1))
        m = m + res[:, :LANES]
        x = res[:, LANES:]
        n *= 2
        yield
    m = m + _dot1(x, m)
    yield
    gv = _dot1(g, v_rs)
    yield
    au = _dot1(m, jnp.concatenate([a_rs, gv], axis=1))
    yield
    rhs = _split2(jnp.concatenate(
        [au, jnp.concatenate([zero, v_rs], axis=1)], axis=0))
    tf = lax.dot_general(jnp.concatenate([bp_rs, kp_rs], axis=0).astype(_BF16), rhs[0], _TN,
                         preferred_element_type=_F32)
    qy = _dot3(_split2(jnp.concatenate([rb, rk], axis=1)), rhs)
    yield
    t = jnp.where(eye, jnp.broadcast_to(jnp.exp(cum_end), (LANES, LANES)), zero) + tf[:, :LANES]
    q = r_rs + qy[:, :LANES]
    out = _dot1(jnp.concatenate([q, t], axis=0), get_h())
    y = out[:LANES] + qy[:, LANES:]
    h_new = out[LANES:] + tf[:, LANES:]
    return y[:c] + y[c:], h_new


def _recording(chain, key, store):
    store[key] = yield from chain
    return store[key]


def _scan_kernel(*refs):
    c = SCAN_CHUNK
    ins, (wl_ref, wa0_ref, ka_ref), y_refs, h_ref = refs[:10], refs[10:13], refs[13:15], refs[15]

    @pl.when(pl.program_id(1) == 0)
    def _():
        h_ref[...] = jnp.zeros_like(h_ref)

    tr = lax.broadcasted_iota(jnp.int32, (c, c), 0)
    tc = lax.broadcasted_iota(jnp.int32, (c, c), 1)
    chains, done = [], {}
    for step in range(SCAN_STEP_CHUNKS):
        for d, sgn in enumerate((1, -1)):
            r_ref, v_ref, kk_ref, k_ref, xw_ref = ins[5 * d:5 * d + 5]
            sub = step if d == 0 else SCAN_STEP_CHUNKS - 1 - step
            rows = slice(sub * c, (sub + 1) * c)
            tri = jnp.where((tc - tr) * sgn <= 0, 1.0, 0.0).astype(_BF16)
            lw, a, kd = _rwkv_gates(xw_ref[0, rows], k_ref[0, rows], wl_ref, wa0_ref, ka_ref[...], d)
            l1 = lw.astype(_BF16)
            rem = lw - l1.astype(_F32)
            l2 = rem.astype(_BF16)
            l3 = (rem - l2.astype(_F32)).astype(_BF16)
            dot = lambda w, tri=tri: jnp.dot(tri, w, preferred_element_type=_F32)
            cum = (dot(l3) + dot(l2)) + dot(l1)
            cum_end = jnp.sum(lw, axis=0, keepdims=True)
            pairs = lw.shape[1] // LANES
            for p in range(pairs):
                sl = slice(p * LANES, (p + 1) * LANES)
                slot = d * pairs + p
                if step == 0:
                    get_h = lambda slot=slot: h_ref[slot]
                else:
                    get_h = lambda key=(step - 1, slot): done[key][1]
                chain = _scan_chunk(get_h, r_ref[0, rows, sl], v_ref[0, rows, sl], kk_ref[0, rows, sl],
                                    lw[:, sl], cum[:, sl], cum_end[:, sl], a[:, sl], kd[:, sl], sgn)
                chains.append(_recording(chain, (step, slot), done))
    _interleave(chains)
    for (step, slot), (y, h) in done.items():
        d, p = divmod(slot, h_ref.shape[0] // 2)
        sub = step if d == 0 else SCAN_STEP_CHUNKS - 1 - step
        y_refs[d][0, sub * c:(sub + 1) * c, p * LANES:(p + 1) * LANES] = y
        if step == SCAN_STEP_CHUNKS - 1:
            h_ref[slot] = h


def _delta_scan(r, v, kk, k, xw, gates):
    b, t, w = r.shape
    rows = SCAN_CHUNK * SCAN_STEP_CHUNKS
    n_blocks = t // rows
    n_ctx = CTX_LEN // rows

    def block(d, j):
        return j if d == 0 else jnp.where(j < n_ctx, n_ctx - 1 - j, n_blocks + n_ctx - 1 - j)

    specs, args = [], []
    for d in range(2):
        wide = pl.BlockSpec((1, rows, w), lambda i, j, d=d: (i, block(d, j), 0))
        narrow = pl.BlockSpec((1, rows, LANES), lambda i, j, d=d: (i, block(d, j), 0))
        specs += [wide, wide, wide, wide, narrow]
        args += [r, v, kk, k, xw]
    specs += [_full(g.shape) for g in gates]
    args += list(gates)
    out_specs = [pl.BlockSpec((1, rows, w), lambda i, j, d=d: (i, block(d, j), 0)) for d in range(2)]
    return pl.pallas_call(
        _scan_kernel,
        grid=(b, n_blocks),
        in_specs=specs,
        out_specs=out_specs,
        out_shape=[jax.ShapeDtypeStruct((b, t, w), _F32)] * 2,
        scratch_shapes=[pltpu.VMEM((2 * w // LANES, LANES, LANES), _F32)],
        name="delta_scan", compiler_params=_params(2),
    )(*args)


def _post_kernel(*refs, even):
    it = iter(refs)
    if even:
        yf_ref, yb_ref, bonus_ref, g_ref, gn_ref = (next(it) for _ in range(5))
    att_ref, x_ref, m_ref, wo_ref, wrh_ref, wrl_ref, br_ref = (next(it) for _ in range(7))
    xo_ref, h_ref, route_ref, count_ref, carry_ref = it
    if even:
        mean_mat = _head_sum_matrix(1.0 / HEAD_DIM)
        parts = [att_ref[...]]
        for s in range(B_WIDTH // LANES):
            sl = slice(s * LANES, (s + 1) * LANES)
            y = yf_ref[:, sl] + yb_ref[:, sl]
            dev = y - _head_sum(y, mean_mat)
            var = _head_sum(dev * dev, mean_mat)
            y = dev * lax.rsqrt(var + RWKV_GN_EPS) * gn_ref[0:1, sl] + gn_ref[1:2, sl] + bonus_ref[:, sl]
            parts.append((y * g_ref[:, sl]).astype(_BF16))
        mixed = jnp.concatenate(parts, axis=1)
    else:
        mixed = att_ref[...]
    out = jnp.dot(mixed, wo_ref[...], preferred_element_type=_F32)
    x = x_ref[...] + m_ref[0, 2:3, :] * out
    xo_ref[...] = x
    h = _rms_rows(x) * (1.0 + m_ref[0, 4:5, :]) + m_ref[0, 3:4, :]
    h_ref[...] = _pack_bf16_pairs(h)
    logits = _dot3(_split2(h), (wrh_ref[...], wrl_ref[...])) + br_ref[...]
    _route(logits, route_ref, count_ref, carry_ref)


def _route(logits, route_ref, count_ref, carry_ref):
    tm = logits.shape[0]
    lane = lax.broadcasted_iota(jnp.int32, (tm, LANES), 1)
    neg = -1e30
    first_lane = lambda hit: jnp.min(jnp.where(hit, lane, LANES), axis=-1, keepdims=True)

    is_grp = lane < N_GROUPS
    grp = jnp.where(is_grp, logits, neg)
    g_max = jnp.max(grp, axis=-1, keepdims=True)
    g_sel = first_lane(grp == g_max)
    grp_w = 1.0 / jnp.sum(jnp.where(is_grp, jnp.exp(grp - g_max), 0.0), axis=-1, keepdims=True)

    base = N_GROUPS + EXPERTS_PER_GROUP * g_sel
    in_grp = (lane >= base) & (lane < base + EXPERTS_PER_GROUP)
    ex = jnp.where(in_grp, logits, neg)
    e_max = jnp.max(ex, axis=-1, keepdims=True)
    pe = jnp.where(in_grp, jnp.exp(ex - e_max), 0.0)
    prob = pe / jnp.sum(pe, axis=-1, keepdims=True)
    p1 = jnp.max(prob, axis=-1, keepdims=True)
    l1 = first_lane(in_grp & (prob == p1))
    rest = jnp.where(in_grp & (lane != l1), prob, -1.0)
    p2 = jnp.max(rest, axis=-1, keepdims=True)
    l2 = first_lane(rest == p2)
    scale = grp_w / (p1 + p2)
    e1, e2 = l1 - N_GROUPS, l2 - N_GROUPS

    @pl.when(pl.program_id(0) == 0)
    def _():
        carry_ref[...] = jnp.zeros_like(carry_ref)

    oh1 = jnp.where(lane == e1, 1.0, 0.0)
    oh2 = jnp.where(lane == e2, 1.0, 0.0)
    rr = lax.broadcasted_iota(jnp.int32, (tm, tm), 0)
    cc = lax.broadcasted_iota(jnp.int32, (tm, tm), 1)
    tri = jnp.where(cc < rr, 1.0, 0.0).astype(_BF16)
    before = lambda oh: jnp.dot(tri, oh.astype(_BF16), preferred_element_type=_F32)
    carry = carry_ref[...]
    tot1 = jnp.sum(oh1, axis=0, keepdims=True)
    rank1 = jnp.sum(oh1 * (before(oh1) + carry), axis=-1, keepdims=True)
    rank2 = jnp.sum(oh2 * (before(oh2) + (carry + tot1)), axis=-1, keepdims=True)
    carry = carry + tot1 + jnp.sum(oh2, axis=0, keepdims=True)
    carry_ref[...] = carry
    count_ref[...] = carry

    out = jnp.zeros((tm, LANES), _F32)
    for i, val in enumerate((p1 * scale, p2 * scale, e1.astype(_F32), e2.astype(_F32), rank1, rank2)):
        out = jnp.where(lane == i, val, out)
    route_ref[...] = out


def _mixer_out(att, x, mods, w_out, w_router, b_router, rwkv=None):
    n, d = x.shape
    tm = ROW_TILE
    row = lambda width: pl.BlockSpec((tm, width), lambda i: (i, 0))
    args, specs = [], []
    if rwkv is not None:
        y_fwd, y_bwd, bonus, g, gn = rwkv
        args += [y_fwd, y_bwd, bonus, g, gn]
        specs += [row(B_WIDTH)] * 4 + [_full(gn.shape)]
    wr_hi, wr_lo = _split2(w_router)
    args += [att, x, mods, w_out, wr_hi, wr_lo, b_router]
    specs += [row(att.shape[1]), row(d), pl.BlockSpec((1, 6, d), lambda i: (_mods_index(i), 0, 0)),
              _full(w_out.shape), _full(wr_hi.shape), _full(wr_lo.shape), _full(b_router.shape)]
    return pl.pallas_call(
        functools.partial(_post_kernel, even=rwkv is not None),
        grid=(n // tm,), in_specs=specs,
        out_specs=[row(d), row(d // 2), row(LANES), _full((1, LANES))],
        out_shape=[jax.ShapeDtypeStruct((n, d), _F32), jax.ShapeDtypeStruct((n, d // 2), _F32),
                   jax.ShapeDtypeStruct((n, LANES), _F32), jax.ShapeDtypeStruct((1, LANES), _F32)],
        scratch_shapes=[pltpu.VMEM((1, LANES), _F32)],
        name="mixer_out_even" if rwkv is not None else "mixer_out_odd", compiler_params=_params(1),
    )(*args)


def _moe_kernel(be_ref, nv_ref, x_ref, wg_ref, wu_ref, wd_ref, o_ref, wg_s, wu_s, wd_s):
    i = pl.program_id(0)
    valid = i < nv_ref[0]
    new_expert = (i == 0) | (be_ref[i] != be_ref[jnp.maximum(i - 1, 0)])

    @pl.when(valid & new_expert)
    def _():
        wg_s[...] = wg_ref[0, 0].astype(_BF16)
        wu_s[...] = wu_ref[0, 0].astype(_BF16)
        wd_s[...] = wd_ref[0, 0].astype(_BF16)

    @pl.when(valid)
    def _():
        x = _unpack_bf16_pairs(x_ref[...])

        def hidden_half(cols):
            gate = jnp.dot(x, wg_s[:, cols], preferred_element_type=_F32)
            up = jnp.dot(x, wu_s[:, cols], preferred_element_type=_F32)
            yield
            hid = (gate * jax.nn.sigmoid(gate) * up).astype(_BF16)
            yield
            return jnp.dot(hid, wd_s[cols, :], preferred_element_type=_F32)

        half = EXPERT_HIDDEN // 2
        lo, hi = _interleave([hidden_half(slice(0, half)), hidden_half(slice(half, EXPERT_HIDDEN))])
        o_ref[...] = lo + hi

    @pl.when(jnp.logical_not(valid))
    def _():
        o_ref[...] = jnp.zeros_like(o_ref)


def _moe_blocks(block_expert, n_valid, xs, wg, wu, wd, layer):
    rows, d = xs.shape[0], 2 * xs.shape[1]
    nb = rows // MOE_ROWS
    w_spec = lambda a, b: pl.BlockSpec((1, 1, a, b), lambda i, be, nv: (layer, be[i], 0, 0))
    grid_spec = pltpu.PrefetchScalarGridSpec(
        num_scalar_prefetch=2,
        grid=(nb,),
        in_specs=[pl.BlockSpec((MOE_ROWS, d // 2), lambda i, be, nv: (i, 0)),
                  w_spec(d, EXPERT_HIDDEN), w_spec(d, EXPERT_HIDDEN), w_spec(EXPERT_HIDDEN, d)],
        out_specs=pl.BlockSpec((MOE_ROWS, d), lambda i, be, nv: (i, 0)),
        scratch_shapes=[pltpu.VMEM((d, EXPERT_HIDDEN), _BF16), pltpu.VMEM((d, EXPERT_HIDDEN), _BF16),
                        pltpu.VMEM((EXPERT_HIDDEN, d), _BF16)],
    )
    return pl.pallas_call(
        _moe_kernel,
        grid_spec=grid_spec,
        out_shape=jax.ShapeDtypeStruct((rows, d), _F32),
        name="moe_ffn", compiler_params=_params(1),
    )(block_expert, n_valid, xs, wg, wu, wd)


def _moe(h, route, counts, w_gate, w_up, w_down, layer):
    n = h.shape[0]
    eid = route[:, 2:4].astype(jnp.int32)
    rank = route[:, 4:6].astype(jnp.int32)
    cnt = counts[0, :N_EXPERTS].astype(jnp.int32)
    padded = (cnt + MOE_ROWS - 1) // MOE_ROWS * MOE_ROWS
    pad_end = jnp.cumsum(padded)
    experts = jnp.arange(N_EXPERTS, dtype=jnp.int32)
    pad_start = jnp.sum(jnp.where(eid[:, :, None] == experts, pad_end - padded, 0), axis=-1)
    slot = pad_start + rank
    n_blocks = -(-(n * TOP_K + N_EXPERTS * (MOE_ROWS - 1)) // MOE_ROWS)
    block_start = jnp.arange(n_blocks, dtype=jnp.int32) * MOE_ROWS
    block_expert = jnp.minimum(jnp.sum(pad_end[None, :] <= block_start[:, None], axis=1),
                               N_EXPERTS - 1).astype(jnp.int32)
    n_valid = (pad_end[-1:] // MOE_ROWS).astype(jnp.int32)
    tok = jnp.broadcast_to(jnp.arange(n, dtype=jnp.int32)[:, None], (n, TOP_K))
    _, tok_by_slot = lax.sort_key_val(slot.reshape(-1), tok.reshape(-1))
    run_start = (jnp.cumsum(cnt) - cnt)[block_expert]
    run_pos = block_start - (pad_end - padded)[block_expert]
    pos = (run_pos[:, None] + jnp.arange(MOE_ROWS, dtype=jnp.int32)).reshape(-1)
    occupied = pos < jnp.repeat(cnt[block_expert], MOE_ROWS)
    src = jnp.minimum(jnp.repeat(run_start, MOE_ROWS) + pos, n * TOP_K - 1)
    slot_tok = jnp.where(occupied, tok_by_slot[src], 0)
    yb = _moe_blocks(block_expert, n_valid, h[slot_tok], w_gate, w_up, w_down, layer)
    return yb[slot[:, 0]], yb[slot[:, 1]]


def _final_kernel(x_ref, y1_ref, y2_ref, g_ref, m_ref, gain_ref, o_ref):
    g = g_ref[...]
    x = x_ref[...] + m_ref[0, 5:6, :] * (g[:, 0:1] * y1_ref[...] + g[:, 1:2] * y2_ref[...])
    o_ref[...] = _rms_rows(x) * gain_ref[...]


def _final(x, y1, y2, route, mods, gain, bsz, n_lat):
    d = x.shape[1]
    tm = ROW_TILE
    lat_tiles = n_lat // tm
    tiles = (CTX_LEN + n_lat) // tm
    src = lambda i: (i // lat_tiles) * tiles + CTX_LEN // tm + i % lat_tiles
    row = lambda width: pl.BlockSpec((tm, width), lambda i: (src(i), 0))
    return pl.pallas_call(
        _final_kernel,
        grid=(bsz * lat_tiles,),
        in_specs=[row(d), row(d), row(d), row(LANES),
                  pl.BlockSpec((1, 6, d), lambda i: ((i // lat_tiles) * 2 + 1, 0, 0)), _full((1, d))],
        out_specs=pl.BlockSpec((tm, d), lambda i: (i, 0)),
        out_shape=jax.ShapeDtypeStruct((bsz * n_lat, d), _F32),
        name="final_norm", compiler_params=_params(1),
    )(x, y1, y2, route, mods, gain.reshape(1, d))


def _rope_tables(n_lat):
    nf = HEAD_DIM // 4
    inv = ROPE_THETA ** (-jnp.arange(nf, dtype=_F32) / nf)
    rows = n_lat // GRID_W
    r_ang = jnp.repeat(jnp.arange(rows, dtype=_F32), GRID_W)[:, None] * inv
    c_ang = jnp.tile(jnp.arange(GRID_W, dtype=_F32), rows)[:, None] * inv
    cos = jnp.concatenate([jnp.cos(r_ang)] * 2 + [jnp.cos(c_ang)] * 2, axis=-1)
    sin = jnp.concatenate([-jnp.sin(r_ang), jnp.sin(r_ang), -jnp.sin(c_ang), jnp.sin(c_ang)], axis=-1)
    cos = jnp.concatenate([jnp.ones((CTX_LEN, HEAD_DIM), _F32), cos], axis=0)
    sin = jnp.concatenate([jnp.zeros((CTX_LEN, HEAD_DIM), _F32), sin], axis=0)
    return jnp.tile(cos, (1, 2)), jnp.tile(sin, (1, 2))


def kernel(x, c, ctx, c_ctx, mod_w, mod_b, ev_w_in, ev_w_out, ev_q_gain, ev_k_gain, ev_mu_prev, ev_mu_next, ev_w0, ev_w2, ev_a0, ev_a2, ev_g2, ev_k_k, ev_k_a, ev_r_k, ev_gn_w, ev_gn_b, ev_v0, ev_v1, ev_v2, od_w_in, od_w_out, od_lq1, od_lk1, od_lq2, od_lk2, od_subln, moe_w_grp, moe_b_grp, moe_w_rt, moe_b_rt, moe_w_gate, moe_w_up, moe_w_down, final_gain):
    bsz, n_lat, d = x.shape
    t = CTX_LEN + n_lat
    n = bsz * t
    cos, sin = _rope_tables(n_lat)
    cond = jax.nn.silu(jnp.concatenate([c, c_ctx[None, :], jnp.zeros((16 - bsz - 1, d), _F32)], axis=0))
    xs = jnp.concatenate([ctx, x], axis=1).reshape(n, d)
    v_first = None
    moe = None
    for layer in range(DEPTH):
        m = _mm(cond, mod_w[layer], tn=1536, precision=_HI) + mod_b[layer]
        m = m.reshape(16, 6, d)
        mods = jnp.stack([jnp.broadcast_to(m[bsz], (bsz, 6, d)), m[:bsz]], axis=1).reshape(2 * bsz, 6, d)
        w_router = jnp.concatenate(
            [moe_w_grp[layer], moe_w_rt[layer], jnp.zeros((d, LANES - N_GROUPS - N_EXPERTS), _F32)], axis=1)
        b_router = jnp.concatenate(
            [moe_b_grp[layer], moe_b_rt[layer], jnp.zeros((LANES - N_GROUPS - N_EXPERTS,), _F32)])[None, :]
        if layer % 2 == 0:
            e = layer // 2
            p = {'q_gain': ev_q_gain[e], 'k_gain': ev_k_gain[e], 'mu_prev': ev_mu_prev[e],
                 'mu_next': ev_mu_next[e], 'w0': ev_w0[e], 'w2': ev_w2[e], 'a0': ev_a0[e],
                 'a2': ev_a2[e], 'g2': ev_g2[e], 'k_k': ev_k_k[e], 'k_a': ev_k_a[e],
                 'r_k': ev_r_k[e].reshape(B_WIDTH)}
            if e > 0:
                p['v0'], p['v1'], p['v2'] = ev_v0[e - 1], ev_v1[e - 1], ev_v2[e - 1]
            proj, xs = _project(xs, mods, ev_w_in[e].astype(_BF16), moe=moe)
            (q, kk2, vv2, r, v, kkn, k, xw, bonus, g), gates = _prepare_even(proj, cos, sin, p, v_first)
            if v_first is None:
                v_first = v
            att = _gqa_attend(q.reshape(bsz, t, -1), kk2.reshape(bsz, t, -1), vv2.reshape(bsz, t, -1))
            seq = lambda z: z.reshape(bsz, t, z.shape[-1])
            y_fwd, y_bwd = (y.reshape(n, B_WIDTH) for y in
                            _delta_scan(seq(r), seq(v), seq(kkn), seq(k), seq(xw), gates))
            gn = jnp.stack([ev_gn_w[e], ev_gn_b[e]])
            xs, h, route, counts = _mixer_out(att.reshape(n, -1), xs, mods, ev_w_out[e].astype(_BF16),
                                       w_router, b_router, rwkv=(y_fwd, y_bwd, bonus, g, gn))
        else:
            o = layer // 2
            lambda_init = 0.8 - 0.6 * math.exp(-0.3 * layer)
            qkv, xs = _project(xs, mods, od_w_in[o].astype(_BF16), moe=moe, rope=(cos, sin))
            lam = (jnp.exp(jnp.sum(od_lq1[o] * od_lk1[o])) - jnp.exp(jnp.sum(od_lq2[o] * od_lk2[o]))
                   + lambda_init).reshape(1, 1)
            gain = (od_subln[o] * (1.0 - lambda_init)).reshape(1, LANES)
            att = _diff_attend(lam, qkv.reshape(bsz, t, -1), gain)
            xs, h, route, counts = _mixer_out(att.reshape(n, -1), xs, mods, od_w_out[o].astype(_BF16),
                                       w_router, b_router)
        y1, y2 = _moe(h, route, counts, moe_w_gate, moe_w_up, moe_w_down, layer)
        moe = (y1, y2, route, mods)
    return _final(xs, y1, y2, route, mods, final_gain, bsz, n_lat).reshape(bsz, n_lat, d)
```

```python
import functools
import math

import jax
import jax.numpy as jnp
from jax import lax
from jax.experimental import pallas as pl
from jax.experimental.pallas import tpu as pltpu

D_MODEL = 1024
DEPTH = 4
SEQ = 2048
GRID_W = 64
CTX_LEN = 256
HEAD_DIM = 64
ROPE_THETA = 10000.0
RMS_EPS = 1e-6
SUBLN_EPS = 1e-5
Q_SCALE = HEAD_DIM ** -0.5 * math.log2(math.e)

A_HEADS = 8
A_KV_HEADS = 2
A_WIDTH = 512
A_KV_WIDTH = 128
A_COLS = 768

B_HEADS = 8
B_WIDTH = 512
B_COLS = 1792
RWKV_W_LORA = 64
RWKV_A_LORA = 64
RWKV_V_LORA = 32
RWKV_G_LORA = 128
RWKV_GN_EPS = 64e-5
DECAY_SCALE = math.exp(-0.5)

C_HEADS = 8
C_QK = 1024

N_GROUPS = 4
EXPERTS_PER_GROUP = 8
N_EXPERTS = 32
TOP_K = 2
EXPERT_HIDDEN = 512
MOE_ROWS = 256

LANES = 128
SCAN_CHUNK = 64
SCAN_STEP_CHUNKS = 2
ROW_TILE = 256
Q_TILE = 128
VMEM_LIMIT = 56 * 1024 * 1024

_HI = lax.Precision.HIGHEST
_F32 = jnp.float32
_BF16 = jnp.bfloat16

_NN = (((1,), (0,)), ((), ()))
_NT = (((1,), (1,)), ((), ()))
_TN = (((0,), (0,)), ((), ()))


def _params(n_axes):
    return pltpu.CompilerParams(dimension_semantics=("arbitrary",) * n_axes,
                                vmem_limit_bytes=VMEM_LIMIT)


def _full(shape):
    return pl.BlockSpec(shape, lambda *_: (0,) * len(shape))


def _split2(x):
    hi = x.astype(_BF16)
    return hi, (x - hi.astype(_F32)).astype(_BF16)


def _dot3(a, b, dims=_NN):
    d = lambda u, w: lax.dot_general(u, w, dims, preferred_element_type=_F32)
    return (d(a[1], b[0]) + d(a[0], b[1])) + d(a[0], b[0])


def _dot1(a, b, dims=_NN):
    return lax.dot_general(a.astype(_BF16), b.astype(_BF16), dims, preferred_element_type=_F32)


def _pack_bf16_pairs(x):
    w = x.shape[1] // 2
    bits = pltpu.bitcast(x.astype(_BF16).astype(_F32), jnp.uint32)
    return pltpu.bitcast((bits[:, :w] >> 16) | (bits[:, w:] & jnp.uint32(0xFFFF0000)), _F32)


def _unpack_bf16_pairs(words):
    packed = pltpu.bitcast(words, jnp.uint32)
    lo = pltpu.bitcast(packed << 16, _F32)
    hi = pltpu.bitcast(packed & jnp.uint32(0xFFFF0000), _F32)
    return jnp.concatenate([lo, hi], axis=1).astype(_BF16)


def _lane_lo(shape):
    return lax.broadcasted_iota(jnp.int32, shape, len(shape) - 1) < HEAD_DIM


def _head_sum_matrix(value):
    row = lax.broadcasted_iota(jnp.int32, (LANES, LANES), 0)
    col = lax.broadcasted_iota(jnp.int32, (LANES, LANES), 1)
    return jnp.where((row // HEAD_DIM) == (col // HEAD_DIM), value, 0.0).astype(_BF16)


def _head_sum(x, mat):
    hi, lo = _split2(x)
    return jnp.dot(lo, mat, preferred_element_type=_F32) + jnp.dot(hi, mat, preferred_element_type=_F32)


def _rms_rows(x, eps=RMS_EPS):
    return x * lax.rsqrt(jnp.mean(x * x, axis=-1, keepdims=True) + eps)


def _rope_slab(x, cos, sin):
    lane = lax.broadcasted_iota(jnp.int32, x.shape, 1)
    partner = jnp.where((lane & 16) == 0, pltpu.roll(x, LANES - 16, axis=1), pltpu.roll(x, 16, axis=1))
    return x * cos + partner * sin


def _mods_index(i):
    tiles = (CTX_LEN + SEQ) // ROW_TILE
    return (i // tiles) * 2 + jnp.minimum(i % tiles, 1)


def _mm_kernel(a_ref, w_ref, o_ref, *, precision):
    o_ref[...] = jnp.dot(a_ref[...], w_ref[...], precision=precision,
                         preferred_element_type=_F32).astype(o_ref.dtype)


def _mm(a, w, *, tn, precision=None):
    m, k = a.shape
    n = w.shape[1]
    return pl.pallas_call(
        functools.partial(_mm_kernel, precision=precision),
        grid=(n // tn,),
        in_specs=[pl.BlockSpec((m, k), lambda j: (0, 0)),
                  pl.BlockSpec((k, tn), lambda j: (0, j))],
        out_specs=pl.BlockSpec((m, tn), lambda j: (0, j)),
        out_shape=jax.ShapeDtypeStruct((m, n), _F32),
        name=f"mm_{k}x{n}",
        compiler_params=_params(1),
    )(a, w)


def _proj_kernel(*refs, combine, rope):
    it = iter(refs)
    x_ref, m_ref = next(it), next(it)
    if combine:
        y1_ref, y2_ref, g_ref, mp_ref = (next(it) for _ in range(4))
    w_ref = next(it)
    if rope:
        cos_ref, sin_ref = next(it), next(it)
    o_ref = next(it)
    x = x_ref[...]
    if combine:
        xo_ref = next(it)
        g = g_ref[...]
        x = x + mp_ref[0, 5:6, :] * (g[:, 0:1] * y1_ref[...] + g[:, 1:2] * y2_ref[...])
        xo_ref[...] = x
    h = _rms_rows(x) * (1.0 + m_ref[0, 1:2, :]) + m_ref[0, 0:1, :]
    p = jnp.dot(h.astype(_BF16), w_ref[...], preferred_element_type=_F32)
    if rope:
        cos, sin = cos_ref[...], sin_ref[...]
        for s in range(2 * C_QK // LANES):
            sl = slice(s * LANES, (s + 1) * LANES)
            slab = _rope_slab(p[:, sl], cos, sin)
            if s < C_QK // LANES:
                slab = slab * Q_SCALE
            o_ref[:, sl] = slab.astype(o_ref.dtype)
        ones = jnp.ones((p.shape[0], LANES), o_ref.dtype)
        for h in range(C_HEADS):
            base = 2 * C_QK + 2 * h * LANES
            o_ref[:, base:base + LANES] = p[:, 2 * C_QK + h * LANES:2 * C_QK + (h + 1) * LANES].astype(o_ref.dtype)
            o_ref[:, base + LANES:base + 2 * LANES] = ones
    else:
        o_ref[...] = p


def _project(x, mods, w, *, moe=None, rope=None):
    n, d = x.shape
    cols = w.shape[1]
    tm = ROW_TILE
    tiles = (CTX_LEN + SEQ) // tm
    row = lambda width: pl.BlockSpec((tm, width), lambda i: (i, 0))
    mod_spec = pl.BlockSpec((1, 6, d), lambda i: (_mods_index(i), 0, 0))
    args, specs = [x, mods], [row(d), mod_spec]
    if moe is not None:
        args += list(moe)
        second = pl.BlockSpec((tm, d), lambda i: (i + n // tm, 0))
        specs += [row(d), second, row(LANES), mod_spec]
    args.append(w)
    specs.append(_full(w.shape))
    if rope is not None:
        args += list(rope)
        specs += [pl.BlockSpec((tm, LANES), lambda i: (i % tiles, 0))] * 2
    if rope is not None:
        out_cols = cols + C_HEADS * LANES
        out_shape = [jax.ShapeDtypeStruct((n, out_cols), _BF16)]
    else:
        out_cols = cols
        out_shape = [jax.ShapeDtypeStruct((n, cols), _F32)]
    out_specs = [row(out_cols)]
    if moe is not None:
        out_shape.append(jax.ShapeDtypeStruct((n, d), _F32))
        out_specs.append(row(d))
    out = pl.pallas_call(
        functools.partial(_proj_kernel, combine=moe is not None, rope=rope is not None),
        grid=(n // tm,), in_specs=specs, out_specs=out_specs, out_shape=out_shape,
        name=f"project_{cols}", compiler_params=_params(1),
    )(*args)
    return out if moe is not None else (out[0], x)


def _prep_kernel(*refs, has_v_first):
    it = iter(refs)
    p_ref, pp_ref, pn_ref, cos_ref, sin_ref = (next(it) for _ in range(5))
    gain_ref, mu_ref, wa0_ref, kv_ref, wl_ref, g2_ref = (next(it) for _ in range(6))
    if has_v_first:
        v1_ref, v2_ref, vf_ref = next(it), next(it), next(it)
    q_ref, kk2_ref, vv2_ref, r_ref, v_ref, kkn_ref, lw_ref, a_ref, kd_ref, g_ref = it

    tiles = (CTX_LEN + SEQ) // ROW_TILE
    j = pl.program_id(0) % tiles
    p = p_ref[...]
    cos, sin = cos_ref[...], sin_ref[...]
    mean_mat = _head_sum_matrix(1.0 / HEAD_DIM)
    ones_mat = _head_sum_matrix(1.0)
    lo = _lane_lo((ROW_TILE, LANES))

    def normed(x, gain):
        return x * lax.rsqrt(_head_sum(x * x, mean_mat) + RMS_EPS) * gain

    for s in range(A_WIDTH // LANES):
        sl = slice(s * LANES, (s + 1) * LANES)
        q = _rope_slab(normed(p[:, sl], gain_ref[0:1, :]), cos, sin)
        q_ref[:, sl] = (q * Q_SCALE).astype(_BF16)
    k = _rope_slab(normed(p[:, A_WIDTH:A_WIDTH + LANES], gain_ref[1:2, :]), cos, sin)
    v = p[:, A_WIDTH + LANES:A_COLS]
    swapped = pltpu.roll(k, HEAD_DIM, axis=1)
    kk2_ref[:, :LANES] = jnp.where(lo, k, swapped).astype(_BF16)
    kk2_ref[:, LANES:] = jnp.where(lo, swapped, k).astype(_BF16)
    vv2_ref[:, :LANES] = jnp.where(lo, v, 1.0).astype(_BF16)
    vv2_ref[:, LANES:] = jnp.where(lo, pltpu.roll(v, HEAD_DIM, axis=1), 1.0).astype(_BF16)

    u = p[:, A_COLS:]
    rid = lax.broadcasted_iota(jnp.int32, (ROW_TILE, 1), 0)
    has_prev = jnp.where(j >= 2, 1.0, 0.0)
    has_next = jnp.where((j >= 1) & (j < tiles - 1), 1.0, 0.0)
    first = pp_ref[7:8, A_COLS:] * has_prev
    last = pn_ref[0:1, A_COLS:] * has_next
    prev = jnp.where(rid == 0, first, pltpu.roll(u, 1, axis=0))
    nxt = jnp.where(rid == ROW_TILE - 1, last, pltpu.roll(u, ROW_TILE - 1, axis=0))
    u = u + mu_ref[0:1, :] * (prev - u) + mu_ref[1:2, :] * (nxt - u)
    r, k, v = u[:, :B_WIDTH], u[:, B_WIDTH:2 * B_WIDTH], u[:, 2 * B_WIDTH:3 * B_WIDTH]
    wa, glr = u[:, 3 * B_WIDTH:3 * B_WIDTH + LANES], u[:, 3 * B_WIDTH + LANES:]
    if has_v_first:
        mid = jnp.dot(v.astype(_BF16), v1_ref[...], preferred_element_type=_F32)
        lora = jnp.dot(mid.astype(_BF16), v2_ref[...], preferred_element_type=_F32)
        v = v + (vf_ref[...] - v) * jax.nn.sigmoid(kv_ref[2:3, :] + lora)
    r_ref[...] = r
    v_ref[...] = v
    kks = k * kv_ref[0:1, :]
    for s in range(B_WIDTH // LANES):
        sl = slice(s * LANES, (s + 1) * LANES)
        x = kks[:, sl]
        kkn_ref[:, sl] = x * lax.rsqrt(jnp.maximum(_head_sum(x * x, ones_mat), 1e-24))
    xw = jnp.where(lo, jnp.tanh(wa), wa).astype(_BF16)
    for dr in range(2):
        z = jnp.dot(xw, wl_ref[dr], preferred_element_type=_F32) + wa0_ref[dr:dr + 1, :]
        a = jax.nn.sigmoid(z[:, B_WIDTH:])
        lw_ref[dr] = -DECAY_SCALE * jax.nn.sigmoid(z[:, :B_WIDTH])
        a_ref[dr] = a
        kd_ref[dr] = k * (1.0 + (a - 1.0) * kv_ref[1:2, :])
    g_ref[...] = jnp.dot(jax.nn.sigmoid(glr).astype(_BF16), g2_ref[...], preferred_element_type=_F32)


def _prepare_even(proj, cos, sin, p, v_first):
    n, cols = proj.shape
    tm = ROW_TILE
    tiles = (CTX_LEN + SEQ) // tm
    halo = 8
    per_tile = tm // halo
    row = lambda width: pl.BlockSpec((tm, width), lambda i: (i, 0))
    both = pl.BlockSpec((2, tm, B_WIDTH), lambda i: (0, i, 0))
    tab = pl.BlockSpec((tm, LANES), lambda i: (i % tiles, 0))
    pad = lambda w, rows, width: jnp.zeros((rows, width), _BF16).at[:w.shape[0], :w.shape[1]].set(w.astype(_BF16))
    gains = jnp.stack([jnp.tile(p['q_gain'], 2), jnp.tile(p['k_gain'], 2)])
    mu = jnp.stack([p['mu_prev'], p['mu_next']])
    wa0 = jnp.concatenate([p['w0'], p['a0']], axis=1)
    kvec = jnp.stack([p['k_k'], p['k_a'], p.get('v0', jnp.zeros((B_WIDTH,), _F32))])
    zero = jnp.zeros((RWKV_W_LORA, B_WIDTH), _F32)
    wl = jnp.stack([jnp.concatenate([jnp.concatenate([p['w2'][dr], zero], axis=1),
                                     jnp.concatenate([zero, p['a2'][dr]], axis=1)], axis=0)
                    for dr in range(2)]).astype(_BF16)
    args = [proj, proj, proj, cos, sin, gains, mu, wa0, kvec, wl, p['g2'].astype(_BF16)]
    specs = [row(cols),
             pl.BlockSpec((halo, cols), lambda i: (jnp.maximum(i * per_tile - 1, 0), 0)),
             pl.BlockSpec((halo, cols), lambda i: (jnp.minimum((i + 1) * per_tile, n // halo - 1), 0)),
             tab, tab, _full(gains.shape), _full(mu.shape), _full(wa0.shape), _full(kvec.shape),
             _full(wl.shape), _full(p['g2'].shape)]
    if v_first is not None:
        v1 = pad(p['v1'], B_WIDTH, LANES)
        v2 = pad(p['v2'], LANES, B_WIDTH)
        args += [v1, v2, v_first]
        specs += [_full(v1.shape), _full(v2.shape), row(B_WIDTH)]
    f32 = lambda *shape: jax.ShapeDtypeStruct(shape, _F32)
    bf16 = lambda *shape: jax.ShapeDtypeStruct(shape, _BF16)
    out_shape = [bf16(n, A_WIDTH), bf16(n, 2 * LANES), bf16(n, 2 * LANES),
                 f32(n, B_WIDTH), f32(n, B_WIDTH), f32(n, B_WIDTH),
                 f32(2, n, B_WIDTH), f32(2, n, B_WIDTH), f32(2, n, B_WIDTH), f32(n, B_WIDTH)]
    out_specs = [row(A_WIDTH), row(2 * LANES), row(2 * LANES), row(B_WIDTH), row(B_WIDTH), row(B_WIDTH),
                 both, both, both, row(B_WIDTH)]
    return pl.pallas_call(
        functools.partial(_prep_kernel, has_v_first=v_first is not None),
        grid=(n // tm,), in_specs=specs, out_specs=out_specs, out_shape=out_shape,
        name="prepare_even", compiler_params=_params(1),
    )(*args)


def _split_heads(q2):
    lo = _lane_lo(q2.shape)
    zero = jnp.zeros_like(q2)
    return jnp.concatenate([jnp.where(lo, q2, zero), jnp.where(lo, zero, q2)], axis=0)


def _scores(qs, k):
    return lax.dot_general(qs, k, _NT, preferred_element_type=_F32)


def _interleave(chains):
    results = {}
    while len(results) < len(chains):
        for p, chain in enumerate(chains):
            try:
                next(chain)
            except StopIteration as stop:
                results[p] = stop.value
    return [results[p] for p in range(len(chains))]


def _softmax_pv(s, v_ones):
    p = jnp.exp2(s - jnp.max(s, axis=-1, keepdims=True)).astype(_BF16)
    yield
    return jnp.dot(p, v_ones, preferred_element_type=_F32)


def _gqa_tile(q, k, v):
    tq = q.shape[0]
    s = _scores(_split_heads(q), k)
    yield
    o = yield from _softmax_pv(s, v)
    o = o / pltpu.roll(o, HEAD_DIM, axis=1)
    return jnp.where(_lane_lo((tq, LANES)), o[:tq], pltpu.roll(o[tq:], HEAD_DIM, axis=1))


def _attend_chains(make_chain, q_ref, o_ref):
    pieces = [(slice(i * Q_TILE, (i + 1) * Q_TILE), j, slice(j * LANES, (j + 1) * LANES))
              for i in range(q_ref.shape[1] // Q_TILE) for j in range(q_ref.shape[2] // LANES)]
    outs = _interleave([make_chain(q_ref[0, rows, lanes], j) for rows, j, lanes in pieces])
    for (rows, _, lanes), o in zip(pieces, outs):
        o_ref[0, rows, lanes] = o.astype(o_ref.dtype)


def _gqa_kernel(q_ref, k_ref, v_ref, o_ref):
    is_ctx = pl.program_id(2) < CTX_LEN // q_ref.shape[1]

    def run(tk):
        _attend_chains(lambda q, slab: _gqa_tile(q, k_ref[0, :tk], v_ref[0, :tk]), q_ref, o_ref)

    pl.when(is_ctx)(lambda: run(CTX_LEN))
    pl.when(jnp.logical_not(is_ctx))(lambda: run(k_ref.shape[1]))


def _gqa_attend(q, kk, vv):
    b, t, _ = q.shape
    tq = 2 * Q_TILE
    kv_spec = pl.BlockSpec((1, t, LANES), lambda i, g, j: (i, 0, g))
    return pl.pallas_call(
        _gqa_kernel,
        grid=(b, A_KV_HEADS, t // tq),
        in_specs=[pl.BlockSpec((1, tq, 2 * LANES), lambda i, g, j: (i, j, g)), kv_spec, kv_spec],
        out_specs=pl.BlockSpec((1, tq, 2 * LANES), lambda i, g, j: (i, j, g)),
        out_shape=jax.ShapeDtypeStruct((b, t, A_WIDTH), _BF16),
        name="gqa_attn", compiler_params=_params(3),
    )(q, kk, vv)


def _diff_tile(lam, q, k, v, gain):
    tq = q.shape[0]
    s = _scores(_split_heads(q), k)
    yield
    o = yield from _softmax_pv(s, v)
    o = o[:, :LANES] / o[:, LANES:]
    o = o[:tq] - lam * o[tq:]
    y = o * lax.rsqrt(jnp.mean(o * o, axis=-1, keepdims=True) + SUBLN_EPS)
    return y * gain


def _diff_kernel(lam_ref, q_ref, k_ref, v_ref, g_ref, o_ref):
    is_ctx = pl.program_id(2) < CTX_LEN // q_ref.shape[1]
    lam, gain = lam_ref[0, 0], g_ref[...]

    def run(tk):
        def chain(q, slab):
            lanes = slice(slab * LANES, (slab + 1) * LANES)
            v_lanes = slice(2 * slab * LANES, 2 * (slab + 1) * LANES)
            return _diff_tile(lam, q, k_ref[0, :tk, lanes], v_ref[0, :tk, v_lanes], gain)
        _attend_chains(chain, q_ref, o_ref)

    pl.when(is_ctx)(lambda: run(CTX_LEN))
    pl.when(jnp.logical_not(is_ctx))(lambda: run(k_ref.shape[1]))


def _diff_attend(lam, qkv, gain):
    b, t, _ = qkv.shape
    tq = 2 * Q_TILE
    pairs = C_HEADS // 2
    return pl.pallas_call(
        _diff_kernel,
        grid=(b, pairs, t // tq),
        in_specs=[pl.BlockSpec(memory_space=pltpu.SMEM),
                  pl.BlockSpec((1, tq, 2 * LANES), lambda i, h, j: (i, j, h)),
                  pl.BlockSpec((1, t, 2 * LANES), lambda i, h, j: (i, 0, pairs + h)),
                  pl.BlockSpec((1, t, 4 * LANES), lambda i, h, j: (i, 0, pairs + h)),
                  _full((1, LANES))],
        out_specs=pl.BlockSpec((1, tq, 2 * LANES), lambda i, h, j: (i, j, h)),
        out_shape=jax.ShapeDtypeStruct((b, t, C_HEADS * LANES), _BF16),
        name="diff_attn", compiler_params=_params(3),
    )(lam, qkv, qkv, qkv, gain)


def _scan_chunk(get_h, r, v, kk, lw, cum, cum_end, a, kd, sgn):
    c = SCAN_CHUNK
    lo = _lane_lo((c, LANES))
    row = lax.broadcasted_iota(jnp.int32, (LANES, LANES), 0)
    col = lax.broadcasted_iota(jnp.int32, (LANES, LANES), 1)
    delta = ((col & (c - 1)) - (row & (c - 1))) * sgn
    before = delta < 0
    upto = delta <= 0
    eye = row == col
    zero = jnp.zeros((LANES, LANES), _F32)

    def blockdiag(x):
        z = jnp.zeros_like(x)
        return jnp.concatenate([jnp.where(lo, x, z), jnp.where(lo, z, x)], axis=0)

    p_inv = jnp.exp(-cum)
    p_end = jnp.exp(cum_end - cum)
    kka = kk * a
    a_rs = blockdiag(-kk * jnp.exp(cum - lw))
    r_rs = blockdiag(r * jnp.exp(cum))
    b_rs = blockdiag(kka * p_inv)
    k_rs = blockdiag(kd * p_inv)
    v_rs = blockdiag(v)
    bp_rs = blockdiag(kka * p_end)
    kp_rs = blockdiag(kd * p_end)

    bk = _split2(jnp.concatenate([b_rs, k_rs], axis=0))
    za = lax.dot_general(a_rs.astype(_BF16), bk[0], _NT, preferred_element_type=_F32)
    zr = _dot3(_split2(r_rs), bk, _NT)
    yield
    low = jnp.where(before, za[:, :LANES], zero)
    g = jnp.where(before, za[:, LANES:], zero)
    rb = jnp.where(upto, zr[:, :LANES], zero)
    rk = jnp.where(upto, zr[:, LANES:], zero)

    x = _dot1(low, low)
    yield
    m = jnp.where(eye, 1.0, 0.0).astype(_F32) + low
    n = 2
    while 2 * n < c:
        res = _dot1(x, jnp.concatenate([m, x], axis=1))
        m = m + res[:, :LANES]
        x = res[:, LANES:]
        n *= 2
        yield
    m = m + _dot1(x, m)
    yield
    gv = _dot1(g, v_rs)
    yield
    au = _dot1(m, jnp.concatenate([a_rs, gv], axis=1))
    yield
    rhs = _split2(jnp.concatenate(
        [au, jnp.concatenate([zero, v_rs], axis=1)], axis=0))
    tf = lax.dot_general(jnp.concatenate([bp_rs, kp_rs], axis=0).astype(_BF16), rhs[0], _TN,
                         preferred_element_type=_F32)
    qy = _dot3(_split2(jnp.concatenate([rb, rk], axis=1)), rhs)
    yield
    t = jnp.where(eye, jnp.broadcast_to(jnp.exp(cum_end), (LANES, LANES)), zero) + tf[:, :LANES]
    q = r_rs + qy[:, :LANES]
    out = _dot1(jnp.concatenate([q, t], axis=0), get_h())
    y = out[:LANES] + qy[:, LANES:]
    h_new = out[LANES:] + tf[:, LANES:]
    return y[:c] + y[c:], h_new


def _recording(chain, key, store):
    store[key] = yield from chain
    return store[key]


def _scan_kernel(*refs):
    c = SCAN_CHUNK
    ins, y_refs, h_ref = refs[:12], refs[12:14], refs[14]

    @pl.when(pl.program_id(1) == 0)
    def _():
        h_ref[...] = jnp.zeros_like(h_ref)

    tr = lax.broadcasted_iota(jnp.int32, (c, c), 0)
    tc = lax.broadcasted_iota(jnp.int32, (c, c), 1)
    chains, done = [], {}
    for step in range(SCAN_STEP_CHUNKS):
        for d, sgn in enumerate((1, -1)):
            r_ref, v_ref, kk_ref, lw_ref, a_ref, kd_ref = ins[6 * d:6 * d + 6]
            sub = step if d == 0 else SCAN_STEP_CHUNKS - 1 - step
            rows = slice(sub * c, (sub + 1) * c)
            tri = jnp.where((tc - tr) * sgn <= 0, 1.0, 0.0).astype(_BF16)
            lw = lw_ref[0, 0, rows]
            l1 = lw.astype(_BF16)
            rem = lw - l1.astype(_F32)
            l2 = rem.astype(_BF16)
            l3 = (rem - l2.astype(_F32)).astype(_BF16)
            dot = lambda w, tri=tri: jnp.dot(tri, w, preferred_element_type=_F32)
            cum = (dot(l3) + dot(l2)) + dot(l1)
            cum_end = jnp.sum(lw, axis=0, keepdims=True)
            pairs = lw.shape[1] // LANES
            for p in range(pairs):
                sl = slice(p * LANES, (p + 1) * LANES)
                slot = d * pairs + p
                if step == 0:
                    get_h = lambda slot=slot: h_ref[slot]
                else:
                    get_h = lambda key=(step - 1, slot): done[key][1]
                chain = _scan_chunk(get_h, r_ref[0, rows, sl], v_ref[0, rows, sl], kk_ref[0, rows, sl],
                                    lw[:, sl], cum[:, sl], cum_end[:, sl],
                                    a_ref[0, 0, rows, sl], kd_ref[0, 0, rows, sl], sgn)
                chains.append(_recording(chain, (step, slot), done))
    _interleave(chains)
    for (step, slot), (y, h) in done.items():
        d, p = divmod(slot, h_ref.shape[0] // 2)
        sub = step if d == 0 else SCAN_STEP_CHUNKS - 1 - step
        y_refs[d][0, sub * c:(sub + 1) * c, p * LANES:(p + 1) * LANES] = y
        if step == SCAN_STEP_CHUNKS - 1:
            h_ref[slot] = h


def _delta_scan(r, v, kk, lw, a, kd):
    b, t, w = r.shape
    rows = SCAN_CHUNK * SCAN_STEP_CHUNKS
    n_blocks = t // rows
    n_ctx = CTX_LEN // rows

    def block(d, j):
        return j if d == 0 else jnp.where(j < n_ctx, n_ctx - 1 - j, n_blocks + n_ctx - 1 - j)

    specs, args = [], []
    for d in range(2):
        shared = pl.BlockSpec((1, rows, w), lambda i, j, d=d: (i, block(d, j), 0))
        per_dir = pl.BlockSpec((1, 1, rows, w), lambda i, j, d=d: (d, i, block(d, j), 0))
        specs += [shared, shared, shared, per_dir, per_dir, per_dir]
        args += [r, v, kk, lw, a, kd]
    out_specs = [pl.BlockSpec((1, rows, w), lambda i, j, d=d: (i, block(d, j), 0)) for d in range(2)]
    return pl.pallas_call(
        _scan_kernel,
        grid=(b, n_blocks),
        in_specs=specs,
        out_specs=out_specs,
        out_shape=[jax.ShapeDtypeStruct((b, t, w), _F32)] * 2,
        scratch_shapes=[pltpu.VMEM((2 * w // LANES, LANES, LANES), _F32)],
        name="delta_scan", compiler_params=_params(2),
    )(*args)


def _post_kernel(*refs, even):
    it = iter(refs)
    if even:
        yf_ref, yb_ref, r_ref, v_ref, kd_ref, g_ref, gn_ref = (next(it) for _ in range(7))
    att_ref, x_ref, m_ref, wo_ref, wrh_ref, wrl_ref, br_ref = (next(it) for _ in range(7))
    xo_ref, h_ref, route_ref, count_ref, carry_ref = it
    if even:
        mean_mat = _head_sum_matrix(1.0 / HEAD_DIM)
        ones_mat = _head_sum_matrix(1.0)
        parts = [att_ref[...]]
        for s in range(B_WIDTH // LANES):
            sl = slice(s * LANES, (s + 1) * LANES)
            y = yf_ref[:, sl] + yb_ref[:, sl]
            dev = y - _head_sum(y, mean_mat)
            var = _head_sum(dev * dev, mean_mat)
            y = dev * lax.rsqrt(var + RWKV_GN_EPS) * gn_ref[0:1, sl] + gn_ref[1:2, sl]
            r, v = r_ref[:, sl], v_ref[:, sl]
            for dr in range(2):
                y = y + _head_sum(r * kd_ref[dr, :, sl] * gn_ref[2:3, sl], ones_mat) * v
            parts.append((y * g_ref[:, sl]).astype(_BF16))
        mixed = jnp.concatenate(parts, axis=1)
    else:
        mixed = att_ref[...]
    out = jnp.dot(mixed, wo_ref[...], preferred_element_type=_F32)
    x = x_ref[...] + m_ref[0, 2:3, :] * out
    xo_ref[...] = x
    h = _rms_rows(x) * (1.0 + m_ref[0, 4:5, :]) + m_ref[0, 3:4, :]
    h_ref[...] = _pack_bf16_pairs(h)
    logits = _dot3(_split2(h), (wrh_ref[...], wrl_ref[...])) + br_ref[...]
    _route(logits, route_ref, count_ref, carry_ref)


def _route(logits, route_ref, count_ref, carry_ref):
    tm = logits.shape[0]
    lane = lax.broadcasted_iota(jnp.int32, (tm, LANES), 1)
    neg = -1e30
    first_lane = lambda hit: jnp.min(jnp.where(hit, lane, LANES), axis=-1, keepdims=True)

    is_grp = lane < N_GROUPS
    grp = jnp.where(is_grp, logits, neg)
    g_max = jnp.max(grp, axis=-1, keepdims=True)
    g_sel = first_lane(grp == g_max)
    grp_w = 1.0 / jnp.sum(jnp.where(is_grp, jnp.exp(grp - g_max), 0.0), axis=-1, keepdims=True)

    base = N_GROUPS + EXPERTS_PER_GROUP * g_sel
    in_grp = (lane >= base) & (lane < base + EXPERTS_PER_GROUP)
    ex = jnp.where(in_grp, logits, neg)
    e_max = jnp.max(ex, axis=-1, keepdims=True)
    pe = jnp.where(in_grp, jnp.exp(ex - e_max), 0.0)
    prob = pe / jnp.sum(pe, axis=-1, keepdims=True)
    p1 = jnp.max(prob, axis=-1, keepdims=True)
    l1 = first_lane(in_grp & (prob == p1))
    rest = jnp.where(in_grp & (lane != l1), prob, -1.0)
    p2 = jnp.max(rest, axis=-1, keepdims=True)
    l2 = first_lane(rest == p2)
    scale = grp_w / (p1 + p2)
    e1, e2 = l1 - N_GROUPS, l2 - N_GROUPS

    @pl.when(pl.program_id(0) == 0)
    def _():
        carry_ref[...] = jnp.zeros_like(carry_ref)

    oh1 = jnp.where(lane == e1, 1.0, 0.0)
    oh2 = jnp.where(lane == e2, 1.0, 0.0)
    rr = lax.broadcasted_iota(jnp.int32, (tm, tm), 0)
    cc = lax.broadcasted_iota(jnp.int32, (tm, tm), 1)
    tri = jnp.where(cc < rr, 1.0, 0.0).astype(_BF16)
    before = lambda oh: jnp.dot(tri, oh.astype(_BF16), preferred_element_type=_F32)
    carry = carry_ref[...]
    tot1 = jnp.sum(oh1, axis=0, keepdims=True)
    rank1 = jnp.sum(oh1 * (before(oh1) + carry), axis=-1, keepdims=True)
    rank2 = jnp.sum(oh2 * (before(oh2) + (carry + tot1)), axis=-1, keepdims=True)
    carry = carry + tot1 + jnp.sum(oh2, axis=0, keepdims=True)
    carry_ref[...] = carry
    count_ref[...] = carry

    out = jnp.zeros((tm, LANES), _F32)
    for i, val in enumerate((p1 * scale, p2 * scale, e1.astype(_F32), e2.astype(_F32), rank1, rank2)):
        out = jnp.where(lane == i, val, out)
    route_ref[...] = out


def _mixer_out(att, x, mods, w_out, w_router, b_router, rwkv=None):
    n, d = x.shape
    tm = ROW_TILE
    row = lambda width: pl.BlockSpec((tm, width), lambda i: (i, 0))
    both = pl.BlockSpec((2, tm, B_WIDTH), lambda i: (0, i, 0))
    args, specs = [], []
    if rwkv is not None:
        y_fwd, y_bwd, r, v, kd, g, gn = rwkv
        args += [y_fwd, y_bwd, r, v, kd, g, gn]
        specs += [row(B_WIDTH)] * 4 + [both, row(B_WIDTH), _full(gn.shape)]
    wr_hi, wr_lo = _split2(w_router)
    args += [att, x, mods, w_out, wr_hi, wr_lo, b_router]
    specs += [row(att.shape[1]), row(d), pl.BlockSpec((1, 6, d), lambda i: (_mods_index(i), 0, 0)),
              _full(w_out.shape), _full(wr_hi.shape), _full(wr_lo.shape), _full(b_router.shape)]
    return pl.pallas_call(
        functools.partial(_post_kernel, even=rwkv is not None),
        grid=(n // tm,), in_specs=specs,
        out_specs=[row(d), row(d // 2), row(LANES), _full((1, LANES))],
        out_shape=[jax.ShapeDtypeStruct((n, d), _F32), jax.ShapeDtypeStruct((n, d // 2), _F32),
                   jax.ShapeDtypeStruct((n, LANES), _F32), jax.ShapeDtypeStruct((1, LANES), _F32)],
        scratch_shapes=[pltpu.VMEM((1, LANES), _F32)],
        name="mixer_out_even" if rwkv is not None else "mixer_out_odd", compiler_params=_params(1),
    )(*args)


def _moe_kernel(be_ref, nv_ref, x_ref, wg_ref, wu_ref, wd_ref, o_ref, wg_s, wu_s, wd_s):
    i = pl.program_id(0)
    valid = i < nv_ref[0]
    new_expert = (i == 0) | (be_ref[i] != be_ref[jnp.maximum(i - 1, 0)])

    @pl.when(valid & new_expert)
    def _():
        wg_s[...] = wg_ref[0, 0].astype(_BF16)
        wu_s[...] = wu_ref[0, 0].astype(_BF16)
        wd_s[...] = wd_ref[0, 0].astype(_BF16)

    @pl.when(valid)
    def _():
        x = _unpack_bf16_pairs(x_ref[...])

        def hidden_half(cols):
            gate = jnp.dot(x, wg_s[:, cols], preferred_element_type=_F32)
            up = jnp.dot(x, wu_s[:, cols], preferred_element_type=_F32)
            yield
            hid = (gate * jax.nn.sigmoid(gate) * up).astype(_BF16)
            yield
            return jnp.dot(hid, wd_s[cols, :], preferred_element_type=_F32)

        half = EXPERT_HIDDEN // 2
        lo, hi = _interleave([hidden_half(slice(0, half)), hidden_half(slice(half, EXPERT_HIDDEN))])
        o_ref[...] = lo + hi

    @pl.when(jnp.logical_not(valid))
    def _():
        o_ref[...] = jnp.zeros_like(o_ref)


def _moe_blocks(block_expert, n_valid, xs, wg, wu, wd, layer):
    rows, d = xs.shape[0], 2 * xs.shape[1]
    nb = rows // MOE_ROWS
    w_spec = lambda a, b: pl.BlockSpec((1, 1, a, b), lambda i, be, nv: (layer, be[i], 0, 0))
    grid_spec = pltpu.PrefetchScalarGridSpec(
        num_scalar_prefetch=2,
        grid=(nb,),
        in_specs=[pl.BlockSpec((MOE_ROWS, d // 2), lambda i, be, nv: (i, 0)),
                  w_spec(d, EXPERT_HIDDEN), w_spec(d, EXPERT_HIDDEN), w_spec(EXPERT_HIDDEN, d)],
        out_specs=pl.BlockSpec((MOE_ROWS, d), lambda i, be, nv: (i, 0)),
        scratch_shapes=[pltpu.VMEM((d, EXPERT_HIDDEN), _BF16), pltpu.VMEM((d, EXPERT_HIDDEN), _BF16),
                        pltpu.VMEM((EXPERT_HIDDEN, d), _BF16)],
    )
    return pl.pallas_call(
        _moe_kernel,
        grid_spec=grid_spec,
        out_shape=jax.ShapeDtypeStruct((rows, d), _F32),
        name="moe_ffn", compiler_params=_params(1),
    )(block_expert, n_valid, xs, wg, wu, wd)


def _moe(h, route, counts, w_gate, w_up, w_down, layer):
    n = h.shape[0]
    eid = route[:, 2:4].astype(jnp.int32)
    rank = route[:, 4:6].astype(jnp.int32)
    cnt = counts[0, :N_EXPERTS].astype(jnp.int32)
    padded = (cnt + MOE_ROWS - 1) // MOE_ROWS * MOE_ROWS
    pad_end = jnp.cumsum(padded)
    experts = jnp.arange(N_EXPERTS, dtype=jnp.int32)
    pad_start = jnp.sum(jnp.where(eid[:, :, None] == experts, pad_end - padded, 0), axis=-1)
    slot = pad_start + rank
    n_blocks = -(-(n * TOP_K + N_EXPERTS * (MOE_ROWS - 1)) // MOE_ROWS)
    block_start = jnp.arange(n_blocks, dtype=jnp.int32) * MOE_ROWS
    block_expert = jnp.minimum(jnp.sum(pad_end[None, :] <= block_start[:, None], axis=1),
                               N_EXPERTS - 1).astype(jnp.int32)
    n_valid = (pad_end[-1:] // MOE_ROWS).astype(jnp.int32)
    tok = jnp.broadcast_to(jnp.arange(n, dtype=jnp.int32)[:, None], (n, TOP_K))
    _, tok_by_slot = lax.sort_key_val(slot.reshape(-1), tok.reshape(-1))
    run_start = (jnp.cumsum(cnt) - cnt)[block_expert]
    run_pos = block_start - (pad_end - padded)[block_expert]
    pos = (run_pos[:, None] + jnp.arange(MOE_ROWS, dtype=jnp.int32)).reshape(-1)
    occupied = pos < jnp.repeat(cnt[block_expert], MOE_ROWS)
    src = jnp.minimum(jnp.repeat(run_start, MOE_ROWS) + pos, n * TOP_K - 1)
    slot_tok = jnp.where(occupied, tok_by_slot[src], 0)
    yb = _moe_blocks(block_expert, n_valid, h[slot_tok], w_gate, w_up, w_down, layer)
    return yb[slot.T.reshape(-1)]


def _final_kernel(x_ref, y1_ref, y2_ref, g_ref, m_ref, gain_ref, o_ref):
    g = g_ref[...]
    x = x_ref[...] + m_ref[0, 5:6, :] * (g[:, 0:1] * y1_ref[...] + g[:, 1:2] * y2_ref[...])
    o_ref[...] = _rms_rows(x) * gain_ref[...]


def _final(x, y1, y2, route, mods, gain, bsz, n_lat):
    d = x.shape[1]
    tm = ROW_TILE
    lat_tiles = n_lat // tm
    tiles = (CTX_LEN + n_lat) // tm
    src = lambda i: (i // lat_tiles) * tiles + CTX_LEN // tm + i % lat_tiles
    row = lambda width: pl.BlockSpec((tm, width), lambda i: (src(i), 0))
    return pl.pallas_call(
        _final_kernel,
        grid=(bsz * lat_tiles,),
        in_specs=[row(d), row(d), pl.BlockSpec((tm, d), lambda i: (src(i) + bsz * tiles, 0)), row(LANES),
                  pl.BlockSpec((1, 6, d), lambda i: ((i // lat_tiles) * 2 + 1, 0, 0)), _full((1, d))],
        out_specs=pl.BlockSpec((tm, d), lambda i: (i, 0)),
        out_shape=jax.ShapeDtypeStruct((bsz * n_lat, d), _F32),
        name="final_norm", compiler_params=_params(1),
    )(x, y1, y2, route, mods, gain.reshape(1, d))


def _rope_tables(n_lat):
    nf = HEAD_DIM // 4
    inv = ROPE_THETA ** (-jnp.arange(nf, dtype=_F32) / nf)
    rows = n_lat // GRID_W
    r_ang = jnp.repeat(jnp.arange(rows, dtype=_F32), GRID_W)[:, None] * inv
    c_ang = jnp.tile(jnp.arange(GRID_W, dtype=_F32), rows)[:, None] * inv
    cos = jnp.concatenate([jnp.cos(r_ang)] * 2 + [jnp.cos(c_ang)] * 2, axis=-1)
    sin = jnp.concatenate([-jnp.sin(r_ang), jnp.sin(r_ang), -jnp.sin(c_ang), jnp.sin(c_ang)], axis=-1)
    cos = jnp.concatenate([jnp.ones((CTX_LEN, HEAD_DIM), _F32), cos], axis=0)
    sin = jnp.concatenate([jnp.zeros((CTX_LEN, HEAD_DIM), _F32), sin], axis=0)
    return jnp.tile(cos, (1, 2)), jnp.tile(sin, (1, 2))


def kernel(x, c, ctx, c_ctx, mod_w, mod_b, ev_w_in, ev_w_out, ev_q_gain, ev_k_gain, ev_mu_prev, ev_mu_next, ev_w0, ev_w2, ev_a0, ev_a2, ev_g2, ev_k_k, ev_k_a, ev_r_k, ev_gn_w, ev_gn_b, ev_v0, ev_v1, ev_v2, od_w_in, od_w_out, od_lq1, od_lk1, od_lq2, od_lk2, od_subln, moe_w_grp, moe_b_grp, moe_w_rt, moe_b_rt, moe_w_gate, moe_w_up, moe_w_down, final_gain):
    bsz, n_lat, d = x.shape
    t = CTX_LEN + n_lat
    n = bsz * t
    cos, sin = _rope_tables(n_lat)
    cond = jax.nn.silu(jnp.concatenate([c, c_ctx[None, :], jnp.zeros((16 - bsz - 1, d), _F32)], axis=0))
    xs = jnp.concatenate([ctx, x], axis=1).reshape(n, d)
    v_first = None
    moe = None
    for layer in range(DEPTH):
        m = _mm(cond, mod_w[layer], tn=1536, precision=_HI) + mod_b[layer]
        m = m.reshape(16, 6, d)
        mods = jnp.stack([jnp.broadcast_to(m[bsz], (bsz, 6, d)), m[:bsz]], axis=1).reshape(2 * bsz, 6, d)
        w_router = jnp.concatenate(
            [moe_w_grp[layer], moe_w_rt[layer], jnp.zeros((d, LANES - N_GROUPS - N_EXPERTS), _F32)], axis=1)
        b_router = jnp.concatenate(
            [moe_b_grp[layer], moe_b_rt[layer], jnp.zeros((LANES - N_GROUPS - N_EXPERTS,), _F32)])[None, :]
        if layer % 2 == 0:
            e = layer // 2
            p = {'q_gain': ev_q_gain[e], 'k_gain': ev_k_gain[e], 'mu_prev': ev_mu_prev[e],
                 'mu_next': ev_mu_next[e], 'w0': ev_w0[e], 'w2': ev_w2[e], 'a0': ev_a0[e],
                 'a2': ev_a2[e], 'g2': ev_g2[e], 'k_k': ev_k_k[e], 'k_a': ev_k_a[e]}
            if e > 0:
                p['v0'], p['v1'], p['v2'] = ev_v0[e - 1], ev_v1[e - 1], ev_v2[e - 1]
            proj, xs = _project(xs, mods, ev_w_in[e].astype(_BF16), moe=moe)
            q, kk2, vv2, r, v, kkn, lw, a, kd, g = _prepare_even(proj, cos, sin, p, v_first)
            if v_first is None:
                v_first = v
            att = _gqa_attend(q.reshape(bsz, t, -1), kk2.reshape(bsz, t, -1), vv2.reshape(bsz, t, -1))
            seq = lambda z: z.reshape(*z.shape[:-2], bsz, t, B_WIDTH)
            y_fwd, y_bwd = (y.reshape(n, B_WIDTH) for y in
                            _delta_scan(seq(r), seq(v), seq(kkn), seq(lw), seq(a), seq(kd)))
            gn = jnp.stack([ev_gn_w[e], ev_gn_b[e], ev_r_k[e].reshape(B_WIDTH)])
            xs, h, route, counts = _mixer_out(att.reshape(n, -1), xs, mods, ev_w_out[e].astype(_BF16),
                                       w_router, b_router, rwkv=(y_fwd, y_bwd, r, v, kd, g, gn))
        else:
            o = layer // 2
            lambda_init = 0.8 - 0.6 * math.exp(-0.3 * layer)
            qkv, xs = _project(xs, mods, od_w_in[o].astype(_BF16), moe=moe, rope=(cos, sin))
            lam = (jnp.exp(jnp.sum(od_lq1[o] * od_lk1[o])) - jnp.exp(jnp.sum(od_lq2[o] * od_lk2[o]))
                   + lambda_init).reshape(1, 1)
            gain = (od_subln[o] * (1.0 - lambda_init)).reshape(1, LANES)
            att = _diff_attend(lam, qkv.reshape(bsz, t, -1), gain)
            xs, h, route, counts = _mixer_out(att.reshape(n, -1), xs, mods, od_w_out[o].astype(_BF16),
                                       w_router, b_router)
        y12 = _moe(h, route, counts, moe_w_gate, moe_w_up, moe_w_down, layer)
        moe = (y12, y12, route, mods)
    return _final(xs, y12, y12, route, mods, final_gain, bsz, n_lat).reshape(bsz, n_lat, d)
```

```python
import functools
import math

import jax
import jax.numpy as jnp
from jax import lax
from jax.experimental import pallas as pl
from jax.experimental.pallas import tpu as pltpu

D_MODEL = 1024
DEPTH = 4
SEQ = 2048
GRID_W = 64
CTX_LEN = 256
HEAD_DIM = 64
ROPE_THETA = 10000.0
RMS_EPS = 1e-6
SUBLN_EPS = 1e-5
Q_SCALE = HEAD_DIM ** -0.5 * math.log2(math.e)

A_HEADS = 8
A_KV_HEADS = 2
A_WIDTH = 512
A_KV_WIDTH = 128
A_COLS = 768

B_HEADS = 8
B_WIDTH = 512
B_COLS = 1792
RWKV_W_LORA = 64
RWKV_A_LORA = 64
RWKV_V_LORA = 32
RWKV_G_LORA = 128
RWKV_GN_EPS = 64e-5
DECAY_SCALE = math.exp(-0.5)

C_HEADS = 8
C_QK = 1024

N_GROUPS = 4
EXPERTS_PER_GROUP = 8
N_EXPERTS = 32
TOP_K = 2
EXPERT_HIDDEN = 512
MOE_ROWS = 256

LANES = 128
SCAN_CHUNK = 64
SCAN_STEP_CHUNKS = 2
ROW_TILE = 256
Q_TILE = 128
VMEM_LIMIT = 56 * 1024 * 1024

_HI = lax.Precision.HIGHEST
_F32 = jnp.float32
_BF16 = jnp.bfloat16

_NN = (((1,), (0,)), ((), ()))
_NT = (((1,), (1,)), ((), ()))
_TN = (((0,), (0,)), ((), ()))


def _params(n_axes):
    return pltpu.CompilerParams(dimension_semantics=("arbitrary",) * n_axes,
                                vmem_limit_bytes=VMEM_LIMIT)


def _full(shape):
    return pl.BlockSpec(shape, lambda *_: (0,) * len(shape))


def _split2(x):
    hi = x.astype(_BF16)
    return hi, (x - hi.astype(_F32)).astype(_BF16)


def _dot3(a, b, dims=_NN):
    d = lambda u, w: lax.dot_general(u, w, dims, preferred_element_type=_F32)
    return (d(a[1], b[0]) + d(a[0], b[1])) + d(a[0], b[0])


def _dot1(a, b, dims=_NN):
    return lax.dot_general(a.astype(_BF16), b.astype(_BF16), dims, preferred_element_type=_F32)


def _pack_bf16_pairs(x):
    w = x.shape[1] // 2
    bits = pltpu.bitcast(x.astype(_BF16).astype(_F32), jnp.uint32)
    return pltpu.bitcast((bits[:, :w] >> 16) | (bits[:, w:] & jnp.uint32(0xFFFF0000)), _F32)


def _unpack_bf16_pairs(words):
    packed = pltpu.bitcast(words, jnp.uint32)
    lo = pltpu.bitcast(packed << 16, _F32)
    hi = pltpu.bitcast(packed & jnp.uint32(0xFFFF0000), _F32)
    return jnp.concatenate([lo, hi], axis=1).astype(_BF16)


def _lane_lo(shape):
    return lax.broadcasted_iota(jnp.int32, shape, len(shape) - 1) < HEAD_DIM


def _head_sum_matrix(value):
    row = lax.broadcasted_iota(jnp.int32, (LANES, LANES), 0)
    col = lax.broadcasted_iota(jnp.int32, (LANES, LANES), 1)
    return jnp.where((row // HEAD_DIM) == (col // HEAD_DIM), value, 0.0).astype(_BF16)


def _head_sum(x, mat):
    hi, lo = _split2(x)
    return jnp.dot(lo, mat, preferred_element_type=_F32) + jnp.dot(hi, mat, preferred_element_type=_F32)


def _rms_rows(x, eps=RMS_EPS):
    return x * lax.rsqrt(jnp.mean(x * x, axis=-1, keepdims=True) + eps)


def _rope_slab(x, cos, sin):
    lane = lax.broadcasted_iota(jnp.int32, x.shape, 1)
    partner = jnp.where((lane & 16) == 0, pltpu.roll(x, LANES - 16, axis=1), pltpu.roll(x, 16, axis=1))
    return x * cos + partner * sin


def _mods_index(i):
    tiles = (CTX_LEN + SEQ) // ROW_TILE
    return (i // tiles) * 2 + jnp.minimum(i % tiles, 1)


def _mm_kernel(a_ref, w_ref, o_ref, *, precision):
    o_ref[...] = jnp.dot(a_ref[...], w_ref[...], precision=precision,
                         preferred_element_type=_F32).astype(o_ref.dtype)


def _mm(a, w, *, tn, precision=None):
    m, k = a.shape
    n = w.shape[1]
    return pl.pallas_call(
        functools.partial(_mm_kernel, precision=precision),
        grid=(n // tn,),
        in_specs=[pl.BlockSpec((m, k), lambda j: (0, 0)),
                  pl.BlockSpec((k, tn), lambda j: (0, j))],
        out_specs=pl.BlockSpec((m, tn), lambda j: (0, j)),
        out_shape=jax.ShapeDtypeStruct((m, n), _F32),
        name=f"mm_{k}x{n}",
        compiler_params=_params(1),
    )(a, w)


def _proj_kernel(*refs, combine, rope):
    it = iter(refs)
    x_ref, m_ref = next(it), next(it)
    if combine:
        y1_ref, y2_ref, g_ref, mp_ref = (next(it) for _ in range(4))
    w_ref = next(it)
    if rope:
        cos_ref, sin_ref = next(it), next(it)
    o_ref = next(it)
    x = x_ref[...]
    if combine:
        xo_ref = next(it)
        g = g_ref[...]
        x = x + mp_ref[0, 5:6, :] * (g[:, 0:1] * y1_ref[...] + g[:, 1:2] * y2_ref[...])
        xo_ref[...] = x
    h = _rms_rows(x) * (1.0 + m_ref[0, 1:2, :]) + m_ref[0, 0:1, :]
    p = jnp.dot(h.astype(_BF16), w_ref[...], preferred_element_type=_F32)
    if rope:
        cos, sin = cos_ref[...], sin_ref[...]
        for s in range(2 * C_QK // LANES):
            sl = slice(s * LANES, (s + 1) * LANES)
            slab = _rope_slab(p[:, sl], cos, sin)
            if s < C_QK // LANES:
                slab = slab * Q_SCALE
            o_ref[:, sl] = slab.astype(o_ref.dtype)
        ones = jnp.ones((p.shape[0], LANES), o_ref.dtype)
        for h in range(C_HEADS):
            base = 2 * C_QK + 2 * h * LANES
            o_ref[:, base:base + LANES] = p[:, 2 * C_QK + h * LANES:2 * C_QK + (h + 1) * LANES].astype(o_ref.dtype)
            o_ref[:, base + LANES:base + 2 * LANES] = ones
    else:
        o_ref[...] = p


def _project(x, mods, w, *, moe=None, rope=None):
    n, d = x.shape
    cols = w.shape[1]
    tm = ROW_TILE
    tiles = (CTX_LEN + SEQ) // tm
    row = lambda width: pl.BlockSpec((tm, width), lambda i: (i, 0))
    mod_spec = pl.BlockSpec((1, 6, d), lambda i: (_mods_index(i), 0, 0))
    args, specs = [x, mods], [row(d), mod_spec]
    if moe is not None:
        args += list(moe)
        second = pl.BlockSpec((tm, d), lambda i: (i + n // tm, 0))
        specs += [row(d), second, row(LANES), mod_spec]
    args.append(w)
    specs.append(_full(w.shape))
    if rope is not None:
        args += list(rope)
        specs += [pl.BlockSpec((tm, LANES), lambda i: (i % tiles, 0))] * 2
    if rope is not None:
        out_cols = cols + C_HEADS * LANES
        out_shape = [jax.ShapeDtypeStruct((n, out_cols), _BF16)]
    else:
        out_cols = cols
        out_shape = [jax.ShapeDtypeStruct((n, cols), _F32)]
    out_specs = [row(out_cols)]
    if moe is not None:
        out_shape.append(jax.ShapeDtypeStruct((n, d), _F32))
        out_specs.append(row(d))
    out = pl.pallas_call(
        functools.partial(_proj_kernel, combine=moe is not None, rope=rope is not None),
        grid=(n // tm,), in_specs=specs, out_specs=out_specs, out_shape=out_shape,
        name=f"project_{cols}", compiler_params=_params(1),
    )(*args)
    return out if moe is not None else (out[0], x)


def _prep_kernel(*refs, has_v_first):
    it = iter(refs)
    p_ref, pp_ref, pn_ref, cos_ref, sin_ref = (next(it) for _ in range(5))
    gain_ref, mu_ref, wa0_ref, kv_ref, wl_ref, g2_ref = (next(it) for _ in range(6))
    if has_v_first:
        v1_ref, v2_ref, vf_ref = next(it), next(it), next(it)
    q_ref, kk2_ref, vv2_ref, r_ref, v_ref, kkn_ref, lw_ref, a_ref, kd_ref, g_ref = it

    tiles = (CTX_LEN + SEQ) // ROW_TILE
    j = pl.program_id(0) % tiles
    p = p_ref[...]
    cos, sin = cos_ref[...], sin_ref[...]
    mean_mat = _head_sum_matrix(1.0 / HEAD_DIM)
    ones_mat = _head_sum_matrix(1.0)
    lo = _lane_lo((ROW_TILE, LANES))

    def normed(x, gain):
        return x * lax.rsqrt(_head_sum(x * x, mean_mat) + RMS_EPS) * gain

    for s in range(A_WIDTH // LANES):
        sl = slice(s * LANES, (s + 1) * LANES)
        q = _rope_slab(normed(p[:, sl], gain_ref[0:1, :]), cos, sin)
        q_ref[:, sl] = (q * Q_SCALE).astype(_BF16)
    k = _rope_slab(normed(p[:, A_WIDTH:A_WIDTH + LANES], gain_ref[1:2, :]), cos, sin)
    v = p[:, A_WIDTH + LANES:A_COLS]
    swapped = pltpu.roll(k, HEAD_DIM, axis=1)
    kk2_ref[:, :LANES] = jnp.where(lo, k, swapped).astype(_BF16)
    kk2_ref[:, LANES:] = jnp.where(lo, swapped, k).astype(_BF16)
    vv2_ref[:, :LANES] = jnp.where(lo, v, 1.0).astype(_BF16)
    vv2_ref[:, LANES:] = jnp.where(lo, pltpu.roll(v, HEAD_DIM, axis=1), 1.0).astype(_BF16)

    u = p[:, A_COLS:]
    rid = lax.broadcasted_iota(jnp.int32, (ROW_TILE, 1), 0)
    has_prev = jnp.where(j >= 2, 1.0, 0.0)
    has_next = jnp.where((j >= 1) & (j < tiles - 1), 1.0, 0.0)
    first = pp_ref[7:8, A_COLS:] * has_prev
    last = pn_ref[0:1, A_COLS:] * has_next
    prev = jnp.where(rid == 0, first, pltpu.roll(u, 1, axis=0))
    nxt = jnp.where(rid == ROW_TILE - 1, last, pltpu.roll(u, ROW_TILE - 1, axis=0))
    u = u + mu_ref[0:1, :] * (prev - u) + mu_ref[1:2, :] * (nxt - u)
    r, k, v = u[:, :B_WIDTH], u[:, B_WIDTH:2 * B_WIDTH], u[:, 2 * B_WIDTH:3 * B_WIDTH]
    wa, glr = u[:, 3 * B_WIDTH:3 * B_WIDTH + LANES], u[:, 3 * B_WIDTH + LANES:]
    if has_v_first:
        mid = jnp.dot(v.astype(_BF16), v1_ref[...], preferred_element_type=_F32)
        lora = jnp.dot(mid.astype(_BF16), v2_ref[...], preferred_element_type=_F32)
        v = v + (vf_ref[...] - v) * jax.nn.sigmoid(kv_ref[2:3, :] + lora)
    r_ref[...] = r
    v_ref[...] = v
    kks = k * kv_ref[0:1, :]
    for s in range(B_WIDTH // LANES):
        sl = slice(s * LANES, (s + 1) * LANES)
        x = kks[:, sl]
        kkn_ref[:, sl] = x * lax.rsqrt(jnp.maximum(_head_sum(x * x, ones_mat), 1e-24))
    xw = jnp.where(lo, jnp.tanh(wa), wa).astype(_BF16)
    for dr in range(2):
        z = jnp.dot(xw, wl_ref[dr], preferred_element_type=_F32) + wa0_ref[dr:dr + 1, :]
        a = jax.nn.sigmoid(z[:, B_WIDTH:])
        lw_ref[dr] = -DECAY_SCALE * jax.nn.sigmoid(z[:, :B_WIDTH])
        a_ref[dr] = a
        kd_ref[dr] = k * (1.0 + (a - 1.0) * kv_ref[1:2, :])
    g_ref[...] = jnp.dot(jax.nn.sigmoid(glr).astype(_BF16), g2_ref[...], preferred_element_type=_F32)


def _prepare_even(proj, cos, sin, p, v_first):
    n, cols = proj.shape
    tm = ROW_TILE
    tiles = (CTX_LEN + SEQ) // tm
    halo = 8
    per_tile = tm // halo
    row = lambda width: pl.BlockSpec((tm, width), lambda i: (i, 0))
    both = pl.BlockSpec((2, tm, B_WIDTH), lambda i: (0, i, 0))
    tab = pl.BlockSpec((tm, LANES), lambda i: (i % tiles, 0))
    pad = lambda w, rows, width: jnp.zeros((rows, width), _BF16).at[:w.shape[0], :w.shape[1]].set(w.astype(_BF16))
    gains = jnp.stack([jnp.tile(p['q_gain'], 2), jnp.tile(p['k_gain'], 2)])
    mu = jnp.stack([p['mu_prev'], p['mu_next']])
    wa0 = jnp.concatenate([p['w0'], p['a0']], axis=1)
    kvec = jnp.stack([p['k_k'], p['k_a'], p.get('v0', jnp.zeros((B_WIDTH,), _F32))])
    zero = jnp.zeros((RWKV_W_LORA, B_WIDTH), _F32)
    wl = jnp.stack([jnp.concatenate([jnp.concatenate([p['w2'][dr], zero], axis=1),
                                     jnp.concatenate([zero, p['a2'][dr]], axis=1)], axis=0)
                    for dr in range(2)]).astype(_BF16)
    args = [proj, proj, proj, cos, sin, gains, mu, wa0, kvec, wl, p['g2'].astype(_BF16)]
    specs = [row(cols),
             pl.BlockSpec((halo, cols), lambda i: (jnp.maximum(i * per_tile - 1, 0), 0)),
             pl.BlockSpec((halo, cols), lambda i: (jnp.minimum((i + 1) * per_tile, n // halo - 1), 0)),
             tab, tab, _full(gains.shape), _full(mu.shape), _full(wa0.shape), _full(kvec.shape),
             _full(wl.shape), _full(p['g2'].shape)]
    if v_first is not None:
        v1 = pad(p['v1'], B_WIDTH, LANES)
        v2 = pad(p['v2'], LANES, B_WIDTH)
        args += [v1, v2, v_first]
        specs += [_full(v1.shape), _full(v2.shape), row(B_WIDTH)]
    f32 = lambda *shape: jax.ShapeDtypeStruct(shape, _F32)
    bf16 = lambda *shape: jax.ShapeDtypeStruct(shape, _BF16)
    out_shape = [bf16(n, A_WIDTH), bf16(n, 2 * LANES), bf16(n, 2 * LANES),
                 f32(n, B_WIDTH), f32(n, B_WIDTH), f32(n, B_WIDTH),
                 f32(2, n, B_WIDTH), f32(2, n, B_WIDTH), f32(2, n, B_WIDTH), f32(n, B_WIDTH)]
    out_specs = [row(A_WIDTH), row(2 * LANES), row(2 * LANES), row(B_WIDTH), row(B_WIDTH), row(B_WIDTH),
                 both, both, both, row(B_WIDTH)]
    return pl.pallas_call(
        functools.partial(_prep_kernel, has_v_first=v_first is not None),
        grid=(n // tm,), in_specs=specs, out_specs=out_specs, out_shape=out_shape,
        name="prepare_even", compiler_params=_params(1),
    )(*args)


def _split_heads(q2):
    lo = _lane_lo(q2.shape)
    zero = jnp.zeros_like(q2)
    return jnp.concatenate([jnp.where(lo, q2, zero), jnp.where(lo, zero, q2)], axis=0)


def _scores(qs, k):
    return lax.dot_general(qs, k, _NT, preferred_element_type=_F32)


def _interleave(chains):
    results = {}
    while len(results) < len(chains):
        for p, chain in enumerate(chains):
            try:
                next(chain)
            except StopIteration as stop:
                results[p] = stop.value
    return [results[p] for p in range(len(chains))]


def _softmax_pv(s, v_ones):
    p = jnp.exp2(s - jnp.max(s, axis=-1, keepdims=True)).astype(_BF16)
    yield
    return jnp.dot(p, v_ones, preferred_element_type=_F32)


def _gqa_tile(q, k, v):
    tq = q.shape[0]
    s = _scores(_split_heads(q), k)
    yield
    o = yield from _softmax_pv(s, v)
    o = o / pltpu.roll(o, HEAD_DIM, axis=1)
    return jnp.where(_lane_lo((tq, LANES)), o[:tq], pltpu.roll(o[tq:], HEAD_DIM, axis=1))


def _attend_chains(make_chain, q_ref, o_ref):
    pieces = [(slice(i * Q_TILE, (i + 1) * Q_TILE), j, slice(j * LANES, (j + 1) * LANES))
              for i in range(q_ref.shape[1] // Q_TILE) for j in range(q_ref.shape[2] // LANES)]
    outs = _interleave([make_chain(q_ref[0, rows, lanes], j) for rows, j, lanes in pieces])
    for (rows, _, lanes), o in zip(pieces, outs):
        o_ref[0, rows, lanes] = o.astype(o_ref.dtype)


def _gqa_kernel(q_ref, k_ref, v_ref, o_ref):
    is_ctx = pl.program_id(2) < CTX_LEN // q_ref.shape[1]

    def run(tk):
        _attend_chains(lambda q, slab: _gqa_tile(q, k_ref[0, :tk], v_ref[0, :tk]), q_ref, o_ref)

    pl.when(is_ctx)(lambda: run(CTX_LEN))
    pl.when(jnp.logical_not(is_ctx))(lambda: run(k_ref.shape[1]))


def _gqa_attend(q, kk, vv):
    b, t, _ = q.shape
    tq = 2 * Q_TILE
    kv_spec = pl.BlockSpec((1, t, LANES), lambda i, g, j: (i, 0, g))
    return pl.pallas_call(
        _gqa_kernel,
        grid=(b, A_KV_HEADS, t // tq),
        in_specs=[pl.BlockSpec((1, tq, 2 * LANES), lambda i, g, j: (i, j, g)), kv_spec, kv_spec],
        out_specs=pl.BlockSpec((1, tq, 2 * LANES), lambda i, g, j: (i, j, g)),
        out_shape=jax.ShapeDtypeStruct((b, t, A_WIDTH), _BF16),
        name="gqa_attn", compiler_params=_params(3),
    )(q, kk, vv)


def _diff_tile(lam, q, k, v, gain):
    tq = q.shape[0]
    s = _scores(_split_heads(q), k)
    yield
    o = yield from _softmax_pv(s, v)
    o = o[:, :LANES] / o[:, LANES:]
    o = o[:tq] - lam * o[tq:]
    y = o * lax.rsqrt(jnp.mean(o * o, axis=-1, keepdims=True) + SUBLN_EPS)
    return y * gain


def _diff_kernel(lam_ref, q_ref, k_ref, v_ref, g_ref, o_ref):
    is_ctx = pl.program_id(2) < CTX_LEN // q_ref.shape[1]
    lam, gain = lam_ref[0, 0], g_ref[...]

    def run(tk):
        def chain(q, slab):
            lanes = slice(slab * LANES, (slab + 1) * LANES)
            v_lanes = slice(2 * slab * LANES, 2 * (slab + 1) * LANES)
            return _diff_tile(lam, q, k_ref[0, :tk, lanes], v_ref[0, :tk, v_lanes], gain)
        _attend_chains(chain, q_ref, o_ref)

    pl.when(is_ctx)(lambda: run(CTX_LEN))
    pl.when(jnp.logical_not(is_ctx))(lambda: run(k_ref.shape[1]))


def _diff_attend(lam, qkv, gain):
    b, t, _ = qkv.shape
    tq = 2 * Q_TILE
    pairs = C_HEADS // 2
    return pl.pallas_call(
        _diff_kernel,
        grid=(b, pairs, t // tq),
        in_specs=[pl.BlockSpec(memory_space=pltpu.SMEM),
                  pl.BlockSpec((1, tq, 2 * LANES), lambda i, h, j: (i, j, h)),
                  pl.BlockSpec((1, t, 2 * LANES), lambda i, h, j: (i, 0, pairs + h)),
                  pl.BlockSpec((1, t, 4 * LANES), lambda i, h, j: (i, 0, pairs + h)),
                  _full((1, LANES))],
        out_specs=pl.BlockSpec((1, tq, 2 * LANES), lambda i, h, j: (i, j, h)),
        out_shape=jax.ShapeDtypeStruct((b, t, C_HEADS * LANES), _BF16),
        name="diff_attn", compiler_params=_params(3),
    )(lam, qkv, qkv, qkv, gain)


def _scan_chunk(get_h, r, v, kk, lw, cum, cum_end, a, kd, sgn):
    c = SCAN_CHUNK
    lo = _lane_lo((c, LANES))
    row = lax.broadcasted_iota(jnp.int32, (LANES, LANES), 0)
    col = lax.broadcasted_iota(jnp.int32, (LANES, LANES), 1)
    delta = ((col & (c - 1)) - (row & (c - 1))) * sgn
    before = delta < 0
    upto = delta <= 0
    eye = row == col
    zero = jnp.zeros((LANES, LANES), _F32)

    def blockdiag(x):
        z = jnp.zeros_like(x)
        return jnp.concatenate([jnp.where(lo, x, z), jnp.where(lo, z, x)], axis=0)

    p_inv = jnp.exp(-cum)
    p_end = jnp.exp(cum_end - cum)
    kka = kk * a
    a_rs = blockdiag(-kk * jnp.exp(cum - lw))
    r_rs = blockdiag(r * jnp.exp(cum))
    b_rs = blockdiag(kka * p_inv)
    k_rs = blockdiag(kd * p_inv)
    v_rs = blockdiag(v)
    bp_rs = blockdiag(kka * p_end)
    kp_rs = blockdiag(kd * p_end)

    bk = _split2(jnp.concatenate([b_rs, k_rs], axis=0))
    za = lax.dot_general(a_rs.astype(_BF16), bk[0], _NT, preferred_element_type=_F32)
    zr = _dot3(_split2(r_rs), bk, _NT)
    yield
    low = jnp.where(before, za[:, :LANES], zero)
    g = jnp.where(before, za[:, LANES:], zero)
    rb = jnp.where(upto, zr[:, :LANES], zero)
    rk = jnp.where(upto, zr[:, LANES:], zero)

    x = _dot1(low, low)
    yield
    m = jnp.where(eye, 1.0, 0.0).astype(_F32) + low
    n = 2
    while 2 * n < c:
        res = _dot1(x, jnp.concatenate([m, x], axis=1))
        m = m + res[:, :LANES]
        x = res[:, LANES:]
        n *= 2
        yield
    m = m + _dot1(x, m)
    yield
    gv = _dot1(g, v_rs)
    yield
    au = _dot1(m, jnp.concatenate([a_rs, gv], axis=1))
    yield
    rhs = _split2(jnp.concatenate(
        [au, jnp.concatenate([zero, v_rs], axis=1)], axis=0))
    tf = lax.dot_general(jnp.concatenate([bp_rs, kp_rs], axis=0).astype(_BF16), rhs[0], _TN,
                         preferred_element_type=_F32)
    qy = _dot3(_split2(jnp.concatenate([rb, rk], axis=1)), rhs)
    yield
    t = jnp.where(eye, jnp.broadcast_to(jnp.exp(cum_end), (LANES, LANES)), zero) + tf[:, :LANES]
    q = r_rs + qy[:, :LANES]
    out = _dot1(jnp.concatenate([q, t], axis=0), get_h())
    y = out[:LANES] + qy[:, LANES:]
    h_new = out[LANES:] + tf[:, LANES:]
    return y[:c] + y[c:], h_new


def _recording(chain, key, store):
    store[key] = yield from chain
    return store[key]


def _scan_kernel(*refs):
    c = SCAN_CHUNK
    ins, y_refs, h_ref = refs[:12], refs[12:14], refs[14]

    @pl.when(pl.program_id(1) == 0)
    def _():
        h_ref[...] = jnp.zeros_like(h_ref)

    tr = lax.broadcasted_iota(jnp.int32, (c, c), 0)
    tc = lax.broadcasted_iota(jnp.int32, (c, c), 1)
    chains, done = [], {}
    for step in range(SCAN_STEP_CHUNKS):
        for d, sgn in enumerate((1, -1)):
            r_ref, v_ref, kk_ref, lw_ref, a_ref, kd_ref = ins[6 * d:6 * d + 6]
            sub = step if d == 0 else SCAN_STEP_CHUNKS - 1 - step
            rows = slice(sub * c, (sub + 1) * c)
            tri = jnp.where((tc - tr) * sgn <= 0, 1.0, 0.0).astype(_BF16)
            lw = lw_ref[0, 0, rows]
            l1 = lw.astype(_BF16)
            rem = lw - l1.astype(_F32)
            l2 = rem.astype(_BF16)
            l3 = (rem - l2.astype(_F32)).astype(_BF16)
            dot = lambda w, tri=tri: jnp.dot(tri, w, preferred_element_type=_F32)
            cum = (dot(l3) + dot(l2)) + dot(l1)
            cum_end = jnp.sum(lw, axis=0, keepdims=True)
            pairs = lw.shape[1] // LANES
            for p in range(pairs):
                sl = slice(p * LANES, (p + 1) * LANES)
                slot = d * pairs + p
                if step == 0:
                    get_h = lambda slot=slot: h_ref[slot]
                else:
                    get_h = lambda key=(step - 1, slot): done[key][1]
                chain = _scan_chunk(get_h, r_ref[0, rows, sl], v_ref[0, rows, sl], kk_ref[0, rows, sl],
                                    lw[:, sl], cum[:, sl], cum_end[:, sl],
                                    a_ref[0, 0, rows, sl], kd_ref[0, 0, rows, sl], sgn)
                chains.append(_recording(chain, (step, slot), done))
    _interleave(chains)
    for (step, slot), (y, h) in done.items():
        d, p = divmod(slot, h_ref.shape[0] // 2)
        sub = step if d == 0 else SCAN_STEP_CHUNKS - 1 - step
        y_refs[d][0, sub * c:(sub + 1) * c, p * LANES:(p + 1) * LANES] = y
        if step == SCAN_STEP_CHUNKS - 1:
            h_ref[slot] = h


def _delta_scan(r, v, kk, lw, a, kd):
    b, t, w = r.shape
    rows = SCAN_CHUNK * SCAN_STEP_CHUNKS
    n_blocks = t // rows
    n_ctx = CTX_LEN // rows

    def block(d, j):
        return j if d == 0 else jnp.where(j < n_ctx, n_ctx - 1 - j, n_blocks + n_ctx - 1 - j)

    specs, args = [], []
    for d in range(2):
        shared = pl.BlockSpec((1, rows, w), lambda i, j, d=d: (i, block(d, j), 0))
        per_dir = pl.BlockSpec((1, 1, rows, w), lambda i, j, d=d: (d, i, block(d, j), 0))
        specs += [shared, shared, shared, per_dir, per_dir, per_dir]
        args += [r, v, kk, lw, a, kd]
    out_specs = [pl.BlockSpec((1, rows, w), lambda i, j, d=d: (i, block(d, j), 0)) for d in range(2)]
    return pl.pallas_call(
        _scan_kernel,
        grid=(b, n_blocks),
        in_specs=specs,
        out_specs=out_specs,
        out_shape=[jax.ShapeDtypeStruct((b, t, w), _F32)] * 2,
        scratch_shapes=[pltpu.VMEM((2 * w // LANES, LANES, LANES), _F32)],
        name="delta_scan", compiler_params=_params(2),
    )(*args)


def _post_kernel(*refs, even):
    it = iter(refs)
    if even:
        yf_ref, yb_ref, r_ref, v_ref, kd_ref, g_ref, gn_ref = (next(it) for _ in range(7))
    att_ref, x_ref, m_ref, wo_ref, wrh_ref, wrl_ref, br_ref = (next(it) for _ in range(7))
    xo_ref, h_ref, route_ref, count_ref, carry_ref = it
    if even:
        mean_mat = _head_sum_matrix(1.0 / HEAD_DIM)
        ones_mat = _head_sum_matrix(1.0)
        parts = [att_ref[...]]
        for s in range(B_WIDTH // LANES):
            sl = slice(s * LANES, (s + 1) * LANES)
            y = yf_ref[:, sl] + yb_ref[:, sl]
            dev = y - _head_sum(y, mean_mat)
            var = _head_sum(dev * dev, mean_mat)
            y = dev * lax.rsqrt(var + RWKV_GN_EPS) * gn_ref[0:1, sl] + gn_ref[1:2, sl]
            r, v = r_ref[:, sl], v_ref[:, sl]
            for dr in range(2):
                y = y + _head_sum(r * kd_ref[dr, :, sl] * gn_ref[2:3, sl], ones_mat) * v
            parts.append((y * g_ref[:, sl]).astype(_BF16))
        mixed = jnp.concatenate(parts, axis=1)
    else:
        mixed = att_ref[...]
    out = jnp.dot(mixed, wo_ref[...], preferred_element_type=_F32)
    x = x_ref[...] + m_ref[0, 2:3, :] * out
    xo_ref[...] = x
    h = _rms_rows(x) * (1.0 + m_ref[0, 4:5, :]) + m_ref[0, 3:4, :]
    h_ref[...] = _pack_bf16_pairs(h)
    logits = _dot3(_split2(h), (wrh_ref[...], wrl_ref[...])) + br_ref[...]
    _route(logits, route_ref, count_ref, carry_ref)


def _route(logits, route_ref, count_ref, carry_ref):
    tm = logits.shape[0]
    lane = lax.broadcasted_iota(jnp.int32, (tm, LANES), 1)
    neg = -1e30
    first_lane = lambda hit: jnp.min(jnp.where(hit, lane, LANES), axis=-1, keepdims=True)

    is_grp = lane < N_GROUPS
    grp = jnp.where(is_grp, logits, neg)
    g_max = jnp.max(grp, axis=-1, keepdims=True)
    g_sel = first_lane(grp == g_max)
    grp_w = 1.0 / jnp.sum(jnp.where(is_grp, jnp.exp(grp - g_max), 0.0), axis=-1, keepdims=True)

    base = N_GROUPS + EXPERTS_PER_GROUP * g_sel
    in_grp = (lane >= base) & (lane < base + EXPERTS_PER_GROUP)
    ex = jnp.where(in_grp, logits, neg)
    e_max = jnp.max(ex, axis=-1, keepdims=True)
    pe = jnp.where(in_grp, jnp.exp(ex - e_max), 0.0)
    prob = pe / jnp.sum(pe, axis=-1, keepdims=True)
    p1 = jnp.max(prob, axis=-1, keepdims=True)
    l1 = first_lane(in_grp & (prob == p1))
    rest = jnp.where(in_grp & (lane != l1), prob, -1.0)
    p2 = jnp.max(rest, axis=-1, keepdims=True)
    l2 = first_lane(rest == p2)
    scale = grp_w / (p1 + p2)
    e1, e2 = l1 - N_GROUPS, l2 - N_GROUPS

    @pl.when(pl.program_id(0) == 0)
    def _():
        carry_ref[...] = jnp.zeros_like(carry_ref)

    oh1 = jnp.where(lane == e1, 1.0, 0.0)
    oh2 = jnp.where(lane == e2, 1.0, 0.0)
    rr = lax.broadcasted_iota(jnp.int32, (tm, tm), 0)
    cc = lax.broadcasted_iota(jnp.int32, (tm, tm), 1)
    tri = jnp.where(cc < rr, 1.0, 0.0).astype(_BF16)
    before = lambda oh: jnp.dot(tri, oh.astype(_BF16), preferred_element_type=_F32)
    carry = carry_ref[...]
    tot1 = jnp.sum(oh1, axis=0, keepdims=True)
    rank1 = jnp.sum(oh1 * (before(oh1) + carry), axis=-1, keepdims=True)
    rank2 = jnp.sum(oh2 * (before(oh2) + (carry + tot1)), axis=-1, keepdims=True)
    carry = carry + tot1 + jnp.sum(oh2, axis=0, keepdims=True)
    carry_ref[...] = carry
    count_ref[...] = carry

    out = jnp.zeros((tm, LANES), _F32)
    for i, val in enumerate((p1 * scale, p2 * scale, e1.astype(_F32), e2.astype(_F32), rank1, rank2)):
        out = jnp.where(lane == i, val, out)
    route_ref[...] = out


def _mixer_out(att, x, mods, w_out, w_router, b_router, rwkv=None):
    n, d = x.shape
    tm = ROW_TILE
    row = lambda width: pl.BlockSpec((tm, width), lambda i: (i, 0))
    both = pl.BlockSpec((2, tm, B_WIDTH), lambda i: (0, i, 0))
    args, specs = [], []
    if rwkv is not None:
        y_fwd, y_bwd, r, v, kd, g, gn = rwkv
        args += [y_fwd, y_bwd, r, v, kd, g, gn]
        specs += [row(B_WIDTH)] * 4 + [both, row(B_WIDTH), _full(gn.shape)]
    wr_hi, wr_lo = _split2(w_router)
    args += [att, x, mods, w_out, wr_hi, wr_lo, b_router]
    specs += [row(att.shape[1]), row(d), pl.BlockSpec((1, 6, d), lambda i: (_mods_index(i), 0, 0)),
              _full(w_out.shape), _full(wr_hi.shape), _full(wr_lo.shape), _full(b_router.shape)]
    return pl.pallas_call(
        functools.partial(_post_kernel, even=rwkv is not None),
        grid=(n // tm,), in_specs=specs,
        out_specs=[row(d), row(d // 2), row(LANES), _full((1, LANES))],
        out_shape=[jax.ShapeDtypeStruct((n, d), _F32), jax.ShapeDtypeStruct((n, d // 2), _F32),
                   jax.ShapeDtypeStruct((n, LANES), _F32), jax.ShapeDtypeStruct((1, LANES), _F32)],
        scratch_shapes=[pltpu.VMEM((1, LANES), _F32)],
        name="mixer_out_even" if rwkv is not None else "mixer_out_odd", compiler_params=_params(1),
    )(*args)


def _moe_kernel(be_ref, nv_ref, nx_ref, sl_ref, x_ref, wg_ref, wu_ref, wd_ref, o_ref,
                wg_s, wu_s, wd_s, wg_f, wu_f, wd_f, sems, *, layer):
    i = pl.program_id(0)
    valid = i < nv_ref[0]
    new_expert = (i == 0) | (be_ref[i] != be_ref[jnp.maximum(i - 1, 0)])

    def copies(expert, slot):
        return [pltpu.make_async_copy(src.at[layer, expert], dst.at[slot], sems.at[slot, n])
                for n, (src, dst) in enumerate(((wg_ref, wg_f), (wu_ref, wu_f), (wd_ref, wd_f)))]

    @pl.when(valid & new_expert)
    def _():
        slot = sl_ref[i]

        @pl.when(i == 0)
        def _():
            for c in copies(be_ref[0], slot):
                c.start()

        for c in copies(be_ref[i], slot):
            c.wait()

        @pl.when(nx_ref[i] >= 0)
        def _():
            for c in copies(nx_ref[i], 1 - slot):
                c.start()

        wg_s[...] = wg_f[slot].astype(_BF16)
        wu_s[...] = wu_f[slot].astype(_BF16)
        wd_s[...] = wd_f[slot].astype(_BF16)

    @pl.when(valid)
    def _():
        x = _unpack_bf16_pairs(x_ref[...])

        def hidden_half(cols):
            gate = jnp.dot(x, wg_s[:, cols], preferred_element_type=_F32)
            up = jnp.dot(x, wu_s[:, cols], preferred_element_type=_F32)
            yield
            hid = (gate * jax.nn.sigmoid(gate) * up).astype(_BF16)
            yield
            return jnp.dot(hid, wd_s[cols, :], preferred_element_type=_F32)

        half = EXPERT_HIDDEN // 2
        lo, hi = _interleave([hidden_half(slice(0, half)), hidden_half(slice(half, EXPERT_HIDDEN))])
        o_ref[...] = lo + hi

    @pl.when(jnp.logical_not(valid))
    def _():
        o_ref[...] = jnp.zeros_like(o_ref)


def _moe_blocks(block_expert, n_valid, xs, wg, wu, wd, layer):
    rows, d = xs.shape[0], 2 * xs.shape[1]
    nb = rows // MOE_ROWS
    idx = jnp.arange(nb, dtype=jnp.int32)
    starts_run = (idx == 0) | (block_expert != jnp.roll(block_expert, 1))
    slot = ((jnp.cumsum(starts_run) - 1) % 2).astype(jnp.int32)
    run_end = jnp.sum(block_expert[None, :] <= block_expert[:, None], axis=1).astype(jnp.int32)
    next_expert = jnp.where(run_end < n_valid[0], block_expert[jnp.minimum(run_end, nb - 1)], -1)
    hbm = pl.BlockSpec(memory_space=pl.ANY)
    grid_spec = pltpu.PrefetchScalarGridSpec(
        num_scalar_prefetch=4,
        grid=(nb,),
        in_specs=[pl.BlockSpec((MOE_ROWS, d // 2), lambda i, *_: (i, 0)), hbm, hbm, hbm],
        out_specs=pl.BlockSpec((MOE_ROWS, d), lambda i, *_: (i, 0)),
        scratch_shapes=[pltpu.VMEM((d, EXPERT_HIDDEN), _BF16), pltpu.VMEM((d, EXPERT_HIDDEN), _BF16),
                        pltpu.VMEM((EXPERT_HIDDEN, d), _BF16),
                        pltpu.VMEM((2, d, EXPERT_HIDDEN), _F32), pltpu.VMEM((2, d, EXPERT_HIDDEN), _F32),
                        pltpu.VMEM((2, EXPERT_HIDDEN, d), _F32), pltpu.SemaphoreType.DMA((2, 3))],
    )
    return pl.pallas_call(
        functools.partial(_moe_kernel, layer=layer),
        grid_spec=grid_spec,
        out_shape=jax.ShapeDtypeStruct((rows, d), _F32),
        name="moe_ffn", compiler_params=_params(1),
    )(block_expert, n_valid, next_expert.astype(jnp.int32), slot, xs, wg, wu, wd)


def _moe(h, route, counts, w_gate, w_up, w_down, layer):
    n = h.shape[0]
    eid = route[:, 2:4].astype(jnp.int32)
    rank = route[:, 4:6].astype(jnp.int32)
    cnt = counts[0, :N_EXPERTS].astype(jnp.int32)
    padded = (cnt + MOE_ROWS - 1) // MOE_ROWS * MOE_ROWS
    pad_end = jnp.cumsum(padded)
    experts = jnp.arange(N_EXPERTS, dtype=jnp.int32)
    pad_start = jnp.sum(jnp.where(eid[:, :, None] == experts, pad_end - padded, 0), axis=-1)
    slot = pad_start + rank
    n_blocks = -(-(n * TOP_K + N_EXPERTS * (MOE_ROWS - 1)) // MOE_ROWS)
    block_start = jnp.arange(n_blocks, dtype=jnp.int32) * MOE_ROWS
    block_expert = jnp.minimum(jnp.sum(pad_end[None, :] <= block_start[:, None], axis=1),
                               N_EXPERTS - 1).astype(jnp.int32)
    n_valid = (pad_end[-1:] // MOE_ROWS).astype(jnp.int32)
    tok = jnp.broadcast_to(jnp.arange(n, dtype=jnp.int32)[:, None], (n, TOP_K))
    _, tok_by_slot = lax.sort_key_val(slot.reshape(-1), tok.reshape(-1))
    run_start = (jnp.cumsum(cnt) - cnt)[block_expert]
    run_pos = block_start - (pad_end - padded)[block_expert]
    pos = (run_pos[:, None] + jnp.arange(MOE_ROWS, dtype=jnp.int32)).reshape(-1)
    occupied = pos < jnp.repeat(cnt[block_expert], MOE_ROWS)
    src = jnp.minimum(jnp.repeat(run_start, MOE_ROWS) + pos, n * TOP_K - 1)
    slot_tok = jnp.where(occupied, tok_by_slot[src], 0)
    yb = _moe_blocks(block_expert, n_valid, h[slot_tok], w_gate, w_up, w_down, layer)
    return yb[slot.T.reshape(-1)]


def _final_kernel(x_ref, y1_ref, y2_ref, g_ref, m_ref, gain_ref, o_ref):
    g = g_ref[...]
    x = x_ref[...] + m_ref[0, 5:6, :] * (g[:, 0:1] * y1_ref[...] + g[:, 1:2] * y2_ref[...])
    o_ref[...] = _rms_rows(x) * gain_ref[...]


def _final(x, y1, y2, route, mods, gain, bsz, n_lat):
    d = x.shape[1]
    tm = ROW_TILE
    lat_tiles = n_lat // tm
    tiles = (CTX_LEN + n_lat) // tm
    src = lambda i: (i // lat_tiles) * tiles + CTX_LEN // tm + i % lat_tiles
    row = lambda width: pl.BlockSpec((tm, width), lambda i: (src(i), 0))
    return pl.pallas_call(
        _final_kernel,
        grid=(bsz * lat_tiles,),
        in_specs=[row(d), row(d), pl.BlockSpec((tm, d), lambda i: (src(i) + bsz * tiles, 0)), row(LANES),
                  pl.BlockSpec((1, 6, d), lambda i: ((i // lat_tiles) * 2 + 1, 0, 0)), _full((1, d))],
        out_specs=pl.BlockSpec((tm, d), lambda i: (i, 0)),
        out_shape=jax.ShapeDtypeStruct((bsz * n_lat, d), _F32),
        name="final_norm", compiler_params=_params(1),
    )(x, y1, y2, route, mods, gain.reshape(1, d))


def _rope_tables(n_lat):
    nf = HEAD_DIM // 4
    inv = ROPE_THETA ** (-jnp.arange(nf, dtype=_F32) / nf)
    rows = n_lat // GRID_W
    r_ang = jnp.repeat(jnp.arange(rows, dtype=_F32), GRID_W)[:, None] * inv
    c_ang = jnp.tile(jnp.arange(GRID_W, dtype=_F32), rows)[:, None] * inv
    cos = jnp.concatenate([jnp.cos(r_ang)] * 2 + [jnp.cos(c_ang)] * 2, axis=-1)
    sin = jnp.concatenate([-jnp.sin(r_ang), jnp.sin(r_ang), -jnp.sin(c_ang), jnp.sin(c_ang)], axis=-1)
    cos = jnp.concatenate([jnp.ones((CTX_LEN, HEAD_DIM), _F32), cos], axis=0)
    sin = jnp.concatenate([jnp.zeros((CTX_LEN, HEAD_DIM), _F32), sin], axis=0)
    return jnp.tile(cos, (1, 2)), jnp.tile(sin, (1, 2))


def kernel(x, c, ctx, c_ctx, mod_w, mod_b, ev_w_in, ev_w_out, ev_q_gain, ev_k_gain, ev_mu_prev, ev_mu_next, ev_w0, ev_w2, ev_a0, ev_a2, ev_g2, ev_k_k, ev_k_a, ev_r_k, ev_gn_w, ev_gn_b, ev_v0, ev_v1, ev_v2, od_w_in, od_w_out, od_lq1, od_lk1, od_lq2, od_lk2, od_subln, moe_w_grp, moe_b_grp, moe_w_rt, moe_b_rt, moe_w_gate, moe_w_up, moe_w_down, final_gain):
    bsz, n_lat, d = x.shape
    t = CTX_LEN + n_lat
    n = bsz * t
    cos, sin = _rope_tables(n_lat)
    cond = jax.nn.silu(jnp.concatenate([c, c_ctx[None, :], jnp.zeros((16 - bsz - 1, d), _F32)], axis=0))
    xs = jnp.concatenate([ctx, x], axis=1).reshape(n, d)
    v_first = None
    moe = None
    for layer in range(DEPTH):
        m = _mm(cond, mod_w[layer], tn=1536, precision=_HI) + mod_b[layer]
        m = m.reshape(16, 6, d)
        mods = jnp.stack([jnp.broadcast_to(m[bsz], (bsz, 6, d)), m[:bsz]], axis=1).reshape(2 * bsz, 6, d)
        w_router = jnp.concatenate(
            [moe_w_grp[layer], moe_w_rt[layer], jnp.zeros((d, LANES - N_GROUPS - N_EXPERTS), _F32)], axis=1)
        b_router = jnp.concatenate(
            [moe_b_grp[layer], moe_b_rt[layer], jnp.zeros((LANES - N_GROUPS - N_EXPERTS,), _F32)])[None, :]
        if layer % 2 == 0:
            e = layer // 2
            p = {'q_gain': ev_q_gain[e], 'k_gain': ev_k_gain[e], 'mu_prev': ev_mu_prev[e],
                 'mu_next': ev_mu_next[e], 'w0': ev_w0[e], 'w2': ev_w2[e], 'a0': ev_a0[e],
                 'a2': ev_a2[e], 'g2': ev_g2[e], 'k_k': ev_k_k[e], 'k_a': ev_k_a[e]}
            if e > 0:
                p['v0'], p['v1'], p['v2'] = ev_v0[e - 1], ev_v1[e - 1], ev_v2[e - 1]
            proj, xs = _project(xs, mods, ev_w_in[e].astype(_BF16), moe=moe)
            q, kk2, vv2, r, v, kkn, lw, a, kd, g = _prepare_even(proj, cos, sin, p, v_first)
            if v_first is None:
                v_first = v
            att = _gqa_attend(q.reshape(bsz, t, -1), kk2.reshape(bsz, t, -1), vv2.reshape(bsz, t, -1))
            seq = lambda z: z.reshape(*z.shape[:-2], bsz, t, B_WIDTH)
            y_fwd, y_bwd = (y.reshape(n, B_WIDTH) for y in
                            _delta_scan(seq(r), seq(v), seq(kkn), seq(lw), seq(a), seq(kd)))
            gn = jnp.stack([ev_gn_w[e], ev_gn_b[e], ev_r_k[e].reshape(B_WIDTH)])
            xs, h, route, counts = _mixer_out(att.reshape(n, -1), xs, mods, ev_w_out[e].astype(_BF16),
                                       w_router, b_router, rwkv=(y_fwd, y_bwd, r, v, kd, g, gn))
        else:
            o = layer // 2
            lambda_init = 0.8 - 0.6 * math.exp(-0.3 * layer)
            qkv, xs = _project(xs, mods, od_w_in[o].astype(_BF16), moe=moe, rope=(cos, sin))
            lam = (jnp.exp(jnp.sum(od_lq1[o] * od_lk1[o])) - jnp.exp(jnp.sum(od_lq2[o] * od_lk2[o]))
                   + lambda_init).reshape(1, 1)
            gain = (od_subln[o] * (1.0 - lambda_init)).reshape(1, LANES)
            att = _diff_attend(lam, qkv.reshape(bsz, t, -1), gain)
            xs, h, route, counts = _mixer_out(att.reshape(n, -1), xs, mods, od_w_out[o].astype(_BF16),
                                       w_router, b_router)
        y12 = _moe(h, route, counts, moe_w_gate, moe_w_up, moe_w_down, layer)
        moe = (y12, y12, route, mods)
    return _final(xs, y12, y12, route, mods, final_gain, bsz, n_lat).reshape(bsz, n_lat, d)
```
